```python
import math
import jax, jax.numpy as jnp
from jax import lax
import numpy as np

D_MODEL = 2048
BATCH = 4
SEQ = 2048
DEPTH = 4

EPS = 1e-6
PLE_DIM = 256
SSD_EXPAND = 2
SSD_INNER = SSD_EXPAND * D_MODEL
SSD_HEAD_DIM = 64
SSD_HEADS = SSD_INNER // SSD_HEAD_DIM
SSD_GROUPS = 8
SSD_STATE = 128
SSD_CONV = 4
SSD_CHUNK = 128
SSD_XBC = SSD_INNER + 2 * SSD_GROUPS * SSD_STATE
SC_WIDTH = D_MODEL
SC_CONV = 3
D_FF = int(math.ceil(8 * D_MODEL / 3 / 256) * 256)
IN_SPLITS = (
    SSD_INNER,
    SSD_XBC,
    SSD_HEADS,
    SC_WIDTH,
    SC_WIDTH,
    SC_WIDTH,
    D_MODEL,
    D_MODEL,
)
D_IN = sum(IN_SPLITS)

kernel_name = "hybrid_ssd_shortconv_gated_trunk"


def rmsnorm(x, g):
    xf = x.astype(jnp.float32)
    y = xf * lax.rsqrt(jnp.mean(xf * xf, axis=-1, keepdims=True) + EPS) * g.astype(jnp.float32)
    return y.astype(x.dtype)


def causal_dwconv(x, w):
    k_w = w.shape[0]
    t = x.shape[1]
    xp = jnp.pad(x, ((0, 0), (k_w - 1, 0), (0, 0)))
    out = xp[:, 0:t, :] * w[0]
    for k in range(1, k_w):
        out = out + xp[:, k:k + t, :] * w[k]
    return out


def split_cols(y, sizes):
    idx = np.cumsum(np.array(sizes))[:-1].tolist()
    return jnp.split(y, idx, axis=-1)


def ssd_chunked(xh, dt, a, bm, cm):
    b, t, h, p = xh.shape
    g, n = bm.shape[2], bm.shape[3]
    r = h // g
    c = t // SSD_CHUNK
    L = SSD_CHUNK
    x = (xh.astype(jnp.float32) * dt[..., None]).reshape(b, c, L, g, r, p)
    adt = (dt * a).reshape(b, c, L, g, r).transpose(0, 3, 4, 1, 2)
    acs = jnp.cumsum(adt, axis=-1)
    B = bm.astype(jnp.float32).reshape(b, c, L, g, n)
    C = cm.astype(jnp.float32).reshape(b, c, L, g, n)
    mask = jnp.tril(jnp.ones((L, L), dtype=bool))
    seg = acs[..., :, None] - acs[..., None, :]
    decay = jnp.exp(jnp.where(mask, seg, -jnp.inf))
    cb = jnp.einsum('bclgn,bcsgn->bgcls', C, B)
    y_diag = jnp.einsum('bgcls,bgrcls,bcsgrp->bclgrp', cb, decay, x)
    decay_states = jnp.exp(acs[..., -1:] - acs)
    states = jnp.einsum('bclgn,bgrcl,bclgrp->cbgrpn', B, decay_states, x)
    chunk_decay = jnp.exp(acs[..., -1]).transpose(3, 0, 1, 2)

    def step(s, inp):
        st, dec = inp
        return s * dec[..., None, None] + st, s

    init = jnp.zeros((b, g, r, p, n), jnp.float32)
    _, prev_states = lax.scan(step, init, (states, chunk_decay))
    y_off = jnp.einsum('bclgn,cbgrpn,bgrcl->bclgrp', C, prev_states, jnp.exp(acs))
    return (y_diag + y_off).reshape(b, t, h, p)


def ssd_branch(z, xbc, dt_raw, conv_w, conv_b, dt_bias, a_log, d_skip, norm_w, w_out):
    b, t, _ = z.shape
    xbc = jax.nn.silu(causal_dwconv(xbc, conv_w) + conv_b)
    xs, bm, cm = split_cols(xbc, (SSD_INNER, SSD_GROUPS * SSD_STATE, SSD_GROUPS * SSD_STATE))
    xh = xs.reshape(b, t, SSD_HEADS, SSD_HEAD_DIM)
    bm = bm.reshape(b, t, SSD_GROUPS, SSD_STATE)
    cm = cm.reshape(b, t, SSD_GROUPS, SSD_STATE)
    dt = jax.nn.softplus(dt_raw.astype(jnp.float32) + dt_bias.astype(jnp.float32))
    a = -jnp.exp(a_log.astype(jnp.float32))
    y = ssd_chunked(xh, dt, a, bm, cm) + d_skip.astype(jnp.float32)[:, None] * xh.astype(jnp.float32)
    y = y.reshape(b, t, SSD_INNER) * jax.nn.silu(z.astype(jnp.float32))
    yg = y.reshape(b, t, SSD_GROUPS, SSD_INNER // SSD_GROUPS)
    yg = yg * lax.rsqrt(jnp.mean(yg * yg, axis=-1, keepdims=True) + EPS)
    y = (yg.reshape(b, t, SSD_INNER) * norm_w.astype(jnp.float32)).astype(z.dtype)
    return y @ w_out


def shortconv_branch(gb, gc, xt, conv_w, w_out):
    y = gc * causal_dwconv(gb * xt, conv_w)
    return y @ w_out


def setup_inputs(seed: int = 0) -> dict:
    key = jax.random.key(seed)
    ks = jax.random.split(key, 24)

    def nrm(k, shape, scale):
        return jax.random.normal(k, shape, jnp.float32) * scale

    res_scale = 1.0 / math.sqrt(2 * DEPTH)
    dt0 = jnp.exp(jax.random.uniform(ks[10], (DEPTH, SSD_HEADS), jnp.float32,
                                     math.log(1e-3), math.log(1e-1)))
    dt_bias = dt0 + jnp.log(-jnp.expm1(-dt0))
    return {
        "x": nrm(ks[0], (BATCH, SEQ, D_MODEL), 1.0),
        "p": nrm(ks[1], (DEPTH, BATCH, SEQ, PLE_DIM), 1.0),
        "norm_mix": 1.0 + nrm(ks[2], (DEPTH, D_MODEL), 0.02),
        "w_in": nrm(ks[3], (DEPTH, D_MODEL, D_IN), D_MODEL ** -0.5),
        "ssd_conv_w": nrm(ks[4], (DEPTH, SSD_CONV, SSD_XBC), SSD_CONV ** -0.5),
        "ssd_conv_b": nrm(ks[5], (DEPTH, SSD_XBC), 0.02),
        "ssd_dt_bias": dt_bias,
        "ssd_a_log": jnp.log(jax.random.uniform(ks[6], (DEPTH, SSD_HEADS), jnp.float32, 1.0, 16.0)),
        "ssd_d": 1.0 + nrm(ks[7], (DEPTH, SSD_HEADS), 0.02),
        "ssd_norm": 1.0 + nrm(ks[8], (DEPTH, SSD_INNER), 0.02),
        "ssd_out": nrm(ks[9], (DEPTH, SSD_INNER, D_MODEL), SSD_INNER ** -0.5),
        "sc_conv_w": nrm(ks[11], (DEPTH, SC_CONV, SC_WIDTH), SC_CONV ** -0.5),
        "sc_out": nrm(ks[12], (DEPTH, SC_WIDTH, D_MODEL), SC_WIDTH ** -0.5),
        "w_o": nrm(ks[13], (DEPTH, D_MODEL, D_MODEL), D_MODEL ** -0.5 * res_scale),
        "norm_ffn": 1.0 + nrm(ks[14], (DEPTH, D_MODEL), 0.02),
        "w_gate_up": nrm(ks[15], (DEPTH, D_MODEL, 2 * D_FF), D_MODEL ** -0.5),
        "w_down": nrm(ks[16], (DEPTH, D_FF, D_MODEL), D_FF ** -0.5 * res_scale),
        "norm_ple": 1.0 + nrm(ks[17], (DEPTH, D_MODEL), 0.02),
        "ple_gate": nrm(ks[18], (DEPTH, D_MODEL, D_MODEL), D_MODEL ** -0.5),
        "ple_proj": nrm(ks[19], (DEPTH, PLE_DIM, D_MODEL), PLE_DIM ** -0.5 * res_scale),
        "norm_final": 1.0 + nrm(ks[20], (D_MODEL,), 0.02),
    }


def reference(x, p, norm_mix, w_in, ssd_conv_w, ssd_conv_b, ssd_dt_bias, ssd_a_log, ssd_d,
              ssd_norm, ssd_out, sc_conv_w, sc_out, w_o, norm_ffn, w_gate_up, w_down,
              norm_ple, ple_gate, ple_proj, norm_final):
    h = x
    for i in range(DEPTH):
        u = rmsnorm(h, norm_mix[i])
        proj = u @ w_in[i]
        z, xbc, dt_raw, sc_b, sc_c, sc_x, g_a, g_b = split_cols(proj, IN_SPLITS)
        y_a = ssd_branch(z, xbc, dt_raw, ssd_conv_w[i], ssd_conv_b[i], ssd_dt_bias[i],
                         ssd_a_log[i], ssd_d[i], ssd_norm[i], ssd_out[i])
        y_b = shortconv_branch(sc_b, sc_c, sc_x, sc_conv_w[i], sc_out[i])
        merged = jax.nn.sigmoid(g_a) * y_a + jax.nn.sigmoid(g_b) * y_b
        h = h + merged @ w_o[i]
        v = rmsnorm(h, norm_ffn[i])
        gate, up = jnp.split(v @ w_gate_up[i], 2, axis=-1)
        h = h + (jax.nn.silu(gate) * up) @ w_down[i]
        e = p[i] @ ple_proj[i]
        pg = jax.nn.sigmoid(rmsnorm(h, norm_ple[i]) @ ple_gate[i])
        h = h + pg * e
    return rmsnorm(h, norm_final)
```

```python
import functools

import jax
import jax.numpy as jnp
from jax import lax
from jax.experimental import pallas as pl
from jax.experimental.pallas import tpu as pltpu

F32 = jnp.float32
BF16 = jnp.bfloat16

EPS = 1e-6
HEAD_DIM = 64
SSD_GROUPS = 8
SSD_STATE = 128
CHUNK = 128
LANES = 128
SUBLANES = 8
VMEM_BYTES_V7X = 64 * 1024 * 1024
VMEM_CAP = VMEM_BYTES_V7X - 8 * 1024 * 1024


def _nbytes(shape, dtype):
    n = 1
    for s in shape:
        n *= s
    return n * jnp.dtype(dtype).itemsize


def _params(blocks, scratch=0, temps=0):
    need = 2 * sum(_nbytes(s, d) for s, d in blocks) + scratch + temps + (4 << 20)
    return dict(vmem_limit_bytes=int(min(max(need, 16 << 20), VMEM_CAP)))


def _cparams(ngrid, blocks, scratch=0, temps=0):
    return pltpu.CompilerParams(dimension_semantics=("arbitrary",) * ngrid,
                                **_params(blocks, scratch, temps))


def _pick(n, candidates):
    for c in candidates:
        if n % c == 0:
            return c
    raise ValueError(f"no block size in {candidates} divides {n}")


def _rmsnorm_kernel(x_ref, g_ref, o_ref):
    x = x_ref[...]
    ms = jnp.mean(x * x, axis=-1, keepdims=True)
    o_ref[...] = (x * lax.rsqrt(ms + EPS) * g_ref[...]).astype(o_ref.dtype)


def _rmsnorm(x, g, out_dtype):
    m, d = x.shape
    bm = _pick(m, (512, 256, 128))
    blocks = [((bm, d), F32), ((1, d), F32), ((bm, d), out_dtype)]
    return pl.pallas_call(
        _rmsnorm_kernel,
        grid=(m // bm,),
        in_specs=[pl.BlockSpec((bm, d), lambda i: (i, 0)),
                  pl.BlockSpec((1, d), lambda i: (0, 0))],
        out_specs=pl.BlockSpec((bm, d), lambda i: (i, 0)),
        out_shape=jax.ShapeDtypeStruct((m, d), out_dtype),
        compiler_params=_cparams(1, blocks, temps=2 * bm * d * 4),
        name="rmsnorm",
    )(x, g.reshape(1, d))


def _mm_kernel(a_ref, w_ref, o_ref):
    o_ref[...] = jnp.dot(a_ref[...], w_ref[...],
                         preferred_element_type=F32).astype(o_ref.dtype)


def _matmul(a, w, out_dtype):
    m, k = a.shape
    n = w.shape[1]
    bm = _pick(m, (1024, 512, 256, 128))
    bn = _pick(n, (1024, 512, 256, 128))
    blocks = [((bm, k), BF16), ((k, bn), BF16), ((bm, bn), out_dtype)]
    return pl.pallas_call(
        _mm_kernel,
        grid=(m // bm, n // bn),
        in_specs=[pl.BlockSpec((bm, k), lambda i, j: (i, 0)),
                  pl.BlockSpec((k, bn), lambda i, j: (0, j))],
        out_specs=pl.BlockSpec((bm, bn), lambda i, j: (i, j)),
        out_shape=jax.ShapeDtypeStruct((m, n), out_dtype),
        compiler_params=_cparams(2, blocks, temps=bm * bn * 4),
        name="in_proj",
    )(a, w)


def _mm_res_kernel(a_ref, w_ref, h_ref, o_ref):
    o_ref[...] = h_ref[...] + jnp.dot(a_ref[...], w_ref[...], preferred_element_type=F32)


def _matmul_residual(a, w, h, name):
    m, k = a.shape
    n = w.shape[1]
    bm = _pick(m, (1024, 512, 256, 128))
    bn = _pick(n, (1024, 512, 256, 128)) if k <= 2048 else _pick(n, (512, 256, 128))
    blocks = [((bm, k), BF16), ((k, bn), BF16), ((bm, bn), F32), ((bm, bn), F32)]
    return pl.pallas_call(
        _mm_res_kernel,
        grid=(m // bm, n // bn),
        in_specs=[pl.BlockSpec((bm, k), lambda i, j: (i, 0)),
                  pl.BlockSpec((k, bn), lambda i, j: (0, j)),
                  pl.BlockSpec((bm, bn), lambda i, j: (i, j))],
        out_specs=pl.BlockSpec((bm, bn), lambda i, j: (i, j)),
        out_shape=jax.ShapeDtypeStruct((m, n), F32),
        input_output_aliases={2: 0},
        compiler_params=_cparams(2, blocks, temps=bm * bn * 4),
        name=name,
    )(a, w, h)


def _swiglu_kernel(v_ref, wg_ref, wu_ref, o_ref):
    v = v_ref[...]
    gate = jnp.dot(v, wg_ref[...], preferred_element_type=F32)
    up = jnp.dot(v, wu_ref[...], preferred_element_type=F32)
    o_ref[...] = (gate * jax.nn.sigmoid(gate) * up).astype(o_ref.dtype)


def _swiglu(v, w_gate_up):
    m, k = v.shape
    d_ff = w_gate_up.shape[1] // 2
    bm = _pick(m, (1024, 512, 256, 128))
    bn = _pick(d_ff, (512, 256, 128))
    nb = d_ff // bn
    blocks = [((bm, k), BF16), ((k, bn), BF16), ((k, bn), BF16), ((bm, bn), BF16)]
    return pl.pallas_call(
        _swiglu_kernel,
        grid=(m // bm, nb),
        in_specs=[pl.BlockSpec((bm, k), lambda i, j: (i, 0)),
                  pl.BlockSpec((k, bn), lambda i, j: (0, j)),
                  pl.BlockSpec((k, bn), lambda i, j: (0, j + nb))],
        out_specs=pl.BlockSpec((bm, bn), lambda i, j: (i, j)),
        out_shape=jax.ShapeDtypeStruct((m, d_ff), BF16),
        compiler_params=_cparams(2, blocks, temps=3 * bm * bn * 4),
        name="swiglu_up",
    )(v, w_gate_up, w_gate_up)


def _merge_kernel(y_ref, ysc_ref, wa_ref, wb_ref, ga_ref, gb_ref, o_ref):
    ya = jnp.dot(y_ref[...], wa_ref[...], preferred_element_type=F32)
    yb = jnp.dot(ysc_ref[...], wb_ref[...], preferred_element_type=F32)
    ga = jax.nn.sigmoid(ga_ref[...].astype(F32))
    gb = jax.nn.sigmoid(gb_ref[...].astype(F32))
    o_ref[...] = (ga * ya + gb * yb).astype(o_ref.dtype)


def _merge(y, ysc, w_a, w_b, proj, off_ga, off_gb):
    m, ka = y.shape
    kb = ysc.shape[1]
    n = w_a.shape[1]
    bm = _pick(m, (1024, 512, 256, 128))
    bn = _pick(n, (512, 256, 128))
    ja, jb = off_ga // bn, off_gb // bn
    assert off_ga % bn == 0 and off_gb % bn == 0
    blocks = [((bm, ka), BF16), ((bm, kb), BF16), ((ka, bn), BF16), ((kb, bn), BF16),
              ((bm, bn), BF16), ((bm, bn), BF16), ((bm, bn), BF16)]
    return pl.pallas_call(
        _merge_kernel,
        grid=(m // bm, n // bn),
        in_specs=[pl.BlockSpec((bm, ka), lambda i, j: (i, 0)),
                  pl.BlockSpec((bm, kb), lambda i, j: (i, 0)),
                  pl.BlockSpec((ka, bn), lambda i, j: (0, j)),
                  pl.BlockSpec((kb, bn), lambda i, j: (0, j)),
                  pl.BlockSpec((bm, bn), lambda i, j: (i, j + ja)),
                  pl.BlockSpec((bm, bn), lambda i, j: (i, j + jb))],
        out_specs=pl.BlockSpec((bm, bn), lambda i, j: (i, j)),
        out_shape=jax.ShapeDtypeStruct((m, n), BF16),
        compiler_params=_cparams(2, blocks, temps=4 * bm * bn * 4),
        name="branch_merge",
    )(y, ysc, w_a, w_b, proj, proj)


def _ple_kernel(hn_ref, wg_ref, p_ref, wp_ref, h_ref, o_ref):
    pg = jax.nn.sigmoid(jnp.dot(hn_ref[...], wg_ref[...], preferred_element_type=F32))
    e = jnp.dot(p_ref[...].astype(BF16), wp_ref[...], preferred_element_type=F32)
    o_ref[...] = h_ref[...] + pg * e


def _ple(hn, w_gate, p, w_proj, h):
    m, k = hn.shape
    kp = p.shape[1]
    n = w_gate.shape[1]
    bm = _pick(m, (1024, 512, 256, 128))
    bn = _pick(n, (1024, 512, 256, 128))
    blocks = [((bm, k), BF16), ((k, bn), BF16), ((bm, kp), F32), ((kp, bn), BF16),
              ((bm, bn), F32), ((bm, bn), F32)]
    return pl.pallas_call(
        _ple_kernel,
        grid=(m // bm, n // bn),
        in_specs=[pl.BlockSpec((bm, k), lambda i, j: (i, 0)),
                  pl.BlockSpec((k, bn), lambda i, j: (0, j)),
                  pl.BlockSpec((bm, kp), lambda i, j: (i, 0)),
                  pl.BlockSpec((kp, bn), lambda i, j: (0, j)),
                  pl.BlockSpec((bm, bn), lambda i, j: (i, j))],
        out_specs=pl.BlockSpec((bm, bn), lambda i, j: (i, j)),
        out_shape=jax.ShapeDtypeStruct((m, n), F32),
        input_output_aliases={4: 0},
        compiler_params=_cparams(2, blocks, temps=3 * bm * bn * 4),
        name="ple",
    )(hn, w_gate, p, w_proj, h)


def _dtprep_kernel(u_ref, w_ref, bias_ref, alog_ref, dtcol_ref, acscol_ref, acsrow_ref,
                   *, nc, groups, r_heads):
    x = jnp.dot(u_ref[...], w_ref[...], preferred_element_type=F32) + bias_ref[...]
    dt = jnp.maximum(x, 0.0) + jnp.log1p(jnp.exp(-jnp.abs(x)))
    adt = dt * (-jnp.exp(alog_ref[...]))
    row = lax.broadcasted_iota(jnp.int32, (CHUNK, LANES), 0)
    for k in range(nc):
        sl = slice(k * CHUNK, (k + 1) * CHUNK)
        acs = adt[sl]
        sh = 1
        while sh < CHUNK:
            acs = acs + jnp.where(row >= sh, pltpu.roll(acs, sh, 0), 0.0)
            sh *= 2
        acsrow_ref[k] = acs.T
        dtk = dt[sl]
        for g in range(groups):
            shift = (LANES - g * r_heads) % LANES
            dtcol_ref[g, sl, :] = pltpu.roll(dtk, shift, 1) if shift else dtk
            acscol_ref[g, sl, :] = pltpu.roll(acs, shift, 1) if shift else acs


def _dtprep(u, w_dt, bias, a_log, groups, r_heads):
    m, k = u.shape
    nc = 4 if m % (4 * CHUNK) == 0 else 1
    ts = nc * CHUNK
    nchunks = m // CHUNK
    blocks = [((ts, k), BF16), ((k, LANES), BF16), ((groups, ts, LANES), F32),
              ((groups, ts, LANES), F32), ((nc, LANES, LANES), F32)]
    return pl.pallas_call(
        functools.partial(_dtprep_kernel, nc=nc, groups=groups, r_heads=r_heads),
        grid=(m // ts,),
        in_specs=[pl.BlockSpec((ts, k), lambda i: (i, 0)),
                  pl.BlockSpec((k, LANES), lambda i: (0, 0)),
                  pl.BlockSpec((1, LANES), lambda i: (0, 0)),
                  pl.BlockSpec((1, LANES), lambda i: (0, 0))],
        out_specs=[pl.BlockSpec((groups, ts, LANES), lambda i: (0, i, 0)),
                   pl.BlockSpec((groups, ts, LANES), lambda i: (0, i, 0)),
                   pl.BlockSpec((nc, LANES, LANES), lambda i: (i, 0, 0))],
        out_shape=[jax.ShapeDtypeStruct((groups, m, LANES), F32),
                   jax.ShapeDtypeStruct((groups, m, LANES), F32),
                   jax.ShapeDtypeStruct((nchunks, LANES, LANES), F32)],
        compiler_params=_cparams(1, blocks, temps=8 * ts * LANES * 4),
        name="dt_prep",
    )(u, w_dt, bias, a_log)


def _ssd_kernel(z_ref, xs_ref, b_ref, c_ref, dtc_ref, acc_ref, acr_ref,
                cwx_ref, cwb_ref, cwc_ref, cbx_ref, cbb_ref, cbc_ref, dsk_ref, nw_ref,
                o_ref, xe_ref, st_ref, *, nc, r_heads, kconv):
    g = pl.program_id(1)
    c = pl.program_id(2)
    L = CHUNK
    gw = r_heads * HEAD_DIM
    npair = gw // LANES
    ts = nc * L
    halo = SUBLANES

    @pl.when(c == 0)
    def _():
        xe_ref[0:halo, :] = jnp.zeros((halo, xe_ref.shape[1]), F32)
        st_ref[...] = jnp.zeros(st_ref.shape, F32)

    xe_ref[halo:halo + ts, 0:gw] = xs_ref[...].astype(F32)
    xe_ref[halo:halo + ts, gw:gw + SSD_STATE] = b_ref[...].astype(F32)
    xe_ref[halo:halo + ts, gw + SSD_STATE:gw + 2 * SSD_STATE] = c_ref[...].astype(F32)

    def conv_silu(col0, width, w_ref, bias_ref, r0):
        acc = None
        for k in range(kconv):
            win = xe_ref[halo + r0 - (kconv - 1) + k:halo + r0 - (kconv - 1) + k + L, col0:col0 + width]
            term = win * w_ref[k:k + 1, :]
            acc = term if acc is None else acc + term
        acc = acc + bias_ref[...]
        return acc * jax.nn.sigmoid(acc)

    tri = (lax.broadcasted_iota(jnp.int32, (L, L), 0) >= lax.broadcasted_iota(jnp.int32, (L, L), 1))
    lo = lax.broadcasted_iota(jnp.int32, (L, LANES), 1) < HEAD_DIM
    lo_row = lo[0:1]

    def pair(col, j):
        return jnp.where(lo if col.shape[0] == L else lo_row,
                         col[:, 2 * j:2 * j + 1], col[:, 2 * j + 1:2 * j + 2])

    for k in range(nc):
        r0 = k * L
        x = conv_silu(0, gw, cwx_ref, cbx_ref, r0)
        bm = conv_silu(gw, SSD_STATE, cwb_ref, cbb_ref, r0)
        cm = conv_silu(gw + SSD_STATE, SSD_STATE, cwc_ref, cbc_ref, r0)
        bb = bm.astype(BF16)
        cbf = cm.astype(BF16)
        cb = lax.dot_general(cbf, bb, (((1,), (1,)), ((), ())), preferred_element_type=F32)
        bt = bm.T.astype(BF16)
        dtc = dtc_ref[0, r0:r0 + L, :]
        acs = acc_ref[0, r0:r0 + L, :]
        last = acs[L - 1:L, :]
        eacs = jnp.exp(acs)
        dstate = jnp.exp(last - acs)
        cdec = jnp.exp(last)
        ssq = jnp.zeros((L, 1), F32)
        ys = []
        for j in range(npair):
            cs = slice(j * LANES, (j + 1) * LANES)
            xp = x[:, cs]
            xdt = xp * pair(dtc, j)
            y = None
            for hh, keep in ((2 * j, lo), (2 * j + 1, jnp.logical_not(lo))):
                a_s = acr_ref[k, pl.ds(g * r_heads + hh, 1), :]
                seg = acs[:, hh:hh + 1] - a_s
                dec = jnp.exp(jnp.where(tri, seg, -jnp.inf))
                mm = (cb * dec).astype(BF16)
                rhs = jnp.where(keep, xdt, 0.0).astype(BF16)
                t = jnp.dot(mm, rhs, preferred_element_type=F32)
                y = t if y is None else y + t
            xds = (xdt * pair(dstate, j)).astype(BF16)
            s_chunk = jnp.dot(bt, xds, preferred_element_type=F32)
            s_prev = st_ref[:, cs]
            y_off = jnp.dot(cbf, s_prev.astype(BF16), preferred_element_type=F32) * pair(eacs, j)
            st_ref[:, cs] = s_prev * pair(cdec, j) + s_chunk
            yt = y + y_off + dsk_ref[:, cs] * xp
            zt = z_ref[r0:r0 + L, cs].astype(F32)
            yt = yt * (zt * jax.nn.sigmoid(zt))
            ssq = ssq + jnp.sum(yt * yt, axis=-1, keepdims=True)
            ys.append(yt)
        rs = lax.rsqrt(ssq / gw + EPS)
        for j in range(npair):
            cs = slice(j * LANES, (j + 1) * LANES)
            o_ref[r0:r0 + L, cs] = (ys[j] * rs * nw_ref[:, cs]).astype(o_ref.dtype)

    xe_ref[0:halo, :] = xe_ref[ts:ts + halo, :]


def _ssd(proj, dtcol, acscol, acsrow, conv_w, conv_b, d_skip, norm_w, *, batch, seq, inner, groups):
    m = proj.shape[0]
    gw = inner // groups
    r_heads = gw // HEAD_DIM
    assert gw % LANES == 0 and r_heads % 2 == 0
    kconv = conv_w.shape[0]
    nc = 2 if seq % (2 * CHUNK) == 0 else 1
    ts = nc * CHUNK
    nt = seq // ts
    gn = groups * SSD_STATE
    xs_blk, b_blk, c_blk = inner // gw, 2 * inner // SSD_STATE, (2 * inner + gn) // SSD_STATE
    wb_blk, wc_blk = inner // SSD_STATE, (inner + gn) // SSD_STATE
    hp = acsrow.shape[1]
    xe_shape = (SUBLANES + ts, gw + 2 * SSD_STATE)
    st_shape = (SSD_STATE, gw)
    blocks = [((ts, gw), BF16), ((ts, gw), BF16), ((ts, SSD_STATE), BF16), ((ts, SSD_STATE), BF16),
              ((1, ts, LANES), F32), ((1, ts, LANES), F32), ((nc, hp, LANES), F32),
              ((kconv, gw), F32), ((kconv, SSD_STATE), F32), ((kconv, SSD_STATE), F32),
              ((1, gw), F32), ((1, SSD_STATE), F32), ((1, SSD_STATE), F32), ((1, gw), F32), ((1, gw), F32),
              ((ts, gw), BF16)]
    row = lambda b, g, c: b * nt + c
    return pl.pallas_call(
        functools.partial(_ssd_kernel, nc=nc, r_heads=r_heads, kconv=kconv),
        grid=(batch, groups, nt),
        in_specs=[
            pl.BlockSpec((ts, gw), lambda b, g, c: (row(b, g, c), g)),
            pl.BlockSpec((ts, gw), lambda b, g, c: (row(b, g, c), xs_blk + g)),
            pl.BlockSpec((ts, SSD_STATE), lambda b, g, c: (row(b, g, c), b_blk + g)),
            pl.BlockSpec((ts, SSD_STATE), lambda b, g, c: (row(b, g, c), c_blk + g)),
            pl.BlockSpec((1, ts, LANES), lambda b, g, c: (g, row(b, g, c), 0)),
            pl.BlockSpec((1, ts, LANES), lambda b, g, c: (g, row(b, g, c), 0)),
            pl.BlockSpec((nc, hp, LANES), lambda b, g, c: (row(b, g, c), 0, 0)),
            pl.BlockSpec((kconv, gw), lambda b, g, c: (0, g)),
            pl.BlockSpec((kconv, SSD_STATE), lambda b, g, c: (0, wb_blk + g)),
            pl.BlockSpec((kconv, SSD_STATE), lambda b, g, c: (0, wc_blk + g)),
            pl.BlockSpec((1, gw), lambda b, g, c: (0, g)),
            pl.BlockSpec((1, SSD_STATE), lambda b, g, c: (0, wb_blk + g)),
            pl.BlockSpec((1, SSD_STATE), lambda b, g, c: (0, wc_blk + g)),
            pl.BlockSpec((1, gw), lambda b, g, c: (0, g)),
            pl.BlockSpec((1, gw), lambda b, g, c: (0, g)),
        ],
        out_specs=pl.BlockSpec((ts, gw), lambda b, g, c: (row(b, g, c), g)),
        out_shape=jax.ShapeDtypeStruct((m, inner), BF16),
        scratch_shapes=[pltpu.VMEM(xe_shape, F32), pltpu.VMEM(st_shape, F32)],
        compiler_params=_cparams(3, blocks, scratch=_nbytes(xe_shape, F32) + _nbytes(st_shape, F32),
                                 temps=8 << 20),
        name="ssd",
    )(proj, proj, proj, proj, dtcol, acscol, acsrow,
      conv_w, conv_w, conv_w, conv_b, conv_b, conv_b, d_skip, norm_w)


def _sconv_kernel(gb_ref, gc_ref, xt_ref, w_ref, o_ref, pe_ref, *, ts, kconv, cw):
    halo = SUBLANES

    @pl.when(pl.program_id(1) == 0)
    def _():
        pe_ref[0:halo, :] = jnp.zeros((halo, pe_ref.shape[1]), F32)

    pe_ref[halo:halo + ts, :] = gb_ref[...].astype(F32) * xt_ref[...].astype(F32)
    width = pe_ref.shape[1]
    for c0 in range(0, width, cw):
        acc = None
        for k in range(kconv):
            r = halo - (kconv - 1) + k
            term = pe_ref[r:r + ts, c0:c0 + cw] * w_ref[k:k + 1, c0:c0 + cw]
            acc = term if acc is None else acc + term
        o_ref[:, c0:c0 + cw] = (gc_ref[:, c0:c0 + cw].astype(F32) * acc).astype(o_ref.dtype)
    pe_ref[0:halo, :] = pe_ref[ts:ts + halo, :]


def _sconv(proj, conv_w, *, batch, seq, width, off_b, off_c, off_x):
    m = proj.shape[0]
    kconv = conv_w.shape[0]
    ts = _pick(seq, (256, 128))
    nt = seq // ts
    assert off_b % width == 0 and off_c % width == 0 and off_x % width == 0
    jb, jc, jx = off_b // width, off_c // width, off_x // width
    cw = _pick(width, (512, 256, 128))
    pe_shape = (SUBLANES + ts, width)
    blocks = [((ts, width), BF16)] * 4 + [((kconv, width), F32)]
    return pl.pallas_call(
        functools.partial(_sconv_kernel, ts=ts, kconv=kconv, cw=cw),
        grid=(batch, nt),
        in_specs=[pl.BlockSpec((ts, width), lambda b, c: (b * nt + c, jb)),
                  pl.BlockSpec((ts, width), lambda b, c: (b * nt + c, jc)),
                  pl.BlockSpec((ts, width), lambda b, c: (b * nt + c, jx)),
                  pl.BlockSpec((kconv, width), lambda b, c: (0, 0))],
        out_specs=pl.BlockSpec((ts, width), lambda b, c: (b * nt + c, 0)),
        out_shape=jax.ShapeDtypeStruct((m, width), BF16),
        scratch_shapes=[pltpu.VMEM(pe_shape, F32)],
        compiler_params=_cparams(2, blocks, scratch=_nbytes(pe_shape, F32), temps=4 << 20),
        name="short_conv",
    )(proj, proj, proj, conv_w)


def kernel(x, p, norm_mix, w_in, ssd_conv_w, ssd_conv_b, ssd_dt_bias, ssd_a_log, ssd_d, ssd_norm,
           ssd_out, sc_conv_w, sc_out, w_o, norm_ffn, w_gate_up, w_down, norm_ple, ple_gate,
           ple_proj, norm_final):
    batch, seq, d = x.shape
    depth = w_in.shape[0]
    m = batch * seq
    heads = ssd_a_log.shape[1]
    inner = heads * HEAD_DIM
    groups = SSD_GROUPS
    r_heads = heads // groups
    xbc = ssd_conv_w.shape[2]
    scw = sc_conv_w.shape[2]
    assert xbc == inner + 2 * groups * SSD_STATE and heads <= LANES
    dt0 = inner + xbc
    off_scb = dt0
    off_scc, off_scx = off_scb + scw, off_scb + 2 * scw
    off_ga = off_scb + 3 * scw
    off_gb = off_ga + d
    hpad = LANES - heads

    h = x.reshape(m, d)
    for i in range(depth):
        w_main = jnp.concatenate([w_in[i, :, :dt0], w_in[i, :, dt0 + heads:]], axis=1).astype(BF16)
        w_dt = jnp.pad(w_in[i, :, dt0:dt0 + heads], ((0, 0), (0, hpad))).astype(BF16)
        dt_bias = jnp.pad(ssd_dt_bias[i], (0, hpad)).reshape(1, LANES)
        a_log = jnp.pad(ssd_a_log[i], (0, hpad)).reshape(1, LANES)
        d_skip = jnp.repeat(ssd_d[i], HEAD_DIM).reshape(1, inner)

        u = _rmsnorm(h, norm_mix[i], BF16)
        proj = _matmul(u, w_main, BF16)
        dtcol, acscol, acsrow = _dtprep(u, w_dt, dt_bias, a_log, groups, r_heads)
        y = _ssd(proj, dtcol, acscol, acsrow, ssd_conv_w[i], ssd_conv_b[i].reshape(1, xbc), d_skip,
                 ssd_norm[i].reshape(1, inner), batch=batch, seq=seq, inner=inner, groups=groups)
        ysc = _sconv(proj, sc_conv_w[i], batch=batch, seq=seq, width=scw,
                     off_b=off_scb, off_c=off_scc, off_x=off_scx)
        merged = _merge(y, ysc, ssd_out[i].astype(BF16), sc_out[i].astype(BF16), proj, off_ga, off_gb)
        h = _matmul_residual(merged, w_o[i].astype(BF16), h, "w_o_residual")
        v = _rmsnorm(h, norm_ffn[i], BF16)
        act = _swiglu(v, w_gate_up[i].astype(BF16))
        h = _matmul_residual(act, w_down[i].astype(BF16), h, "w_down_residual")
        hn = _rmsnorm(h, norm_ple[i], BF16)
        h = _ple(hn, ple_gate[i].astype(BF16), p[i].reshape(m, -1), ple_proj[i].astype(BF16), h)
    out = _rmsnorm(h, norm_final, F32)
    return out.reshape(batch, seq, d)
```

```python
import functools

import jax
import jax.numpy as jnp
from jax import lax
from jax.experimental import pallas as pl
from jax.experimental.pallas import tpu as pltpu

F32 = jnp.float32
BF16 = jnp.bfloat16

EPS = 1e-6
HEAD_DIM = 64
SSD_GROUPS = 8
SSD_STATE = 128
CHUNK = 128
LANES = 128
SUBLANES = 8
CAST_ROWS = 256
VMEM_BYTES_V7X = 64 * 1024 * 1024
VMEM_CAP = VMEM_BYTES_V7X - 8 * 1024 * 1024


def _nbytes(shape, dtype):
    n = 1
    for s in shape:
        n *= s
    return n * jnp.dtype(dtype).itemsize


def _cparams(ngrid, blocks, scratch=(), temps=0):
    need = (2 * sum(_nbytes(s, d) for s, d in blocks) + sum(_nbytes(s, d) for s, d in scratch)
            + temps + (4 << 20))
    return pltpu.CompilerParams(dimension_semantics=("arbitrary",) * ngrid,
                                vmem_limit_bytes=int(min(max(need, 16 << 20), VMEM_CAP)))


def _pick(n, candidates):
    for c in candidates:
        if n % c == 0:
            return c
    raise ValueError(f"no block size in {candidates} divides {n}")


def _cast_tile(w_ref, wsc_ref):
    k = w_ref.shape[0]
    rc = _pick(k, (CAST_ROWS, LANES, SUBLANES))
    for r in range(0, k, rc):
        wsc_ref[r:r + rc, :] = w_ref[r:r + rc, :].astype(BF16)


def _first_token_tile():
    return pl.program_id(1) == 0


def _rmsnorm_kernel(x_ref, g_ref, o_ref):
    x = x_ref[...]
    ms = jnp.mean(x * x, axis=-1, keepdims=True)
    o_ref[...] = (x * lax.rsqrt(ms + EPS) * g_ref[...]).astype(o_ref.dtype)


def _rmsnorm(x, g, out_dtype):
    m, d = x.shape
    bm = _pick(m, (512, 256, 128))
    blocks = [((bm, d), F32), ((1, d), F32), ((bm, d), out_dtype)]
    return pl.pallas_call(
        _rmsnorm_kernel,
        grid=(m // bm,),
        in_specs=[pl.BlockSpec((bm, d), lambda i: (i, 0)),
                  pl.BlockSpec((1, d), lambda i: (0, 0))],
        out_specs=pl.BlockSpec((bm, d), lambda i: (i, 0)),
        out_shape=jax.ShapeDtypeStruct((m, d), out_dtype),
        compiler_params=_cparams(1, blocks, temps=2 * bm * d * 4),
        name="rmsnorm",
    )(x, g.reshape(1, d))


def _inproj_kernel(a_ref, w_ref, wn_ref, o_ref, wsc_ref, *, n_aligned, skip):
    j = pl.program_id(0)
    k, bn = w_ref.shape
    rc = _pick(k, (CAST_ROWS, LANES, SUBLANES))

    @pl.when(jnp.logical_and(_first_token_tile(), j < n_aligned))
    def _():
        _cast_tile(w_ref, wsc_ref)

    @pl.when(jnp.logical_and(_first_token_tile(), j >= n_aligned))
    def _():
        from_next = lax.broadcasted_iota(jnp.int32, (rc, LANES), 1) >= LANES - skip
        for r in range(0, k, rc):
            rolled = pltpu.roll(w_ref[r:r + rc, :], bn - skip, 1)
            nxt = pltpu.roll(wn_ref[r:r + rc, :], LANES - skip, 1)
            if bn > LANES:
                wsc_ref[r:r + rc, 0:bn - LANES] = rolled[:, 0:bn - LANES].astype(BF16)
            wsc_ref[r:r + rc, bn - LANES:bn] = jnp.where(from_next, nxt, rolled[:, bn - LANES:bn]).astype(BF16)

    o_ref[...] = jnp.dot(a_ref[...], wsc_ref[...], preferred_element_type=F32).astype(o_ref.dtype)


def _inproj(u, w_in, layer, dt0, heads):
    m, k = u.shape
    d_in = w_in.shape[2]
    n = d_in - heads
    bm = _pick(m, (1024, 512, 256, 128))
    bn = next(c for c in (1024, 512, 256, 128) if dt0 % c == 0 and (n - dt0) % c == 0)
    assert 0 < heads < LANES and dt0 % LANES == 0
    n_aligned = dt0 // bn
    blocks = [((bm, k), BF16), ((k, bn), F32), ((k, LANES), F32), ((bm, bn), BF16)]
    scratch = [((k, bn), BF16)]
    return pl.pallas_call(
        functools.partial(_inproj_kernel, n_aligned=n_aligned, skip=heads),
        grid=(n // bn, m // bm),
        in_specs=[pl.BlockSpec((bm, k), lambda j, i: (i, 0)),
                  pl.BlockSpec((None, k, bn), lambda j, i: (layer, 0, j)),
                  pl.BlockSpec((None, k, LANES), lambda j, i: (layer, 0, (j + 1) * (bn // LANES)))],
        out_specs=pl.BlockSpec((bm, bn), lambda j, i: (i, j)),
        out_shape=jax.ShapeDtypeStruct((m, n), BF16),
        scratch_shapes=[pltpu.VMEM(s, d) for s, d in scratch],
        compiler_params=_cparams(2, blocks, scratch, temps=bm * bn * 4 + 4 * CAST_ROWS * bn * 4),
        name="in_proj",
    )(u, w_in, w_in)


def _mm_res_kernel(a_ref, w_ref, h_ref, o_ref, wsc_ref):
    @pl.when(_first_token_tile())
    def _():
        _cast_tile(w_ref, wsc_ref)

    o_ref[...] = h_ref[...] + jnp.dot(a_ref[...], wsc_ref[...], preferred_element_type=F32)


def _matmul_residual(a, w, layer, h, name):
    m, k = a.shape
    n = w.shape[2]
    big_k = k > 2048
    bm = _pick(m, (512, 256, 128)) if big_k else _pick(m, (1024, 512, 256, 128))
    bn = _pick(n, (512, 256, 128)) if big_k else _pick(n, (1024, 512, 256, 128))
    blocks = [((bm, k), BF16), ((k, bn), F32), ((bm, bn), F32), ((bm, bn), F32)]
    scratch = [((k, bn), BF16)]
    return pl.pallas_call(
        _mm_res_kernel,
        grid=(n // bn, m // bm),
        in_specs=[pl.BlockSpec((bm, k), lambda j, i: (i, 0)),
                  pl.BlockSpec((None, k, bn), lambda j, i: (layer, 0, j)),
                  pl.BlockSpec((bm, bn), lambda j, i: (i, j))],
        out_specs=pl.BlockSpec((bm, bn), lambda j, i: (i, j)),
        out_shape=jax.ShapeDtypeStruct((m, n), F32),
        scratch_shapes=[pltpu.VMEM(s, d) for s, d in scratch],
        input_output_aliases={2: 0},
        compiler_params=_cparams(2, blocks, scratch, temps=bm * bn * 4 + 2 * CAST_ROWS * bn * 4),
        name=name,
    )(a, w, h)


def _swiglu_kernel(v_ref, wg_ref, wu_ref, o_ref, wgsc_ref, wusc_ref):
    @pl.when(_first_token_tile())
    def _():
        _cast_tile(wg_ref, wgsc_ref)
        _cast_tile(wu_ref, wusc_ref)

    v = v_ref[...]
    gate = jnp.dot(v, wgsc_ref[...], preferred_element_type=F32)
    up = jnp.dot(v, wusc_ref[...], preferred_element_type=F32)
    o_ref[...] = (gate * jax.nn.sigmoid(gate) * up).astype(o_ref.dtype)


def _swiglu(v, w_gate_up, layer):
    m, k = v.shape
    d_ff = w_gate_up.shape[2] // 2
    bm = _pick(m, (1024, 512, 256, 128))
    bn = _pick(d_ff, (512, 256, 128))
    nb = d_ff // bn
    blocks = [((bm, k), BF16), ((k, bn), F32), ((k, bn), F32), ((bm, bn), BF16)]
    scratch = [((k, bn), BF16), ((k, bn), BF16)]
    return pl.pallas_call(
        _swiglu_kernel,
        grid=(nb, m // bm),
        in_specs=[pl.BlockSpec((bm, k), lambda j, i: (i, 0)),
                  pl.BlockSpec((None, k, bn), lambda j, i: (layer, 0, j)),
                  pl.BlockSpec((None, k, bn), lambda j, i: (layer, 0, j + nb))],
        out_specs=pl.BlockSpec((bm, bn), lambda j, i: (i, j)),
        out_shape=jax.ShapeDtypeStruct((m, d_ff), BF16),
        scratch_shapes=[pltpu.VMEM(s, d) for s, d in scratch],
        compiler_params=_cparams(2, blocks, scratch, temps=3 * bm * bn * 4 + 2 * CAST_ROWS * bn * 4),
        name="swiglu_up",
    )(v, w_gate_up, w_gate_up)


def _merge_kernel(y_ref, ysc_ref, wa_ref, wb_ref, ga_ref, gb_ref, o_ref, wasc_ref, wbsc_ref):
    @pl.when(_first_token_tile())
    def _():
        _cast_tile(wa_ref, wasc_ref)
        _cast_tile(wb_ref, wbsc_ref)

    ya = jnp.dot(y_ref[...], wasc_ref[...], preferred_element_type=F32)
    yb = jnp.dot(ysc_ref[...], wbsc_ref[...], preferred_element_type=F32)
    ga = jax.nn.sigmoid(ga_ref[...].astype(F32))
    gb = jax.nn.sigmoid(gb_ref[...].astype(F32))
    o_ref[...] = (ga * ya + gb * yb).astype(o_ref.dtype)


def _merge(y, ysc, w_a, w_b, layer, proj, off_ga, off_gb):
    m, ka = y.shape
    kb = ysc.shape[1]
    n = w_a.shape[2]
    bm = _pick(m, (512, 256, 128))
    bn = _pick(n, (512, 256, 128))
    ja, jb = off_ga // bn, off_gb // bn
    assert off_ga % bn == 0 and off_gb % bn == 0
    blocks = [((bm, ka), BF16), ((bm, kb), BF16), ((ka, bn), F32), ((kb, bn), F32),
              ((bm, bn), BF16), ((bm, bn), BF16), ((bm, bn), BF16)]
    scratch = [((ka, bn), BF16), ((kb, bn), BF16)]
    return pl.pallas_call(
        _merge_kernel,
        grid=(n // bn, m // bm),
        in_specs=[pl.BlockSpec((bm, ka), lambda j, i: (i, 0)),
                  pl.BlockSpec((bm, kb), lambda j, i: (i, 0)),
                  pl.BlockSpec((None, ka, bn), lambda j, i: (layer, 0, j)),
                  pl.BlockSpec((None, kb, bn), lambda j, i: (layer, 0, j)),
                  pl.BlockSpec((bm, bn), lambda j, i: (i, j + ja)),
                  pl.BlockSpec((bm, bn), lambda j, i: (i, j + jb))],
        out_specs=pl.BlockSpec((bm, bn), lambda j, i: (i, j)),
        out_shape=jax.ShapeDtypeStruct((m, n), BF16),
        scratch_shapes=[pltpu.VMEM(s, d) for s, d in scratch],
        compiler_params=_cparams(2, blocks, scratch, temps=4 * bm * bn * 4 + 2 * CAST_ROWS * bn * 4),
        name="branch_merge",
    )(y, ysc, w_a, w_b, proj, proj)


def _ple_kernel(hn_ref, wg_ref, p_ref, wp_ref, h_ref, o_ref, wgsc_ref, wpsc_ref):
    @pl.when(_first_token_tile())
    def _():
        _cast_tile(wg_ref, wgsc_ref)
        _cast_tile(wp_ref, wpsc_ref)

    pg = jax.nn.sigmoid(jnp.dot(hn_ref[...], wgsc_ref[...], preferred_element_type=F32))
    e = jnp.dot(p_ref[...].astype(BF16), wpsc_ref[...], preferred_element_type=F32)
    o_ref[...] = h_ref[...] + pg * e


def _ple(hn, w_gate, p, w_proj, layer, h):
    m, k = hn.shape
    kp = p.shape[2]
    n = w_gate.shape[2]
    bm = _pick(m, (1024, 512, 256, 128))
    bn = _pick(n, (1024, 512, 256, 128))
    blocks = [((bm, k), BF16), ((k, bn), F32), ((bm, kp), F32), ((kp, bn), F32),
              ((bm, bn), F32), ((bm, bn), F32)]
    scratch = [((k, bn), BF16), ((kp, bn), BF16)]
    return pl.pallas_call(
        _ple_kernel,
        grid=(n // bn, m // bm),
        in_specs=[pl.BlockSpec((bm, k), lambda j, i: (i, 0)),
                  pl.BlockSpec((None, k, bn), lambda j, i: (layer, 0, j)),
                  pl.BlockSpec((None, bm, kp), lambda j, i: (layer, i, 0)),
                  pl.BlockSpec((None, kp, bn), lambda j, i: (layer, 0, j)),
                  pl.BlockSpec((bm, bn), lambda j, i: (i, j))],
        out_specs=pl.BlockSpec((bm, bn), lambda j, i: (i, j)),
        out_shape=jax.ShapeDtypeStruct((m, n), F32),
        scratch_shapes=[pltpu.VMEM(s, d) for s, d in scratch],
        input_output_aliases={4: 0},
        compiler_params=_cparams(2, blocks, scratch, temps=3 * bm * bn * 4 + 2 * CAST_ROWS * bn * 4),
        name="ple",
    )(hn, w_gate, p, w_proj, h)


def _dtprep_kernel(u_ref, w_ref, bias_ref, alog_ref, dtrow_ref, wrow_ref, acsrow_ref, acscol_ref,
                   *, nc, groups, r_heads):
    x = jnp.dot(u_ref[...], w_ref[...].astype(BF16), preferred_element_type=F32) + bias_ref[...]
    dt = jnp.maximum(x, 0.0) + jnp.log1p(jnp.exp(-jnp.abs(x)))
    adt = dt * (-jnp.exp(alog_ref[...]))
    row = lax.broadcasted_iota(jnp.int32, (CHUNK, LANES), 0)
    for k in range(nc):
        sl = slice(k * CHUNK, (k + 1) * CHUNK)
        acs = adt[sl]
        sh = 1
        while sh < CHUNK:
            acs = acs + jnp.where(row >= sh, pltpu.roll(acs, sh, 0), 0.0)
            sh *= 2
        acsrow_ref[k] = acs.T
        dtrow_ref[k] = dt[sl].T
        wrow_ref[k] = (dt[sl] * jnp.exp(acs[CHUNK - 1:CHUNK, :] - acs)).T
        for g in range(groups):
            shift = (LANES - g * r_heads) % LANES
            acscol_ref[g, sl, :] = pltpu.roll(acs, shift, 1) if shift else acs


def _dtprep(u, w_in, layer, dt0, bias, a_log, groups, r_heads):
    m, k = u.shape
    nc = 4 if m % (4 * CHUNK) == 0 else 1
    ts = nc * CHUNK
    nchunks = m // CHUNK
    rows = ((nc, LANES, LANES), F32)
    blocks = [((ts, k), BF16), ((k, LANES), F32), rows, rows, rows, ((groups, ts, LANES), F32)]
    row_spec = pl.BlockSpec((nc, LANES, LANES), lambda i: (i, 0, 0))
    row_shape = jax.ShapeDtypeStruct((nchunks, LANES, LANES), F32)
    return pl.pallas_call(
        functools.partial(_dtprep_kernel, nc=nc, groups=groups, r_heads=r_heads),
        grid=(m // ts,),
        in_specs=[pl.BlockSpec((ts, k), lambda i: (i, 0)),
                  pl.BlockSpec((None, k, LANES), lambda i: (layer, 0, dt0 // LANES)),
                  pl.BlockSpec((1, LANES), lambda i: (0, 0)),
                  pl.BlockSpec((1, LANES), lambda i: (0, 0))],
        out_specs=[row_spec, row_spec, row_spec,
                   pl.BlockSpec((groups, ts, LANES), lambda i: (0, i, 0))],
        out_shape=[row_shape, row_shape, row_shape,
                   jax.ShapeDtypeStruct((groups, m, LANES), F32)],
        compiler_params=_cparams(1, blocks, temps=8 * ts * LANES * 4 + k * LANES * 2),
        name="dt_prep",
    )(u, w_in, bias, a_log)


def _ssd_kernel(z_ref, xs_ref, b_ref, c_ref, xsp_ref, bp_ref, cp_ref,
                dtr_ref, wr_ref, acr_ref, acc_ref, shift_ref,
                cwx_ref, cwb_ref, cwc_ref, cbx_ref, cbb_ref, cbc_ref, dsk_ref, nw_ref,
                o_ref, st_ref, *, nc, r_heads, kconv):
    g = pl.program_id(1)
    c = pl.program_id(2)
    L = CHUNK
    N = SSD_STATE
    gw = r_heads * HEAD_DIM
    npair = gw // LANES

    @pl.when(c == 0)
    def _():
        st_ref[...] = jnp.zeros(st_ref.shape, F32)

    tri = (lax.broadcasted_iota(jnp.int32, (L, L), 0) >= lax.broadcasted_iota(jnp.int32, (L, L), 1))
    lo = lax.broadcasted_iota(jnp.int32, (L, LANES), 1) < HEAD_DIM
    hi = jnp.logical_not(lo)
    lo_row = lo[0:1]

    def conv_silu(ref, prev_ref, k, w_ref, bias_ref):
        if k == 0:
            prev = prev_ref[...]
            prev = jnp.where(c > 0, prev, jnp.zeros_like(prev))
            x2 = jnp.concatenate([prev, ref[0:L, :]], axis=0)
        else:
            x2 = ref[(k - 1) * L:(k + 1) * L, :]
        shifted = jnp.dot(shift_ref[...], x2, preferred_element_type=F32)
        acc = None
        for tap in range(kconv - 1):
            term = shifted[tap * L:(tap + 1) * L, :] * w_ref[tap:tap + 1, :]
            acc = term if acc is None else acc + term
        acc = acc + x2[L:2 * L].astype(F32) * w_ref[kconv - 1:kconv, :]
        acc = acc + bias_ref[...]
        return acc * jax.nn.sigmoid(acc)

    for k in range(nc):
        r0 = k * L
        x = conv_silu(xs_ref, xsp_ref, k, cwx_ref, cbx_ref)
        bm = conv_silu(b_ref, bp_ref, k, cwb_ref, cbb_ref)
        cm = conv_silu(c_ref, cp_ref, k, cwc_ref, cbc_ref)
        cb = lax.dot_general(cm.astype(BF16), bm.astype(BF16), (((1,), (1,)), ((), ())),
                             preferred_element_type=F32)
        bt = bm.T
        acol = acc_ref[0, r0:r0 + L, :]
        cdec = jnp.exp(acol[L - 1:L, :])
        ssq = jnp.zeros((L, 1), F32)
        ys = []
        for j in range(npair):
            cs = slice(j * LANES, (j + 1) * LANES)
            xp = x[:, cs]
            s_prev = st_ref[:, cs]
            s_new = s_prev * jnp.where(lo_row, cdec[:, 2 * j:2 * j + 1], cdec[:, 2 * j + 1:2 * j + 2])
            y = None
            for hh, keep in ((2 * j, lo), (2 * j + 1, hi)):
                head = pl.ds(g * r_heads + hh, 1)
                arow = acr_ref[k, head, :]
                drow = dtr_ref[k, head, :]
                wrow = wr_ref[k, head, :]
                xm = jnp.where(keep, xp, 0.0).astype(BF16)
                sm = jnp.where(keep, s_prev, 0.0).astype(BF16)
                ab = jnp.broadcast_to(acol[:, hh:hh + 1], (L, L))
                dec = jnp.exp(jnp.where(tri, ab - arow, -jnp.inf))
                mh = cb * dec * drow
                ch = cm * jnp.exp(ab)
                lhs = jnp.concatenate([mh, ch], axis=1).astype(BF16)
                rhs = jnp.concatenate([xm, sm], axis=0)
                t = jnp.dot(lhs, rhs, preferred_element_type=F32)
                y = t if y is None else y + t
                bth = (bt * wrow).astype(BF16)
                s_new = s_new + jnp.dot(bth, xm, preferred_element_type=F32)
            st_ref[:, cs] = s_new
            yt = y + dsk_ref[:, cs] * xp
            zt = z_ref[r0:r0 + L, cs].astype(F32)
            yt = yt * (zt * jax.nn.sigmoid(zt))
            ssq = ssq + jnp.sum(yt * yt, axis=-1, keepdims=True)
            ys.append(yt)
        rs = lax.rsqrt(ssq / gw + EPS)
        for j in range(npair):
            cs = slice(j * LANES, (j + 1) * LANES)
            o_ref[r0:r0 + L, cs] = (ys[j] * rs * nw_ref[:, cs]).astype(o_ref.dtype)


def _ssd(proj, dtrow, wrow, acsrow, acscol, conv_w, conv_b, d_skip, norm_w, *, batch, seq, inner, groups):
    m = proj.shape[0]
    gw = inner // groups
    r_heads = gw // HEAD_DIM
    assert gw % LANES == 0 and r_heads % 2 == 0 and SSD_STATE == LANES
    kconv = conv_w.shape[0]
    assert kconv - 1 <= CHUNK
    nc = _pick(seq // CHUNK, (4, 2, 1))
    ts = nc * CHUNK
    nt = seq // ts
    gn = groups * SSD_STATE
    xs_blk, b_blk, c_blk = inner // gw, 2 * inner // SSD_STATE, (2 * inner + gn) // SSD_STATE
    wb_blk, wc_blk = inner // SSD_STATE, (inner + gn) // SSD_STATE
    hp = acsrow.shape[1]
    t_idx = jnp.arange(CHUNK)[None, :, None]
    d_idx = (kconv - 1 - jnp.arange(kconv - 1))[:, None, None]
    s_idx = jnp.arange(2 * CHUNK)[None, None, :]
    shift = (s_idx == CHUNK + t_idx - d_idx).astype(BF16).reshape((kconv - 1) * CHUNK, 2 * CHUNK)
    st_shape = (SSD_STATE, gw)
    blocks = [((ts, gw), BF16), ((ts, gw), BF16), ((ts, SSD_STATE), BF16), ((ts, SSD_STATE), BF16),
              ((CHUNK, gw), BF16), ((CHUNK, SSD_STATE), BF16), ((CHUNK, SSD_STATE), BF16),
              ((nc, hp, LANES), F32), ((nc, hp, LANES), F32), ((nc, hp, LANES), F32),
              ((1, ts, LANES), F32), (shift.shape, BF16),
              ((kconv, gw), F32), ((kconv, SSD_STATE), F32), ((kconv, SSD_STATE), F32),
              ((1, gw), F32), ((1, SSD_STATE), F32), ((1, SSD_STATE), F32), ((1, gw), F32), ((1, gw), F32),
              ((ts, gw), BF16)]
    scratch = [(st_shape, F32)]
    row = lambda b, g, c: b * nt + c
    prev = lambda b, g, c: jnp.maximum(row(b, g, c) * nc - 1, 0)
    return pl.pallas_call(
        functools.partial(_ssd_kernel, nc=nc, r_heads=r_heads, kconv=kconv),
        grid=(batch, groups, nt),
        in_specs=[
            pl.BlockSpec((ts, gw), lambda b, g, c: (row(b, g, c), g)),
            pl.BlockSpec((ts, gw), lambda b, g, c: (row(b, g, c), xs_blk + g)),
            pl.BlockSpec((ts, SSD_STATE), lambda b, g, c: (row(b, g, c), b_blk + g)),
            pl.BlockSpec((ts, SSD_STATE), lambda b, g, c: (row(b, g, c), c_blk + g)),
            pl.BlockSpec((CHUNK, gw), lambda b, g, c: (prev(b, g, c), xs_blk + g)),
            pl.BlockSpec((CHUNK, SSD_STATE), lambda b, g, c: (prev(b, g, c), b_blk + g)),
            pl.BlockSpec((CHUNK, SSD_STATE), lambda b, g, c: (prev(b, g, c), c_blk + g)),
            pl.BlockSpec((nc, hp, LANES), lambda b, g, c: (row(b, g, c), 0, 0)),
            pl.BlockSpec((nc, hp, LANES), lambda b, g, c: (row(b, g, c), 0, 0)),
            pl.BlockSpec((nc, hp, LANES), lambda b, g, c: (row(b, g, c), 0, 0)),
            pl.BlockSpec((1, ts, LANES), lambda b, g, c: (g, row(b, g, c), 0)),
            pl.BlockSpec(shift.shape, lambda b, g, c: (0, 0)),
            pl.BlockSpec((kconv, gw), lambda b, g, c: (0, g)),
            pl.BlockSpec((kconv, SSD_STATE), lambda b, g, c: (0, wb_blk + g)),
            pl.BlockSpec((kconv, SSD_STATE), lambda b, g, c: (0, wc_blk + g)),
            pl.BlockSpec((1, gw), lambda b, g, c: (0, g)),
            pl.BlockSpec((1, SSD_STATE), lambda b, g, c: (0, wb_blk + g)),
            pl.BlockSpec((1, SSD_STATE), lambda b, g, c: (0, wc_blk + g)),
            pl.BlockSpec((1, gw), lambda b, g, c: (0, g)),
            pl.BlockSpec((1, gw), lambda b, g, c: (0, g)),
        ],
        out_specs=pl.BlockSpec((ts, gw), lambda b, g, c: (row(b, g, c), g)),
        out_shape=jax.ShapeDtypeStruct((m, inner), BF16),
        scratch_shapes=[pltpu.VMEM(s, d) for s, d in scratch],
        compiler_params=_cparams(3, blocks, scratch, temps=8 << 20),
        name="ssd",
    )(proj, proj, proj, proj, proj, proj, proj, dtrow, wrow, acsrow, acscol, shift,
      conv_w, conv_w, conv_w, conv_b, conv_b, conv_b, d_skip, norm_w)


def _sconv_kernel(gb_ref, gc_ref, xt_ref, w_ref, o_ref, pe_ref, *, ts, kconv, cw):
    halo = SUBLANES

    @pl.when(pl.program_id(1) == 0)
    def _():
        pe_ref[0:halo, :] = jnp.zeros((halo, pe_ref.shape[1]), F32)

    pe_ref[halo:halo + ts, :] = gb_ref[...].astype(F32) * xt_ref[...].astype(F32)
    width = pe_ref.shape[1]
    for c0 in range(0, width, cw):
        acc = None
        for k in range(kconv):
            r = halo - (kconv - 1) + k
            term = pe_ref[r:r + ts, c0:c0 + cw] * w_ref[k:k + 1, c0:c0 + cw]
            acc = term if acc is None else acc + term
        o_ref[:, c0:c0 + cw] = (gc_ref[:, c0:c0 + cw].astype(F32) * acc).astype(o_ref.dtype)
    pe_ref[0:halo, :] = pe_ref[ts:ts + halo, :]


def _sconv(proj, conv_w, *, batch, seq, width, off_b, off_c, off_x):
    m = proj.shape[0]
    kconv = conv_w.shape[0]
    ts = _pick(seq, (256, 128))
    nt = seq // ts
    assert off_b % width == 0 and off_c % width == 0 and off_x % width == 0 and kconv - 1 <= SUBLANES
    jb, jc, jx = off_b // width, off_c // width, off_x // width
    cw = _pick(width, (512, 256, 128))
    pe_shape = (SUBLANES + ts, width)
    blocks = [((ts, width), BF16)] * 4 + [((kconv, width), F32)]
    scratch = [(pe_shape, F32)]
    return pl.pallas_call(
        functools.partial(_sconv_kernel, ts=ts, kconv=kconv, cw=cw),
        grid=(batch, nt),
        in_specs=[pl.BlockSpec((ts, width), lambda b, c: (b * nt + c, jb)),
                  pl.BlockSpec((ts, width), lambda b, c: (b * nt + c, jc)),
                  pl.BlockSpec((ts, width), lambda b, c: (b * nt + c, jx)),
                  pl.BlockSpec((kconv, width), lambda b, c: (0, 0))],
        out_specs=pl.BlockSpec((ts, width), lambda b, c: (b * nt + c, 0)),
        out_shape=jax.ShapeDtypeStruct((m, width), BF16),
        scratch_shapes=[pltpu.VMEM(s, d) for s, d in scratch],
        compiler_params=_cparams(2, blocks, scratch, temps=4 << 20),
        name="short_conv",
    )(proj, proj, proj, conv_w)


def kernel(x, p, norm_mix, w_in, ssd_conv_w, ssd_conv_b, ssd_dt_bias, ssd_a_log, ssd_d, ssd_norm,
           ssd_out, sc_conv_w, sc_out, w_o, norm_ffn, w_gate_up, w_down, norm_ple, ple_gate,
           ple_proj, norm_final):
    batch, seq, d = x.shape
    depth = w_in.shape[0]
    m = batch * seq
    heads = ssd_a_log.shape[1]
    inner = heads * HEAD_DIM
    groups = SSD_GROUPS
    r_heads = heads // groups
    xbc = ssd_conv_w.shape[2]
    scw = sc_conv_w.shape[2]
    assert xbc == inner + 2 * groups * SSD_STATE and heads <= LANES
    dt0 = inner + xbc
    off_scb = dt0
    off_scc, off_scx = off_scb + scw, off_scb + 2 * scw
    off_ga = off_scb + 3 * scw
    off_gb = off_ga + d
    hpad = LANES - heads
    p2 = p.reshape(depth, m, p.shape[-1])

    h = x.reshape(m, d)
    for i in range(depth):
        dt_bias = jnp.pad(ssd_dt_bias[i], (0, hpad)).reshape(1, LANES)
        a_log = jnp.pad(ssd_a_log[i], (0, hpad)).reshape(1, LANES)
        d_skip = jnp.repeat(ssd_d[i], HEAD_DIM).reshape(1, inner)

        u = _rmsnorm(h, norm_mix[i], BF16)
        proj = _inproj(u, w_in, i, dt0, heads)
        dtrow, wrow, acsrow, acscol = _dtprep(u, w_in, i, dt0, dt_bias, a_log, groups, r_heads)
        y = _ssd(proj, dtrow, wrow, acsrow, acscol, ssd_conv_w[i], ssd_conv_b[i].reshape(1, xbc), d_skip,
                 ssd_norm[i].reshape(1, inner), batch=batch, seq=seq, inner=inner, groups=groups)
        ysc = _sconv(proj, sc_conv_w[i], batch=batch, seq=seq, width=scw,
                     off_b=off_scb, off_c=off_scc, off_x=off_scx)
        merged = _merge(y, ysc, ssd_out, sc_out, i, proj, off_ga, off_gb)
        h = _matmul_residual(merged, w_o, i, h, "w_o_residual")
        v = _rmsnorm(h, norm_ffn[i], BF16)
        act = _swiglu(v, w_gate_up, i)
        h = _matmul_residual(act, w_down, i, h, "w_down_residual")
        hn = _rmsnorm(h, norm_ple[i], BF16)
        h = _ple(hn, ple_gate, p2, ple_proj, i, h)
    out = _rmsnorm(h, norm_final, F32)
    return out.reshape(batch, seq, d)
```

```python
import functools

import jax
import jax.numpy as jnp
from jax import lax
from jax.experimental import pallas as pl
from jax.experimental.pallas import tpu as pltpu

F32 = jnp.float32
BF16 = jnp.bfloat16

EPS = 1e-6
HEAD_DIM = 64
SSD_GROUPS = 8
SSD_STATE = 128
CHUNK = 128
LANES = 128
SUBLANES = 8
CAST_ROWS = 256
VMEM_BYTES_V7X = 64 * 1024 * 1024
VMEM_CAP = VMEM_BYTES_V7X - 8 * 1024 * 1024


def _nbytes(shape, dtype):
    n = 1
    for s in shape:
        n *= s
    return n * jnp.dtype(dtype).itemsize


def _cparams(ngrid, blocks, scratch=(), temps=0):
    need = (2 * sum(_nbytes(s, d) for s, d in blocks) + sum(_nbytes(s, d) for s, d in scratch)
            + temps + (4 << 20))
    return pltpu.CompilerParams(dimension_semantics=("arbitrary",) * ngrid,
                                vmem_limit_bytes=int(min(max(need, 16 << 20), VMEM_CAP)))


def _pick(n, candidates):
    for c in candidates:
        if n % c == 0:
            return c
    raise ValueError(f"no block size in {candidates} divides {n}")


def _cast_tile(w_ref, wsc_ref):
    k = w_ref.shape[0]
    rc = _pick(k, (CAST_ROWS, LANES, SUBLANES))
    for r in range(0, k, rc):
        wsc_ref[r:r + rc, :] = w_ref[r:r + rc, :].astype(BF16)


def _first_token_tile():
    return pl.program_id(1) == 0


def _rmsnorm_kernel(x_ref, g_ref, o_ref):
    x = x_ref[...]
    ms = jnp.mean(x * x, axis=-1, keepdims=True)
    o_ref[...] = (x * lax.rsqrt(ms + EPS) * g_ref[...]).astype(o_ref.dtype)


def _rmsnorm(x, g, out_dtype):
    m, d = x.shape
    bm = _pick(m, (512, 256, 128))
    blocks = [((bm, d), F32), ((1, d), F32), ((bm, d), out_dtype)]
    return pl.pallas_call(
        _rmsnorm_kernel,
        grid=(m // bm,),
        in_specs=[pl.BlockSpec((bm, d), lambda i: (i, 0)),
                  pl.BlockSpec((1, d), lambda i: (0, 0))],
        out_specs=pl.BlockSpec((bm, d), lambda i: (i, 0)),
        out_shape=jax.ShapeDtypeStruct((m, d), out_dtype),
        compiler_params=_cparams(1, blocks, temps=2 * bm * d * 4),
        name="rmsnorm",
    )(x, g.reshape(1, d))


def _inproj_kernel(a_ref, w_ref, wn_ref, o_ref, wsc_ref, *, n_aligned, skip):
    j = pl.program_id(0)
    bn = w_ref.shape[0]
    rc = _pick(bn, (CAST_ROWS, LANES))

    @pl.when(jnp.logical_and(_first_token_tile(), j < n_aligned))
    def _():
        _cast_tile(w_ref, wsc_ref)

    @pl.when(jnp.logical_and(_first_token_tile(), j >= n_aligned))
    def _():
        for r in range(0, bn - rc, rc):
            wsc_ref[r:r + rc, :] = w_ref[r + skip:r + skip + rc, :].astype(BF16)
        wsc_ref[bn - rc:bn - skip, :] = w_ref[bn - rc + skip:bn, :].astype(BF16)
        wsc_ref[bn - skip:bn, :] = wn_ref[...].astype(BF16)

    o_ref[...] = lax.dot_general(a_ref[...], wsc_ref[...], (((1,), (1,)), ((), ())),
                                 preferred_element_type=F32).astype(o_ref.dtype)


def _inproj(u, w_in_t, layer, dt0, heads):
    m, k = u.shape
    d_in = w_in_t.shape[1]
    n = d_in - heads
    bm = _pick(m, (1024, 512, 256, 128))
    bn = next(c for c in (1024, 512, 256, 128) if dt0 % c == 0 and (n - dt0) % c == 0)
    assert heads % (2 * SUBLANES) == 0 and bn % heads == 0 and d_in % heads == 0 and heads < LANES
    n_aligned = dt0 // bn
    blocks = [((bm, k), BF16), ((bn, k), F32), ((heads, k), F32), ((bm, bn), BF16)]
    scratch = [((bn, k), BF16)]
    return pl.pallas_call(
        functools.partial(_inproj_kernel, n_aligned=n_aligned, skip=heads),
        grid=(n // bn, m // bm),
        in_specs=[pl.BlockSpec((bm, k), lambda j, i: (i, 0)),
                  pl.BlockSpec((None, bn, k), lambda j, i: (layer, j, 0)),
                  pl.BlockSpec((None, heads, k), lambda j, i: (layer, (j + 1) * (bn // heads), 0))],
        out_specs=pl.BlockSpec((bm, bn), lambda j, i: (i, j)),
        out_shape=jax.ShapeDtypeStruct((m, n), BF16),
        scratch_shapes=[pltpu.VMEM(s, d) for s, d in scratch],
        compiler_params=_cparams(2, blocks, scratch, temps=bm * bn * 4 + 2 * CAST_ROWS * k * 4),
        name="in_proj",
    )(u, w_in_t, w_in_t)


def _mm_res_kernel(a_ref, w_ref, h_ref, o_ref, wsc_ref):
    @pl.when(_first_token_tile())
    def _():
        _cast_tile(w_ref, wsc_ref)

    o_ref[...] = h_ref[...] + jnp.dot(a_ref[...], wsc_ref[...], preferred_element_type=F32)


def _matmul_residual(a, w, layer, h, name, in_place=True):
    m, k = a.shape
    n = w.shape[2]
    big_k = k > 2048
    bm = _pick(m, (512, 256, 128)) if big_k else _pick(m, (1024, 512, 256, 128))
    bn = _pick(n, (512, 256, 128)) if big_k else _pick(n, (1024, 512, 256, 128))
    blocks = [((bm, k), BF16), ((k, bn), F32), ((bm, bn), F32), ((bm, bn), F32)]
    scratch = [((k, bn), BF16)]
    return pl.pallas_call(
        _mm_res_kernel,
        grid=(n // bn, m // bm),
        in_specs=[pl.BlockSpec((bm, k), lambda j, i: (i, 0)),
                  pl.BlockSpec((None, k, bn), lambda j, i: (layer, 0, j)),
                  pl.BlockSpec((bm, bn), lambda j, i: (i, j))],
        out_specs=pl.BlockSpec((bm, bn), lambda j, i: (i, j)),
        out_shape=jax.ShapeDtypeStruct((m, n), F32),
        scratch_shapes=[pltpu.VMEM(s, d) for s, d in scratch],
        input_output_aliases={2: 0} if in_place else {},
        compiler_params=_cparams(2, blocks, scratch, temps=bm * bn * 4 + 2 * CAST_ROWS * bn * 4),
        name=name,
    )(a, w, h)


def _swiglu_kernel(v_ref, wg_ref, wu_ref, o_ref, wgsc_ref, wusc_ref):
    @pl.when(_first_token_tile())
    def _():
        _cast_tile(wg_ref, wgsc_ref)
        _cast_tile(wu_ref, wusc_ref)

    v = v_ref[...]
    gate = jnp.dot(v, wgsc_ref[...], preferred_element_type=F32)
    up = jnp.dot(v, wusc_ref[...], preferred_element_type=F32)
    o_ref[...] = (gate * jax.nn.sigmoid(gate) * up).astype(o_ref.dtype)


def _swiglu(v, w_gate_up, layer):
    m, k = v.shape
    d_ff = w_gate_up.shape[2] // 2
    bm = _pick(m, (1024, 512, 256, 128))
    bn = _pick(d_ff, (512, 256, 128))
    nb = d_ff // bn
    blocks = [((bm, k), BF16), ((k, bn), F32), ((k, bn), F32), ((bm, bn), BF16)]
    scratch = [((k, bn), BF16), ((k, bn), BF16)]
    return pl.pallas_call(
        _swiglu_kernel,
        grid=(nb, m // bm),
        in_specs=[pl.BlockSpec((bm, k), lambda j, i: (i, 0)),
                  pl.BlockSpec((None, k, bn), lambda j, i: (layer, 0, j)),
                  pl.BlockSpec((None, k, bn), lambda j, i: (layer, 0, j + nb))],
        out_specs=pl.BlockSpec((bm, bn), lambda j, i: (i, j)),
        out_shape=jax.ShapeDtypeStruct((m, d_ff), BF16),
        scratch_shapes=[pltpu.VMEM(s, d) for s, d in scratch],
        compiler_params=_cparams(2, blocks, scratch, temps=3 * bm * bn * 4 + 2 * CAST_ROWS * bn * 4),
        name="swiglu_up",
    )(v, w_gate_up, w_gate_up)


def _merge_kernel(y_ref, ysc_ref, wa_ref, wb_ref, ga_ref, gb_ref, o_ref, wasc_ref, wbsc_ref):
    @pl.when(_first_token_tile())
    def _():
        _cast_tile(wa_ref, wasc_ref)
        _cast_tile(wb_ref, wbsc_ref)

    ya = jnp.dot(y_ref[...], wasc_ref[...], preferred_element_type=F32)
    yb = jnp.dot(ysc_ref[...], wbsc_ref[...], preferred_element_type=F32)
    ga = jax.nn.sigmoid(ga_ref[...].astype(F32))
    gb = jax.nn.sigmoid(gb_ref[...].astype(F32))
    o_ref[...] = (ga * ya + gb * yb).astype(o_ref.dtype)


def _merge(y, ysc, w_a, w_b, layer, proj, off_ga, off_gb):
    m, ka = y.shape
    kb = ysc.shape[1]
    n = w_a.shape[2]
    bm = _pick(m, (512, 256, 128))
    bn = _pick(n, (512, 256, 128))
    ja, jb = off_ga // bn, off_gb // bn
    assert off_ga % bn == 0 and off_gb % bn == 0
    blocks = [((bm, ka), BF16), ((bm, kb), BF16), ((ka, bn), F32), ((kb, bn), F32),
              ((bm, bn), BF16), ((bm, bn), BF16), ((bm, bn), BF16)]
    scratch = [((ka, bn), BF16), ((kb, bn), BF16)]
    return pl.pallas_call(
        _merge_kernel,
        grid=(n // bn, m // bm),
        in_specs=[pl.BlockSpec((bm, ka), lambda j, i: (i, 0)),
                  pl.BlockSpec((bm, kb), lambda j, i: (i, 0)),
                  pl.BlockSpec((None, ka, bn), lambda j, i: (layer, 0, j)),
                  pl.BlockSpec((None, kb, bn), lambda j, i: (layer, 0, j)),
                  pl.BlockSpec((bm, bn), lambda j, i: (i, j + ja)),
                  pl.BlockSpec((bm, bn), lambda j, i: (i, j + jb))],
        out_specs=pl.BlockSpec((bm, bn), lambda j, i: (i, j)),
        out_shape=jax.ShapeDtypeStruct((m, n), BF16),
        scratch_shapes=[pltpu.VMEM(s, d) for s, d in scratch],
        compiler_params=_cparams(2, blocks, scratch, temps=4 * bm * bn * 4 + 2 * CAST_ROWS * bn * 4),
        name="branch_merge",
    )(y, ysc, w_a, w_b, proj, proj)


def _ple_kernel(hn_ref, wg_ref, p_ref, wp_ref, h_ref, o_ref, wgsc_ref, wpsc_ref):
    @pl.when(_first_token_tile())
    def _():
        _cast_tile(wg_ref, wgsc_ref)
        _cast_tile(wp_ref, wpsc_ref)

    pg = jax.nn.sigmoid(jnp.dot(hn_ref[...], wgsc_ref[...], preferred_element_type=F32))
    e = jnp.dot(p_ref[...].astype(BF16), wpsc_ref[...], preferred_element_type=F32)
    o_ref[...] = h_ref[...] + pg * e


def _ple(hn, w_gate, p, w_proj, layer, h):
    m, k = hn.shape
    kp = p.shape[2]
    n = w_gate.shape[2]
    bm = _pick(m, (1024, 512, 256, 128))
    bn = _pick(n, (1024, 512, 256, 128))
    blocks = [((bm, k), BF16), ((k, bn), F32), ((bm, kp), F32), ((kp, bn), F32),
              ((bm, bn), F32), ((bm, bn), F32)]
    scratch = [((k, bn), BF16), ((kp, bn), BF16)]
    return pl.pallas_call(
        _ple_kernel,
        grid=(n // bn, m // bm),
        in_specs=[pl.BlockSpec((bm, k), lambda j, i: (i, 0)),
                  pl.BlockSpec((None, k, bn), lambda j, i: (layer, 0, j)),
                  pl.BlockSpec((None, bm, kp), lambda j, i: (layer, i, 0)),
                  pl.BlockSpec((None, kp, bn), lambda j, i: (layer, 0, j)),
                  pl.BlockSpec((bm, bn), lambda j, i: (i, j))],
        out_specs=pl.BlockSpec((bm, bn), lambda j, i: (i, j)),
        out_shape=jax.ShapeDtypeStruct((m, n), F32),
        scratch_shapes=[pltpu.VMEM(s, d) for s, d in scratch],
        input_output_aliases={4: 0},
        compiler_params=_cparams(2, blocks, scratch, temps=3 * bm * bn * 4 + 2 * CAST_ROWS * bn * 4),
        name="ple",
    )(hn, w_gate, p, w_proj, h)


def _dtprep_kernel(u_ref, w_ref, bias_ref, alog_ref, dtrow_ref, wrow_ref, acsrow_ref, acscol_ref,
                   *, nc, groups, r_heads):
    x = lax.dot_general(u_ref[...], w_ref[...].astype(BF16), (((1,), (1,)), ((), ())),
                        preferred_element_type=F32) + bias_ref[...]
    dt = jnp.maximum(x, 0.0) + jnp.log1p(jnp.exp(-jnp.abs(x)))
    adt = dt * (-jnp.exp(alog_ref[...]))
    row = lax.broadcasted_iota(jnp.int32, (CHUNK, LANES), 0)
    for k in range(nc):
        sl = slice(k * CHUNK, (k + 1) * CHUNK)
        acs = adt[sl]
        sh = 1
        while sh < CHUNK:
            acs = acs + jnp.where(row >= sh, pltpu.roll(acs, sh, 0), 0.0)
            sh *= 2
        acsrow_ref[k] = acs.T
        dtrow_ref[k] = dt[sl].T
        wrow_ref[k] = (dt[sl] * jnp.exp(acs[CHUNK - 1:CHUNK, :] - acs)).T
        for g in range(groups):
            shift = (LANES - g * r_heads) % LANES
            acscol_ref[g, sl, :] = pltpu.roll(acs, shift, 1) if shift else acs


def _dtprep(u, w_in_t, layer, dt0, bias, a_log, groups, r_heads):
    m, k = u.shape
    nc = 4 if m % (4 * CHUNK) == 0 else 1
    ts = nc * CHUNK
    nchunks = m // CHUNK
    rows = ((nc, LANES, LANES), F32)
    blocks = [((ts, k), BF16), ((LANES, k), F32), rows, rows, rows, ((groups, ts, LANES), F32)]
    row_spec = pl.BlockSpec((nc, LANES, LANES), lambda i: (i, 0, 0))
    row_shape = jax.ShapeDtypeStruct((nchunks, LANES, LANES), F32)
    return pl.pallas_call(
        functools.partial(_dtprep_kernel, nc=nc, groups=groups, r_heads=r_heads),
        grid=(m // ts,),
        in_specs=[pl.BlockSpec((ts, k), lambda i: (i, 0)),
                  pl.BlockSpec((None, LANES, k), lambda i: (layer, dt0 // LANES, 0)),
                  pl.BlockSpec((1, LANES), lambda i: (0, 0)),
                  pl.BlockSpec((1, LANES), lambda i: (0, 0))],
        out_specs=[row_spec, row_spec, row_spec,
                   pl.BlockSpec((groups, ts, LANES), lambda i: (0, i, 0))],
        out_shape=[row_shape, row_shape, row_shape,
                   jax.ShapeDtypeStruct((groups, m, LANES), F32)],
        compiler_params=_cparams(1, blocks, temps=8 * ts * LANES * 4 + k * LANES * 2),
        name="dt_prep",
    )(u, w_in_t, bias, a_log)


def _ssd_kernel(z_ref, xs_ref, b_ref, c_ref, xsp_ref, bp_ref, cp_ref,
                dtr_ref, wr_ref, acr_ref, acc_ref, shift_ref,
                cwx_ref, cwb_ref, cwc_ref, cbx_ref, cbb_ref, cbc_ref, dsk_ref, nw_ref,
                o_ref, st_ref, *, nc, r_heads, kconv):
    g = pl.program_id(1)
    c = pl.program_id(2)
    L = CHUNK
    N = SSD_STATE
    gw = r_heads * HEAD_DIM
    npair = gw // LANES

    @pl.when(c == 0)
    def _():
        st_ref[...] = jnp.zeros(st_ref.shape, F32)

    tri = (lax.broadcasted_iota(jnp.int32, (L, L), 0) >= lax.broadcasted_iota(jnp.int32, (L, L), 1))
    lo = lax.broadcasted_iota(jnp.int32, (L, LANES), 1) < HEAD_DIM
    hi = jnp.logical_not(lo)
    lo_row = lo[0:1]

    def conv_silu(ref, prev_ref, k, w_ref, bias_ref):
        if k == 0:
            prev = prev_ref[...]
            prev = jnp.where(c > 0, prev, jnp.zeros_like(prev))
            x2 = jnp.concatenate([prev, ref[0:L, :]], axis=0)
        else:
            x2 = ref[(k - 1) * L:(k + 1) * L, :]
        shifted = jnp.dot(shift_ref[...], x2, preferred_element_type=F32)
        acc = None
        for tap in range(kconv - 1):
            term = shifted[tap * L:(tap + 1) * L, :] * w_ref[tap:tap + 1, :]
            acc = term if acc is None else acc + term
        acc = acc + x2[L:2 * L].astype(F32) * w_ref[kconv - 1:kconv, :]
        acc = acc + bias_ref[...]
        return acc * jax.nn.sigmoid(acc)

    for k in range(nc):
        r0 = k * L
        x = conv_silu(xs_ref, xsp_ref, k, cwx_ref, cbx_ref)
        bm = conv_silu(b_ref, bp_ref, k, cwb_ref, cbb_ref)
        cm = conv_silu(c_ref, cp_ref, k, cwc_ref, cbc_ref)
        cb = lax.dot_general(cm.astype(BF16), bm.astype(BF16), (((1,), (1,)), ((), ())),
                             preferred_element_type=F32)
        bt = bm.T
        acol = acc_ref[0, r0:r0 + L, :]
        cdec = jnp.exp(acol[L - 1:L, :])
        ssq = jnp.zeros((L, 1), F32)
        ys = []
        for j in range(npair):
            cs = slice(j * LANES, (j + 1) * LANES)
            xp = x[:, cs]
            s_prev = st_ref[:, cs]
            s_new = s_prev * jnp.where(lo_row, cdec[:, 2 * j:2 * j + 1], cdec[:, 2 * j + 1:2 * j + 2])
            y = None
            for hh, keep in ((2 * j, lo), (2 * j + 1, hi)):
                head = pl.ds(g * r_heads + hh, 1)
                arow = acr_ref[k, head, :]
                drow = dtr_ref[k, head, :]
                wrow = wr_ref[k, head, :]
                xm = jnp.where(keep, xp, 0.0).astype(BF16)
                sm = jnp.where(keep, s_prev, 0.0).astype(BF16)
                ab = jnp.broadcast_to(acol[:, hh:hh + 1], (L, L))
                dec = jnp.exp(jnp.where(tri, ab - arow, -jnp.inf))
                mh = cb * dec * drow
                ch = cm * jnp.exp(ab)
                lhs = jnp.concatenate([mh, ch], axis=1).astype(BF16)
                rhs = jnp.concatenate([xm, sm], axis=0)
                t = jnp.dot(lhs, rhs, preferred_element_type=F32)
                y = t if y is None else y + t
                bth = (bt * wrow).astype(BF16)
                s_new = s_new + jnp.dot(bth, xm, preferred_element_type=F32)
            st_ref[:, cs] = s_new
            yt = y + dsk_ref[:, cs] * xp
            zt = z_ref[r0:r0 + L, cs].astype(F32)
            yt = yt * (zt * jax.nn.sigmoid(zt))
            ssq = ssq + jnp.sum(yt * yt, axis=-1, keepdims=True)
            ys.append(yt)
        rs = lax.rsqrt(ssq / gw + EPS)
        for j in range(npair):
            cs = slice(j * LANES, (j + 1) * LANES)
            o_ref[r0:r0 + L, cs] = (ys[j] * rs * nw_ref[:, cs]).astype(o_ref.dtype)


def _ssd(proj, dtrow, wrow, acsrow, acscol, conv_w, conv_b, d_skip, norm_w, *, batch, seq, inner, groups):
    m = proj.shape[0]
    gw = inner // groups
    r_heads = gw // HEAD_DIM
    assert gw % LANES == 0 and r_heads % 2 == 0 and SSD_STATE == LANES
    kconv = conv_w.shape[0]
    assert kconv - 1 <= CHUNK
    nc = _pick(seq // CHUNK, (4, 2, 1))
    ts = nc * CHUNK
    nt = seq // ts
    gn = groups * SSD_STATE
    xs_blk, b_blk, c_blk = inner // gw, 2 * inner // SSD_STATE, (2 * inner + gn) // SSD_STATE
    wb_blk, wc_blk = inner // SSD_STATE, (inner + gn) // SSD_STATE
    hp = acsrow.shape[1]
    t_idx = jnp.arange(CHUNK)[None, :, None]
    d_idx = (kconv - 1 - jnp.arange(kconv - 1))[:, None, None]
    s_idx = jnp.arange(2 * CHUNK)[None, None, :]
    shift = (s_idx == CHUNK + t_idx - d_idx).astype(BF16).reshape((kconv - 1) * CHUNK, 2 * CHUNK)
    st_shape = (SSD_STATE, gw)
    blocks = [((ts, gw), BF16), ((ts, gw), BF16), ((ts, SSD_STATE), BF16), ((ts, SSD_STATE), BF16),
              ((CHUNK, gw), BF16), ((CHUNK, SSD_STATE), BF16), ((CHUNK, SSD_STATE), BF16),
              ((nc, hp, LANES), F32), ((nc, hp, LANES), F32), ((nc, hp, LANES), F32),
              ((1, ts, LANES), F32), (shift.shape, BF16),
              ((kconv, gw), F32), ((kconv, SSD_STATE), F32), ((kconv, SSD_STATE), F32),
              ((1, gw), F32), ((1, SSD_STATE), F32), ((1, SSD_STATE), F32), ((1, gw), F32), ((1, gw), F32),
              ((ts, gw), BF16)]
    scratch = [(st_shape, F32)]
    row = lambda b, g, c: b * nt + c
    prev = lambda b, g, c: jnp.maximum(row(b, g, c) * nc - 1, 0)
    return pl.pallas_call(
        functools.partial(_ssd_kernel, nc=nc, r_heads=r_heads, kconv=kconv),
        grid=(batch, groups, nt),
        in_specs=[
            pl.BlockSpec((ts, gw), lambda b, g, c: (row(b, g, c), g)),
            pl.BlockSpec((ts, gw), lambda b, g, c: (row(b, g, c), xs_blk + g)),
            pl.BlockSpec((ts, SSD_STATE), lambda b, g, c: (row(b, g, c), b_blk + g)),
            pl.BlockSpec((ts, SSD_STATE), lambda b, g, c: (row(b, g, c), c_blk + g)),
            pl.BlockSpec((CHUNK, gw), lambda b, g, c: (prev(b, g, c), xs_blk + g)),
            pl.BlockSpec((CHUNK, SSD_STATE), lambda b, g, c: (prev(b, g, c), b_blk + g)),
            pl.BlockSpec((CHUNK, SSD_STATE), lambda b, g, c: (prev(b, g, c), c_blk + g)),
            pl.BlockSpec((nc, hp, LANES), lambda b, g, c: (row(b, g, c), 0, 0)),
            pl.BlockSpec((nc, hp, LANES), lambda b, g, c: (row(b, g, c), 0, 0)),
            pl.BlockSpec((nc, hp, LANES), lambda b, g, c: (row(b, g, c), 0, 0)),
            pl.BlockSpec((1, ts, LANES), lambda b, g, c: (g, row(b, g, c), 0)),
            pl.BlockSpec(shift.shape, lambda b, g, c: (0, 0)),
            pl.BlockSpec((kconv, gw), lambda b, g, c: (0, g)),
            pl.BlockSpec((kconv, SSD_STATE), lambda b, g, c: (0, wb_blk + g)),
            pl.BlockSpec((kconv, SSD_STATE), lambda b, g, c: (0, wc_blk + g)),
            pl.BlockSpec((1, gw), lambda b, g, c: (0, g)),
            pl.BlockSpec((1, SSD_STATE), lambda b, g, c: (0, wb_blk + g)),
            pl.BlockSpec((1, SSD_STATE), lambda b, g, c: (0, wc_blk + g)),
            pl.BlockSpec((1, gw), lambda b, g, c: (0, g)),
            pl.BlockSpec((1, gw), lambda b, g, c: (0, g)),
        ],
        out_specs=pl.BlockSpec((ts, gw), lambda b, g, c: (row(b, g, c), g)),
        out_shape=jax.ShapeDtypeStruct((m, inner), BF16),
        scratch_shapes=[pltpu.VMEM(s, d) for s, d in scratch],
        compiler_params=_cparams(3, blocks, scratch, temps=8 << 20),
        name="ssd",
    )(proj, proj, proj, proj, proj, proj, proj, dtrow, wrow, acsrow, acscol, shift,
      conv_w, conv_w, conv_w, conv_b, conv_b, conv_b, d_skip, norm_w)


def _sconv_kernel(gb_ref, gc_ref, xt_ref, w_ref, o_ref, pe_ref, *, ts, kconv, cw):
    halo = SUBLANES

    @pl.when(pl.program_id(1) == 0)
    def _():
        pe_ref[0:halo, :] = jnp.zeros((halo, pe_ref.shape[1]), F32)

    pe_ref[halo:halo + ts, :] = gb_ref[...].astype(F32) * xt_ref[...].astype(F32)
    width = pe_ref.shape[1]
    for c0 in range(0, width, cw):
        acc = None
        for k in range(kconv):
            r = halo - (kconv - 1) + k
            term = pe_ref[r:r + ts, c0:c0 + cw] * w_ref[k:k + 1, c0:c0 + cw]
            acc = term if acc is None else acc + term
        o_ref[:, c0:c0 + cw] = (gc_ref[:, c0:c0 + cw].astype(F32) * acc).astype(o_ref.dtype)
    pe_ref[0:halo, :] = pe_ref[ts:ts + halo, :]


def _sconv(proj, conv_w, *, batch, seq, width, off_b, off_c, off_x):
    m = proj.shape[0]
    kconv = conv_w.shape[0]
    ts = _pick(seq, (256, 128))
    nt = seq // ts
    assert off_b % width == 0 and off_c % width == 0 and off_x % width == 0 and kconv - 1 <= SUBLANES
    jb, jc, jx = off_b // width, off_c // width, off_x // width
    cw = _pick(width, (512, 256, 128))
    pe_shape = (SUBLANES + ts, width)
    blocks = [((ts, width), BF16)] * 4 + [((kconv, width), F32)]
    scratch = [(pe_shape, F32)]
    return pl.pallas_call(
        functools.partial(_sconv_kernel, ts=ts, kconv=kconv, cw=cw),
        grid=(batch, nt),
        in_specs=[pl.BlockSpec((ts, width), lambda b, c: (b * nt + c, jb)),
                  pl.BlockSpec((ts, width), lambda b, c: (b * nt + c, jc)),
                  pl.BlockSpec((ts, width), lambda b, c: (b * nt + c, jx)),
                  pl.BlockSpec((kconv, width), lambda b, c: (0, 0))],
        out_specs=pl.BlockSpec((ts, width), lambda b, c: (b * nt + c, 0)),
        out_shape=jax.ShapeDtypeStruct((m, width), BF16),
        scratch_shapes=[pltpu.VMEM(s, d) for s, d in scratch],
        compiler_params=_cparams(2, blocks, scratch, temps=4 << 20),
        name="short_conv",
    )(proj, proj, proj, conv_w)


def kernel(x, p, norm_mix, w_in, ssd_conv_w, ssd_conv_b, ssd_dt_bias, ssd_a_log, ssd_d, ssd_norm,
           ssd_out, sc_conv_w, sc_out, w_o, norm_ffn, w_gate_up, w_down, norm_ple, ple_gate,
           ple_proj, norm_final):
    batch, seq, d = x.shape
    depth = w_in.shape[0]
    m = batch * seq
    heads = ssd_a_log.shape[1]
    inner = heads * HEAD_DIM
    groups = SSD_GROUPS
    r_heads = heads // groups
    xbc = ssd_conv_w.shape[2]
    scw = sc_conv_w.shape[2]
    assert xbc == inner + 2 * groups * SSD_STATE and heads <= LANES
    dt0 = inner + xbc
    off_scb = dt0
    off_scc, off_scx = off_scb + scw, off_scb + 2 * scw
    off_ga = off_scb + 3 * scw
    off_gb = off_ga + d
    hpad = LANES - heads
    p2 = p.reshape(depth, m, p.shape[-1])
    w_in_t = jnp.swapaxes(w_in, 1, 2)

    h = x.reshape(m, d)
    for i in range(depth):
        dt_bias = jnp.pad(ssd_dt_bias[i], (0, hpad)).reshape(1, LANES)
        a_log = jnp.pad(ssd_a_log[i], (0, hpad)).reshape(1, LANES)
        d_skip = jnp.repeat(ssd_d[i], HEAD_DIM).reshape(1, inner)

        u = _rmsnorm(h, norm_mix[i], BF16)
        proj = _inproj(u, w_in_t, i, dt0, heads)
        dtrow, wrow, acsrow, acscol = _dtprep(u, w_in_t, i, dt0, dt_bias, a_log, groups, r_heads)
        y = _ssd(proj, dtrow, wrow, acsrow, acscol, ssd_conv_w[i], ssd_conv_b[i].reshape(1, xbc), d_skip,
                 ssd_norm[i].reshape(1, inner), batch=batch, seq=seq, inner=inner, groups=groups)
        ysc = _sconv(proj, sc_conv_w[i], batch=batch, seq=seq, width=scw,
                     off_b=off_scb, off_c=off_scc, off_x=off_scx)
        merged = _merge(y, ysc, ssd_out, sc_out, i, proj, off_ga, off_gb)
        h = _matmul_residual(merged, w_o, i, h, "w_o_residual", in_place=i > 0)
        v = _rmsnorm(h, norm_ffn[i], BF16)
        act = _swiglu(v, w_gate_up, i)
        h = _matmul_residual(act, w_down, i, h, "w_down_residual")
        hn = _rmsnorm(h, norm_ple[i], BF16)
        h = _ple(hn, ple_gate, p2, ple_proj, i, h)
    out = _rmsnorm(h, norm_final, F32)
    return out.reshape(batch, seq, d)
```

```python
import functools

import jax
import jax.numpy as jnp
from jax import lax
from jax.experimental import pallas as pl
from jax.experimental.pallas import tpu as pltpu

F32 = jnp.float32
BF16 = jnp.bfloat16

EPS = 1e-6
HEAD_DIM = 64
SSD_GROUPS = 8
SSD_STATE = 128
CHUNK = 128
LANES = 128
SUBLANES = 8
CAST_ROWS = 256
VMEM_BYTES_V7X = 64 * 1024 * 1024
VMEM_CAP = VMEM_BYTES_V7X - 8 * 1024 * 1024


def _nbytes(shape, dtype):
    n = 1
    for s in shape:
        n *= s
    return n * jnp.dtype(dtype).itemsize


def _cparams(ngrid, blocks, scratch=(), temps=0):
    need = (2 * sum(_nbytes(s, d) for s, d in blocks) + sum(_nbytes(s, d) for s, d in scratch)
            + temps + (4 << 20))
    return pltpu.CompilerParams(dimension_semantics=("arbitrary",) * ngrid,
                                vmem_limit_bytes=int(min(max(need, 16 << 20), VMEM_CAP)))


def _pick(n, candidates):
    for c in candidates:
        if n % c == 0:
            return c
    raise ValueError(f"no block size in {candidates} divides {n}")


def _cast_tile(w_ref, wsc_ref):
    k = w_ref.shape[0]
    rc = _pick(k, (CAST_ROWS, LANES, SUBLANES))
    for r in range(0, k, rc):
        wsc_ref[r:r + rc, :] = w_ref[r:r + rc, :].astype(BF16)


def _first_token_tile():
    return pl.program_id(1) == 0


def _emit_scaled(h_new, g_ref, hg_ref, ssq_ref):
    hg_ref[...] = (h_new * g_ref[...]).astype(BF16)
    ssq_ref[...] = jnp.broadcast_to(jnp.sum(h_new * h_new, axis=-1, keepdims=True), ssq_ref.shape)


def _row_rsqrt(ssq_ref, d):
    s = ssq_ref[0]
    for q in range(1, ssq_ref.shape[0]):
        s = s + ssq_ref[q]
    return lax.rsqrt(s[:, 0:1] / d + EPS)


def _prenorm_kernel(x_ref, g_ref, hg_ref, ssq_ref):
    _emit_scaled(x_ref[...], g_ref, hg_ref, ssq_ref.at[0])


def _prenorm(x, g):
    m, d = x.shape
    bm = _pick(m, (512, 256, 128))
    blocks = [((bm, d), F32), ((1, d), F32), ((bm, d), BF16), ((1, bm, LANES), F32)]
    return pl.pallas_call(
        _prenorm_kernel,
        grid=(m // bm,),
        in_specs=[pl.BlockSpec((bm, d), lambda i: (i, 0)),
                  pl.BlockSpec((1, d), lambda i: (0, 0))],
        out_specs=[pl.BlockSpec((bm, d), lambda i: (i, 0)),
                   pl.BlockSpec((1, bm, LANES), lambda i: (0, i, 0))],
        out_shape=[jax.ShapeDtypeStruct((m, d), BF16), jax.ShapeDtypeStruct((1, m, LANES), F32)],
        compiler_params=_cparams(1, blocks, temps=2 * bm * d * 4),
        name="prenorm",
    )(x, g.reshape(1, d))


def _rmsnorm_kernel(x_ref, g_ref, o_ref):
    x = x_ref[...]
    ms = jnp.mean(x * x, axis=-1, keepdims=True)
    o_ref[...] = (x * lax.rsqrt(ms + EPS) * g_ref[...]).astype(o_ref.dtype)


def _rmsnorm(x, g, out_dtype):
    m, d = x.shape
    bm = _pick(m, (512, 256, 128))
    blocks = [((bm, d), F32), ((1, d), F32), ((bm, d), out_dtype)]
    return pl.pallas_call(
        _rmsnorm_kernel,
        grid=(m // bm,),
        in_specs=[pl.BlockSpec((bm, d), lambda i: (i, 0)),
                  pl.BlockSpec((1, d), lambda i: (0, 0))],
        out_specs=pl.BlockSpec((bm, d), lambda i: (i, 0)),
        out_shape=jax.ShapeDtypeStruct((m, d), out_dtype),
        compiler_params=_cparams(1, blocks, temps=2 * bm * d * 4),
        name="rmsnorm",
    )(x, g.reshape(1, d))


def _inproj_kernel(a_ref, ssq_ref, w_ref, wn_ref, o_ref, wsc_ref, *, n_aligned, skip):
    j = pl.program_id(0)
    bn = w_ref.shape[0]
    rc = _pick(bn, (CAST_ROWS, LANES))

    @pl.when(jnp.logical_and(_first_token_tile(), j < n_aligned))
    def _():
        _cast_tile(w_ref, wsc_ref)

    @pl.when(jnp.logical_and(_first_token_tile(), j >= n_aligned))
    def _():
        for r in range(0, bn - rc, rc):
            wsc_ref[r:r + rc, :] = w_ref[r + skip:r + skip + rc, :].astype(BF16)
        wsc_ref[bn - rc:bn - skip, :] = w_ref[bn - rc + skip:bn, :].astype(BF16)
        wsc_ref[bn - skip:bn, :] = wn_ref[...].astype(BF16)

    acc = lax.dot_general(a_ref[...], wsc_ref[...], (((1,), (1,)), ((), ())), preferred_element_type=F32)
    o_ref[...] = (acc * _row_rsqrt(ssq_ref, a_ref.shape[1])).astype(o_ref.dtype)


def _inproj(u, ssq, w_in_t, layer, dt0, heads):
    m, k = u.shape
    parts = ssq.shape[0]
    d_in = w_in_t.shape[1]
    n = d_in - heads
    bm = _pick(m, (1024, 512, 256, 128))
    bn = next(c for c in (1024, 512, 256, 128) if dt0 % c == 0 and (n - dt0) % c == 0)
    assert heads % (2 * SUBLANES) == 0 and bn % heads == 0 and d_in % heads == 0 and heads < LANES
    n_aligned = dt0 // bn
    blocks = [((bm, k), BF16), ((parts, bm, LANES), F32), ((bn, k), F32), ((heads, k), F32), ((bm, bn), BF16)]
    scratch = [((bn, k), BF16)]
    return pl.pallas_call(
        functools.partial(_inproj_kernel, n_aligned=n_aligned, skip=heads),
        grid=(n // bn, m // bm),
        in_specs=[pl.BlockSpec((bm, k), lambda j, i: (i, 0)),
                  pl.BlockSpec((parts, bm, LANES), lambda j, i: (0, i, 0)),
                  pl.BlockSpec((None, bn, k), lambda j, i: (layer, j, 0)),
                  pl.BlockSpec((None, heads, k), lambda j, i: (layer, (j + 1) * (bn // heads), 0))],
        out_specs=pl.BlockSpec((bm, bn), lambda j, i: (i, j)),
        out_shape=jax.ShapeDtypeStruct((m, n), BF16),
        scratch_shapes=[pltpu.VMEM(s, d) for s, d in scratch],
        compiler_params=_cparams(2, blocks, scratch, temps=bm * bn * 4 + 2 * CAST_ROWS * k * 4),
        name="in_proj",
    )(u, ssq, w_in_t, w_in_t)


def _mm_res_kernel(a_ref, w_ref, h_ref, g_ref, o_ref, hg_ref, ssq_ref, wsc_ref):
    @pl.when(_first_token_tile())
    def _():
        _cast_tile(w_ref, wsc_ref)

    h_new = h_ref[...] + jnp.dot(a_ref[...], wsc_ref[...], preferred_element_type=F32)
    o_ref[...] = h_new
    _emit_scaled(h_new, g_ref, hg_ref, ssq_ref)


def _matmul_residual(a, w, layer, h, g_next, name, in_place=True):
    m, k = a.shape
    n = w.shape[2]
    big_k = k > 2048
    bm = _pick(m, (512, 256, 128)) if big_k else _pick(m, (1024, 512, 256, 128))
    bn = _pick(n, (512, 256, 128))
    blocks = [((bm, k), BF16), ((k, bn), F32), ((bm, bn), F32), ((1, bn), F32),
              ((bm, bn), F32), ((bm, bn), BF16), ((bm, LANES), F32)]
    scratch = [((k, bn), BF16)]
    return pl.pallas_call(
        _mm_res_kernel,
        grid=(n // bn, m // bm),
        in_specs=[pl.BlockSpec((bm, k), lambda j, i: (i, 0)),
                  pl.BlockSpec((None, k, bn), lambda j, i: (layer, 0, j)),
                  pl.BlockSpec((bm, bn), lambda j, i: (i, j)),
                  pl.BlockSpec((1, bn), lambda j, i: (0, j))],
        out_specs=[pl.BlockSpec((bm, bn), lambda j, i: (i, j)),
                   pl.BlockSpec((bm, bn), lambda j, i: (i, j)),
                   pl.BlockSpec((None, bm, LANES), lambda j, i: (j, i, 0))],
        out_shape=[jax.ShapeDtypeStruct((m, n), F32), jax.ShapeDtypeStruct((m, n), BF16),
                   jax.ShapeDtypeStruct((n // bn, m, LANES), F32)],
        scratch_shapes=[pltpu.VMEM(s, d) for s, d in scratch],
        input_output_aliases={2: 0} if in_place else {},
        compiler_params=_cparams(2, blocks, scratch, temps=2 * bm * bn * 4 + 2 * CAST_ROWS * bn * 4),
        name=name,
    )(a, w, h, g_next.reshape(1, n))


def _swiglu_kernel(v_ref, ssq_ref, wg_ref, wu_ref, o_ref, wgsc_ref, wusc_ref):
    @pl.when(_first_token_tile())
    def _():
        _cast_tile(wg_ref, wgsc_ref)
        _cast_tile(wu_ref, wusc_ref)

    v = v_ref[...]
    rs = _row_rsqrt(ssq_ref, v.shape[1])
    gate = jnp.dot(v, wgsc_ref[...], preferred_element_type=F32) * rs
    up = jnp.dot(v, wusc_ref[...], preferred_element_type=F32) * rs
    o_ref[...] = (gate * jax.nn.sigmoid(gate) * up).astype(o_ref.dtype)


def _swiglu(v, ssq, w_gate_up, layer):
    m, k = v.shape
    parts = ssq.shape[0]
    d_ff = w_gate_up.shape[2] // 2
    bm = _pick(m, (1024, 512, 256, 128))
    bn = _pick(d_ff, (512, 256, 128))
    nb = d_ff // bn
    blocks = [((bm, k), BF16), ((parts, bm, LANES), F32), ((k, bn), F32), ((k, bn), F32), ((bm, bn), BF16)]
    scratch = [((k, bn), BF16), ((k, bn), BF16)]
    return pl.pallas_call(
        _swiglu_kernel,
        grid=(nb, m // bm),
        in_specs=[pl.BlockSpec((bm, k), lambda j, i: (i, 0)),
                  pl.BlockSpec((parts, bm, LANES), lambda j, i: (0, i, 0)),
                  pl.BlockSpec((None, k, bn), lambda j, i: (layer, 0, j)),
                  pl.BlockSpec((None, k, bn), lambda j, i: (layer, 0, j + nb))],
        out_specs=pl.BlockSpec((bm, bn), lambda j, i: (i, j)),
        out_shape=jax.ShapeDtypeStruct((m, d_ff), BF16),
        scratch_shapes=[pltpu.VMEM(s, d) for s, d in scratch],
        compiler_params=_cparams(2, blocks, scratch, temps=3 * bm * bn * 4 + 2 * CAST_ROWS * bn * 4),
        name="swiglu_up",
    )(v, ssq, w_gate_up, w_gate_up)


def _merge_kernel(y_ref, ysc_ref, wa_ref, wb_ref, ga_ref, gb_ref, o_ref, wasc_ref, wbsc_ref):
    @pl.when(_first_token_tile())
    def _():
        _cast_tile(wa_ref, wasc_ref)
        _cast_tile(wb_ref, wbsc_ref)

    ya = jnp.dot(y_ref[...], wasc_ref[...], preferred_element_type=F32)
    yb = jnp.dot(ysc_ref[...], wbsc_ref[...], preferred_element_type=F32)
    ga = jax.nn.sigmoid(ga_ref[...].astype(F32))
    gb = jax.nn.sigmoid(gb_ref[...].astype(F32))
    o_ref[...] = (ga * ya + gb * yb).astype(o_ref.dtype)


def _merge(y, ysc, w_a, w_b, layer, proj, off_ga, off_gb):
    m, ka = y.shape
    kb = ysc.shape[1]
    n = w_a.shape[2]
    bm = _pick(m, (512, 256, 128))
    bn = _pick(n, (512, 256, 128))
    ja, jb = off_ga // bn, off_gb // bn
    assert off_ga % bn == 0 and off_gb % bn == 0
    blocks = [((bm, ka), BF16), ((bm, kb), BF16), ((ka, bn), F32), ((kb, bn), F32),
              ((bm, bn), BF16), ((bm, bn), BF16), ((bm, bn), BF16)]
    scratch = [((ka, bn), BF16), ((kb, bn), BF16)]
    return pl.pallas_call(
        _merge_kernel,
        grid=(n // bn, m // bm),
        in_specs=[pl.BlockSpec((bm, ka), lambda j, i: (i, 0)),
                  pl.BlockSpec((bm, kb), lambda j, i: (i, 0)),
                  pl.BlockSpec((None, ka, bn), lambda j, i: (layer, 0, j)),
                  pl.BlockSpec((None, kb, bn), lambda j, i: (layer, 0, j)),
                  pl.BlockSpec((bm, bn), lambda j, i: (i, j + ja)),
                  pl.BlockSpec((bm, bn), lambda j, i: (i, j + jb))],
        out_specs=pl.BlockSpec((bm, bn), lambda j, i: (i, j)),
        out_shape=jax.ShapeDtypeStruct((m, n), BF16),
        scratch_shapes=[pltpu.VMEM(s, d) for s, d in scratch],
        compiler_params=_cparams(2, blocks, scratch, temps=4 * bm * bn * 4 + 2 * CAST_ROWS * bn * 4),
        name="branch_merge",
    )(y, ysc, w_a, w_b, proj, proj)


def _ple_kernel(hn_ref, ssq_ref, wg_ref, p_ref, wp_ref, h_ref, g_ref, o_ref, hg_ref, ssqo_ref,
                wgsc_ref, wpsc_ref):
    @pl.when(_first_token_tile())
    def _():
        _cast_tile(wg_ref, wgsc_ref)
        _cast_tile(wp_ref, wpsc_ref)

    rs = _row_rsqrt(ssq_ref, hn_ref.shape[1])
    pg = jax.nn.sigmoid(jnp.dot(hn_ref[...], wgsc_ref[...], preferred_element_type=F32) * rs)
    e = jnp.dot(p_ref[...].astype(BF16), wpsc_ref[...], preferred_element_type=F32)
    h_new = h_ref[...] + pg * e
    o_ref[...] = h_new
    _emit_scaled(h_new, g_ref, hg_ref, ssqo_ref)


def _ple(hn, ssq, w_gate, p, w_proj, layer, h, g_next):
    m, k = hn.shape
    parts = ssq.shape[0]
    kp = p.shape[2]
    n = w_gate.shape[2]
    bm = _pick(m, (1024, 512, 256, 128))
    bn = _pick(n, (512, 256, 128))
    blocks = [((bm, k), BF16), ((parts, bm, LANES), F32), ((k, bn), F32), ((bm, kp), F32), ((kp, bn), F32),
              ((bm, bn), F32), ((1, bn), F32), ((bm, bn), F32), ((bm, bn), BF16), ((bm, LANES), F32)]
    scratch = [((k, bn), BF16), ((kp, bn), BF16)]
    return pl.pallas_call(
        _ple_kernel,
        grid=(n // bn, m // bm),
        in_specs=[pl.BlockSpec((bm, k), lambda j, i: (i, 0)),
                  pl.BlockSpec((parts, bm, LANES), lambda j, i: (0, i, 0)),
                  pl.BlockSpec((None, k, bn), lambda j, i: (layer, 0, j)),
                  pl.BlockSpec((None, bm, kp), lambda j, i: (layer, i, 0)),
                  pl.BlockSpec((None, kp, bn), lambda j, i: (layer, 0, j)),
                  pl.BlockSpec((bm, bn), lambda j, i: (i, j)),
                  pl.BlockSpec((1, bn), lambda j, i: (0, j))],
        out_specs=[pl.BlockSpec((bm, bn), lambda j, i: (i, j)),
                   pl.BlockSpec((bm, bn), lambda j, i: (i, j)),
                   pl.BlockSpec((None, bm, LANES), lambda j, i: (j, i, 0))],
        out_shape=[jax.ShapeDtypeStruct((m, n), F32), jax.ShapeDtypeStruct((m, n), BF16),
                   jax.ShapeDtypeStruct((n // bn, m, LANES), F32)],
        scratch_shapes=[pltpu.VMEM(s, d) for s, d in scratch],
        input_output_aliases={5: 0},
        compiler_params=_cparams(2, blocks, scratch, temps=4 * bm * bn * 4 + 2 * CAST_ROWS * bn * 4),
        name="ple",
    )(hn, ssq, w_gate, p, w_proj, h, g_next.reshape(1, n))


def _dtprep_kernel(u_ref, ssq_ref, w_ref, bias_ref, alog_ref, dtrow_ref, wrow_ref, acsrow_ref, acscol_ref,
                   *, nc, groups, r_heads):
    x = lax.dot_general(u_ref[...], w_ref[...].astype(BF16), (((1,), (1,)), ((), ())),
                        preferred_element_type=F32)
    x = x * _row_rsqrt(ssq_ref, u_ref.shape[1]) + bias_ref[...]
    dt = jnp.maximum(x, 0.0) + jnp.log1p(jnp.exp(-jnp.abs(x)))
    adt = dt * (-jnp.exp(alog_ref[...]))
    row = lax.broadcasted_iota(jnp.int32, (CHUNK, LANES), 0)
    for k in range(nc):
        sl = slice(k * CHUNK, (k + 1) * CHUNK)
        acs = adt[sl]
        sh = 1
        while sh < CHUNK:
            acs = acs + jnp.where(row >= sh, pltpu.roll(acs, sh, 0), 0.0)
            sh *= 2
        acsrow_ref[k] = acs.T
        dtrow_ref[k] = dt[sl].T
        wrow_ref[k] = (dt[sl] * jnp.exp(acs[CHUNK - 1:CHUNK, :] - acs)).T
        for g in range(groups):
            shift = (LANES - g * r_heads) % LANES
            acscol_ref[g, sl, :] = pltpu.roll(acs, shift, 1) if shift else acs


def _dtprep(u, ssq, w_in_t, layer, dt0, bias, a_log, groups, r_heads):
    m, k = u.shape
    parts = ssq.shape[0]
    nc = 4 if m % (4 * CHUNK) == 0 else 1
    ts = nc * CHUNK
    nchunks = m // CHUNK
    rows = ((nc, LANES, LANES), F32)
    blocks = [((ts, k), BF16), ((parts, ts, LANES), F32), ((LANES, k), F32), rows, rows, rows,
              ((groups, ts, LANES), F32)]
    row_spec = pl.BlockSpec((nc, LANES, LANES), lambda i: (i, 0, 0))
    row_shape = jax.ShapeDtypeStruct((nchunks, LANES, LANES), F32)
    return pl.pallas_call(
        functools.partial(_dtprep_kernel, nc=nc, groups=groups, r_heads=r_heads),
        grid=(m // ts,),
        in_specs=[pl.BlockSpec((ts, k), lambda i: (i, 0)),
                  pl.BlockSpec((parts, ts, LANES), lambda i: (0, i, 0)),
                  pl.BlockSpec((None, LANES, k), lambda i: (layer, dt0 // LANES, 0)),
                  pl.BlockSpec((1, LANES), lambda i: (0, 0)),
                  pl.BlockSpec((1, LANES), lambda i: (0, 0))],
        out_specs=[row_spec, row_spec, row_spec,
                   pl.BlockSpec((groups, ts, LANES), lambda i: (0, i, 0))],
        out_shape=[row_shape, row_shape, row_shape,
                   jax.ShapeDtypeStruct((groups, m, LANES), F32)],
        compiler_params=_cparams(1, blocks, temps=8 * ts * LANES * 4 + k * LANES * 2),
        name="dt_prep",
    )(u, ssq, w_in_t, bias, a_log)


def _ssd_kernel(z_ref, xs_ref, b_ref, c_ref, xsp_ref, bp_ref, cp_ref,
                dtr_ref, wr_ref, acr_ref, acc_ref, shift_ref,
                cwx_ref, cwb_ref, cwc_ref, cbx_ref, cbb_ref, cbc_ref, dsk_ref, nw_ref,
                o_ref, st_ref, *, nc, r_heads, kconv):
    g = pl.program_id(1)
    c = pl.program_id(2)
    L = CHUNK
    N = SSD_STATE
    gw = r_heads * HEAD_DIM
    npair = gw // LANES

    @pl.when(c == 0)
    def _():
        st_ref[...] = jnp.zeros(st_ref.shape, F32)

    tri = (lax.broadcasted_iota(jnp.int32, (L, L), 0) >= lax.broadcasted_iota(jnp.int32, (L, L), 1))
    lo = lax.broadcasted_iota(jnp.int32, (L, LANES), 1) < HEAD_DIM
    hi = jnp.logical_not(lo)
    lo_row = lo[0:1]

    def conv_silu(ref, prev_ref, k, w_ref, bias_ref):
        if k == 0:
            prev = prev_ref[...]
            prev = jnp.where(c > 0, prev, jnp.zeros_like(prev))
            x2 = jnp.concatenate([prev, ref[0:L, :]], axis=0)
        else:
            x2 = ref[(k - 1) * L:(k + 1) * L, :]
        shifted = jnp.dot(shift_ref[...], x2, preferred_element_type=F32)
        acc = None
        for tap in range(kconv - 1):
            term = shifted[tap * L:(tap + 1) * L, :] * w_ref[tap:tap + 1, :]
            acc = term if acc is None else acc + term
        acc = acc + x2[L:2 * L].astype(F32) * w_ref[kconv - 1:kconv, :]
        acc = acc + bias_ref[...]
        return acc * jax.nn.sigmoid(acc)

    for k in range(nc):
        r0 = k * L
        x = conv_silu(xs_ref, xsp_ref, k, cwx_ref, cbx_ref)
        bm = conv_silu(b_ref, bp_ref, k, cwb_ref, cbb_ref)
        cm = conv_silu(c_ref, cp_ref, k, cwc_ref, cbc_ref)
        cb = lax.dot_general(cm.astype(BF16), bm.astype(BF16), (((1,), (1,)), ((), ())),
                             preferred_element_type=F32)
        bt = bm.T
        acol = acc_ref[0, r0:r0 + L, :]
        cdec = jnp.exp(acol[L - 1:L, :])
        ssq = jnp.zeros((L, 1), F32)
        ys = []
        for j in range(npair):
            cs = slice(j * LANES, (j + 1) * LANES)
            xp = x[:, cs]
            s_prev = st_ref[:, cs]
            s_new = s_prev * jnp.where(lo_row, cdec[:, 2 * j:2 * j + 1], cdec[:, 2 * j + 1:2 * j + 2])
            y = None
            for hh, keep in ((2 * j, lo), (2 * j + 1, hi)):
                head = pl.ds(g * r_heads + hh, 1)
                arow = acr_ref[k, head, :]
                drow = dtr_ref[k, head, :]
                wrow = wr_ref[k, head, :]
                xm = jnp.where(keep, xp, 0.0).astype(BF16)
                sm = jnp.where(keep, s_prev, 0.0).astype(BF16)
                ab = jnp.broadcast_to(acol[:, hh:hh + 1], (L, L))
                dec = jnp.exp(jnp.where(tri, ab - arow, -jnp.inf))
                mh = cb * dec * drow
                ch = cm * jnp.exp(ab)
                lhs = jnp.concatenate([mh, ch], axis=1).astype(BF16)
                rhs = jnp.concatenate([xm, sm], axis=0)
                t = jnp.dot(lhs, rhs, preferred_element_type=F32)
                y = t if y is None else y + t
                bth = (bt * wrow).astype(BF16)
                s_new = s_new + jnp.dot(bth, xm, preferred_element_type=F32)
            st_ref[:, cs] = s_new
            yt = y + dsk_ref[:, cs] * xp
            zt = z_ref[r0:r0 + L, cs].astype(F32)
            yt = yt * (zt * jax.nn.sigmoid(zt))
            ssq = ssq + jnp.sum(yt * yt, axis=-1, keepdims=True)
            ys.append(yt)
        rs = lax.rsqrt(ssq / gw + EPS)
        for j in range(npair):
            cs = slice(j * LANES, (j + 1) * LANES)
            o_ref[r0:r0 + L, cs] = (ys[j] * rs * nw_ref[:, cs]).astype(o_ref.dtype)


def _ssd(proj, dtrow, wrow, acsrow, acscol, conv_w, conv_b, d_skip, norm_w, *, batch, seq, inner, groups):
    m = proj.shape[0]
    gw = inner // groups
    r_heads = gw // HEAD_DIM
    assert gw % LANES == 0 and r_heads % 2 == 0 and SSD_STATE == LANES
    kconv = conv_w.shape[0]
    assert kconv - 1 <= CHUNK
    nc = _pick(seq // CHUNK, (4, 2, 1))
    ts = nc * CHUNK
    nt = seq // ts
    gn = groups * SSD_STATE
    xs_blk, b_blk, c_blk = inner // gw, 2 * inner // SSD_STATE, (2 * inner + gn) // SSD_STATE
    wb_blk, wc_blk = inner // SSD_STATE, (inner + gn) // SSD_STATE
    hp = acsrow.shape[1]
    t_idx = jnp.arange(CHUNK)[None, :, None]
    d_idx = (kconv - 1 - jnp.arange(kconv - 1))[:, None, None]
    s_idx = jnp.arange(2 * CHUNK)[None, None, :]
    shift = (s_idx == CHUNK + t_idx - d_idx).astype(BF16).reshape((kconv - 1) * CHUNK, 2 * CHUNK)
    st_shape = (SSD_STATE, gw)
    blocks = [((ts, gw), BF16), ((ts, gw), BF16), ((ts, SSD_STATE), BF16), ((ts, SSD_STATE), BF16),
              ((CHUNK, gw), BF16), ((CHUNK, SSD_STATE), BF16), ((CHUNK, SSD_STATE), BF16),
              ((nc, hp, LANES), F32), ((nc, hp, LANES), F32), ((nc, hp, LANES), F32),
              ((1, ts, LANES), F32), (shift.shape, BF16),
              ((kconv, gw), F32), ((kconv, SSD_STATE), F32), ((kconv, SSD_STATE), F32),
              ((1, gw), F32), ((1, SSD_STATE), F32), ((1, SSD_STATE), F32), ((1, gw), F32), ((1, gw), F32),
              ((ts, gw), BF16)]
    scratch = [(st_shape, F32)]
    row = lambda b, g, c: b * nt + c
    prev = lambda b, g, c: jnp.maximum(row(b, g, c) * nc - 1, 0)
    return pl.pallas_call(
        functools.partial(_ssd_kernel, nc=nc, r_heads=r_heads, kconv=kconv),
        grid=(batch, groups, nt),
        in_specs=[
            pl.BlockSpec((ts, gw), lambda b, g, c: (row(b, g, c), g)),
            pl.BlockSpec((ts, gw), lambda b, g, c: (row(b, g, c), xs_blk + g)),
            pl.BlockSpec((ts, SSD_STATE), lambda b, g, c: (row(b, g, c), b_blk + g)),
            pl.BlockSpec((ts, SSD_STATE), lambda b, g, c: (row(b, g, c), c_blk + g)),
            pl.BlockSpec((CHUNK, gw), lambda b, g, c: (prev(b, g, c), xs_blk + g)),
            pl.BlockSpec((CHUNK, SSD_STATE), lambda b, g, c: (prev(b, g, c), b_blk + g)),
            pl.BlockSpec((CHUNK, SSD_STATE), lambda b, g, c: (prev(b, g, c), c_blk + g)),
            pl.BlockSpec((nc, hp, LANES), lambda b, g, c: (row(b, g, c), 0, 0)),
            pl.BlockSpec((nc, hp, LANES), lambda b, g, c: (row(b, g, c), 0, 0)),
            pl.BlockSpec((nc, hp, LANES), lambda b, g, c: (row(b, g, c), 0, 0)),
            pl.BlockSpec((1, ts, LANES), lambda b, g, c: (g, row(b, g, c), 0)),
            pl.BlockSpec(shift.shape, lambda b, g, c: (0, 0)),
            pl.BlockSpec((kconv, gw), lambda b, g, c: (0, g)),
            pl.BlockSpec((kconv, SSD_STATE), lambda b, g, c: (0, wb_blk + g)),
            pl.BlockSpec((kconv, SSD_STATE), lambda b, g, c: (0, wc_blk + g)),
            pl.BlockSpec((1, gw), lambda b, g, c: (0, g)),
            pl.BlockSpec((1, SSD_STATE), lambda b, g, c: (0, wb_blk + g)),
            pl.BlockSpec((1, SSD_STATE), lambda b, g, c: (0, wc_blk + g)),
            pl.BlockSpec((1, gw), lambda b, g, c: (0, g)),
            pl.BlockSpec((1, gw), lambda b, g, c: (0, g)),
        ],
        out_specs=pl.BlockSpec((ts, gw), lambda b, g, c: (row(b, g, c), g)),
        out_shape=jax.ShapeDtypeStruct((m, inner), BF16),
        scratch_shapes=[pltpu.VMEM(s, d) for s, d in scratch],
        compiler_params=_cparams(3, blocks, scratch, temps=8 << 20),
        name="ssd",
    )(proj, proj, proj, proj, proj, proj, proj, dtrow, wrow, acsrow, acscol, shift,
      conv_w, conv_w, conv_w, conv_b, conv_b, conv_b, d_skip, norm_w)


def _sconv_kernel(gb_ref, gc_ref, xt_ref, w_ref, o_ref, pe_ref, *, ts, kconv, cw):
    halo = SUBLANES

    @pl.when(pl.program_id(1) == 0)
    def _():
        pe_ref[0:halo, :] = jnp.zeros((halo, pe_ref.shape[1]), F32)

    pe_ref[halo:halo + ts, :] = gb_ref[...].astype(F32) * xt_ref[...].astype(F32)
    width = pe_ref.shape[1]
    for c0 in range(0, width, cw):
        acc = None
        for k in range(kconv):
            r = halo - (kconv - 1) + k
            term = pe_ref[r:r + ts, c0:c0 + cw] * w_ref[k:k + 1, c0:c0 + cw]
            acc = term if acc is None else acc + term
        o_ref[:, c0:c0 + cw] = (gc_ref[:, c0:c0 + cw].astype(F32) * acc).astype(o_ref.dtype)
    pe_ref[0:halo, :] = pe_ref[ts:ts + halo, :]


def _sconv(proj, conv_w, *, batch, seq, width, off_b, off_c, off_x):
    m = proj.shape[0]
    kconv = conv_w.shape[0]
    ts = _pick(seq, (256, 128))
    nt = seq // ts
    assert off_b % width == 0 and off_c % width == 0 and off_x % width == 0 and kconv - 1 <= SUBLANES
    jb, jc, jx = off_b // width, off_c // width, off_x // width
    cw = _pick(width, (512, 256, 128))
    pe_shape = (SUBLANES + ts, width)
    blocks = [((ts, width), BF16)] * 4 + [((kconv, width), F32)]
    scratch = [(pe_shape, F32)]
    return pl.pallas_call(
        functools.partial(_sconv_kernel, ts=ts, kconv=kconv, cw=cw),
        grid=(batch, nt),
        in_specs=[pl.BlockSpec((ts, width), lambda b, c: (b * nt + c, jb)),
                  pl.BlockSpec((ts, width), lambda b, c: (b * nt + c, jc)),
                  pl.BlockSpec((ts, width), lambda b, c: (b * nt + c, jx)),
                  pl.BlockSpec((kconv, width), lambda b, c: (0, 0))],
        out_specs=pl.BlockSpec((ts, width), lambda b, c: (b * nt + c, 0)),
        out_shape=jax.ShapeDtypeStruct((m, width), BF16),
        scratch_shapes=[pltpu.VMEM(s, d) for s, d in scratch],
        compiler_params=_cparams(2, blocks, scratch, temps=4 << 20),
        name="short_conv",
    )(proj, proj, proj, conv_w)


def kernel(x, p, norm_mix, w_in, ssd_conv_w, ssd_conv_b, ssd_dt_bias, ssd_a_log, ssd_d, ssd_norm,
           ssd_out, sc_conv_w, sc_out, w_o, norm_ffn, w_gate_up, w_down, norm_ple, ple_gate,
           ple_proj, norm_final):
    batch, seq, d = x.shape
    depth = w_in.shape[0]
    m = batch * seq
    heads = ssd_a_log.shape[1]
    inner = heads * HEAD_DIM
    groups = SSD_GROUPS
    r_heads = heads // groups
    xbc = ssd_conv_w.shape[2]
    scw = sc_conv_w.shape[2]
    assert xbc == inner + 2 * groups * SSD_STATE and heads <= LANES
    dt0 = inner + xbc
    off_scb = dt0
    off_scc, off_scx = off_scb + scw, off_scb + 2 * scw
    off_ga = off_scb + 3 * scw
    off_gb = off_ga + d
    hpad = LANES - heads
    p2 = p.reshape(depth, m, p.shape[-1])
    w_in_t = jnp.swapaxes(w_in, 1, 2)

    h = x.reshape(m, d)
    u, u_ssq = _prenorm(h, norm_mix[0])
    for i in range(depth):
        dt_bias = jnp.pad(ssd_dt_bias[i], (0, hpad)).reshape(1, LANES)
        a_log = jnp.pad(ssd_a_log[i], (0, hpad)).reshape(1, LANES)
        d_skip = jnp.repeat(ssd_d[i], HEAD_DIM).reshape(1, inner)

        proj = _inproj(u, u_ssq, w_in_t, i, dt0, heads)
        dtrow, wrow, acsrow, acscol = _dtprep(u, u_ssq, w_in_t, i, dt0, dt_bias, a_log, groups, r_heads)
        y = _ssd(proj, dtrow, wrow, acsrow, acscol, ssd_conv_w[i], ssd_conv_b[i].reshape(1, xbc), d_skip,
                 ssd_norm[i].reshape(1, inner), batch=batch, seq=seq, inner=inner, groups=groups)
        ysc = _sconv(proj, sc_conv_w[i], batch=batch, seq=seq, width=scw,
                     off_b=off_scb, off_c=off_scc, off_x=off_scx)
        merged = _merge(y, ysc, ssd_out, sc_out, i, proj, off_ga, off_gb)
        h, v, v_ssq = _matmul_residual(merged, w_o, i, h, norm_ffn[i], "w_o_residual", in_place=i > 0)
        act = _swiglu(v, v_ssq, w_gate_up, i)
        h, hn, hn_ssq = _matmul_residual(act, w_down, i, h, norm_ple[i], "w_down_residual")
        g_next = norm_mix[i + 1] if i + 1 < depth else norm_final
        h, u, u_ssq = _ple(hn, hn_ssq, ple_gate, p2, ple_proj, i, h, g_next)
    out = _rmsnorm(h, norm_final, F32)
    return out.reshape(batch, seq, d)
```

```python
import functools

import jax
import jax.numpy as jnp
from jax import lax
from jax.experimental import pallas as pl
from jax.experimental.pallas import tpu as pltpu

F32 = jnp.float32
BF16 = jnp.bfloat16

EPS = 1e-6
HEAD_DIM = 64
SSD_GROUPS = 8
SSD_STATE = 128
CHUNK = 128
LANES = 128
SUBLANES = 8
CAST_ROWS = 256
VMEM_BYTES_V7X = 64 * 1024 * 1024
VMEM_CAP = VMEM_BYTES_V7X - 8 * 1024 * 1024


def _nbytes(shape, dtype):
    n = 1
    for s in shape:
        n *= s
    return n * jnp.dtype(dtype).itemsize


def _cparams(ngrid, blocks, scratch=(), temps=0):
    need = (2 * sum(_nbytes(s, d) for s, d in blocks) + sum(_nbytes(s, d) for s, d in scratch)
            + temps + (4 << 20))
    return pltpu.CompilerParams(dimension_semantics=("arbitrary",) * ngrid,
                                vmem_limit_bytes=int(min(max(need, 16 << 20), VMEM_CAP)))


def _pick(n, candidates):
    for c in candidates:
        if n % c == 0:
            return c
    raise ValueError(f"no block size in {candidates} divides {n}")


def _cast_tile(w_ref, wsc_ref):
    k = w_ref.shape[0]
    rc = _pick(k, (CAST_ROWS, LANES, SUBLANES))
    for r in range(0, k, rc):
        wsc_ref[r:r + rc, :] = w_ref[r:r + rc, :].astype(BF16)


def _first_token_tile():
    return pl.program_id(1) == 0


def _emit_scaled(h_new, g_ref, hg_ref, ssq_ref):
    hg_ref[...] = (h_new * g_ref[...]).astype(BF16)
    col = jnp.sum(h_new * h_new, axis=-1, keepdims=True)
    ssq_ref[...] = jnp.transpose(jnp.broadcast_to(col, (col.shape[0], LANES)))[0:1, :]


def _row_rsqrt(ssq_ref, d):
    s = ssq_ref[0]
    for q in range(1, ssq_ref.shape[0]):
        s = s + ssq_ref[q]
    row = lax.rsqrt(s / d + EPS)
    return jnp.transpose(jnp.broadcast_to(row, (LANES, row.shape[1])))[:, 0:1]


def _prenorm_kernel(x_ref, g_ref, hg_ref, ssq_ref):
    _emit_scaled(x_ref[...], g_ref, hg_ref, ssq_ref.at[0])


def _prenorm(x, g):
    m, d = x.shape
    bm = _pick(m, (512, 256, 128))
    blocks = [((bm, d), F32), ((1, d), F32), ((bm, d), BF16), ((1, 1, bm), F32)]
    return pl.pallas_call(
        _prenorm_kernel,
        grid=(m // bm,),
        in_specs=[pl.BlockSpec((bm, d), lambda i: (i, 0)),
                  pl.BlockSpec((1, d), lambda i: (0, 0))],
        out_specs=[pl.BlockSpec((bm, d), lambda i: (i, 0)),
                   pl.BlockSpec((1, 1, bm), lambda i: (0, 0, i))],
        out_shape=[jax.ShapeDtypeStruct((m, d), BF16), jax.ShapeDtypeStruct((1, 1, m), F32)],
        compiler_params=_cparams(1, blocks, temps=2 * bm * d * 4),
        name="prenorm",
    )(x, g.reshape(1, d))


def _rmsnorm_kernel(x_ref, g_ref, o_ref):
    x = x_ref[...]
    ms = jnp.mean(x * x, axis=-1, keepdims=True)
    o_ref[...] = (x * lax.rsqrt(ms + EPS) * g_ref[...]).astype(o_ref.dtype)


def _rmsnorm(x, g, out_dtype):
    m, d = x.shape
    bm = _pick(m, (512, 256, 128))
    blocks = [((bm, d), F32), ((1, d), F32), ((bm, d), out_dtype)]
    return pl.pallas_call(
        _rmsnorm_kernel,
        grid=(m // bm,),
        in_specs=[pl.BlockSpec((bm, d), lambda i: (i, 0)),
                  pl.BlockSpec((1, d), lambda i: (0, 0))],
        out_specs=pl.BlockSpec((bm, d), lambda i: (i, 0)),
        out_shape=jax.ShapeDtypeStruct((m, d), out_dtype),
        compiler_params=_cparams(1, blocks, temps=2 * bm * d * 4),
        name="rmsnorm",
    )(x, g.reshape(1, d))


def _inproj_kernel(a_ref, ssq_ref, w_ref, wn_ref, o_ref, wsc_ref, *, n_aligned, skip):
    j = pl.program_id(0)
    bn = w_ref.shape[0]
    rc = _pick(bn, (CAST_ROWS, LANES))

    @pl.when(jnp.logical_and(_first_token_tile(), j < n_aligned))
    def _():
        _cast_tile(w_ref, wsc_ref)

    @pl.when(jnp.logical_and(_first_token_tile(), j >= n_aligned))
    def _():
        for r in range(0, bn - rc, rc):
            wsc_ref[r:r + rc, :] = w_ref[r + skip:r + skip + rc, :].astype(BF16)
        wsc_ref[bn - rc:bn - skip, :] = w_ref[bn - rc + skip:bn, :].astype(BF16)
        wsc_ref[bn - skip:bn, :] = wn_ref[...].astype(BF16)

    acc = lax.dot_general(a_ref[...], wsc_ref[...], (((1,), (1,)), ((), ())), preferred_element_type=F32)
    o_ref[...] = (acc * _row_rsqrt(ssq_ref, a_ref.shape[1])).astype(o_ref.dtype)


def _inproj(u, ssq, w_in_t, layer, dt0, heads):
    m, k = u.shape
    parts = ssq.shape[0]
    d_in = w_in_t.shape[1]
    n = d_in - heads
    bm = _pick(m, (1024, 512, 256, 128))
    bn = next(c for c in (1024, 512, 256, 128) if dt0 % c == 0 and (n - dt0) % c == 0)
    assert heads % (2 * SUBLANES) == 0 and bn % heads == 0 and d_in % heads == 0 and heads < LANES
    n_aligned = dt0 // bn
    blocks = [((bm, k), BF16), ((parts, 1, bm), F32), ((bn, k), F32), ((heads, k), F32), ((bm, bn), BF16)]
    scratch = [((bn, k), BF16)]
    return pl.pallas_call(
        functools.partial(_inproj_kernel, n_aligned=n_aligned, skip=heads),
        grid=(n // bn, m // bm),
        in_specs=[pl.BlockSpec((bm, k), lambda j, i: (i, 0)),
                  pl.BlockSpec((parts, 1, bm), lambda j, i: (0, 0, i)),
                  pl.BlockSpec((None, bn, k), lambda j, i: (layer, j, 0)),
                  pl.BlockSpec((None, heads, k), lambda j, i: (layer, (j + 1) * (bn // heads), 0))],
        out_specs=pl.BlockSpec((bm, bn), lambda j, i: (i, j)),
        out_shape=jax.ShapeDtypeStruct((m, n), BF16),
        scratch_shapes=[pltpu.VMEM(s, d) for s, d in scratch],
        compiler_params=_cparams(2, blocks, scratch, temps=bm * bn * 4 + 2 * CAST_ROWS * k * 4),
        name="in_proj",
    )(u, ssq, w_in_t, w_in_t)


def _mm_res_kernel(a_ref, w_ref, h_ref, g_ref, o_ref, hg_ref, ssq_ref, wsc_ref):
    @pl.when(_first_token_tile())
    def _():
        _cast_tile(w_ref, wsc_ref)

    h_new = h_ref[...] + jnp.dot(a_ref[...], wsc_ref[...], preferred_element_type=F32)
    o_ref[...] = h_new
    _emit_scaled(h_new, g_ref, hg_ref, ssq_ref)


def _matmul_residual(a, w, layer, h, g_next, name, in_place=True):
    m, k = a.shape
    n = w.shape[2]
    big_k = k > 2048
    bm = _pick(m, (512, 256, 128))
    bn = _pick(n, (512, 256, 128)) if big_k else _pick(n, (1024, 512, 256, 128))
    blocks = [((bm, k), BF16), ((k, bn), F32), ((bm, bn), F32), ((1, bn), F32),
              ((bm, bn), F32), ((bm, bn), BF16), ((1, bm), F32)]
    scratch = [((k, bn), BF16)]
    return pl.pallas_call(
        _mm_res_kernel,
        grid=(n // bn, m // bm),
        in_specs=[pl.BlockSpec((bm, k), lambda j, i: (i, 0)),
                  pl.BlockSpec((None, k, bn), lambda j, i: (layer, 0, j)),
                  pl.BlockSpec((bm, bn), lambda j, i: (i, j)),
                  pl.BlockSpec((1, bn), lambda j, i: (0, j))],
        out_specs=[pl.BlockSpec((bm, bn), lambda j, i: (i, j)),
                   pl.BlockSpec((bm, bn), lambda j, i: (i, j)),
                   pl.BlockSpec((None, 1, bm), lambda j, i: (j, 0, i))],
        out_shape=[jax.ShapeDtypeStruct((m, n), F32), jax.ShapeDtypeStruct((m, n), BF16),
                   jax.ShapeDtypeStruct((n // bn, 1, m), F32)],
        scratch_shapes=[pltpu.VMEM(s, d) for s, d in scratch],
        input_output_aliases={2: 0} if in_place else {},
        compiler_params=_cparams(2, blocks, scratch, temps=2 * bm * bn * 4 + 2 * CAST_ROWS * bn * 4),
        name=name,
    )(a, w, h, g_next.reshape(1, n))


def _swiglu_kernel(v_ref, ssq_ref, wg_ref, wu_ref, o_ref, wgsc_ref, wusc_ref):
    @pl.when(_first_token_tile())
    def _():
        _cast_tile(wg_ref, wgsc_ref)
        _cast_tile(wu_ref, wusc_ref)

    v = v_ref[...]
    rs = _row_rsqrt(ssq_ref, v.shape[1])
    gate = jnp.dot(v, wgsc_ref[...], preferred_element_type=F32) * rs
    up = jnp.dot(v, wusc_ref[...], preferred_element_type=F32) * rs
    o_ref[...] = (gate * jax.nn.sigmoid(gate) * up).astype(o_ref.dtype)


def _swiglu(v, ssq, w_gate_up, layer):
    m, k = v.shape
    parts = ssq.shape[0]
    d_ff = w_gate_up.shape[2] // 2
    bm = _pick(m, (1024, 512, 256, 128))
    bn = _pick(d_ff, (512, 256, 128))
    nb = d_ff // bn
    blocks = [((bm, k), BF16), ((parts, 1, bm), F32), ((k, bn), F32), ((k, bn), F32), ((bm, bn), BF16)]
    scratch = [((k, bn), BF16), ((k, bn), BF16)]
    return pl.pallas_call(
        _swiglu_kernel,
        grid=(nb, m // bm),
        in_specs=[pl.BlockSpec((bm, k), lambda j, i: (i, 0)),
                  pl.BlockSpec((parts, 1, bm), lambda j, i: (0, 0, i)),
                  pl.BlockSpec((None, k, bn), lambda j, i: (layer, 0, j)),
                  pl.BlockSpec((None, k, bn), lambda j, i: (layer, 0, j + nb))],
        out_specs=pl.BlockSpec((bm, bn), lambda j, i: (i, j)),
        out_shape=jax.ShapeDtypeStruct((m, d_ff), BF16),
        scratch_shapes=[pltpu.VMEM(s, d) for s, d in scratch],
        compiler_params=_cparams(2, blocks, scratch, temps=3 * bm * bn * 4 + 2 * CAST_ROWS * bn * 4),
        name="swiglu_up",
    )(v, ssq, w_gate_up, w_gate_up)


def _merge_kernel(y_ref, ysc_ref, wa_ref, wb_ref, ga_ref, gb_ref, o_ref, wasc_ref, wbsc_ref):
    @pl.when(_first_token_tile())
    def _():
        _cast_tile(wa_ref, wasc_ref)
        _cast_tile(wb_ref, wbsc_ref)

    ya = jnp.dot(y_ref[...], wasc_ref[...], preferred_element_type=F32)
    yb = jnp.dot(ysc_ref[...], wbsc_ref[...], preferred_element_type=F32)
    ga = jax.nn.sigmoid(ga_ref[...].astype(F32))
    gb = jax.nn.sigmoid(gb_ref[...].astype(F32))
    o_ref[...] = (ga * ya + gb * yb).astype(o_ref.dtype)


def _merge(y, ysc, w_a, w_b, layer, proj, off_ga, off_gb):
    m, ka = y.shape
    kb = ysc.shape[1]
    n = w_a.shape[2]
    bm = _pick(m, (512, 256, 128))
    bn = _pick(n, (512, 256, 128))
    ja, jb = off_ga // bn, off_gb // bn
    assert off_ga % bn == 0 and off_gb % bn == 0
    blocks = [((bm, ka), BF16), ((bm, kb), BF16), ((ka, bn), F32), ((kb, bn), F32),
              ((bm, bn), BF16), ((bm, bn), BF16), ((bm, bn), BF16)]
    scratch = [((ka, bn), BF16), ((kb, bn), BF16)]
    return pl.pallas_call(
        _merge_kernel,
        grid=(n // bn, m // bm),
        in_specs=[pl.BlockSpec((bm, ka), lambda j, i: (i, 0)),
                  pl.BlockSpec((bm, kb), lambda j, i: (i, 0)),
                  pl.BlockSpec((None, ka, bn), lambda j, i: (layer, 0, j)),
                  pl.BlockSpec((None, kb, bn), lambda j, i: (layer, 0, j)),
                  pl.BlockSpec((bm, bn), lambda j, i: (i, j + ja)),
                  pl.BlockSpec((bm, bn), lambda j, i: (i, j + jb))],
        out_specs=pl.BlockSpec((bm, bn), lambda j, i: (i, j)),
        out_shape=jax.ShapeDtypeStruct((m, n), BF16),
        scratch_shapes=[pltpu.VMEM(s, d) for s, d in scratch],
        compiler_params=_cparams(2, blocks, scratch, temps=4 * bm * bn * 4 + 2 * CAST_ROWS * bn * 4),
        name="branch_merge",
    )(y, ysc, w_a, w_b, proj, proj)


def _ple_kernel(hn_ref, ssq_ref, wg_ref, p_ref, wp_ref, h_ref, g_ref, o_ref, hg_ref, ssqo_ref,
                wgsc_ref, wpsc_ref):
    @pl.when(_first_token_tile())
    def _():
        _cast_tile(wg_ref, wgsc_ref)
        _cast_tile(wp_ref, wpsc_ref)

    rs = _row_rsqrt(ssq_ref, hn_ref.shape[1])
    pg = jax.nn.sigmoid(jnp.dot(hn_ref[...], wgsc_ref[...], preferred_element_type=F32) * rs)
    e = jnp.dot(p_ref[...].astype(BF16), wpsc_ref[...], preferred_element_type=F32)
    h_new = h_ref[...] + pg * e
    o_ref[...] = h_new
    _emit_scaled(h_new, g_ref, hg_ref, ssqo_ref)


def _ple(hn, ssq, w_gate, p, w_proj, layer, h, g_next):
    m, k = hn.shape
    parts = ssq.shape[0]
    kp = p.shape[2]
    n = w_gate.shape[2]
    bm = _pick(m, (512, 256, 128))
    bn = _pick(n, (1024, 512, 256, 128))
    blocks = [((bm, k), BF16), ((parts, 1, bm), F32), ((k, bn), F32), ((bm, kp), F32), ((kp, bn), F32),
              ((bm, bn), F32), ((1, bn), F32), ((bm, bn), F32), ((bm, bn), BF16), ((1, bm), F32)]
    scratch = [((k, bn), BF16), ((kp, bn), BF16)]
    return pl.pallas_call(
        _ple_kernel,
        grid=(n // bn, m // bm),
        in_specs=[pl.BlockSpec((bm, k), lambda j, i: (i, 0)),
                  pl.BlockSpec((parts, 1, bm), lambda j, i: (0, 0, i)),
                  pl.BlockSpec((None, k, bn), lambda j, i: (layer, 0, j)),
                  pl.BlockSpec((None, bm, kp), lambda j, i: (layer, i, 0)),
                  pl.BlockSpec((None, kp, bn), lambda j, i: (layer, 0, j)),
                  pl.BlockSpec((bm, bn), lambda j, i: (i, j)),
                  pl.BlockSpec((1, bn), lambda j, i: (0, j))],
        out_specs=[pl.BlockSpec((bm, bn), lambda j, i: (i, j)),
                   pl.BlockSpec((bm, bn), lambda j, i: (i, j)),
                   pl.BlockSpec((None, 1, bm), lambda j, i: (j, 0, i))],
        out_shape=[jax.ShapeDtypeStruct((m, n), F32), jax.ShapeDtypeStruct((m, n), BF16),
                   jax.ShapeDtypeStruct((n // bn, 1, m), F32)],
        scratch_shapes=[pltpu.VMEM(s, d) for s, d in scratch],
        input_output_aliases={5: 0},
        compiler_params=_cparams(2, blocks, scratch, temps=4 * bm * bn * 4 + 2 * CAST_ROWS * bn * 4),
        name="ple",
    )(hn, ssq, w_gate, p, w_proj, h, g_next.reshape(1, n))


def _dtprep_kernel(u_ref, ssq_ref, w_ref, bias_ref, alog_ref, dtrow_ref, wrow_ref, acsrow_ref, acscol_ref,
                   *, nc, groups, r_heads):
    x = lax.dot_general(u_ref[...], w_ref[...].astype(BF16), (((1,), (1,)), ((), ())),
                        preferred_element_type=F32)
    x = x * _row_rsqrt(ssq_ref, u_ref.shape[1]) + bias_ref[...]
    dt = jnp.maximum(x, 0.0) + jnp.log1p(jnp.exp(-jnp.abs(x)))
    adt = dt * (-jnp.exp(alog_ref[...]))
    row = lax.broadcasted_iota(jnp.int32, (CHUNK, LANES), 0)
    for k in range(nc):
        sl = slice(k * CHUNK, (k + 1) * CHUNK)
        acs = adt[sl]
        sh = 1
        while sh < CHUNK:
            acs = acs + jnp.where(row >= sh, pltpu.roll(acs, sh, 0), 0.0)
            sh *= 2
        acsrow_ref[k] = acs.T
        dtrow_ref[k] = dt[sl].T
        wrow_ref[k] = (dt[sl] * jnp.exp(acs[CHUNK - 1:CHUNK, :] - acs)).T
        for g in range(groups):
            shift = (LANES - g * r_heads) % LANES
            acscol_ref[g, sl, :] = pltpu.roll(acs, shift, 1) if shift else acs


def _dtprep(u, ssq, w_in_t, layer, dt0, bias, a_log, groups, r_heads):
    m, k = u.shape
    parts = ssq.shape[0]
    nc = 4 if m % (4 * CHUNK) == 0 else 1
    ts = nc * CHUNK
    nchunks = m // CHUNK
    rows = ((nc, LANES, LANES), F32)
    blocks = [((ts, k), BF16), ((parts, 1, ts), F32), ((LANES, k), F32), rows, rows, rows,
              ((groups, ts, LANES), F32)]
    row_spec = pl.BlockSpec((nc, LANES, LANES), lambda i: (i, 0, 0))
    row_shape = jax.ShapeDtypeStruct((nchunks, LANES, LANES), F32)
    return pl.pallas_call(
        functools.partial(_dtprep_kernel, nc=nc, groups=groups, r_heads=r_heads),
        grid=(m // ts,),
        in_specs=[pl.BlockSpec((ts, k), lambda i: (i, 0)),
                  pl.BlockSpec((parts, 1, ts), lambda i: (0, 0, i)),
                  pl.BlockSpec((None, LANES, k), lambda i: (layer, dt0 // LANES, 0)),
                  pl.BlockSpec((1, LANES), lambda i: (0, 0)),
                  pl.BlockSpec((1, LANES), lambda i: (0, 0))],
        out_specs=[row_spec, row_spec, row_spec,
                   pl.BlockSpec((groups, ts, LANES), lambda i: (0, i, 0))],
        out_shape=[row_shape, row_shape, row_shape,
                   jax.ShapeDtypeStruct((groups, m, LANES), F32)],
        compiler_params=_cparams(1, blocks, temps=8 * ts * LANES * 4 + k * LANES * 2),
        name="dt_prep",
    )(u, ssq, w_in_t, bias, a_log)


def _ssd_kernel(z_ref, xs_ref, b_ref, c_ref, xsp_ref, bp_ref, cp_ref,
                dtr_ref, wr_ref, acr_ref, acc_ref, shift_ref,
                cwx_ref, cwb_ref, cwc_ref, cbx_ref, cbb_ref, cbc_ref, dsk_ref, nw_ref,
                o_ref, st_ref, *, nc, r_heads, kconv):
    g = pl.program_id(1)
    c = pl.program_id(2)
    L = CHUNK
    N = SSD_STATE
    gw = r_heads * HEAD_DIM
    npair = gw // LANES

    @pl.when(c == 0)
    def _():
        st_ref[...] = jnp.zeros(st_ref.shape, F32)

    tri = (lax.broadcasted_iota(jnp.int32, (L, L), 0) >= lax.broadcasted_iota(jnp.int32, (L, L), 1))
    lo = lax.broadcasted_iota(jnp.int32, (L, LANES), 1) < HEAD_DIM
    hi = jnp.logical_not(lo)
    lo_row = lo[0:1]

    def conv_silu(ref, prev_ref, k, w_ref, bias_ref):
        if k == 0:
            prev = prev_ref[...]
            prev = jnp.where(c > 0, prev, jnp.zeros_like(prev))
            x2 = jnp.concatenate([prev, ref[0:L, :]], axis=0)
        else:
            x2 = ref[(k - 1) * L:(k + 1) * L, :]
        shifted = jnp.dot(shift_ref[...], x2, preferred_element_type=F32)
        acc = None
        for tap in range(kconv - 1):
            term = shifted[tap * L:(tap + 1) * L, :] * w_ref[tap:tap + 1, :]
            acc = term if acc is None else acc + term
        acc = acc + x2[L:2 * L].astype(F32) * w_ref[kconv - 1:kconv, :]
        acc = acc + bias_ref[...]
        return acc * jax.nn.sigmoid(acc)

    for k in range(nc):
        r0 = k * L
        x = conv_silu(xs_ref, xsp_ref, k, cwx_ref, cbx_ref)
        bm = conv_silu(b_ref, bp_ref, k, cwb_ref, cbb_ref)
        cm = conv_silu(c_ref, cp_ref, k, cwc_ref, cbc_ref)
        cb = lax.dot_general(cm.astype(BF16), bm.astype(BF16), (((1,), (1,)), ((), ())),
                             preferred_element_type=F32)
        bt = bm.T
        acol = acc_ref[0, r0:r0 + L, :]
        cdec = jnp.exp(acol[L - 1:L, :])
        ssq = jnp.zeros((L, 1), F32)
        ys = []
        for j in range(npair):
            cs = slice(j * LANES, (j + 1) * LANES)
            xp = x[:, cs]
            s_prev = st_ref[:, cs]
            s_new = s_prev * jnp.where(lo_row, cdec[:, 2 * j:2 * j + 1], cdec[:, 2 * j + 1:2 * j + 2])
            y = None
            for hh, keep in ((2 * j, lo), (2 * j + 1, hi)):
                head = pl.ds(g * r_heads + hh, 1)
                arow = acr_ref[k, head, :]
                drow = dtr_ref[k, head, :]
                wrow = wr_ref[k, head, :]
                xm = jnp.where(keep, xp, 0.0).astype(BF16)
                sm = jnp.where(keep, s_prev, 0.0).astype(BF16)
                ab = jnp.broadcast_to(acol[:, hh:hh + 1], (L, L))
                dec = jnp.exp(jnp.where(tri, ab - arow, -jnp.inf))
                mh = cb * dec * drow
                ch = cm * jnp.exp(ab)
                lhs = jnp.concatenate([mh, ch], axis=1).astype(BF16)
                rhs = jnp.concatenate([xm, sm], axis=0)
                t = jnp.dot(lhs, rhs, preferred_element_type=F32)
                y = t if y is None else y + t
                bth = (bt * wrow).astype(BF16)
                s_new = s_new + jnp.dot(bth, xm, preferred_element_type=F32)
            st_ref[:, cs] = s_new
            yt = y + dsk_ref[:, cs] * xp
            zt = z_ref[r0:r0 + L, cs].astype(F32)
            yt = yt * (zt * jax.nn.sigmoid(zt))
            ssq = ssq + jnp.sum(yt * yt, axis=-1, keepdims=True)
            ys.append(yt)
        rs = lax.rsqrt(ssq / gw + EPS)
        for j in range(npair):
            cs = slice(j * LANES, (j + 1) * LANES)
            o_ref[r0:r0 + L, cs] = (ys[j] * rs * nw_ref[:, cs]).astype(o_ref.dtype)


def _ssd(proj, dtrow, wrow, acsrow, acscol, conv_w, conv_b, d_skip, norm_w, *, batch, seq, inner, groups):
    m = proj.shape[0]
    gw = inner // groups
    r_heads = gw // HEAD_DIM
    assert gw % LANES == 0 and r_heads % 2 == 0 and SSD_STATE == LANES
    kconv = conv_w.shape[0]
    assert kconv - 1 <= CHUNK
    nc = _pick(seq // CHUNK, (4, 2, 1))
    ts = nc * CHUNK
    nt = seq // ts
    gn = groups * SSD_STATE
    xs_blk, b_blk, c_blk = inner // gw, 2 * inner // SSD_STATE, (2 * inner + gn) // SSD_STATE
    wb_blk, wc_blk = inner // SSD_STATE, (inner + gn) // SSD_STATE
    hp = acsrow.shape[1]
    t_idx = jnp.arange(CHUNK)[None, :, None]
    d_idx = (kconv - 1 - jnp.arange(kconv - 1))[:, None, None]
    s_idx = jnp.arange(2 * CHUNK)[None, None, :]
    shift = (s_idx == CHUNK + t_idx - d_idx).astype(BF16).reshape((kconv - 1) * CHUNK, 2 * CHUNK)
    st_shape = (SSD_STATE, gw)
    blocks = [((ts, gw), BF16), ((ts, gw), BF16), ((ts, SSD_STATE), BF16), ((ts, SSD_STATE), BF16),
              ((CHUNK, gw), BF16), ((CHUNK, SSD_STATE), BF16), ((CHUNK, SSD_STATE), BF16),
              ((nc, hp, LANES), F32), ((nc, hp, LANES), F32), ((nc, hp, LANES), F32),
              ((1, ts, LANES), F32), (shift.shape, BF16),
              ((kconv, gw), F32), ((kconv, SSD_STATE), F32), ((kconv, SSD_STATE), F32),
              ((1, gw), F32), ((1, SSD_STATE), F32), ((1, SSD_STATE), F32), ((1, gw), F32), ((1, gw), F32),
              ((ts, gw), BF16)]
    scratch = [(st_shape, F32)]
    row = lambda b, g, c: b * nt + c
    prev = lambda b, g, c: jnp.maximum(row(b, g, c) * nc - 1, 0)
    return pl.pallas_call(
        functools.partial(_ssd_kernel, nc=nc, r_heads=r_heads, kconv=kconv),
        grid=(batch, groups, nt),
        in_specs=[
            pl.BlockSpec((ts, gw), lambda b, g, c: (row(b, g, c), g)),
            pl.BlockSpec((ts, gw), lambda b, g, c: (row(b, g, c), xs_blk + g)),
            pl.BlockSpec((ts, SSD_STATE), lambda b, g, c: (row(b, g, c), b_blk + g)),
            pl.BlockSpec((ts, SSD_STATE), lambda b, g, c: (row(b, g, c), c_blk + g)),
            pl.BlockSpec((CHUNK, gw), lambda b, g, c: (prev(b, g, c), xs_blk + g)),
            pl.BlockSpec((CHUNK, SSD_STATE), lambda b, g, c: (prev(b, g, c), b_blk + g)),
            pl.BlockSpec((CHUNK, SSD_STATE), lambda b, g, c: (prev(b, g, c), c_blk + g)),
            pl.BlockSpec((nc, hp, LANES), lambda b, g, c: (row(b, g, c), 0, 0)),
            pl.BlockSpec((nc, hp, LANES), lambda b, g, c: (row(b, g, c), 0, 0)),
            pl.BlockSpec((nc, hp, LANES), lambda b, g, c: (row(b, g, c), 0, 0)),
            pl.BlockSpec((1, ts, LANES), lambda b, g, c: (g, row(b, g, c), 0)),
            pl.BlockSpec(shift.shape, lambda b, g, c: (0, 0)),
            pl.BlockSpec((kconv, gw), lambda b, g, c: (0, g)),
            pl.BlockSpec((kconv, SSD_STATE), lambda b, g, c: (0, wb_blk + g)),
            pl.BlockSpec((kconv, SSD_STATE), lambda b, g, c: (0, wc_blk + g)),
            pl.BlockSpec((1, gw), lambda b, g, c: (0, g)),
            pl.BlockSpec((1, SSD_STATE), lambda b, g, c: (0, wb_blk + g)),
            pl.BlockSpec((1, SSD_STATE), lambda b, g, c: (0, wc_blk + g)),
            pl.BlockSpec((1, gw), lambda b, g, c: (0, g)),
            pl.BlockSpec((1, gw), lambda b, g, c: (0, g)),
        ],
        out_specs=pl.BlockSpec((ts, gw), lambda b, g, c: (row(b, g, c), g)),
        out_shape=jax.ShapeDtypeStruct((m, inner), BF16),
        scratch_shapes=[pltpu.VMEM(s, d) for s, d in scratch],
        compiler_params=_cparams(3, blocks, scratch, temps=8 << 20),
        name="ssd",
    )(proj, proj, proj, proj, proj, proj, proj, dtrow, wrow, acsrow, acscol, shift,
      conv_w, conv_w, conv_w, conv_b, conv_b, conv_b, d_skip, norm_w)


def _sconv_kernel(gb_ref, gc_ref, xt_ref, w_ref, o_ref, pe_ref, *, ts, kconv, cw):
    halo = SUBLANES

    @pl.when(pl.program_id(1) == 0)
    def _():
        pe_ref[0:halo, :] = jnp.zeros((halo, pe_ref.shape[1]), F32)

    pe_ref[halo:halo + ts, :] = gb_ref[...].astype(F32) * xt_ref[...].astype(F32)
    width = pe_ref.shape[1]
    for c0 in range(0, width, cw):
        acc = None
        for k in range(kconv):
            r = halo - (kconv - 1) + k
            term = pe_ref[r:r + ts, c0:c0 + cw] * w_ref[k:k + 1, c0:c0 + cw]
            acc = term if acc is None else acc + term
        o_ref[:, c0:c0 + cw] = (gc_ref[:, c0:c0 + cw].astype(F32) * acc).astype(o_ref.dtype)
    pe_ref[0:halo, :] = pe_ref[ts:ts + halo, :]


def _sconv(proj, conv_w, *, batch, seq, width, off_b, off_c, off_x):
    m = proj.shape[0]
    kconv = conv_w.shape[0]
    ts = _pick(seq, (256, 128))
    nt = seq // ts
    assert off_b % width == 0 and off_c % width == 0 and off_x % width == 0 and kconv - 1 <= SUBLANES
    jb, jc, jx = off_b // width, off_c // width, off_x // width
    cw = _pick(width, (512, 256, 128))
    pe_shape = (SUBLANES + ts, width)
    blocks = [((ts, width), BF16)] * 4 + [((kconv, width), F32)]
    scratch = [(pe_shape, F32)]
    return pl.pallas_call(
        functools.partial(_sconv_kernel, ts=ts, kconv=kconv, cw=cw),
        grid=(batch, nt),
        in_specs=[pl.BlockSpec((ts, width), lambda b, c: (b * nt + c, jb)),
                  pl.BlockSpec((ts, width), lambda b, c: (b * nt + c, jc)),
                  pl.BlockSpec((ts, width), lambda b, c: (b * nt + c, jx)),
                  pl.BlockSpec((kconv, width), lambda b, c: (0, 0))],
        out_specs=pl.BlockSpec((ts, width), lambda b, c: (b * nt + c, 0)),
        out_shape=jax.ShapeDtypeStruct((m, width), BF16),
        scratch_shapes=[pltpu.VMEM(s, d) for s, d in scratch],
        compiler_params=_cparams(2, blocks, scratch, temps=4 << 20),
        name="short_conv",
    )(proj, proj, proj, conv_w)


def kernel(x, p, norm_mix, w_in, ssd_conv_w, ssd_conv_b, ssd_dt_bias, ssd_a_log, ssd_d, ssd_norm,
           ssd_out, sc_conv_w, sc_out, w_o, norm_ffn, w_gate_up, w_down, norm_ple, ple_gate,
           ple_proj, norm_final):
    batch, seq, d = x.shape
    depth = w_in.shape[0]
    m = batch * seq
    heads = ssd_a_log.shape[1]
    inner = heads * HEAD_DIM
    groups = SSD_GROUPS
    r_heads = heads // groups
    xbc = ssd_conv_w.shape[2]
    scw = sc_conv_w.shape[2]
    assert xbc == inner + 2 * groups * SSD_STATE and heads <= LANES
    dt0 = inner + xbc
    off_scb = dt0
    off_scc, off_scx = off_scb + scw, off_scb + 2 * scw
    off_ga = off_scb + 3 * scw
    off_gb = off_ga + d
    hpad = LANES - heads
    p2 = p.reshape(depth, m, p.shape[-1])
    w_in_t = jnp.swapaxes(w_in, 1, 2)

    h = x.reshape(m, d)
    u, u_ssq = _prenorm(h, norm_mix[0])
    for i in range(depth):
        dt_bias = jnp.pad(ssd_dt_bias[i], (0, hpad)).reshape(1, LANES)
        a_log = jnp.pad(ssd_a_log[i], (0, hpad)).reshape(1, LANES)
        d_skip = jnp.repeat(ssd_d[i], HEAD_DIM).reshape(1, inner)

        proj = _inproj(u, u_ssq, w_in_t, i, dt0, heads)
        dtrow, wrow, acsrow, acscol = _dtprep(u, u_ssq, w_in_t, i, dt0, dt_bias, a_log, groups, r_heads)
        y = _ssd(proj, dtrow, wrow, acsrow, acscol, ssd_conv_w[i], ssd_conv_b[i].reshape(1, xbc), d_skip,
                 ssd_norm[i].reshape(1, inner), batch=batch, seq=seq, inner=inner, groups=groups)
        ysc = _sconv(proj, sc_conv_w[i], batch=batch, seq=seq, width=scw,
                     off_b=off_scb, off_c=off_scc, off_x=off_scx)
        merged = _merge(y, ysc, ssd_out, sc_out, i, proj, off_ga, off_gb)
        h, v, v_ssq = _matmul_residual(merged, w_o, i, h, norm_ffn[i], "w_o_residual", in_place=i > 0)
        act = _swiglu(v, v_ssq, w_gate_up, i)
        h, hn, hn_ssq = _matmul_residual(act, w_down, i, h, norm_ple[i], "w_down_residual")
        g_next = norm_mix[i + 1] if i + 1 < depth else norm_final
        h, u, u_ssq = _ple(hn, hn_ssq, ple_gate, p2, ple_proj, i, h, g_next)
    out = _rmsnorm(h, norm_final, F32)
    return out.reshape(batch, seq, d)
```

```python
import functools

import jax
import jax.numpy as jnp
from jax import lax
from jax.experimental import pallas as pl
from jax.experimental.pallas import tpu as pltpu

F32 = jnp.float32
BF16 = jnp.bfloat16

EPS = 1e-6
HEAD_DIM = 64
SSD_GROUPS = 8
SSD_STATE = 128
CHUNK = 128
LANES = 128
SUBLANES = 8
CAST_ROWS = 256
LOG2E = 1.4426950408889634
VMEM_BYTES_V7X = 64 * 1024 * 1024
VMEM_CAP = VMEM_BYTES_V7X - 8 * 1024 * 1024


def _nbytes(shape, dtype):
    n = 1
    for s in shape:
        n *= s
    return n * jnp.dtype(dtype).itemsize


def _cparams(ngrid, blocks, scratch=(), temps=0):
    need = (2 * sum(_nbytes(s, d) for s, d in blocks) + sum(_nbytes(s, d) for s, d in scratch)
            + temps + (4 << 20))
    return pltpu.CompilerParams(dimension_semantics=("arbitrary",) * ngrid,
                                vmem_limit_bytes=int(min(max(need, 16 << 20), VMEM_CAP)))


def _pick(n, candidates):
    for c in candidates:
        if n % c == 0:
            return c
    raise ValueError(f"no block size in {candidates} divides {n}")


def _cast_tile(w_ref, wsc_ref):
    k = w_ref.shape[0]
    rc = _pick(k, (CAST_ROWS, LANES, SUBLANES))
    for r in range(0, k, rc):
        wsc_ref[r:r + rc, :] = w_ref[r:r + rc, :].astype(BF16)


def _first_token_tile():
    return pl.program_id(1) == 0


def _sigmoid(x):
    return 0.5 * jnp.tanh(0.5 * x) + 0.5


def _silu(x):
    h = 0.5 * x
    return h * jnp.tanh(h) + h


def _emit_scaled(h_new, g_ref, hg_ref, ssq_ref):
    hg_ref[...] = (h_new * g_ref[...]).astype(BF16)
    col = jnp.sum(h_new * h_new, axis=-1, keepdims=True)
    ssq_ref[...] = jnp.transpose(jnp.broadcast_to(col, (col.shape[0], LANES)))[0:1, :]


def _row_rsqrt(ssq_ref, d):
    s = ssq_ref[0]
    for q in range(1, ssq_ref.shape[0]):
        s = s + ssq_ref[q]
    row = lax.rsqrt(s / d + EPS)
    return jnp.transpose(jnp.broadcast_to(row, (LANES, row.shape[1])))[:, 0:1]


def _prenorm_kernel(x_ref, g_ref, hg_ref, ssq_ref):
    _emit_scaled(x_ref[...], g_ref, hg_ref, ssq_ref.at[0])


def _prenorm(x, g):
    m, d = x.shape
    bm = _pick(m, (512, 256, 128))
    blocks = [((bm, d), F32), ((1, d), F32), ((bm, d), BF16), ((1, 1, bm), F32)]
    return pl.pallas_call(
        _prenorm_kernel,
        grid=(m // bm,),
        in_specs=[pl.BlockSpec((bm, d), lambda i: (i, 0)),
                  pl.BlockSpec((1, d), lambda i: (0, 0))],
        out_specs=[pl.BlockSpec((bm, d), lambda i: (i, 0)),
                   pl.BlockSpec((1, 1, bm), lambda i: (0, 0, i))],
        out_shape=[jax.ShapeDtypeStruct((m, d), BF16), jax.ShapeDtypeStruct((1, 1, m), F32)],
        compiler_params=_cparams(1, blocks, temps=2 * bm * d * 4),
        name="prenorm",
    )(x, g.reshape(1, d))


def _rmsnorm_kernel(x_ref, g_ref, o_ref):
    x = x_ref[...]
    ms = jnp.mean(x * x, axis=-1, keepdims=True)
    o_ref[...] = (x * lax.rsqrt(ms + EPS) * g_ref[...]).astype(o_ref.dtype)


def _rmsnorm(x, g, out_dtype):
    m, d = x.shape
    bm = _pick(m, (512, 256, 128))
    blocks = [((bm, d), F32), ((1, d), F32), ((bm, d), out_dtype)]
    return pl.pallas_call(
        _rmsnorm_kernel,
        grid=(m // bm,),
        in_specs=[pl.BlockSpec((bm, d), lambda i: (i, 0)),
                  pl.BlockSpec((1, d), lambda i: (0, 0))],
        out_specs=pl.BlockSpec((bm, d), lambda i: (i, 0)),
        out_shape=jax.ShapeDtypeStruct((m, d), out_dtype),
        compiler_params=_cparams(1, blocks, temps=2 * bm * d * 4),
        name="rmsnorm",
    )(x, g.reshape(1, d))


def _inproj_kernel(a_ref, ssq_ref, w_ref, wn_ref, o_ref, wsc_ref, *, n_aligned, skip):
    j = pl.program_id(0)
    bn = w_ref.shape[0]
    rc = _pick(bn, (CAST_ROWS, LANES))

    @pl.when(jnp.logical_and(_first_token_tile(), j < n_aligned))
    def _():
        _cast_tile(w_ref, wsc_ref)

    @pl.when(jnp.logical_and(_first_token_tile(), j >= n_aligned))
    def _():
        for r in range(0, bn - rc, rc):
            wsc_ref[r:r + rc, :] = w_ref[r + skip:r + skip + rc, :].astype(BF16)
        wsc_ref[bn - rc:bn - skip, :] = w_ref[bn - rc + skip:bn, :].astype(BF16)
        wsc_ref[bn - skip:bn, :] = wn_ref[...].astype(BF16)

    acc = lax.dot_general(a_ref[...], wsc_ref[...], (((1,), (1,)), ((), ())), preferred_element_type=F32)
    o_ref[...] = (acc * _row_rsqrt(ssq_ref, a_ref.shape[1])).astype(o_ref.dtype)


def _inproj(u, ssq, w_in_t, layer, dt0, heads):
    m, k = u.shape
    parts = ssq.shape[0]
    d_in = w_in_t.shape[1]
    n = d_in - heads
    bm = _pick(m, (1024, 512, 256, 128))
    bn = next(c for c in (1024, 512, 256, 128) if dt0 % c == 0 and (n - dt0) % c == 0)
    assert heads % (2 * SUBLANES) == 0 and bn % heads == 0 and d_in % heads == 0 and heads < LANES
    n_aligned = dt0 // bn
    blocks = [((bm, k), BF16), ((parts, 1, bm), F32), ((bn, k), F32), ((heads, k), F32), ((bm, bn), BF16)]
    scratch = [((bn, k), BF16)]
    return pl.pallas_call(
        functools.partial(_inproj_kernel, n_aligned=n_aligned, skip=heads),
        grid=(n // bn, m // bm),
        in_specs=[pl.BlockSpec((bm, k), lambda j, i: (i, 0)),
                  pl.BlockSpec((parts, 1, bm), lambda j, i: (0, 0, i)),
                  pl.BlockSpec((None, bn, k), lambda j, i: (layer, j, 0)),
                  pl.BlockSpec((None, heads, k), lambda j, i: (layer, (j + 1) * (bn // heads), 0))],
        out_specs=pl.BlockSpec((bm, bn), lambda j, i: (i, j)),
        out_shape=jax.ShapeDtypeStruct((m, n), BF16),
        scratch_shapes=[pltpu.VMEM(s, d) for s, d in scratch],
        compiler_params=_cparams(2, blocks, scratch, temps=bm * bn * 4 + 2 * CAST_ROWS * k * 4),
        name="in_proj",
    )(u, ssq, w_in_t, w_in_t)


def _mm_res_kernel(a_ref, w_ref, h_ref, g_ref, o_ref, hg_ref, ssq_ref, wsc_ref):
    @pl.when(_first_token_tile())
    def _():
        _cast_tile(w_ref, wsc_ref)

    h_new = h_ref[...] + jnp.dot(a_ref[...], wsc_ref[...], preferred_element_type=F32)
    o_ref[...] = h_new
    _emit_scaled(h_new, g_ref, hg_ref, ssq_ref)


def _matmul_residual(a, w, layer, h, g_next, name, in_place=True):
    m, k = a.shape
    n = w.shape[2]
    big_k = k > 2048
    bm = _pick(m, (512, 256, 128))
    bn = _pick(n, (512, 256, 128)) if big_k else _pick(n, (1024, 512, 256, 128))
    blocks = [((bm, k), BF16), ((k, bn), F32), ((bm, bn), F32), ((1, bn), F32),
              ((bm, bn), F32), ((bm, bn), BF16), ((1, bm), F32)]
    scratch = [((k, bn), BF16)]
    return pl.pallas_call(
        _mm_res_kernel,
        grid=(n // bn, m // bm),
        in_specs=[pl.BlockSpec((bm, k), lambda j, i: (i, 0)),
                  pl.BlockSpec((None, k, bn), lambda j, i: (layer, 0, j)),
                  pl.BlockSpec((bm, bn), lambda j, i: (i, j)),
                  pl.BlockSpec((1, bn), lambda j, i: (0, j))],
        out_specs=[pl.BlockSpec((bm, bn), lambda j, i: (i, j)),
                   pl.BlockSpec((bm, bn), lambda j, i: (i, j)),
                   pl.BlockSpec((None, 1, bm), lambda j, i: (j, 0, i))],
        out_shape=[jax.ShapeDtypeStruct((m, n), F32), jax.ShapeDtypeStruct((m, n), BF16),
                   jax.ShapeDtypeStruct((n // bn, 1, m), F32)],
        scratch_shapes=[pltpu.VMEM(s, d) for s, d in scratch],
        input_output_aliases={2: 0} if in_place else {},
        compiler_params=_cparams(2, blocks, scratch, temps=2 * bm * bn * 4 + 2 * CAST_ROWS * bn * 4),
        name=name,
    )(a, w, h, g_next.reshape(1, n))


def _swiglu_kernel(v_ref, ssq_ref, wg_ref, wu_ref, o_ref, wgsc_ref, wusc_ref):
    @pl.when(_first_token_tile())
    def _():
        _cast_tile(wg_ref, wgsc_ref)
        _cast_tile(wu_ref, wusc_ref)

    v = v_ref[...]
    rs = _row_rsqrt(ssq_ref, v.shape[1])
    gate = jnp.dot(v, wgsc_ref[...], preferred_element_type=F32) * rs
    up = jnp.dot(v, wusc_ref[...], preferred_element_type=F32) * rs
    o_ref[...] = (_silu(gate) * up).astype(o_ref.dtype)


def _swiglu(v, ssq, w_gate_up, layer):
    m, k = v.shape
    parts = ssq.shape[0]
    d_ff = w_gate_up.shape[2] // 2
    bm = _pick(m, (1024, 512, 256, 128))
    bn = _pick(d_ff, (512, 256, 128))
    nb = d_ff // bn
    blocks = [((bm, k), BF16), ((parts, 1, bm), F32), ((k, bn), F32), ((k, bn), F32), ((bm, bn), BF16)]
    scratch = [((k, bn), BF16), ((k, bn), BF16)]
    return pl.pallas_call(
        _swiglu_kernel,
        grid=(nb, m // bm),
        in_specs=[pl.BlockSpec((bm, k), lambda j, i: (i, 0)),
                  pl.BlockSpec((parts, 1, bm), lambda j, i: (0, 0, i)),
                  pl.BlockSpec((None, k, bn), lambda j, i: (layer, 0, j)),
                  pl.BlockSpec((None, k, bn), lambda j, i: (layer, 0, j + nb))],
        out_specs=pl.BlockSpec((bm, bn), lambda j, i: (i, j)),
        out_shape=jax.ShapeDtypeStruct((m, d_ff), BF16),
        scratch_shapes=[pltpu.VMEM(s, d) for s, d in scratch],
        compiler_params=_cparams(2, blocks, scratch, temps=3 * bm * bn * 4 + 2 * CAST_ROWS * bn * 4),
        name="swiglu_up",
    )(v, ssq, w_gate_up, w_gate_up)


def _merge_kernel(y_ref, ysc_ref, wa_ref, wb_ref, ga_ref, gb_ref, o_ref, wasc_ref, wbsc_ref):
    @pl.when(_first_token_tile())
    def _():
        _cast_tile(wa_ref, wasc_ref)
        _cast_tile(wb_ref, wbsc_ref)

    ya = jnp.dot(y_ref[...], wasc_ref[...], preferred_element_type=F32)
    yb = jnp.dot(ysc_ref[...], wbsc_ref[...], preferred_element_type=F32)
    ga = _sigmoid(ga_ref[...].astype(F32))
    gb = _sigmoid(gb_ref[...].astype(F32))
    o_ref[...] = (ga * ya + gb * yb).astype(o_ref.dtype)


def _merge(y, ysc, w_a, w_b, layer, proj, off_ga, off_gb):
    m, ka = y.shape
    kb = ysc.shape[1]
    n = w_a.shape[2]
    bm = _pick(m, (512, 256, 128))
    bn = _pick(n, (512, 256, 128))
    ja, jb = off_ga // bn, off_gb // bn
    assert off_ga % bn == 0 and off_gb % bn == 0
    blocks = [((bm, ka), BF16), ((bm, kb), BF16), ((ka, bn), F32), ((kb, bn), F32),
              ((bm, bn), BF16), ((bm, bn), BF16), ((bm, bn), BF16)]
    scratch = [((ka, bn), BF16), ((kb, bn), BF16)]
    return pl.pallas_call(
        _merge_kernel,
        grid=(n // bn, m // bm),
        in_specs=[pl.BlockSpec((bm, ka), lambda j, i: (i, 0)),
                  pl.BlockSpec((bm, kb), lambda j, i: (i, 0)),
                  pl.BlockSpec((None, ka, bn), lambda j, i: (layer, 0, j)),
                  pl.BlockSpec((None, kb, bn), lambda j, i: (layer, 0, j)),
                  pl.BlockSpec((bm, bn), lambda j, i: (i, j + ja)),
                  pl.BlockSpec((bm, bn), lambda j, i: (i, j + jb))],
        out_specs=pl.BlockSpec((bm, bn), lambda j, i: (i, j)),
        out_shape=jax.ShapeDtypeStruct((m, n), BF16),
        scratch_shapes=[pltpu.VMEM(s, d) for s, d in scratch],
        compiler_params=_cparams(2, blocks, scratch, temps=4 * bm * bn * 4 + 2 * CAST_ROWS * bn * 4),
        name="branch_merge",
    )(y, ysc, w_a, w_b, proj, proj)


def _ple_kernel(hn_ref, ssq_ref, wg_ref, p_ref, wp_ref, h_ref, g_ref, o_ref, hg_ref, ssqo_ref,
                wgsc_ref, wpsc_ref):
    @pl.when(_first_token_tile())
    def _():
        _cast_tile(wg_ref, wgsc_ref)
        _cast_tile(wp_ref, wpsc_ref)

    rs = _row_rsqrt(ssq_ref, hn_ref.shape[1])
    pg = _sigmoid(jnp.dot(hn_ref[...], wgsc_ref[...], preferred_element_type=F32) * rs)
    e = jnp.dot(p_ref[...].astype(BF16), wpsc_ref[...], preferred_element_type=F32)
    h_new = h_ref[...] + pg * e
    o_ref[...] = h_new
    _emit_scaled(h_new, g_ref, hg_ref, ssqo_ref)


def _ple(hn, ssq, w_gate, p, w_proj, layer, h, g_next):
    m, k = hn.shape
    parts = ssq.shape[0]
    kp = p.shape[2]
    n = w_gate.shape[2]
    bm = _pick(m, (512, 256, 128))
    bn = _pick(n, (1024, 512, 256, 128))
    blocks = [((bm, k), BF16), ((parts, 1, bm), F32), ((k, bn), F32), ((bm, kp), F32), ((kp, bn), F32),
              ((bm, bn), F32), ((1, bn), F32), ((bm, bn), F32), ((bm, bn), BF16), ((1, bm), F32)]
    scratch = [((k, bn), BF16), ((kp, bn), BF16)]
    return pl.pallas_call(
        _ple_kernel,
        grid=(n // bn, m // bm),
        in_specs=[pl.BlockSpec((bm, k), lambda j, i: (i, 0)),
                  pl.BlockSpec((parts, 1, bm), lambda j, i: (0, 0, i)),
                  pl.BlockSpec((None, k, bn), lambda j, i: (layer, 0, j)),
                  pl.BlockSpec((None, bm, kp), lambda j, i: (layer, i, 0)),
                  pl.BlockSpec((None, kp, bn), lambda j, i: (layer, 0, j)),
                  pl.BlockSpec((bm, bn), lambda j, i: (i, j)),
                  pl.BlockSpec((1, bn), lambda j, i: (0, j))],
        out_specs=[pl.BlockSpec((bm, bn), lambda j, i: (i, j)),
                   pl.BlockSpec((bm, bn), lambda j, i: (i, j)),
                   pl.BlockSpec((None, 1, bm), lambda j, i: (j, 0, i))],
        out_shape=[jax.ShapeDtypeStruct((m, n), F32), jax.ShapeDtypeStruct((m, n), BF16),
                   jax.ShapeDtypeStruct((n // bn, 1, m), F32)],
        scratch_shapes=[pltpu.VMEM(s, d) for s, d in scratch],
        input_output_aliases={5: 0},
        compiler_params=_cparams(2, blocks, scratch, temps=4 * bm * bn * 4 + 2 * CAST_ROWS * bn * 4),
        name="ple",
    )(hn, ssq, w_gate, p, w_proj, h, g_next.reshape(1, n))


def _dtprep_kernel(u_ref, ssq_ref, w_ref, bias_ref, alog_ref, dtrow_ref, wrow_ref, acsrow_ref, acscol_ref,
                   *, nc, groups, r_heads):
    x = lax.dot_general(u_ref[...], w_ref[...].astype(BF16), (((1,), (1,)), ((), ())),
                        preferred_element_type=F32)
    x = x * _row_rsqrt(ssq_ref, u_ref.shape[1]) + bias_ref[...]
    dt = jnp.maximum(x, 0.0) + jnp.log1p(jnp.exp(-jnp.abs(x)))
    adt = dt * (-jnp.exp(alog_ref[...]))
    row = lax.broadcasted_iota(jnp.int32, (CHUNK, LANES), 0)
    for k in range(nc):
        sl = slice(k * CHUNK, (k + 1) * CHUNK)
        acs = adt[sl]
        sh = 1
        while sh < CHUNK:
            acs = acs + jnp.where(row >= sh, pltpu.roll(acs, sh, 0), 0.0)
            sh *= 2
        acs2 = acs * LOG2E
        acsrow_ref[k] = acs2.T
        dtrow_ref[k] = dt[sl].T
        wrow_ref[k] = (dt[sl] * jnp.exp(acs[CHUNK - 1:CHUNK, :] - acs)).T
        for g in range(groups):
            shift = (LANES - g * r_heads) % LANES
            acscol_ref[g, sl, :] = pltpu.roll(acs2, shift, 1) if shift else acs2


def _dtprep(u, ssq, w_in_t, layer, dt0, bias, a_log, groups, r_heads):
    m, k = u.shape
    parts = ssq.shape[0]
    nc = 4 if m % (4 * CHUNK) == 0 else 1
    ts = nc * CHUNK
    nchunks = m // CHUNK
    rows = ((nc, LANES, LANES), F32)
    blocks = [((ts, k), BF16), ((parts, 1, ts), F32), ((LANES, k), F32), rows, rows, rows,
              ((groups, ts, LANES), F32)]
    row_spec = pl.BlockSpec((nc, LANES, LANES), lambda i: (i, 0, 0))
    row_shape = jax.ShapeDtypeStruct((nchunks, LANES, LANES), F32)
    return pl.pallas_call(
        functools.partial(_dtprep_kernel, nc=nc, groups=groups, r_heads=r_heads),
        grid=(m // ts,),
        in_specs=[pl.BlockSpec((ts, k), lambda i: (i, 0)),
                  pl.BlockSpec((parts, 1, ts), lambda i: (0, 0, i)),
                  pl.BlockSpec((None, LANES, k), lambda i: (layer, dt0 // LANES, 0)),
                  pl.BlockSpec((1, LANES), lambda i: (0, 0)),
                  pl.BlockSpec((1, LANES), lambda i: (0, 0))],
        out_specs=[row_spec, row_spec, row_spec,
                   pl.BlockSpec((groups, ts, LANES), lambda i: (0, i, 0))],
        out_shape=[row_shape, row_shape, row_shape,
                   jax.ShapeDtypeStruct((groups, m, LANES), F32)],
        compiler_params=_cparams(1, blocks, temps=8 * ts * LANES * 4 + k * LANES * 2),
        name="dt_prep",
    )(u, ssq, w_in_t, bias, a_log)


def _ssd_kernel(z_ref, xs_ref, b_ref, c_ref, xsp_ref, bp_ref, cp_ref,
                dtr_ref, wr_ref, acr_ref, acc_ref, shift_ref,
                cwx_ref, cwb_ref, cwc_ref, cbx_ref, cbb_ref, cbc_ref, dsk_ref, nw_ref,
                o_ref, st_ref, *, nc, r_heads, kconv):
    g = pl.program_id(1)
    c = pl.program_id(2)
    L = CHUNK
    N = SSD_STATE
    gw = r_heads * HEAD_DIM
    npair = gw // LANES

    @pl.when(c == 0)
    def _():
        st_ref[...] = jnp.zeros(st_ref.shape, F32)

    tri = (lax.broadcasted_iota(jnp.int32, (L, L), 0) >= lax.broadcasted_iota(jnp.int32, (L, L), 1))
    lo = lax.broadcasted_iota(jnp.int32, (L, LANES), 1) < HEAD_DIM
    hi = jnp.logical_not(lo)
    lo_row = lo[0:1]

    def conv_silu(ref, prev_ref, k, w_ref, bias_ref):
        if k == 0:
            prev = prev_ref[...]
            prev = jnp.where(c > 0, prev, jnp.zeros_like(prev))
            x2 = jnp.concatenate([prev, ref[0:L, :]], axis=0)
        else:
            x2 = ref[(k - 1) * L:(k + 1) * L, :]
        shifted = jnp.dot(shift_ref[...], x2, preferred_element_type=F32)
        acc = None
        for tap in range(kconv - 1):
            term = shifted[tap * L:(tap + 1) * L, :] * w_ref[tap:tap + 1, :]
            acc = term if acc is None else acc + term
        acc = acc + x2[L:2 * L].astype(F32) * w_ref[kconv - 1:kconv, :]
        acc = acc + bias_ref[...]
        return _silu(acc)

    for k in range(nc):
        r0 = k * L
        x = conv_silu(xs_ref, xsp_ref, k, cwx_ref, cbx_ref)
        bm = conv_silu(b_ref, bp_ref, k, cwb_ref, cbb_ref)
        cm = conv_silu(c_ref, cp_ref, k, cwc_ref, cbc_ref)
        cb = lax.dot_general(cm.astype(BF16), bm.astype(BF16), (((1,), (1,)), ((), ())),
                             preferred_element_type=F32)
        bt = bm.T
        acol = acc_ref[0, r0:r0 + L, :]
        cdec = jnp.exp2(acol[L - 1:L, :])
        ssq = jnp.zeros((L, 1), F32)
        ys = []
        for j in range(npair):
            cs = slice(j * LANES, (j + 1) * LANES)
            xp = x[:, cs]
            s_prev = st_ref[:, cs]
            s_new = s_prev * jnp.where(lo_row, cdec[:, 2 * j:2 * j + 1], cdec[:, 2 * j + 1:2 * j + 2])
            y = None
            for hh, keep in ((2 * j, lo), (2 * j + 1, hi)):
                head = pl.ds(g * r_heads + hh, 1)
                arow = acr_ref[k, head, :]
                drow = dtr_ref[k, head, :]
                wrow = wr_ref[k, head, :]
                xm = jnp.where(keep, xp, 0.0).astype(BF16)
                sm = jnp.where(keep, s_prev, 0.0).astype(BF16)
                ab = jnp.broadcast_to(acol[:, hh:hh + 1], (L, L))
                dec = jnp.exp2(jnp.where(tri, ab - arow, -jnp.inf))
                mh = cb * dec * drow
                ch = cm * jnp.exp2(ab)
                lhs = jnp.concatenate([mh, ch], axis=1).astype(BF16)
                rhs = jnp.concatenate([xm, sm], axis=0)
                t = jnp.dot(lhs, rhs, preferred_element_type=F32)
                y = t if y is None else y + t
                bth = (bt * wrow).astype(BF16)
                s_new = s_new + jnp.dot(bth, xm, preferred_element_type=F32)
            st_ref[:, cs] = s_new
            yt = y + dsk_ref[:, cs] * xp
            zt = z_ref[r0:r0 + L, cs].astype(F32)
            yt = yt * _silu(zt)
            ssq = ssq + jnp.sum(yt * yt, axis=-1, keepdims=True)
            ys.append(yt)
        rs = lax.rsqrt(ssq / gw + EPS)
        for j in range(npair):
            cs = slice(j * LANES, (j + 1) * LANES)
            o_ref[r0:r0 + L, cs] = (ys[j] * rs * nw_ref[:, cs]).astype(o_ref.dtype)


def _ssd(proj, dtrow, wrow, acsrow, acscol, conv_w, conv_b, d_skip, norm_w, *, batch, seq, inner, groups):
    m = proj.shape[0]
    gw = inner // groups
    r_heads = gw // HEAD_DIM
    assert gw % LANES == 0 and r_heads % 2 == 0 and SSD_STATE == LANES
    kconv = conv_w.shape[0]
    assert kconv - 1 <= CHUNK
    nc = _pick(seq // CHUNK, (4, 2, 1))
    ts = nc * CHUNK
    nt = seq // ts
    gn = groups * SSD_STATE
    xs_blk, b_blk, c_blk = inner // gw, 2 * inner // SSD_STATE, (2 * inner + gn) // SSD_STATE
    wb_blk, wc_blk = inner // SSD_STATE, (inner + gn) // SSD_STATE
    hp = acsrow.shape[1]
    t_idx = jnp.arange(CHUNK)[None, :, None]
    d_idx = (kconv - 1 - jnp.arange(kconv - 1))[:, None, None]
    s_idx = jnp.arange(2 * CHUNK)[None, None, :]
    shift = (s_idx == CHUNK + t_idx - d_idx).astype(BF16).reshape((kconv - 1) * CHUNK, 2 * CHUNK)
    st_shape = (SSD_STATE, gw)
    blocks = [((ts, gw), BF16), ((ts, gw), BF16), ((ts, SSD_STATE), BF16), ((ts, SSD_STATE), BF16),
              ((CHUNK, gw), BF16), ((CHUNK, SSD_STATE), BF16), ((CHUNK, SSD_STATE), BF16),
              ((nc, hp, LANES), F32), ((nc, hp, LANES), F32), ((nc, hp, LANES), F32),
              ((1, ts, LANES), F32), (shift.shape, BF16),
              ((kconv, gw), F32), ((kconv, SSD_STATE), F32), ((kconv, SSD_STATE), F32),
              ((1, gw), F32), ((1, SSD_STATE), F32), ((1, SSD_STATE), F32), ((1, gw), F32), ((1, gw), F32),
              ((ts, gw), BF16)]
    scratch = [(st_shape, F32)]
    row = lambda b, g, c: b * nt + c
    prev = lambda b, g, c: jnp.maximum(row(b, g, c) * nc - 1, 0)
    return pl.pallas_call(
        functools.partial(_ssd_kernel, nc=nc, r_heads=r_heads, kconv=kconv),
        grid=(batch, groups, nt),
        in_specs=[
            pl.BlockSpec((ts, gw), lambda b, g, c: (row(b, g, c), g)),
            pl.BlockSpec((ts, gw), lambda b, g, c: (row(b, g, c), xs_blk + g)),
            pl.BlockSpec((ts, SSD_STATE), lambda b, g, c: (row(b, g, c), b_blk + g)),
            pl.BlockSpec((ts, SSD_STATE), lambda b, g, c: (row(b, g, c), c_blk + g)),
            pl.BlockSpec((CHUNK, gw), lambda b, g, c: (prev(b, g, c), xs_blk + g)),
            pl.BlockSpec((CHUNK, SSD_STATE), lambda b, g, c: (prev(b, g, c), b_blk + g)),
            pl.BlockSpec((CHUNK, SSD_STATE), lambda b, g, c: (prev(b, g, c), c_blk + g)),
            pl.BlockSpec((nc, hp, LANES), lambda b, g, c: (row(b, g, c), 0, 0)),
            pl.BlockSpec((nc, hp, LANES), lambda b, g, c: (row(b, g, c), 0, 0)),
            pl.BlockSpec((nc, hp, LANES), lambda b, g, c: (row(b, g, c), 0, 0)),
            pl.BlockSpec((1, ts, LANES), lambda b, g, c: (g, row(b, g, c), 0)),
            pl.BlockSpec(shift.shape, lambda b, g, c: (0, 0)),
            pl.BlockSpec((kconv, gw), lambda b, g, c: (0, g)),
            pl.BlockSpec((kconv, SSD_STATE), lambda b, g, c: (0, wb_blk + g)),
            pl.BlockSpec((kconv, SSD_STATE), lambda b, g, c: (0, wc_blk + g)),
            pl.BlockSpec((1, gw), lambda b, g, c: (0, g)),
            pl.BlockSpec((1, SSD_STATE), lambda b, g, c: (0, wb_blk + g)),
            pl.BlockSpec((1, SSD_STATE), lambda b, g, c: (0, wc_blk + g)),
            pl.BlockSpec((1, gw), lambda b, g, c: (0, g)),
            pl.BlockSpec((1, gw), lambda b, g, c: (0, g)),
        ],
        out_specs=pl.BlockSpec((ts, gw), lambda b, g, c: (row(b, g, c), g)),
        out_shape=jax.ShapeDtypeStruct((m, inner), BF16),
        scratch_shapes=[pltpu.VMEM(s, d) for s, d in scratch],
        compiler_params=_cparams(3, blocks, scratch, temps=8 << 20),
        name="ssd",
    )(proj, proj, proj, proj, proj, proj, proj, dtrow, wrow, acsrow, acscol, shift,
      conv_w, conv_w, conv_w, conv_b, conv_b, conv_b, d_skip, norm_w)


def _sconv_kernel(gb_ref, gc_ref, xt_ref, w_ref, o_ref, pe_ref, *, ts, kconv, cw):
    halo = SUBLANES

    @pl.when(pl.program_id(1) == 0)
    def _():
        pe_ref[0:halo, :] = jnp.zeros((halo, pe_ref.shape[1]), F32)

    pe_ref[halo:halo + ts, :] = gb_ref[...].astype(F32) * xt_ref[...].astype(F32)
    width = pe_ref.shape[1]
    for c0 in range(0, width, cw):
        acc = None
        for k in range(kconv):
            r = halo - (kconv - 1) + k
            term = pe_ref[r:r + ts, c0:c0 + cw] * w_ref[k:k + 1, c0:c0 + cw]
            acc = term if acc is None else acc + term
        o_ref[:, c0:c0 + cw] = (gc_ref[:, c0:c0 + cw].astype(F32) * acc).astype(o_ref.dtype)
    pe_ref[0:halo, :] = pe_ref[ts:ts + halo, :]


def _sconv(proj, conv_w, *, batch, seq, width, off_b, off_c, off_x):
    m = proj.shape[0]
    kconv = conv_w.shape[0]
    ts = _pick(seq, (256, 128))
    nt = seq // ts
    assert off_b % width == 0 and off_c % width == 0 and off_x % width == 0 and kconv - 1 <= SUBLANES
    jb, jc, jx = off_b // width, off_c // width, off_x // width
    cw = _pick(width, (512, 256, 128))
    pe_shape = (SUBLANES + ts, width)
    blocks = [((ts, width), BF16)] * 4 + [((kconv, width), F32)]
    scratch = [(pe_shape, F32)]
    return pl.pallas_call(
        functools.partial(_sconv_kernel, ts=ts, kconv=kconv, cw=cw),
        grid=(batch, nt),
        in_specs=[pl.BlockSpec((ts, width), lambda b, c: (b * nt + c, jb)),
                  pl.BlockSpec((ts, width), lambda b, c: (b * nt + c, jc)),
                  pl.BlockSpec((ts, width), lambda b, c: (b * nt + c, jx)),
                  pl.BlockSpec((kconv, width), lambda b, c: (0, 0))],
        out_specs=pl.BlockSpec((ts, width), lambda b, c: (b * nt + c, 0)),
        out_shape=jax.ShapeDtypeStruct((m, width), BF16),
        scratch_shapes=[pltpu.VMEM(s, d) for s, d in scratch],
        compiler_params=_cparams(2, blocks, scratch, temps=4 << 20),
        name="short_conv",
    )(proj, proj, proj, conv_w)


def kernel(x, p, norm_mix, w_in, ssd_conv_w, ssd_conv_b, ssd_dt_bias, ssd_a_log, ssd_d, ssd_norm,
           ssd_out, sc_conv_w, sc_out, w_o, norm_ffn, w_gate_up, w_down, norm_ple, ple_gate,
           ple_proj, norm_final):
    batch, seq, d = x.shape
    depth = w_in.shape[0]
    m = batch * seq
    heads = ssd_a_log.shape[1]
    inner = heads * HEAD_DIM
    groups = SSD_GROUPS
    r_heads = heads // groups
    xbc = ssd_conv_w.shape[2]
    scw = sc_conv_w.shape[2]
    assert xbc == inner + 2 * groups * SSD_STATE and heads <= LANES
    dt0 = inner + xbc
    off_scb = dt0
    off_scc, off_scx = off_scb + scw, off_scb + 2 * scw
    off_ga = off_scb + 3 * scw
    off_gb = off_ga + d
    hpad = LANES - heads
    p2 = p.reshape(depth, m, p.shape[-1])
    w_in_t = jnp.swapaxes(w_in, 1, 2)

    h = x.reshape(m, d)
    u, u_ssq = _prenorm(h, norm_mix[0])
    for i in range(depth):
        dt_bias = jnp.pad(ssd_dt_bias[i], (0, hpad)).reshape(1, LANES)
        a_log = jnp.pad(ssd_a_log[i], (0, hpad)).reshape(1, LANES)
        d_skip = jnp.repeat(ssd_d[i], HEAD_DIM).reshape(1, inner)

        proj = _inproj(u, u_ssq, w_in_t, i, dt0, heads)
        dtrow, wrow, acsrow, acscol = _dtprep(u, u_ssq, w_in_t, i, dt0, dt_bias, a_log, groups, r_heads)
        y = _ssd(proj, dtrow, wrow, acsrow, acscol, ssd_conv_w[i], ssd_conv_b[i].reshape(1, xbc), d_skip,
                 ssd_norm[i].reshape(1, inner), batch=batch, seq=seq, inner=inner, groups=groups)
        ysc = _sconv(proj, sc_conv_w[i], batch=batch, seq=seq, width=scw,
                     off_b=off_scb, off_c=off_scc, off_x=off_scx)
        merged = _merge(y, ysc, ssd_out, sc_out, i, proj, off_ga, off_gb)
        h, v, v_ssq = _matmul_residual(merged, w_o, i, h, norm_ffn[i], "w_o_residual", in_place=i > 0)
        act = _swiglu(v, v_ssq, w_gate_up, i)
        h, hn, hn_ssq = _matmul_residual(act, w_down, i, h, norm_ple[i], "w_down_residual")
        g_next = norm_mix[i + 1] if i + 1 < depth else norm_final
        h, u, u_ssq = _ple(hn, hn_ssq, ple_gate, p2, ple_proj, i, h, g_next)
    out = _rmsnorm(h, norm_final, F32)
    return out.reshape(batch, seq, d)
```

```python
import functools

import jax
import jax.numpy as jnp
from jax import lax
from jax.experimental import pallas as pl
from jax.experimental.pallas import tpu as pltpu

F32 = jnp.float32
BF16 = jnp.bfloat16

EPS = 1e-6
HEAD_DIM = 64
SSD_GROUPS = 8
SSD_STATE = 128
CHUNK = 128
LANES = 128
SUBLANES = 8
CAST_ROWS = 256
LOG2E = 1.4426950408889634
VMEM_BYTES_V7X = 64 * 1024 * 1024
VMEM_CAP = VMEM_BYTES_V7X - 8 * 1024 * 1024


def _nbytes(shape, dtype):
    n = 1
    for s in shape:
        n *= s
    return n * jnp.dtype(dtype).itemsize


def _cparams(ngrid, blocks, scratch=(), temps=0):
    need = (2 * sum(_nbytes(s, d) for s, d in blocks) + sum(_nbytes(s, d) for s, d in scratch)
            + temps + (4 << 20))
    return pltpu.CompilerParams(dimension_semantics=("arbitrary",) * ngrid,
                                vmem_limit_bytes=int(min(max(need, 16 << 20), VMEM_CAP)))


def _pick(n, candidates):
    for c in candidates:
        if n % c == 0:
            return c
    raise ValueError(f"no block size in {candidates} divides {n}")


def _cast_tile(w_ref, wsc_ref):
    k = w_ref.shape[0]
    rc = _pick(k, (CAST_ROWS, LANES, SUBLANES))
    for r in range(0, k, rc):
        wsc_ref[r:r + rc, :] = w_ref[r:r + rc, :].astype(BF16)


def _first_token_tile():
    return pl.program_id(1) == 0


def _sigmoid(x):
    return 0.5 * jnp.tanh(0.5 * x) + 0.5


def _silu(x):
    h = 0.5 * x
    return h * jnp.tanh(h) + h


def _emit_scaled(h_new, g_ref, hg_ref, ssq_ref):
    hg_ref[...] = (h_new * g_ref[...]).astype(BF16)
    col = jnp.sum(h_new * h_new, axis=-1, keepdims=True)
    ssq_ref[...] = jnp.transpose(jnp.broadcast_to(col, (col.shape[0], LANES)))[0:1, :]


def _row_rsqrt(ssq_ref, d):
    s = ssq_ref[0]
    for q in range(1, ssq_ref.shape[0]):
        s = s + ssq_ref[q]
    row = lax.rsqrt(s / d + EPS)
    return jnp.transpose(jnp.broadcast_to(row, (LANES, row.shape[1])))[:, 0:1]


def _prenorm_kernel(x_ref, g_ref, hg_ref, ssq_ref):
    _emit_scaled(x_ref[...], g_ref, hg_ref, ssq_ref.at[0])


def _prenorm(x, g):
    m, d = x.shape
    bm = _pick(m, (512, 256, 128))
    blocks = [((bm, d), F32), ((1, d), F32), ((bm, d), BF16), ((1, 1, bm), F32)]
    return pl.pallas_call(
        _prenorm_kernel,
        grid=(m // bm,),
        in_specs=[pl.BlockSpec((bm, d), lambda i: (i, 0)),
                  pl.BlockSpec((1, d), lambda i: (0, 0))],
        out_specs=[pl.BlockSpec((bm, d), lambda i: (i, 0)),
                   pl.BlockSpec((1, 1, bm), lambda i: (0, 0, i))],
        out_shape=[jax.ShapeDtypeStruct((m, d), BF16), jax.ShapeDtypeStruct((1, 1, m), F32)],
        compiler_params=_cparams(1, blocks, temps=2 * bm * d * 4),
        name="prenorm",
    )(x, g.reshape(1, d))


def _rmsnorm_kernel(x_ref, g_ref, o_ref):
    x = x_ref[...]
    ms = jnp.mean(x * x, axis=-1, keepdims=True)
    o_ref[...] = (x * lax.rsqrt(ms + EPS) * g_ref[...]).astype(o_ref.dtype)


def _rmsnorm(x, g, out_dtype):
    m, d = x.shape
    bm = _pick(m, (512, 256, 128))
    blocks = [((bm, d), F32), ((1, d), F32), ((bm, d), out_dtype)]
    return pl.pallas_call(
        _rmsnorm_kernel,
        grid=(m // bm,),
        in_specs=[pl.BlockSpec((bm, d), lambda i: (i, 0)),
                  pl.BlockSpec((1, d), lambda i: (0, 0))],
        out_specs=pl.BlockSpec((bm, d), lambda i: (i, 0)),
        out_shape=jax.ShapeDtypeStruct((m, d), out_dtype),
        compiler_params=_cparams(1, blocks, temps=2 * bm * d * 4),
        name="rmsnorm",
    )(x, g.reshape(1, d))


def _inproj_kernel(a_ref, ssq_ref, w_ref, wn_ref, o_ref, wsc_ref, *, n_aligned, skip):
    j = pl.program_id(0)
    bn = w_ref.shape[0]
    rc = _pick(bn, (CAST_ROWS, LANES))

    @pl.when(jnp.logical_and(_first_token_tile(), j < n_aligned))
    def _():
        _cast_tile(w_ref, wsc_ref)

    @pl.when(jnp.logical_and(_first_token_tile(), j >= n_aligned))
    def _():
        for r in range(0, bn - rc, rc):
            wsc_ref[r:r + rc, :] = w_ref[r + skip:r + skip + rc, :].astype(BF16)
        wsc_ref[bn - rc:bn - skip, :] = w_ref[bn - rc + skip:bn, :].astype(BF16)
        wsc_ref[bn - skip:bn, :] = wn_ref[...].astype(BF16)

    acc = lax.dot_general(a_ref[...], wsc_ref[...], (((1,), (1,)), ((), ())), preferred_element_type=F32)
    o_ref[...] = (acc * _row_rsqrt(ssq_ref, a_ref.shape[1])).astype(o_ref.dtype)


def _inproj(u, ssq, w_in_t, layer, dt0, heads):
    m, k = u.shape
    parts = ssq.shape[0]
    d_in = w_in_t.shape[1]
    n = d_in - heads
    bm = _pick(m, (1024, 512, 256, 128))
    bn = next(c for c in (1024, 512, 256, 128) if dt0 % c == 0 and (n - dt0) % c == 0)
    assert heads % (2 * SUBLANES) == 0 and bn % heads == 0 and d_in % heads == 0 and heads < LANES
    n_aligned = dt0 // bn
    blocks = [((bm, k), BF16), ((parts, 1, bm), F32), ((bn, k), F32), ((heads, k), F32), ((bm, bn), BF16)]
    scratch = [((bn, k), BF16)]
    return pl.pallas_call(
        functools.partial(_inproj_kernel, n_aligned=n_aligned, skip=heads),
        grid=(n // bn, m // bm),
        in_specs=[pl.BlockSpec((bm, k), lambda j, i: (i, 0)),
                  pl.BlockSpec((parts, 1, bm), lambda j, i: (0, 0, i)),
                  pl.BlockSpec((None, bn, k), lambda j, i: (layer, j, 0)),
                  pl.BlockSpec((None, heads, k), lambda j, i: (layer, (j + 1) * (bn // heads), 0))],
        out_specs=pl.BlockSpec((bm, bn), lambda j, i: (i, j)),
        out_shape=jax.ShapeDtypeStruct((m, n), BF16),
        scratch_shapes=[pltpu.VMEM(s, d) for s, d in scratch],
        compiler_params=_cparams(2, blocks, scratch, temps=bm * bn * 4 + 2 * CAST_ROWS * k * 4),
        name="in_proj",
    )(u, ssq, w_in_t, w_in_t)


def _mm_res_kernel(a_ref, w_ref, h_ref, g_ref, o_ref, hg_ref, ssq_ref, wsc_ref):
    @pl.when(_first_token_tile())
    def _():
        _cast_tile(w_ref, wsc_ref)

    h_new = h_ref[...] + jnp.dot(a_ref[...], wsc_ref[...], preferred_element_type=F32)
    o_ref[...] = h_new
    _emit_scaled(h_new, g_ref, hg_ref, ssq_ref)


def _matmul_residual(a, w, layer, h, g_next, name, in_place=True):
    m, k = a.shape
    n = w.shape[2]
    big_k = k > 2048
    bm = _pick(m, (512, 256, 128))
    bn = _pick(n, (512, 256, 128)) if big_k else _pick(n, (1024, 512, 256, 128))
    blocks = [((bm, k), BF16), ((k, bn), F32), ((bm, bn), F32), ((1, bn), F32),
              ((bm, bn), F32), ((bm, bn), BF16), ((1, bm), F32)]
    scratch = [((k, bn), BF16)]
    return pl.pallas_call(
        _mm_res_kernel,
        grid=(n // bn, m // bm),
        in_specs=[pl.BlockSpec((bm, k), lambda j, i: (i, 0)),
                  pl.BlockSpec((None, k, bn), lambda j, i: (layer, 0, j)),
                  pl.BlockSpec((bm, bn), lambda j, i: (i, j)),
                  pl.BlockSpec((1, bn), lambda j, i: (0, j))],
        out_specs=[pl.BlockSpec((bm, bn), lambda j, i: (i, j)),
                   pl.BlockSpec((bm, bn), lambda j, i: (i, j)),
                   pl.BlockSpec((None, 1, bm), lambda j, i: (j, 0, i))],
        out_shape=[jax.ShapeDtypeStruct((m, n), F32), jax.ShapeDtypeStruct((m, n), BF16),
                   jax.ShapeDtypeStruct((n // bn, 1, m), F32)],
        scratch_shapes=[pltpu.VMEM(s, d) for s, d in scratch],
        input_output_aliases={2: 0} if in_place else {},
        compiler_params=_cparams(2, blocks, scratch, temps=2 * bm * bn * 4 + 2 * CAST_ROWS * bn * 4),
        name=name,
    )(a, w, h, g_next.reshape(1, n))


def _swiglu_kernel(v_ref, ssq_ref, wg_ref, wu_ref, o_ref, wgsc_ref, wusc_ref):
    @pl.when(_first_token_tile())
    def _():
        _cast_tile(wg_ref, wgsc_ref)
        _cast_tile(wu_ref, wusc_ref)

    v = v_ref[...]
    rs = _row_rsqrt(ssq_ref, v.shape[1])
    gate = jnp.dot(v, wgsc_ref[...], preferred_element_type=F32) * rs
    up = jnp.dot(v, wusc_ref[...], preferred_element_type=F32) * rs
    o_ref[...] = (_silu(gate) * up).astype(o_ref.dtype)


def _swiglu(v, ssq, w_gate_up, layer):
    m, k = v.shape
    parts = ssq.shape[0]
    d_ff = w_gate_up.shape[2] // 2
    bm = _pick(m, (1024, 512, 256, 128))
    bn = _pick(d_ff, (512, 256, 128))
    nb = d_ff // bn
    blocks = [((bm, k), BF16), ((parts, 1, bm), F32), ((k, bn), F32), ((k, bn), F32), ((bm, bn), BF16)]
    scratch = [((k, bn), BF16), ((k, bn), BF16)]
    return pl.pallas_call(
        _swiglu_kernel,
        grid=(nb, m // bm),
        in_specs=[pl.BlockSpec((bm, k), lambda j, i: (i, 0)),
                  pl.BlockSpec((parts, 1, bm), lambda j, i: (0, 0, i)),
                  pl.BlockSpec((None, k, bn), lambda j, i: (layer, 0, j)),
                  pl.BlockSpec((None, k, bn), lambda j, i: (layer, 0, j + nb))],
        out_specs=pl.BlockSpec((bm, bn), lambda j, i: (i, j)),
        out_shape=jax.ShapeDtypeStruct((m, d_ff), BF16),
        scratch_shapes=[pltpu.VMEM(s, d) for s, d in scratch],
        compiler_params=_cparams(2, blocks, scratch, temps=3 * bm * bn * 4 + 2 * CAST_ROWS * bn * 4),
        name="swiglu_up",
    )(v, ssq, w_gate_up, w_gate_up)


def _merge_kernel(y_ref, ysc_ref, wa_ref, wb_ref, ga_ref, gb_ref, o_ref, wasc_ref, wbsc_ref):
    @pl.when(_first_token_tile())
    def _():
        _cast_tile(wa_ref, wasc_ref)
        _cast_tile(wb_ref, wbsc_ref)

    ya = jnp.dot(y_ref[...], wasc_ref[...], preferred_element_type=F32)
    yb = jnp.dot(ysc_ref[...], wbsc_ref[...], preferred_element_type=F32)
    ga = _sigmoid(ga_ref[...].astype(F32))
    gb = _sigmoid(gb_ref[...].astype(F32))
    o_ref[...] = (ga * ya + gb * yb).astype(o_ref.dtype)


def _merge(y, ysc, w_a, w_b, layer, proj, off_ga, off_gb):
    m, ka = y.shape
    kb = ysc.shape[1]
    n = w_a.shape[2]
    bm = _pick(m, (512, 256, 128))
    bn = _pick(n, (512, 256, 128))
    ja, jb = off_ga // bn, off_gb // bn
    assert off_ga % bn == 0 and off_gb % bn == 0
    blocks = [((bm, ka), BF16), ((bm, kb), BF16), ((ka, bn), F32), ((kb, bn), F32),
              ((bm, bn), BF16), ((bm, bn), BF16), ((bm, bn), BF16)]
    scratch = [((ka, bn), BF16), ((kb, bn), BF16)]
    return pl.pallas_call(
        _merge_kernel,
        grid=(n // bn, m // bm),
        in_specs=[pl.BlockSpec((bm, ka), lambda j, i: (i, 0)),
                  pl.BlockSpec((bm, kb), lambda j, i: (i, 0)),
                  pl.BlockSpec((None, ka, bn), lambda j, i: (layer, 0, j)),
                  pl.BlockSpec((None, kb, bn), lambda j, i: (layer, 0, j)),
                  pl.BlockSpec((bm, bn), lambda j, i: (i, j + ja)),
                  pl.BlockSpec((bm, bn), lambda j, i: (i, j + jb))],
        out_specs=pl.BlockSpec((bm, bn), lambda j, i: (i, j)),
        out_shape=jax.ShapeDtypeStruct((m, n), BF16),
        scratch_shapes=[pltpu.VMEM(s, d) for s, d in scratch],
        compiler_params=_cparams(2, blocks, scratch, temps=4 * bm * bn * 4 + 2 * CAST_ROWS * bn * 4),
        name="branch_merge",
    )(y, ysc, w_a, w_b, proj, proj)


def _ple_kernel(hn_ref, ssq_ref, wg_ref, p_ref, wp_ref, h_ref, g_ref, o_ref, hg_ref, ssqo_ref,
                wgsc_ref, wpsc_ref):
    @pl.when(_first_token_tile())
    def _():
        _cast_tile(wg_ref, wgsc_ref)
        _cast_tile(wp_ref, wpsc_ref)

    rs = _row_rsqrt(ssq_ref, hn_ref.shape[1])
    pg = _sigmoid(jnp.dot(hn_ref[...], wgsc_ref[...], preferred_element_type=F32) * rs)
    e = jnp.dot(p_ref[...].astype(BF16), wpsc_ref[...], preferred_element_type=F32)
    h_new = h_ref[...] + pg * e
    o_ref[...] = h_new
    _emit_scaled(h_new, g_ref, hg_ref, ssqo_ref)


def _ple(hn, ssq, w_gate, p, w_proj, layer, h, g_next):
    m, k = hn.shape
    parts = ssq.shape[0]
    kp = p.shape[2]
    n = w_gate.shape[2]
    bm = _pick(m, (512, 256, 128))
    bn = _pick(n, (1024, 512, 256, 128))
    blocks = [((bm, k), BF16), ((parts, 1, bm), F32), ((k, bn), F32), ((bm, kp), F32), ((kp, bn), F32),
              ((bm, bn), F32), ((1, bn), F32), ((bm, bn), F32), ((bm, bn), BF16), ((1, bm), F32)]
    scratch = [((k, bn), BF16), ((kp, bn), BF16)]
    return pl.pallas_call(
        _ple_kernel,
        grid=(n // bn, m // bm),
        in_specs=[pl.BlockSpec((bm, k), lambda j, i: (i, 0)),
                  pl.BlockSpec((parts, 1, bm), lambda j, i: (0, 0, i)),
                  pl.BlockSpec((None, k, bn), lambda j, i: (layer, 0, j)),
                  pl.BlockSpec((None, bm, kp), lambda j, i: (layer, i, 0)),
                  pl.BlockSpec((None, kp, bn), lambda j, i: (layer, 0, j)),
                  pl.BlockSpec((bm, bn), lambda j, i: (i, j)),
                  pl.BlockSpec((1, bn), lambda j, i: (0, j))],
        out_specs=[pl.BlockSpec((bm, bn), lambda j, i: (i, j)),
                   pl.BlockSpec((bm, bn), lambda j, i: (i, j)),
                   pl.BlockSpec((None, 1, bm), lambda j, i: (j, 0, i))],
        out_shape=[jax.ShapeDtypeStruct((m, n), F32), jax.ShapeDtypeStruct((m, n), BF16),
                   jax.ShapeDtypeStruct((n // bn, 1, m), F32)],
        scratch_shapes=[pltpu.VMEM(s, d) for s, d in scratch],
        input_output_aliases={5: 0},
        compiler_params=_cparams(2, blocks, scratch, temps=4 * bm * bn * 4 + 2 * CAST_ROWS * bn * 4),
        name="ple",
    )(hn, ssq, w_gate, p, w_proj, h, g_next.reshape(1, n))


def _dtprep_kernel(u_ref, ssq_ref, w_ref, bias_ref, alog_ref, dtrow_ref, wrow_ref, acsrow_ref, acscol_ref,
                   *, nc, groups, r_heads):
    x = lax.dot_general(u_ref[...], w_ref[...].astype(BF16), (((1,), (1,)), ((), ())),
                        preferred_element_type=F32)
    x = x * _row_rsqrt(ssq_ref, u_ref.shape[1]) + bias_ref[...]
    dt = jnp.maximum(x, 0.0) + jnp.log1p(jnp.exp(-jnp.abs(x)))
    adt = dt * (-jnp.exp(alog_ref[...]))
    row = lax.broadcasted_iota(jnp.int32, (CHUNK, LANES), 0)
    for k in range(nc):
        sl = slice(k * CHUNK, (k + 1) * CHUNK)
        acs = adt[sl]
        sh = 1
        while sh < CHUNK:
            acs = acs + jnp.where(row >= sh, pltpu.roll(acs, sh, 0), 0.0)
            sh *= 2
        acs2 = acs * LOG2E
        acsrow_ref[k] = acs2.T
        dtrow_ref[k] = dt[sl].T
        wrow_ref[k] = (dt[sl] * jnp.exp(acs[CHUNK - 1:CHUNK, :] - acs)).T
        for g in range(groups):
            shift = (LANES - g * r_heads) % LANES
            acscol_ref[g, sl, :] = pltpu.roll(acs2, shift, 1) if shift else acs2


def _dtprep(u, ssq, w_in_t, layer, dt0, bias, a_log, groups, r_heads):
    m, k = u.shape
    parts = ssq.shape[0]
    nc = 4 if m % (4 * CHUNK) == 0 else 1
    ts = nc * CHUNK
    nchunks = m // CHUNK
    rows = ((nc, LANES, LANES), F32)
    blocks = [((ts, k), BF16), ((parts, 1, ts), F32), ((LANES, k), F32), rows, rows, rows,
              ((groups, ts, LANES), F32)]
    row_spec = pl.BlockSpec((nc, LANES, LANES), lambda i: (i, 0, 0))
    row_shape = jax.ShapeDtypeStruct((nchunks, LANES, LANES), F32)
    return pl.pallas_call(
        functools.partial(_dtprep_kernel, nc=nc, groups=groups, r_heads=r_heads),
        grid=(m // ts,),
        in_specs=[pl.BlockSpec((ts, k), lambda i: (i, 0)),
                  pl.BlockSpec((parts, 1, ts), lambda i: (0, 0, i)),
                  pl.BlockSpec((None, LANES, k), lambda i: (layer, dt0 // LANES, 0)),
                  pl.BlockSpec((1, LANES), lambda i: (0, 0)),
                  pl.BlockSpec((1, LANES), lambda i: (0, 0))],
        out_specs=[row_spec, row_spec, row_spec,
                   pl.BlockSpec((groups, ts, LANES), lambda i: (0, i, 0))],
        out_shape=[row_shape, row_shape, row_shape,
                   jax.ShapeDtypeStruct((groups, m, LANES), F32)],
        compiler_params=_cparams(1, blocks, temps=8 * ts * LANES * 4 + k * LANES * 2),
        name="dt_prep",
    )(u, ssq, w_in_t, bias, a_log)


def _ssd_kernel(z_ref, xs_ref, b_ref, c_ref, xsp_ref, bp_ref, cp_ref,
                dtr_ref, wr_ref, acr_ref, acc_ref, shift_ref,
                cwx_ref, cwb_ref, cwc_ref, cbx_ref, cbb_ref, cbc_ref, dsk_ref, nw_ref,
                o_ref, st_ref, yt_ref, *, nc, r_heads, kconv):
    g = pl.program_id(1)
    c = pl.program_id(2)
    L = CHUNK
    N = SSD_STATE
    gw = r_heads * HEAD_DIM
    npair = gw // LANES

    @pl.when(c == 0)
    def _():
        st_ref[...] = jnp.zeros(st_ref.shape, F32)

    tri = (lax.broadcasted_iota(jnp.int32, (L, L), 0) >= lax.broadcasted_iota(jnp.int32, (L, L), 1))
    lo = lax.broadcasted_iota(jnp.int32, (L, LANES), 1) < HEAD_DIM
    hi = jnp.logical_not(lo)
    lo_row = lo[0:1]

    def two_chunks(ref, prev_ref, k):
        if k == 0:
            prev = prev_ref[...]
            prev = jnp.where(c > 0, prev, jnp.zeros_like(prev))
            return jnp.concatenate([prev, ref[0:L, :]], axis=0)
        return ref[(k - 1) * L:(k + 1) * L, :]

    def conv_silu(shifted, cur, w_ref, bias_ref):
        acc = None
        for tap in range(kconv - 1):
            term = shifted[tap * L:(tap + 1) * L, :] * w_ref[tap:tap + 1, :]
            acc = term if acc is None else acc + term
        acc = acc + cur.astype(F32) * w_ref[kconv - 1:kconv, :]
        acc = acc + bias_ref[...]
        return _silu(acc)

    for k in range(nc):
        r0 = k * L
        x2 = jnp.concatenate([two_chunks(xs_ref, xsp_ref, k), two_chunks(b_ref, bp_ref, k),
                              two_chunks(c_ref, cp_ref, k)], axis=1)
        shifted = jnp.dot(shift_ref[...], x2, preferred_element_type=F32)
        cur = x2[L:2 * L]
        x = conv_silu(shifted[:, 0:gw], cur[:, 0:gw], cwx_ref, cbx_ref)
        bm = conv_silu(shifted[:, gw:gw + N], cur[:, gw:gw + N], cwb_ref, cbb_ref)
        cm = conv_silu(shifted[:, gw + N:gw + 2 * N], cur[:, gw + N:gw + 2 * N], cwc_ref, cbc_ref)
        cb = lax.dot_general(cm.astype(BF16), bm.astype(BF16), (((1,), (1,)), ((), ())),
                             preferred_element_type=F32)
        bt = bm.T
        acol = acc_ref[0, r0:r0 + L, :]
        cdec = jnp.exp2(acol[L - 1:L, :])
        ssq = jnp.zeros((L, 1), F32)
        for j in range(npair):
            cs = slice(j * LANES, (j + 1) * LANES)
            xp = x[:, cs]
            s_prev = st_ref[:, cs]
            s_new = s_prev * jnp.where(lo_row, cdec[:, 2 * j:2 * j + 1], cdec[:, 2 * j + 1:2 * j + 2])
            y = None
            for hh, keep in ((2 * j, lo), (2 * j + 1, hi)):
                head = pl.ds(g * r_heads + hh, 1)
                arow = acr_ref[k, head, :]
                drow = dtr_ref[k, head, :]
                wrow = wr_ref[k, head, :]
                xm = jnp.where(keep, xp, 0.0).astype(BF16)
                sm = jnp.where(keep, s_prev, 0.0).astype(BF16)
                ab = jnp.broadcast_to(acol[:, hh:hh + 1], (L, L))
                dec = jnp.exp2(jnp.where(tri, ab - arow, -jnp.inf))
                mh = cb * dec * drow
                ch = cm * jnp.exp2(ab)
                lhs = jnp.concatenate([mh, ch], axis=1).astype(BF16)
                rhs = jnp.concatenate([xm, sm], axis=0)
                t = jnp.dot(lhs, rhs, preferred_element_type=F32)
                y = t if y is None else y + t
                bth = (bt * wrow).astype(BF16)
                s_new = s_new + jnp.dot(bth, xm, preferred_element_type=F32)
            st_ref[:, cs] = s_new
            yt = y + dsk_ref[:, cs] * xp
            zt = z_ref[r0:r0 + L, cs].astype(F32)
            yt = yt * _silu(zt)
            ssq = ssq + jnp.sum(yt * yt, axis=-1, keepdims=True)
            yt_ref[:, cs] = yt
        rs = lax.rsqrt(ssq / gw + EPS)
        for j in range(npair):
            cs = slice(j * LANES, (j + 1) * LANES)
            o_ref[r0:r0 + L, cs] = (yt_ref[:, cs] * rs * nw_ref[:, cs]).astype(o_ref.dtype)


def _ssd(proj, dtrow, wrow, acsrow, acscol, conv_w, conv_b, d_skip, norm_w, *, batch, seq, inner, groups):
    m = proj.shape[0]
    gw = inner // groups
    r_heads = gw // HEAD_DIM
    assert gw % LANES == 0 and r_heads % 2 == 0 and SSD_STATE == LANES
    kconv = conv_w.shape[0]
    assert kconv - 1 <= CHUNK
    nc = _pick(seq // CHUNK, (4, 2, 1))
    ts = nc * CHUNK
    nt = seq // ts
    gn = groups * SSD_STATE
    xs_blk, b_blk, c_blk = inner // gw, 2 * inner // SSD_STATE, (2 * inner + gn) // SSD_STATE
    wb_blk, wc_blk = inner // SSD_STATE, (inner + gn) // SSD_STATE
    hp = acsrow.shape[1]
    t_idx = jnp.arange(CHUNK)[None, :, None]
    d_idx = (kconv - 1 - jnp.arange(kconv - 1))[:, None, None]
    s_idx = jnp.arange(2 * CHUNK)[None, None, :]
    shift = (s_idx == CHUNK + t_idx - d_idx).astype(BF16).reshape((kconv - 1) * CHUNK, 2 * CHUNK)
    st_shape = (SSD_STATE, gw)
    blocks = [((ts, gw), BF16), ((ts, gw), BF16), ((ts, SSD_STATE), BF16), ((ts, SSD_STATE), BF16),
              ((CHUNK, gw), BF16), ((CHUNK, SSD_STATE), BF16), ((CHUNK, SSD_STATE), BF16),
              ((nc, hp, LANES), F32), ((nc, hp, LANES), F32), ((nc, hp, LANES), F32),
              ((1, ts, LANES), F32), (shift.shape, BF16),
              ((kconv, gw), F32), ((kconv, SSD_STATE), F32), ((kconv, SSD_STATE), F32),
              ((1, gw), F32), ((1, SSD_STATE), F32), ((1, SSD_STATE), F32), ((1, gw), F32), ((1, gw), F32),
              ((ts, gw), BF16)]
    scratch = [(st_shape, F32), ((CHUNK, gw), F32)]
    row = lambda b, g, c: b * nt + c
    prev = lambda b, g, c: jnp.maximum(row(b, g, c) * nc - 1, 0)
    return pl.pallas_call(
        functools.partial(_ssd_kernel, nc=nc, r_heads=r_heads, kconv=kconv),
        grid=(batch, groups, nt),
        in_specs=[
            pl.BlockSpec((ts, gw), lambda b, g, c: (row(b, g, c), g)),
            pl.BlockSpec((ts, gw), lambda b, g, c: (row(b, g, c), xs_blk + g)),
            pl.BlockSpec((ts, SSD_STATE), lambda b, g, c: (row(b, g, c), b_blk + g)),
            pl.BlockSpec((ts, SSD_STATE), lambda b, g, c: (row(b, g, c), c_blk + g)),
            pl.BlockSpec((CHUNK, gw), lambda b, g, c: (prev(b, g, c), xs_blk + g)),
            pl.BlockSpec((CHUNK, SSD_STATE), lambda b, g, c: (prev(b, g, c), b_blk + g)),
            pl.BlockSpec((CHUNK, SSD_STATE), lambda b, g, c: (prev(b, g, c), c_blk + g)),
            pl.BlockSpec((nc, hp, LANES), lambda b, g, c: (row(b, g, c), 0, 0)),
            pl.BlockSpec((nc, hp, LANES), lambda b, g, c: (row(b, g, c), 0, 0)),
            pl.BlockSpec((nc, hp, LANES), lambda b, g, c: (row(b, g, c), 0, 0)),
            pl.BlockSpec((1, ts, LANES), lambda b, g, c: (g, row(b, g, c), 0)),
            pl.BlockSpec(shift.shape, lambda b, g, c: (0, 0)),
            pl.BlockSpec((kconv, gw), lambda b, g, c: (0, g)),
            pl.BlockSpec((kconv, SSD_STATE), lambda b, g, c: (0, wb_blk + g)),
            pl.BlockSpec((kconv, SSD_STATE), lambda b, g, c: (0, wc_blk + g)),
            pl.BlockSpec((1, gw), lambda b, g, c: (0, g)),
            pl.BlockSpec((1, SSD_STATE), lambda b, g, c: (0, wb_blk + g)),
            pl.BlockSpec((1, SSD_STATE), lambda b, g, c: (0, wc_blk + g)),
            pl.BlockSpec((1, gw), lambda b, g, c: (0, g)),
            pl.BlockSpec((1, gw), lambda b, g, c: (0, g)),
        ],
        out_specs=pl.BlockSpec((ts, gw), lambda b, g, c: (row(b, g, c), g)),
        out_shape=jax.ShapeDtypeStruct((m, inner), BF16),
        scratch_shapes=[pltpu.VMEM(s, d) for s, d in scratch],
        compiler_params=_cparams(3, blocks, scratch, temps=8 << 20),
        name="ssd",
    )(proj, proj, proj, proj, proj, proj, proj, dtrow, wrow, acsrow, acscol, shift,
      conv_w, conv_w, conv_w, conv_b, conv_b, conv_b, d_skip, norm_w)


def _sconv_kernel(gb_ref, gc_ref, xt_ref, w_ref, o_ref, pe_ref, *, ts, kconv, cw):
    halo = SUBLANES

    @pl.when(pl.program_id(1) == 0)
    def _():
        pe_ref[0:halo, :] = jnp.zeros((halo, pe_ref.shape[1]), F32)

    pe_ref[halo:halo + ts, :] = gb_ref[...].astype(F32) * xt_ref[...].astype(F32)
    width = pe_ref.shape[1]
    for c0 in range(0, width, cw):
        acc = None
        for k in range(kconv):
            r = halo - (kconv - 1) + k
            term = pe_ref[r:r + ts, c0:c0 + cw] * w_ref[k:k + 1, c0:c0 + cw]
            acc = term if acc is None else acc + term
        o_ref[:, c0:c0 + cw] = (gc_ref[:, c0:c0 + cw].astype(F32) * acc).astype(o_ref.dtype)
    pe_ref[0:halo, :] = pe_ref[ts:ts + halo, :]


def _sconv(proj, conv_w, *, batch, seq, width, off_b, off_c, off_x):
    m = proj.shape[0]
    kconv = conv_w.shape[0]
    ts = _pick(seq, (256, 128))
    nt = seq // ts
    assert off_b % width == 0 and off_c % width == 0 and off_x % width == 0 and kconv - 1 <= SUBLANES
    jb, jc, jx = off_b // width, off_c // width, off_x // width
    cw = _pick(width, (512, 256, 128))
    pe_shape = (SUBLANES + ts, width)
    blocks = [((ts, width), BF16)] * 4 + [((kconv, width), F32)]
    scratch = [(pe_shape, F32)]
    return pl.pallas_call(
        functools.partial(_sconv_kernel, ts=ts, kconv=kconv, cw=cw),
        grid=(batch, nt),
        in_specs=[pl.BlockSpec((ts, width), lambda b, c: (b * nt + c, jb)),
                  pl.BlockSpec((ts, width), lambda b, c: (b * nt + c, jc)),
                  pl.BlockSpec((ts, width), lambda b, c: (b * nt + c, jx)),
                  pl.BlockSpec((kconv, width), lambda b, c: (0, 0))],
        out_specs=pl.BlockSpec((ts, width), lambda b, c: (b * nt + c, 0)),
        out_shape=jax.ShapeDtypeStruct((m, width), BF16),
        scratch_shapes=[pltpu.VMEM(s, d) for s, d in scratch],
        compiler_params=_cparams(2, blocks, scratch, temps=4 << 20),
        name="short_conv",
    )(proj, proj, proj, conv_w)


def kernel(x, p, norm_mix, w_in, ssd_conv_w, ssd_conv_b, ssd_dt_bias, ssd_a_log, ssd_d, ssd_norm,
           ssd_out, sc_conv_w, sc_out, w_o, norm_ffn, w_gate_up, w_down, norm_ple, ple_gate,
           ple_proj, norm_final):
    batch, seq, d = x.shape
    depth = w_in.shape[0]
    m = batch * seq
    heads = ssd_a_log.shape[1]
    inner = heads * HEAD_DIM
    groups = SSD_GROUPS
    r_heads = heads // groups
    xbc = ssd_conv_w.shape[2]
    scw = sc_conv_w.shape[2]
    assert xbc == inner + 2 * groups * SSD_STATE and heads <= LANES
    dt0 = inner + xbc
    off_scb = dt0
    off_scc, off_scx = off_scb + scw, off_scb + 2 * scw
    off_ga = off_scb + 3 * scw
    off_gb = off_ga + d
    hpad = LANES - heads
    p2 = p.reshape(depth, m, p.shape[-1])
    w_in_t = jnp.swapaxes(w_in, 1, 2)

    h = x.reshape(m, d)
    u, u_ssq = _prenorm(h, norm_mix[0])
    for i in range(depth):
        dt_bias = jnp.pad(ssd_dt_bias[i], (0, hpad)).reshape(1, LANES)
        a_log = jnp.pad(ssd_a_log[i], (0, hpad)).reshape(1, LANES)
        d_skip = jnp.repeat(ssd_d[i], HEAD_DIM).reshape(1, inner)

        proj = _inproj(u, u_ssq, w_in_t, i, dt0, heads)
        dtrow, wrow, acsrow, acscol = _dtprep(u, u_ssq, w_in_t, i, dt0, dt_bias, a_log, groups, r_heads)
        y = _ssd(proj, dtrow, wrow, acsrow, acscol, ssd_conv_w[i], ssd_conv_b[i].reshape(1, xbc), d_skip,
                 ssd_norm[i].reshape(1, inner), batch=batch, seq=seq, inner=inner, groups=groups)
        ysc = _sconv(proj, sc_conv_w[i], batch=batch, seq=seq, width=scw,
                     off_b=off_scb, off_c=off_scc, off_x=off_scx)
        merged = _merge(y, ysc, ssd_out, sc_out, i, proj, off_ga, off_gb)
        h, v, v_ssq = _matmul_residual(merged, w_o, i, h, norm_ffn[i], "w_o_residual", in_place=i > 0)
        act = _swiglu(v, v_ssq, w_gate_up, i)
        h, hn, hn_ssq = _matmul_residual(act, w_down, i, h, norm_ple[i], "w_down_residual")
        g_next = norm_mix[i + 1] if i + 1 < depth else norm_final
        h, u, u_ssq = _ple(hn, hn_ssq, ple_gate, p2, ple_proj, i, h, g_next)
    out = _rmsnorm(h, norm_final, F32)
    return out.reshape(batch, seq, d)
```

```python
import functools

import jax
import jax.numpy as jnp
from jax import lax
from jax.experimental import pallas as pl
from jax.experimental.pallas import tpu as pltpu

F32 = jnp.float32
BF16 = jnp.bfloat16

EPS = 1e-6
HEAD_DIM = 64
SSD_GROUPS = 8
SSD_STATE = 128
CHUNK = 128
LANES = 128
SUBLANES = 8
CAST_ROWS = 256
LOG2E = 1.4426950408889634
VMEM_BYTES_V7X = 64 * 1024 * 1024
VMEM_CAP = VMEM_BYTES_V7X - 4 * 1024 * 1024


def _nbytes(shape, dtype):
    n = 1
    for s in shape:
        n *= s
    return n * jnp.dtype(dtype).itemsize


def _cparams(ngrid, blocks, scratch=(), temps=0):
    need = (2 * sum(_nbytes(s, d) for s, d in blocks) + sum(_nbytes(s, d) for s, d in scratch)
            + temps + (4 << 20))
    return pltpu.CompilerParams(dimension_semantics=("arbitrary",) * ngrid,
                                vmem_limit_bytes=int(min(max(need, 16 << 20), VMEM_CAP)))


def _pick(n, candidates):
    for c in candidates:
        if n % c == 0:
            return c
    raise ValueError(f"no block size in {candidates} divides {n}")


def _cast_tile(w_ref, wsc_ref):
    k = w_ref.shape[0]
    rc = _pick(k, (CAST_ROWS, LANES, SUBLANES))
    for r in range(0, k, rc):
        wsc_ref[r:r + rc, :] = w_ref[r:r + rc, :].astype(BF16)


def _first_token_tile():
    return pl.program_id(1) == 0


def _sigmoid(x):
    return 0.5 * jnp.tanh(0.5 * x) + 0.5


def _silu(x):
    h = 0.5 * x
    return h * jnp.tanh(h) + h


def _emit_scaled(h_new, g_ref, hg_ref, ssq_ref):
    hg_ref[...] = (h_new * g_ref[...]).astype(BF16)
    col = jnp.sum(h_new * h_new, axis=-1, keepdims=True)
    ssq_ref[...] = jnp.transpose(jnp.broadcast_to(col, (col.shape[0], LANES)))[0:1, :]


def _row_rsqrt(ssq_ref, d):
    s = ssq_ref[0]
    for q in range(1, ssq_ref.shape[0]):
        s = s + ssq_ref[q]
    row = lax.rsqrt(s / d + EPS)
    return jnp.transpose(jnp.broadcast_to(row, (LANES, row.shape[1])))[:, 0:1]


def _prenorm_kernel(x_ref, g_ref, hg_ref, ssq_ref):
    _emit_scaled(x_ref[...], g_ref, hg_ref, ssq_ref.at[0])


def _prenorm(x, g):
    m, d = x.shape
    bm = _pick(m, (512, 256, 128))
    blocks = [((bm, d), F32), ((1, d), F32), ((bm, d), BF16), ((1, 1, bm), F32)]
    return pl.pallas_call(
        _prenorm_kernel,
        grid=(m // bm,),
        in_specs=[pl.BlockSpec((bm, d), lambda i: (i, 0)),
                  pl.BlockSpec((1, d), lambda i: (0, 0))],
        out_specs=[pl.BlockSpec((bm, d), lambda i: (i, 0)),
                   pl.BlockSpec((1, 1, bm), lambda i: (0, 0, i))],
        out_shape=[jax.ShapeDtypeStruct((m, d), BF16), jax.ShapeDtypeStruct((1, 1, m), F32)],
        compiler_params=_cparams(1, blocks, temps=2 * bm * d * 4),
        name="prenorm",
    )(x, g.reshape(1, d))


def _rmsnorm_kernel(x_ref, g_ref, o_ref):
    x = x_ref[...]
    ms = jnp.mean(x * x, axis=-1, keepdims=True)
    o_ref[...] = (x * lax.rsqrt(ms + EPS) * g_ref[...]).astype(o_ref.dtype)


def _rmsnorm(x, g, out_dtype):
    m, d = x.shape
    bm = _pick(m, (512, 256, 128))
    blocks = [((bm, d), F32), ((1, d), F32), ((bm, d), out_dtype)]
    return pl.pallas_call(
        _rmsnorm_kernel,
        grid=(m // bm,),
        in_specs=[pl.BlockSpec((bm, d), lambda i: (i, 0)),
                  pl.BlockSpec((1, d), lambda i: (0, 0))],
        out_specs=pl.BlockSpec((bm, d), lambda i: (i, 0)),
        out_shape=jax.ShapeDtypeStruct((m, d), out_dtype),
        compiler_params=_cparams(1, blocks, temps=2 * bm * d * 4),
        name="rmsnorm",
    )(x, g.reshape(1, d))


def _inproj_kernel(a_ref, ssq_ref, w_ref, wn_ref, o_ref, wsc_ref, *, n_aligned, skip):
    j = pl.program_id(0)
    bn = w_ref.shape[0]
    rc = _pick(bn, (CAST_ROWS, LANES))

    @pl.when(jnp.logical_and(_first_token_tile(), j < n_aligned))
    def _():
        _cast_tile(w_ref, wsc_ref)

    @pl.when(jnp.logical_and(_first_token_tile(), j >= n_aligned))
    def _():
        for r in range(0, bn - rc, rc):
            wsc_ref[r:r + rc, :] = w_ref[r + skip:r + skip + rc, :].astype(BF16)
        wsc_ref[bn - rc:bn - skip, :] = w_ref[bn - rc + skip:bn, :].astype(BF16)
        wsc_ref[bn - skip:bn, :] = wn_ref[...].astype(BF16)

    acc = lax.dot_general(a_ref[...], wsc_ref[...], (((1,), (1,)), ((), ())), preferred_element_type=F32)
    o_ref[...] = (acc * _row_rsqrt(ssq_ref, a_ref.shape[1])).astype(o_ref.dtype)


def _inproj(u, ssq, w_in_t, layer, dt0, heads):
    m, k = u.shape
    parts = ssq.shape[0]
    d_in = w_in_t.shape[1]
    n = d_in - heads
    bm = _pick(m, (1024, 512, 256, 128))
    bn = next(c for c in (1024, 512, 256, 128) if dt0 % c == 0 and (n - dt0) % c == 0)
    assert heads % (2 * SUBLANES) == 0 and bn % heads == 0 and d_in % heads == 0 and heads < LANES
    n_aligned = dt0 // bn
    blocks = [((bm, k), BF16), ((parts, 1, bm), F32), ((bn, k), F32), ((heads, k), F32), ((bm, bn), BF16)]
    scratch = [((bn, k), BF16)]
    return pl.pallas_call(
        functools.partial(_inproj_kernel, n_aligned=n_aligned, skip=heads),
        grid=(n // bn, m // bm),
        in_specs=[pl.BlockSpec((bm, k), lambda j, i: (i, 0)),
                  pl.BlockSpec((parts, 1, bm), lambda j, i: (0, 0, i)),
                  pl.BlockSpec((None, bn, k), lambda j, i: (layer, j, 0)),
                  pl.BlockSpec((None, heads, k), lambda j, i: (layer, (j + 1) * (bn // heads), 0))],
        out_specs=pl.BlockSpec((bm, bn), lambda j, i: (i, j)),
        out_shape=jax.ShapeDtypeStruct((m, n), BF16),
        scratch_shapes=[pltpu.VMEM(s, d) for s, d in scratch],
        compiler_params=_cparams(2, blocks, scratch, temps=bm * bn * 4 + 2 * CAST_ROWS * k * 4),
        name="in_proj",
    )(u, ssq, w_in_t, w_in_t)


def _mm_res_kernel(a_ref, w_ref, h_ref, g_ref, o_ref, hg_ref, ssq_ref, wsc_ref):
    @pl.when(_first_token_tile())
    def _():
        _cast_tile(w_ref, wsc_ref)

    h_new = h_ref[...] + jnp.dot(a_ref[...], wsc_ref[...], preferred_element_type=F32)
    o_ref[...] = h_new
    _emit_scaled(h_new, g_ref, hg_ref, ssq_ref)


def _matmul_residual(a, w, layer, h, g_next, name, in_place=True):
    m, k = a.shape
    n = w.shape[2]
    big_k = k > 2048
    bm = _pick(m, (512, 256, 128)) if big_k else _pick(m, (1024, 512, 256, 128))
    bn = _pick(n, (512, 256, 128)) if big_k else _pick(n, (1024, 512, 256, 128))
    blocks = [((bm, k), BF16), ((k, bn), F32), ((bm, bn), F32), ((1, bn), F32),
              ((bm, bn), F32), ((bm, bn), BF16), ((1, bm), F32)]
    scratch = [((k, bn), BF16)]
    return pl.pallas_call(
        _mm_res_kernel,
        grid=(n // bn, m // bm),
        in_specs=[pl.BlockSpec((bm, k), lambda j, i: (i, 0)),
                  pl.BlockSpec((None, k, bn), lambda j, i: (layer, 0, j)),
                  pl.BlockSpec((bm, bn), lambda j, i: (i, j)),
                  pl.BlockSpec((1, bn), lambda j, i: (0, j))],
        out_specs=[pl.BlockSpec((bm, bn), lambda j, i: (i, j)),
                   pl.BlockSpec((bm, bn), lambda j, i: (i, j)),
                   pl.BlockSpec((None, 1, bm), lambda j, i: (j, 0, i))],
        out_shape=[jax.ShapeDtypeStruct((m, n), F32), jax.ShapeDtypeStruct((m, n), BF16),
                   jax.ShapeDtypeStruct((n // bn, 1, m), F32)],
        scratch_shapes=[pltpu.VMEM(s, d) for s, d in scratch],
        input_output_aliases={2: 0} if in_place else {},
        compiler_params=_cparams(2, blocks, scratch, temps=2 * bm * bn * 4 + 2 * CAST_ROWS * bn * 4),
        name=name,
    )(a, w, h, g_next.reshape(1, n))


def _swiglu_kernel(v_ref, ssq_ref, wg_ref, wu_ref, o_ref, wgsc_ref, wusc_ref):
    @pl.when(_first_token_tile())
    def _():
        _cast_tile(wg_ref, wgsc_ref)
        _cast_tile(wu_ref, wusc_ref)

    v = v_ref[...]
    rs = _row_rsqrt(ssq_ref, v.shape[1])
    gate = jnp.dot(v, wgsc_ref[...], preferred_element_type=F32) * rs
    up = jnp.dot(v, wusc_ref[...], preferred_element_type=F32) * rs
    o_ref[...] = (_silu(gate) * up).astype(o_ref.dtype)


def _swiglu(v, ssq, w_gate_up, layer):
    m, k = v.shape
    parts = ssq.shape[0]
    d_ff = w_gate_up.shape[2] // 2
    bm = _pick(m, (1024, 512, 256, 128))
    bn = _pick(d_ff, (512, 256, 128))
    nb = d_ff // bn
    blocks = [((bm, k), BF16), ((parts, 1, bm), F32), ((k, bn), F32), ((k, bn), F32), ((bm, bn), BF16)]
    scratch = [((k, bn), BF16), ((k, bn), BF16)]
    return pl.pallas_call(
        _swiglu_kernel,
        grid=(nb, m // bm),
        in_specs=[pl.BlockSpec((bm, k), lambda j, i: (i, 0)),
                  pl.BlockSpec((parts, 1, bm), lambda j, i: (0, 0, i)),
                  pl.BlockSpec((None, k, bn), lambda j, i: (layer, 0, j)),
                  pl.BlockSpec((None, k, bn), lambda j, i: (layer, 0, j + nb))],
        out_specs=pl.BlockSpec((bm, bn), lambda j, i: (i, j)),
        out_shape=jax.ShapeDtypeStruct((m, d_ff), BF16),
        scratch_shapes=[pltpu.VMEM(s, d) for s, d in scratch],
        compiler_params=_cparams(2, blocks, scratch, temps=3 * bm * bn * 4 + 2 * CAST_ROWS * bn * 4),
        name="swiglu_up",
    )(v, ssq, w_gate_up, w_gate_up)


def _merge_kernel(y_ref, ysc_ref, wa_ref, wb_ref, ga_ref, gb_ref, o_ref, wasc_ref, wbsc_ref):
    @pl.when(_first_token_tile())
    def _():
        _cast_tile(wa_ref, wasc_ref)
        _cast_tile(wb_ref, wbsc_ref)

    ya = jnp.dot(y_ref[...], wasc_ref[...], preferred_element_type=F32)
    yb = jnp.dot(ysc_ref[...], wbsc_ref[...], preferred_element_type=F32)
    ga = _sigmoid(ga_ref[...].astype(F32))
    gb = _sigmoid(gb_ref[...].astype(F32))
    o_ref[...] = (ga * ya + gb * yb).astype(o_ref.dtype)


def _merge(y, ysc, w_a, w_b, layer, proj, off_ga, off_gb):
    m, ka = y.shape
    kb = ysc.shape[1]
    n = w_a.shape[2]
    bm = _pick(m, (512, 256, 128))
    bn = _pick(n, (512, 256, 128))
    ja, jb = off_ga // bn, off_gb // bn
    assert off_ga % bn == 0 and off_gb % bn == 0
    blocks = [((bm, ka), BF16), ((bm, kb), BF16), ((ka, bn), F32), ((kb, bn), F32),
              ((bm, bn), BF16), ((bm, bn), BF16), ((bm, bn), BF16)]
    scratch = [((ka, bn), BF16), ((kb, bn), BF16)]
    return pl.pallas_call(
        _merge_kernel,
        grid=(n // bn, m // bm),
        in_specs=[pl.BlockSpec((bm, ka), lambda j, i: (i, 0)),
                  pl.BlockSpec((bm, kb), lambda j, i: (i, 0)),
                  pl.BlockSpec((None, ka, bn), lambda j, i: (layer, 0, j)),
                  pl.BlockSpec((None, kb, bn), lambda j, i: (layer, 0, j)),
                  pl.BlockSpec((bm, bn), lambda j, i: (i, j + ja)),
                  pl.BlockSpec((bm, bn), lambda j, i: (i, j + jb))],
        out_specs=pl.BlockSpec((bm, bn), lambda j, i: (i, j)),
        out_shape=jax.ShapeDtypeStruct((m, n), BF16),
        scratch_shapes=[pltpu.VMEM(s, d) for s, d in scratch],
        compiler_params=_cparams(2, blocks, scratch, temps=4 * bm * bn * 4 + 2 * CAST_ROWS * bn * 4),
        name="branch_merge",
    )(y, ysc, w_a, w_b, proj, proj)


def _ple_kernel(hn_ref, ssq_ref, wg_ref, p_ref, wp_ref, h_ref, g_ref, o_ref, hg_ref, ssqo_ref,
                wgsc_ref, wpsc_ref):
    @pl.when(_first_token_tile())
    def _():
        _cast_tile(wg_ref, wgsc_ref)
        _cast_tile(wp_ref, wpsc_ref)

    rs = _row_rsqrt(ssq_ref, hn_ref.shape[1])
    pg = _sigmoid(jnp.dot(hn_ref[...], wgsc_ref[...], preferred_element_type=F32) * rs)
    e = jnp.dot(p_ref[...].astype(BF16), wpsc_ref[...], preferred_element_type=F32)
    h_new = h_ref[...] + pg * e
    o_ref[...] = h_new
    _emit_scaled(h_new, g_ref, hg_ref, ssqo_ref)


def _ple(hn, ssq, w_gate, p, w_proj, layer, h, g_next):
    m, k = hn.shape
    parts = ssq.shape[0]
    kp = p.shape[2]
    n = w_gate.shape[2]
    bm = _pick(m, (1024, 512, 256, 128))
    bn = _pick(n, (1024, 512, 256, 128))
    blocks = [((bm, k), BF16), ((parts, 1, bm), F32), ((k, bn), F32), ((bm, kp), F32), ((kp, bn), F32),
              ((bm, bn), F32), ((1, bn), F32), ((bm, bn), F32), ((bm, bn), BF16), ((1, bm), F32)]
    scratch = [((k, bn), BF16), ((kp, bn), BF16)]
    return pl.pallas_call(
        _ple_kernel,
        grid=(n // bn, m // bm),
        in_specs=[pl.BlockSpec((bm, k), lambda j, i: (i, 0)),
                  pl.BlockSpec((parts, 1, bm), lambda j, i: (0, 0, i)),
                  pl.BlockSpec((None, k, bn), lambda j, i: (layer, 0, j)),
                  pl.BlockSpec((None, bm, kp), lambda j, i: (layer, i, 0)),
                  pl.BlockSpec((None, kp, bn), lambda j, i: (layer, 0, j)),
                  pl.BlockSpec((bm, bn), lambda j, i: (i, j)),
                  pl.BlockSpec((1, bn), lambda j, i: (0, j))],
        out_specs=[pl.BlockSpec((bm, bn), lambda j, i: (i, j)),
                   pl.BlockSpec((bm, bn), lambda j, i: (i, j)),
                   pl.BlockSpec((None, 1, bm), lambda j, i: (j, 0, i))],
        out_shape=[jax.ShapeDtypeStruct((m, n), F32), jax.ShapeDtypeStruct((m, n), BF16),
                   jax.ShapeDtypeStruct((n // bn, 1, m), F32)],
        scratch_shapes=[pltpu.VMEM(s, d) for s, d in scratch],
        input_output_aliases={5: 0},
        compiler_params=_cparams(2, blocks, scratch, temps=4 * bm * bn * 4 + 2 * CAST_ROWS * bn * 4),
        name="ple",
    )(hn, ssq, w_gate, p, w_proj, h, g_next.reshape(1, n))


def _dtprep_kernel(u_ref, ssq_ref, w_ref, bias_ref, alog_ref, dtrow_ref, wrow_ref, acsrow_ref, acscol_ref,
                   *, nc, groups, r_heads):
    x = lax.dot_general(u_ref[...], w_ref[...].astype(BF16), (((1,), (1,)), ((), ())),
                        preferred_element_type=F32)
    x = x * _row_rsqrt(ssq_ref, u_ref.shape[1]) + bias_ref[...]
    dt = jnp.maximum(x, 0.0) + jnp.log1p(jnp.exp(-jnp.abs(x)))
    adt = dt * (-jnp.exp(alog_ref[...]))
    row = lax.broadcasted_iota(jnp.int32, (CHUNK, LANES), 0)
    for k in range(nc):
        sl = slice(k * CHUNK, (k + 1) * CHUNK)
        acs = adt[sl]
        sh = 1
        while sh < CHUNK:
            acs = acs + jnp.where(row >= sh, pltpu.roll(acs, sh, 0), 0.0)
            sh *= 2
        acs2 = acs * LOG2E
        acsrow_ref[k] = acs2.T
        dtrow_ref[k] = dt[sl].T
        wrow_ref[k] = (dt[sl] * jnp.exp(acs[CHUNK - 1:CHUNK, :] - acs)).T
        for g in range(groups):
            shift = (LANES - g * r_heads) % LANES
            acscol_ref[g, sl, :] = pltpu.roll(acs2, shift, 1) if shift else acs2


def _dtprep(u, ssq, w_in_t, layer, dt0, bias, a_log, groups, r_heads):
    m, k = u.shape
    parts = ssq.shape[0]
    nc = 4 if m % (4 * CHUNK) == 0 else 1
    ts = nc * CHUNK
    nchunks = m // CHUNK
    rows = ((nc, LANES, LANES), F32)
    blocks = [((ts, k), BF16), ((parts, 1, ts), F32), ((LANES, k), F32), rows, rows, rows,
              ((groups, ts, LANES), F32)]
    row_spec = pl.BlockSpec((nc, LANES, LANES), lambda i: (i, 0, 0))
    row_shape = jax.ShapeDtypeStruct((nchunks, LANES, LANES), F32)
    return pl.pallas_call(
        functools.partial(_dtprep_kernel, nc=nc, groups=groups, r_heads=r_heads),
        grid=(m // ts,),
        in_specs=[pl.BlockSpec((ts, k), lambda i: (i, 0)),
                  pl.BlockSpec((parts, 1, ts), lambda i: (0, 0, i)),
                  pl.BlockSpec((None, LANES, k), lambda i: (layer, dt0 // LANES, 0)),
                  pl.BlockSpec((1, LANES), lambda i: (0, 0)),
                  pl.BlockSpec((1, LANES), lambda i: (0, 0))],
        out_specs=[row_spec, row_spec, row_spec,
                   pl.BlockSpec((groups, ts, LANES), lambda i: (0, i, 0))],
        out_shape=[row_shape, row_shape, row_shape,
                   jax.ShapeDtypeStruct((groups, m, LANES), F32)],
        compiler_params=_cparams(1, blocks, temps=8 * ts * LANES * 4 + k * LANES * 2),
        name="dt_prep",
    )(u, ssq, w_in_t, bias, a_log)


def _ssd_kernel(z_ref, xs_ref, b_ref, c_ref, xsp_ref, bp_ref, cp_ref,
                dtr_ref, wr_ref, acr_ref, acc_ref, shift_ref,
                cwx_ref, cwb_ref, cwc_ref, cbx_ref, cbb_ref, cbc_ref, dsk_ref, nw_ref,
                o_ref, st_ref, yt_ref, *, nc, r_heads, kconv):
    g = pl.program_id(1)
    c = pl.program_id(2)
    L = CHUNK
    N = SSD_STATE
    gw = r_heads * HEAD_DIM
    npair = gw // LANES

    @pl.when(c == 0)
    def _():
        st_ref[...] = jnp.zeros(st_ref.shape, F32)

    tri = (lax.broadcasted_iota(jnp.int32, (L, L), 0) >= lax.broadcasted_iota(jnp.int32, (L, L), 1))
    lo = lax.broadcasted_iota(jnp.int32, (L, LANES), 1) < HEAD_DIM
    hi = jnp.logical_not(lo)
    lo_row = lo[0:1]

    def two_chunks(ref, prev_ref, k):
        if k == 0:
            prev = prev_ref[...]
            prev = jnp.where(c > 0, prev, jnp.zeros_like(prev))
            return jnp.concatenate([prev, ref[0:L, :]], axis=0)
        return ref[(k - 1) * L:(k + 1) * L, :]

    def conv_silu(shifted, cur, w_ref, bias_ref):
        acc = None
        for tap in range(kconv - 1):
            term = shifted[tap * L:(tap + 1) * L, :] * w_ref[tap:tap + 1, :]
            acc = term if acc is None else acc + term
        acc = acc + cur.astype(F32) * w_ref[kconv - 1:kconv, :]
        acc = acc + bias_ref[...]
        return _silu(acc)

    for k in range(nc):
        r0 = k * L
        x2 = jnp.concatenate([two_chunks(xs_ref, xsp_ref, k), two_chunks(b_ref, bp_ref, k),
                              two_chunks(c_ref, cp_ref, k)], axis=1)
        shifted = jnp.dot(shift_ref[...], x2, preferred_element_type=F32)
        cur = x2[L:2 * L]
        x = conv_silu(shifted[:, 0:gw], cur[:, 0:gw], cwx_ref, cbx_ref)
        bm = conv_silu(shifted[:, gw:gw + N], cur[:, gw:gw + N], cwb_ref, cbb_ref)
        cm = conv_silu(shifted[:, gw + N:gw + 2 * N], cur[:, gw + N:gw + 2 * N], cwc_ref, cbc_ref)
        cb = lax.dot_general(cm.astype(BF16), bm.astype(BF16), (((1,), (1,)), ((), ())),
                             preferred_element_type=F32)
        bt = bm.T
        acol = acc_ref[0, r0:r0 + L, :]
        cdec = jnp.exp2(acol[L - 1:L, :])
        ssq = jnp.zeros((L, 1), F32)
        for j in range(npair):
            cs = slice(j * LANES, (j + 1) * LANES)
            xp = x[:, cs]
            s_prev = st_ref[:, cs]
            s_new = s_prev * jnp.where(lo_row, cdec[:, 2 * j:2 * j + 1], cdec[:, 2 * j + 1:2 * j + 2])
            y = None
            for hh, keep in ((2 * j, lo), (2 * j + 1, hi)):
                head = pl.ds(g * r_heads + hh, 1)
                arow = acr_ref[k, head, :]
                drow = dtr_ref[k, head, :]
                wrow = wr_ref[k, head, :]
                xm = jnp.where(keep, xp, 0.0).astype(BF16)
                sm = jnp.where(keep, s_prev, 0.0).astype(BF16)
                ab = jnp.broadcast_to(acol[:, hh:hh + 1], (L, L))
                dec = jnp.exp2(jnp.where(tri, ab - arow, -jnp.inf))
                mh = cb * dec * drow
                ch = cm * jnp.exp2(ab)
                lhs = jnp.concatenate([mh, ch], axis=1).astype(BF16)
                rhs = jnp.concatenate([xm, sm], axis=0)
                t = jnp.dot(lhs, rhs, preferred_element_type=F32)
                y = t if y is None else y + t
                bth = (bt * wrow).astype(BF16)
                s_new = s_new + jnp.dot(bth, xm, preferred_element_type=F32)
            st_ref[:, cs] = s_new
            yt = y + dsk_ref[:, cs] * xp
            zt = z_ref[r0:r0 + L, cs].astype(F32)
            yt = yt * _silu(zt)
            ssq = ssq + jnp.sum(yt * yt, axis=-1, keepdims=True)
            yt_ref[:, cs] = yt
        rs = lax.rsqrt(ssq / gw + EPS)
        for j in range(npair):
            cs = slice(j * LANES, (j + 1) * LANES)
            o_ref[r0:r0 + L, cs] = (yt_ref[:, cs] * rs * nw_ref[:, cs]).astype(o_ref.dtype)


def _ssd(proj, dtrow, wrow, acsrow, acscol, conv_w, conv_b, d_skip, norm_w, *, batch, seq, inner, groups):
    m = proj.shape[0]
    gw = inner // groups
    r_heads = gw // HEAD_DIM
    assert gw % LANES == 0 and r_heads % 2 == 0 and SSD_STATE == LANES
    kconv = conv_w.shape[0]
    assert kconv - 1 <= CHUNK
    nc = _pick(seq // CHUNK, (4, 2, 1))
    ts = nc * CHUNK
    nt = seq // ts
    gn = groups * SSD_STATE
    xs_blk, b_blk, c_blk = inner // gw, 2 * inner // SSD_STATE, (2 * inner + gn) // SSD_STATE
    wb_blk, wc_blk = inner // SSD_STATE, (inner + gn) // SSD_STATE
    hp = acsrow.shape[1]
    t_idx = jnp.arange(CHUNK)[None, :, None]
    d_idx = (kconv - 1 - jnp.arange(kconv - 1))[:, None, None]
    s_idx = jnp.arange(2 * CHUNK)[None, None, :]
    shift = (s_idx == CHUNK + t_idx - d_idx).astype(BF16).reshape((kconv - 1) * CHUNK, 2 * CHUNK)
    st_shape = (SSD_STATE, gw)
    blocks = [((ts, gw), BF16), ((ts, gw), BF16), ((ts, SSD_STATE), BF16), ((ts, SSD_STATE), BF16),
              ((CHUNK, gw), BF16), ((CHUNK, SSD_STATE), BF16), ((CHUNK, SSD_STATE), BF16),
              ((nc, hp, LANES), F32), ((nc, hp, LANES), F32), ((nc, hp, LANES), F32),
              ((1, ts, LANES), F32), (shift.shape, BF16),
              ((kconv, gw), F32), ((kconv, SSD_STATE), F32), ((kconv, SSD_STATE), F32),
              ((1, gw), F32), ((1, SSD_STATE), F32), ((1, SSD_STATE), F32), ((1, gw), F32), ((1, gw), F32),
              ((ts, gw), BF16)]
    scratch = [(st_shape, F32), ((CHUNK, gw), F32)]
    row = lambda b, g, c: b * nt + c
    prev = lambda b, g, c: jnp.maximum(row(b, g, c) * nc - 1, 0)
    return pl.pallas_call(
        functools.partial(_ssd_kernel, nc=nc, r_heads=r_heads, kconv=kconv),
        grid=(batch, groups, nt),
        in_specs=[
            pl.BlockSpec((ts, gw), lambda b, g, c: (row(b, g, c), g)),
            pl.BlockSpec((ts, gw), lambda b, g, c: (row(b, g, c), xs_blk + g)),
            pl.BlockSpec((ts, SSD_STATE), lambda b, g, c: (row(b, g, c), b_blk + g)),
            pl.BlockSpec((ts, SSD_STATE), lambda b, g, c: (row(b, g, c), c_blk + g)),
            pl.BlockSpec((CHUNK, gw), lambda b, g, c: (prev(b, g, c), xs_blk + g)),
            pl.BlockSpec((CHUNK, SSD_STATE), lambda b, g, c: (prev(b, g, c), b_blk + g)),
            pl.BlockSpec((CHUNK, SSD_STATE), lambda b, g, c: (prev(b, g, c), c_blk + g)),
            pl.BlockSpec((nc, hp, LANES), lambda b, g, c: (row(b, g, c), 0, 0)),
            pl.BlockSpec((nc, hp, LANES), lambda b, g, c: (row(b, g, c), 0, 0)),
            pl.BlockSpec((nc, hp, LANES), lambda b, g, c: (row(b, g, c), 0, 0)),
            pl.BlockSpec((1, ts, LANES), lambda b, g, c: (g, row(b, g, c), 0)),
            pl.BlockSpec(shift.shape, lambda b, g, c: (0, 0)),
            pl.BlockSpec((kconv, gw), lambda b, g, c: (0, g)),
            pl.BlockSpec((kconv, SSD_STATE), lambda b, g, c: (0, wb_blk + g)),
            pl.BlockSpec((kconv, SSD_STATE), lambda b, g, c: (0, wc_blk + g)),
            pl.BlockSpec((1, gw), lambda b, g, c: (0, g)),
            pl.BlockSpec((1, SSD_STATE), lambda b, g, c: (0, wb_blk + g)),
            pl.BlockSpec((1, SSD_STATE), lambda b, g, c: (0, wc_blk + g)),
            pl.BlockSpec((1, gw), lambda b, g, c: (0, g)),
            pl.BlockSpec((1, gw), lambda b, g, c: (0, g)),
        ],
        out_specs=pl.BlockSpec((ts, gw), lambda b, g, c: (row(b, g, c), g)),
        out_shape=jax.ShapeDtypeStruct((m, inner), BF16),
        scratch_shapes=[pltpu.VMEM(s, d) for s, d in scratch],
        compiler_params=_cparams(3, blocks, scratch, temps=8 << 20),
        name="ssd",
    )(proj, proj, proj, proj, proj, proj, proj, dtrow, wrow, acsrow, acscol, shift,
      conv_w, conv_w, conv_w, conv_b, conv_b, conv_b, d_skip, norm_w)


def _sconv_kernel(gb_ref, gc_ref, xt_ref, w_ref, o_ref, pe_ref, *, ts, kconv, cw):
    halo = SUBLANES

    @pl.when(pl.program_id(1) == 0)
    def _():
        pe_ref[0:halo, :] = jnp.zeros((halo, pe_ref.shape[1]), F32)

    pe_ref[halo:halo + ts, :] = gb_ref[...].astype(F32) * xt_ref[...].astype(F32)
    width = pe_ref.shape[1]
    for c0 in range(0, width, cw):
        acc = None
        for k in range(kconv):
            r = halo - (kconv - 1) + k
            term = pe_ref[r:r + ts, c0:c0 + cw] * w_ref[k:k + 1, c0:c0 + cw]
            acc = term if acc is None else acc + term
        o_ref[:, c0:c0 + cw] = (gc_ref[:, c0:c0 + cw].astype(F32) * acc).astype(o_ref.dtype)
    pe_ref[0:halo, :] = pe_ref[ts:ts + halo, :]


def _sconv(proj, conv_w, *, batch, seq, width, off_b, off_c, off_x):
    m = proj.shape[0]
    kconv = conv_w.shape[0]
    ts = _pick(seq, (256, 128))
    nt = seq // ts
    assert off_b % width == 0 and off_c % width == 0 and off_x % width == 0 and kconv - 1 <= SUBLANES
    jb, jc, jx = off_b // width, off_c // width, off_x // width
    cw = _pick(width, (512, 256, 128))
    pe_shape = (SUBLANES + ts, width)
    blocks = [((ts, width), BF16)] * 4 + [((kconv, width), F32)]
    scratch = [(pe_shape, F32)]
    return pl.pallas_call(
        functools.partial(_sconv_kernel, ts=ts, kconv=kconv, cw=cw),
        grid=(batch, nt),
        in_specs=[pl.BlockSpec((ts, width), lambda b, c: (b * nt + c, jb)),
                  pl.BlockSpec((ts, width), lambda b, c: (b * nt + c, jc)),
                  pl.BlockSpec((ts, width), lambda b, c: (b * nt + c, jx)),
                  pl.BlockSpec((kconv, width), lambda b, c: (0, 0))],
        out_specs=pl.BlockSpec((ts, width), lambda b, c: (b * nt + c, 0)),
        out_shape=jax.ShapeDtypeStruct((m, width), BF16),
        scratch_shapes=[pltpu.VMEM(s, d) for s, d in scratch],
        compiler_params=_cparams(2, blocks, scratch, temps=4 << 20),
        name="short_conv",
    )(proj, proj, proj, conv_w)


def kernel(x, p, norm_mix, w_in, ssd_conv_w, ssd_conv_b, ssd_dt_bias, ssd_a_log, ssd_d, ssd_norm,
           ssd_out, sc_conv_w, sc_out, w_o, norm_ffn, w_gate_up, w_down, norm_ple, ple_gate,
           ple_proj, norm_final):
    batch, seq, d = x.shape
    depth = w_in.shape[0]
    m = batch * seq
    heads = ssd_a_log.shape[1]
    inner = heads * HEAD_DIM
    groups = SSD_GROUPS
    r_heads = heads // groups
    xbc = ssd_conv_w.shape[2]
    scw = sc_conv_w.shape[2]
    assert xbc == inner + 2 * groups * SSD_STATE and heads <= LANES
    dt0 = inner + xbc
    off_scb = dt0
    off_scc, off_scx = off_scb + scw, off_scb + 2 * scw
    off_ga = off_scb + 3 * scw
    off_gb = off_ga + d
    hpad = LANES - heads
    p2 = p.reshape(depth, m, p.shape[-1])
    w_in_t = jnp.swapaxes(w_in, 1, 2)

    h = x.reshape(m, d)
    u, u_ssq = _prenorm(h, norm_mix[0])
    for i in range(depth):
        dt_bias = jnp.pad(ssd_dt_bias[i], (0, hpad)).reshape(1, LANES)
        a_log = jnp.pad(ssd_a_log[i], (0, hpad)).reshape(1, LANES)
        d_skip = jnp.repeat(ssd_d[i], HEAD_DIM).reshape(1, inner)

        proj = _inproj(u, u_ssq, w_in_t, i, dt0, heads)
        dtrow, wrow, acsrow, acscol = _dtprep(u, u_ssq, w_in_t, i, dt0, dt_bias, a_log, groups, r_heads)
        y = _ssd(proj, dtrow, wrow, acsrow, acscol, ssd_conv_w[i], ssd_conv_b[i].reshape(1, xbc), d_skip,
                 ssd_norm[i].reshape(1, inner), batch=batch, seq=seq, inner=inner, groups=groups)
        ysc = _sconv(proj, sc_conv_w[i], batch=batch, seq=seq, width=scw,
                     off_b=off_scb, off_c=off_scc, off_x=off_scx)
        merged = _merge(y, ysc, ssd_out, sc_out, i, proj, off_ga, off_gb)
        h, v, v_ssq = _matmul_residual(merged, w_o, i, h, norm_ffn[i], "w_o_residual", in_place=i > 0)
        act = _swiglu(v, v_ssq, w_gate_up, i)
        h, hn, hn_ssq = _matmul_residual(act, w_down, i, h, norm_ple[i], "w_down_residual")
        g_next = norm_mix[i + 1] if i + 1 < depth else norm_final
        h, u, u_ssq = _ple(hn, hn_ssq, ple_gate, p2, ple_proj, i, h, g_next)
    out = _rmsnorm(h, norm_final, F32)
    return out.reshape(batch, seq, d)
```

```python
import functools

import jax
import jax.numpy as jnp
from jax import lax
from jax.experimental import pallas as pl
from jax.experimental.pallas import tpu as pltpu

F32 = jnp.float32
BF16 = jnp.bfloat16

EPS = 1e-6
HEAD_DIM = 64
SSD_GROUPS = 8
SSD_STATE = 128
CHUNK = 128
LANES = 128
SUBLANES = 8
CAST_ROWS = 256
LOG2E = 1.4426950408889634
VMEM_BYTES_V7X = 64 * 1024 * 1024
VMEM_CAP = VMEM_BYTES_V7X - 4 * 1024 * 1024


def _nbytes(shape, dtype):
    n = 1
    for s in shape:
        n *= s
    return n * jnp.dtype(dtype).itemsize


def _cparams(ngrid, blocks, scratch=(), temps=0):
    need = (2 * sum(_nbytes(s, d) for s, d in blocks) + sum(_nbytes(s, d) for s, d in scratch)
            + temps + (4 << 20))
    return pltpu.CompilerParams(dimension_semantics=("arbitrary",) * ngrid,
                                vmem_limit_bytes=int(min(max(need, 16 << 20), VMEM_CAP)))


def _pick(n, candidates):
    for c in candidates:
        if n % c == 0:
            return c
    raise ValueError(f"no block size in {candidates} divides {n}")


def _cast_tile(w_ref, wsc_ref):
    k = w_ref.shape[0]
    rc = _pick(k, (CAST_ROWS, LANES, SUBLANES))
    for r in range(0, k, rc):
        wsc_ref[r:r + rc, :] = w_ref[r:r + rc, :].astype(BF16)


def _first_token_tile():
    return pl.program_id(1) == 0


def _sigmoid(x):
    return 0.5 * jnp.tanh(0.5 * x) + 0.5


def _silu(x):
    h = 0.5 * x
    return h * jnp.tanh(h) + h


def _emit_scaled(h_new, g_ref, hg_ref, ssq_ref):
    hg_ref[...] = (h_new * g_ref[...]).astype(BF16)
    col = jnp.sum(h_new * h_new, axis=-1, keepdims=True)
    ssq_ref[...] = jnp.transpose(jnp.broadcast_to(col, (col.shape[0], LANES)))[0:1, :]


def _row_rsqrt(ssq_ref, d):
    s = ssq_ref[0]
    for q in range(1, ssq_ref.shape[0]):
        s = s + ssq_ref[q]
    row = lax.rsqrt(s / d + EPS)
    return jnp.transpose(jnp.broadcast_to(row, (LANES, row.shape[1])))[:, 0:1]


def _prenorm_kernel(x_ref, g_ref, hg_ref, ssq_ref):
    _emit_scaled(x_ref[...], g_ref, hg_ref, ssq_ref.at[0])


def _prenorm(x, g):
    m, d = x.shape
    bm = _pick(m, (512, 256, 128))
    blocks = [((bm, d), F32), ((1, d), F32), ((bm, d), BF16), ((1, 1, bm), F32)]
    return pl.pallas_call(
        _prenorm_kernel,
        grid=(m // bm,),
        in_specs=[pl.BlockSpec((bm, d), lambda i: (i, 0)),
                  pl.BlockSpec((1, d), lambda i: (0, 0))],
        out_specs=[pl.BlockSpec((bm, d), lambda i: (i, 0)),
                   pl.BlockSpec((1, 1, bm), lambda i: (0, 0, i))],
        out_shape=[jax.ShapeDtypeStruct((m, d), BF16), jax.ShapeDtypeStruct((1, 1, m), F32)],
        compiler_params=_cparams(1, blocks, temps=2 * bm * d * 4),
        name="prenorm",
    )(x, g.reshape(1, d))


def _rmsnorm_kernel(x_ref, g_ref, o_ref):
    x = x_ref[...]
    ms = jnp.mean(x * x, axis=-1, keepdims=True)
    o_ref[...] = (x * lax.rsqrt(ms + EPS) * g_ref[...]).astype(o_ref.dtype)


def _rmsnorm(x, g, out_dtype):
    m, d = x.shape
    bm = _pick(m, (512, 256, 128))
    blocks = [((bm, d), F32), ((1, d), F32), ((bm, d), out_dtype)]
    return pl.pallas_call(
        _rmsnorm_kernel,
        grid=(m // bm,),
        in_specs=[pl.BlockSpec((bm, d), lambda i: (i, 0)),
                  pl.BlockSpec((1, d), lambda i: (0, 0))],
        out_specs=pl.BlockSpec((bm, d), lambda i: (i, 0)),
        out_shape=jax.ShapeDtypeStruct((m, d), out_dtype),
        compiler_params=_cparams(1, blocks, temps=2 * bm * d * 4),
        name="rmsnorm",
    )(x, g.reshape(1, d))


def _inproj_kernel(a_ref, ssq_ref, w_ref, wn_ref, o_ref, wsc_ref, *, n_aligned, skip):
    j = pl.program_id(0)
    bn = w_ref.shape[0]
    rc = _pick(bn, (CAST_ROWS, LANES))

    @pl.when(jnp.logical_and(_first_token_tile(), j < n_aligned))
    def _():
        _cast_tile(w_ref, wsc_ref)

    @pl.when(jnp.logical_and(_first_token_tile(), j >= n_aligned))
    def _():
        for r in range(0, bn - rc, rc):
            wsc_ref[r:r + rc, :] = w_ref[r + skip:r + skip + rc, :].astype(BF16)
        wsc_ref[bn - rc:bn - skip, :] = w_ref[bn - rc + skip:bn, :].astype(BF16)
        wsc_ref[bn - skip:bn, :] = wn_ref[...].astype(BF16)

    acc = lax.dot_general(a_ref[...], wsc_ref[...], (((1,), (1,)), ((), ())), preferred_element_type=F32)
    o_ref[...] = (acc * _row_rsqrt(ssq_ref, a_ref.shape[1])).astype(o_ref.dtype)


def _inproj(u, ssq, w_in_t, layer, dt0, heads):
    m, k = u.shape
    parts = ssq.shape[0]
    d_in = w_in_t.shape[1]
    n = d_in - heads
    bm = _pick(m, (1024, 512, 256, 128))
    bn = next(c for c in (1024, 512, 256, 128) if dt0 % c == 0 and (n - dt0) % c == 0)
    assert heads % (2 * SUBLANES) == 0 and bn % heads == 0 and d_in % heads == 0 and heads < LANES
    n_aligned = dt0 // bn
    blocks = [((bm, k), BF16), ((parts, 1, bm), F32), ((bn, k), F32), ((heads, k), F32), ((bm, bn), BF16)]
    scratch = [((bn, k), BF16)]
    return pl.pallas_call(
        functools.partial(_inproj_kernel, n_aligned=n_aligned, skip=heads),
        grid=(n // bn, m // bm),
        in_specs=[pl.BlockSpec((bm, k), lambda j, i: (i, 0)),
                  pl.BlockSpec((parts, 1, bm), lambda j, i: (0, 0, i)),
                  pl.BlockSpec((None, bn, k), lambda j, i: (layer, j, 0)),
                  pl.BlockSpec((None, heads, k), lambda j, i: (layer, (j + 1) * (bn // heads), 0))],
        out_specs=pl.BlockSpec((bm, bn), lambda j, i: (i, j)),
        out_shape=jax.ShapeDtypeStruct((m, n), BF16),
        scratch_shapes=[pltpu.VMEM(s, d) for s, d in scratch],
        compiler_params=_cparams(2, blocks, scratch, temps=bm * bn * 4 + 2 * CAST_ROWS * k * 4),
        name="in_proj",
    )(u, ssq, w_in_t, w_in_t)


def _mm_res_kernel(a_ref, w_ref, h_ref, g_ref, o_ref, hg_ref, ssq_ref, wsc_ref):
    @pl.when(_first_token_tile())
    def _():
        _cast_tile(w_ref, wsc_ref)

    h_new = h_ref[...] + jnp.dot(a_ref[...], wsc_ref[...], preferred_element_type=F32)
    o_ref[...] = h_new
    _emit_scaled(h_new, g_ref, hg_ref, ssq_ref)


def _matmul_residual(a, w, layer, h, g_next, name, in_place=True):
    m, k = a.shape
    n = w.shape[2]
    big_k = k > 2048
    bm = _pick(m, (512, 256, 128)) if big_k else _pick(m, (1024, 512, 256, 128))
    bn = _pick(n, (512, 256, 128)) if big_k else _pick(n, (1024, 512, 256, 128))
    blocks = [((bm, k), BF16), ((k, bn), F32), ((bm, bn), F32), ((1, bn), F32),
              ((bm, bn), F32), ((bm, bn), BF16), ((1, bm), F32)]
    scratch = [((k, bn), BF16)]
    return pl.pallas_call(
        _mm_res_kernel,
        grid=(n // bn, m // bm),
        in_specs=[pl.BlockSpec((bm, k), lambda j, i: (i, 0)),
                  pl.BlockSpec((None, k, bn), lambda j, i: (layer, 0, j)),
                  pl.BlockSpec((bm, bn), lambda j, i: (i, j)),
                  pl.BlockSpec((1, bn), lambda j, i: (0, j))],
        out_specs=[pl.BlockSpec((bm, bn), lambda j, i: (i, j)),
                   pl.BlockSpec((bm, bn), lambda j, i: (i, j)),
                   pl.BlockSpec((None, 1, bm), lambda j, i: (j, 0, i))],
        out_shape=[jax.ShapeDtypeStruct((m, n), F32), jax.ShapeDtypeStruct((m, n), BF16),
                   jax.ShapeDtypeStruct((n // bn, 1, m), F32)],
        scratch_shapes=[pltpu.VMEM(s, d) for s, d in scratch],
        input_output_aliases={2: 0} if in_place else {},
        compiler_params=_cparams(2, blocks, scratch, temps=2 * bm * bn * 4 + 2 * CAST_ROWS * bn * 4),
        name=name,
    )(a, w, h, g_next.reshape(1, n))


def _swiglu_kernel(v_ref, ssq_ref, wg_ref, wu_ref, o_ref, wgsc_ref, wusc_ref):
    @pl.when(_first_token_tile())
    def _():
        _cast_tile(wg_ref, wgsc_ref)
        _cast_tile(wu_ref, wusc_ref)

    v = v_ref[...]
    rs = _row_rsqrt(ssq_ref, v.shape[1])
    gate = jnp.dot(v, wgsc_ref[...], preferred_element_type=F32) * rs
    up = jnp.dot(v, wusc_ref[...], preferred_element_type=F32) * rs
    o_ref[...] = (_silu(gate) * up).astype(o_ref.dtype)


def _swiglu(v, ssq, w_gate_up, layer):
    m, k = v.shape
    parts = ssq.shape[0]
    d_ff = w_gate_up.shape[2] // 2
    bm = _pick(m, (1024, 512, 256, 128))
    bn = _pick(d_ff, (512, 256, 128))
    nb = d_ff // bn
    blocks = [((bm, k), BF16), ((parts, 1, bm), F32), ((k, bn), F32), ((k, bn), F32), ((bm, bn), BF16)]
    scratch = [((k, bn), BF16), ((k, bn), BF16)]
    return pl.pallas_call(
        _swiglu_kernel,
        grid=(nb, m // bm),
        in_specs=[pl.BlockSpec((bm, k), lambda j, i: (i, 0)),
                  pl.BlockSpec((parts, 1, bm), lambda j, i: (0, 0, i)),
                  pl.BlockSpec((None, k, bn), lambda j, i: (layer, 0, j)),
                  pl.BlockSpec((None, k, bn), lambda j, i: (layer, 0, j + nb))],
        out_specs=pl.BlockSpec((bm, bn), lambda j, i: (i, j)),
        out_shape=jax.ShapeDtypeStruct((m, d_ff), BF16),
        scratch_shapes=[pltpu.VMEM(s, d) for s, d in scratch],
        compiler_params=_cparams(2, blocks, scratch, temps=3 * bm * bn * 4 + 2 * CAST_ROWS * bn * 4),
        name="swiglu_up",
    )(v, ssq, w_gate_up, w_gate_up)


def _merge_kernel(y_ref, ysc_ref, wa_ref, wb_ref, ga_ref, gb_ref, o_ref, wasc_ref, wbsc_ref):
    @pl.when(_first_token_tile())
    def _():
        _cast_tile(wa_ref, wasc_ref)
        _cast_tile(wb_ref, wbsc_ref)

    ya = jnp.dot(y_ref[...], wasc_ref[...], preferred_element_type=F32)
    yb = jnp.dot(ysc_ref[...], wbsc_ref[...], preferred_element_type=F32)
    ga = _sigmoid(ga_ref[...].astype(F32))
    gb = _sigmoid(gb_ref[...].astype(F32))
    o_ref[...] = (ga * ya + gb * yb).astype(o_ref.dtype)


def _merge(y, ysc, w_a, w_b, layer, proj, off_ga, off_gb):
    m, ka = y.shape
    kb = ysc.shape[1]
    n = w_a.shape[2]
    bm = _pick(m, (512, 256, 128))
    bn = _pick(n, (512, 256, 128))
    ja, jb = off_ga // bn, off_gb // bn
    assert off_ga % bn == 0 and off_gb % bn == 0
    blocks = [((bm, ka), BF16), ((bm, kb), BF16), ((ka, bn), F32), ((kb, bn), F32),
              ((bm, bn), BF16), ((bm, bn), BF16), ((bm, bn), BF16)]
    scratch = [((ka, bn), BF16), ((kb, bn), BF16)]
    return pl.pallas_call(
        _merge_kernel,
        grid=(n // bn, m // bm),
        in_specs=[pl.BlockSpec((bm, ka), lambda j, i: (i, 0)),
                  pl.BlockSpec((bm, kb), lambda j, i: (i, 0)),
                  pl.BlockSpec((None, ka, bn), lambda j, i: (layer, 0, j)),
                  pl.BlockSpec((None, kb, bn), lambda j, i: (layer, 0, j)),
                  pl.BlockSpec((bm, bn), lambda j, i: (i, j + ja)),
                  pl.BlockSpec((bm, bn), lambda j, i: (i, j + jb))],
        out_specs=pl.BlockSpec((bm, bn), lambda j, i: (i, j)),
        out_shape=jax.ShapeDtypeStruct((m, n), BF16),
        scratch_shapes=[pltpu.VMEM(s, d) for s, d in scratch],
        compiler_params=_cparams(2, blocks, scratch, temps=4 * bm * bn * 4 + 2 * CAST_ROWS * bn * 4),
        name="branch_merge",
    )(y, ysc, w_a, w_b, proj, proj)


def _ple_kernel(hn_ref, ssq_ref, wg_ref, p_ref, wp_ref, h_ref, g_ref, o_ref, hg_ref, ssqo_ref,
                wgsc_ref, wpsc_ref):
    @pl.when(_first_token_tile())
    def _():
        _cast_tile(wg_ref, wgsc_ref)
        _cast_tile(wp_ref, wpsc_ref)

    rs = _row_rsqrt(ssq_ref, hn_ref.shape[1])
    pg = _sigmoid(jnp.dot(hn_ref[...], wgsc_ref[...], preferred_element_type=F32) * rs)
    e = jnp.dot(p_ref[...].astype(BF16), wpsc_ref[...], preferred_element_type=F32)
    h_new = h_ref[...] + pg * e
    o_ref[...] = h_new
    _emit_scaled(h_new, g_ref, hg_ref, ssqo_ref)


def _ple(hn, ssq, w_gate, p, w_proj, layer, h, g_next):
    m, k = hn.shape
    parts = ssq.shape[0]
    kp = p.shape[2]
    n = w_gate.shape[2]
    bm = _pick(m, (1024, 512, 256, 128))
    bn = _pick(n, (1024, 512, 256, 128))
    blocks = [((bm, k), BF16), ((parts, 1, bm), F32), ((k, bn), F32), ((bm, kp), F32), ((kp, bn), F32),
              ((bm, bn), F32), ((1, bn), F32), ((bm, bn), F32), ((bm, bn), BF16), ((1, bm), F32)]
    scratch = [((k, bn), BF16), ((kp, bn), BF16)]
    return pl.pallas_call(
        _ple_kernel,
        grid=(n // bn, m // bm),
        in_specs=[pl.BlockSpec((bm, k), lambda j, i: (i, 0)),
                  pl.BlockSpec((parts, 1, bm), lambda j, i: (0, 0, i)),
                  pl.BlockSpec((None, k, bn), lambda j, i: (layer, 0, j)),
                  pl.BlockSpec((None, bm, kp), lambda j, i: (layer, i, 0)),
                  pl.BlockSpec((None, kp, bn), lambda j, i: (layer, 0, j)),
                  pl.BlockSpec((bm, bn), lambda j, i: (i, j)),
                  pl.BlockSpec((1, bn), lambda j, i: (0, j))],
        out_specs=[pl.BlockSpec((bm, bn), lambda j, i: (i, j)),
                   pl.BlockSpec((bm, bn), lambda j, i: (i, j)),
                   pl.BlockSpec((None, 1, bm), lambda j, i: (j, 0, i))],
        out_shape=[jax.ShapeDtypeStruct((m, n), F32), jax.ShapeDtypeStruct((m, n), BF16),
                   jax.ShapeDtypeStruct((n // bn, 1, m), F32)],
        scratch_shapes=[pltpu.VMEM(s, d) for s, d in scratch],
        input_output_aliases={5: 0},
        compiler_params=_cparams(2, blocks, scratch, temps=4 * bm * bn * 4 + 2 * CAST_ROWS * bn * 4),
        name="ple",
    )(hn, ssq, w_gate, p, w_proj, h, g_next.reshape(1, n))


def _dtprep_kernel(u_ref, ssq_ref, w_ref, bias_ref, alog_ref, dtrow_ref, wrow_ref, acsrow_ref, acscol_ref,
                   *, nc, groups, r_heads):
    x = lax.dot_general(u_ref[...], w_ref[...].astype(BF16), (((1,), (1,)), ((), ())),
                        preferred_element_type=F32)
    x = x * _row_rsqrt(ssq_ref, u_ref.shape[1]) + bias_ref[...]
    dt = jnp.maximum(x, 0.0) + jnp.log1p(jnp.exp(-jnp.abs(x)))
    adt = dt * (-jnp.exp(alog_ref[...]))
    row = lax.broadcasted_iota(jnp.int32, (CHUNK, LANES), 0)
    for k in range(nc):
        sl = slice(k * CHUNK, (k + 1) * CHUNK)
        acs = adt[sl]
        sh = 1
        while sh < CHUNK:
            acs = acs + jnp.where(row >= sh, pltpu.roll(acs, sh, 0), 0.0)
            sh *= 2
        acs2 = acs * LOG2E
        acsrow_ref[k] = acs2.T
        dtrow_ref[k] = dt[sl].T
        wrow_ref[k] = (dt[sl] * jnp.exp(acs[CHUNK - 1:CHUNK, :] - acs)).T
        for g in range(groups):
            shift = (LANES - g * r_heads) % LANES
            acscol_ref[g, sl, :] = pltpu.roll(acs2, shift, 1) if shift else acs2


def _dtprep(u, ssq, w_in_t, layer, dt0, bias, a_log, groups, r_heads):
    m, k = u.shape
    parts = ssq.shape[0]
    nc = 4 if m % (4 * CHUNK) == 0 else 1
    ts = nc * CHUNK
    nchunks = m // CHUNK
    rows = ((nc, LANES, LANES), F32)
    blocks = [((ts, k), BF16), ((parts, 1, ts), F32), ((LANES, k), F32), rows, rows, rows,
              ((groups, ts, LANES), F32)]
    row_spec = pl.BlockSpec((nc, LANES, LANES), lambda i: (i, 0, 0))
    row_shape = jax.ShapeDtypeStruct((nchunks, LANES, LANES), F32)
    return pl.pallas_call(
        functools.partial(_dtprep_kernel, nc=nc, groups=groups, r_heads=r_heads),
        grid=(m // ts,),
        in_specs=[pl.BlockSpec((ts, k), lambda i: (i, 0)),
                  pl.BlockSpec((parts, 1, ts), lambda i: (0, 0, i)),
                  pl.BlockSpec((None, LANES, k), lambda i: (layer, dt0 // LANES, 0)),
                  pl.BlockSpec((1, LANES), lambda i: (0, 0)),
                  pl.BlockSpec((1, LANES), lambda i: (0, 0))],
        out_specs=[row_spec, row_spec, row_spec,
                   pl.BlockSpec((groups, ts, LANES), lambda i: (0, i, 0))],
        out_shape=[row_shape, row_shape, row_shape,
                   jax.ShapeDtypeStruct((groups, m, LANES), F32)],
        compiler_params=_cparams(1, blocks, temps=8 * ts * LANES * 4 + k * LANES * 2),
        name="dt_prep",
    )(u, ssq, w_in_t, bias, a_log)


def _ssd_kernel(z_ref, xs_ref, b_ref, c_ref, xsp_ref, bp_ref, cp_ref,
                dtr_ref, wr_ref, acr_ref, acc_ref, shift_ref,
                cwx_ref, cwb_ref, cwc_ref, cbx_ref, cbb_ref, cbc_ref, dsk_ref, nw_ref,
                o_ref, st_ref, yt_ref, *, nc, r_heads, kconv):
    g = pl.program_id(1)
    c = pl.program_id(2)
    L = CHUNK
    N = SSD_STATE
    gw = r_heads * HEAD_DIM
    npair = gw // LANES

    @pl.when(c == 0)
    def _():
        st_ref[...] = jnp.zeros(st_ref.shape, F32)

    tri = (lax.broadcasted_iota(jnp.int32, (L, L), 0) >= lax.broadcasted_iota(jnp.int32, (L, L), 1))
    lo = lax.broadcasted_iota(jnp.int32, (L, LANES), 1) < HEAD_DIM
    hi = jnp.logical_not(lo)
    lo_row = lo[0:1]

    def two_chunks(ref, prev_ref, k):
        if k == 0:
            prev = prev_ref[...]
            prev = jnp.where(c > 0, prev, jnp.zeros_like(prev))
            return jnp.concatenate([prev, ref[0:L, :]], axis=0)
        return ref[(k - 1) * L:(k + 1) * L, :]

    def conv_silu(shifted, cur, w_ref, bias_ref):
        acc = None
        for tap in range(kconv - 1):
            term = shifted[tap * L:(tap + 1) * L, :] * w_ref[tap:tap + 1, :]
            acc = term if acc is None else acc + term
        acc = acc + cur.astype(F32) * w_ref[kconv - 1:kconv, :]
        acc = acc + bias_ref[...]
        return _silu(acc)

    for k in range(nc):
        r0 = k * L
        x2 = jnp.concatenate([two_chunks(xs_ref, xsp_ref, k), two_chunks(b_ref, bp_ref, k),
                              two_chunks(c_ref, cp_ref, k)], axis=1)
        shifted = jnp.dot(shift_ref[...], x2, preferred_element_type=F32)
        cur = x2[L:2 * L]
        x = conv_silu(shifted[:, 0:gw], cur[:, 0:gw], cwx_ref, cbx_ref)
        bm = conv_silu(shifted[:, gw:gw + N], cur[:, gw:gw + N], cwb_ref, cbb_ref)
        cm = conv_silu(shifted[:, gw + N:gw + 2 * N], cur[:, gw + N:gw + 2 * N], cwc_ref, cbc_ref)
        cb = lax.dot_general(cm.astype(BF16), bm.astype(BF16), (((1,), (1,)), ((), ())),
                             preferred_element_type=F32)
        bt = bm.T
        acol = acc_ref[0, r0:r0 + L, :]
        cdec = jnp.exp2(acol[L - 1:L, :])
        ssq = jnp.zeros((L, 1), F32)
        for j in range(npair):
            cs = slice(j * LANES, (j + 1) * LANES)
            xp = x[:, cs]
            s_prev = st_ref[:, cs]
            s_new = s_prev * jnp.where(lo_row, cdec[:, 2 * j:2 * j + 1], cdec[:, 2 * j + 1:2 * j + 2])
            y = None
            for hh, keep in ((2 * j, lo), (2 * j + 1, hi)):
                head = pl.ds(g * r_heads + hh, 1)
                arow = acr_ref[k, head, :]
                drow = dtr_ref[k, head, :]
                wrow = wr_ref[k, head, :]
                xm = jnp.where(keep, xp, 0.0).astype(BF16)
                sm = jnp.where(keep, s_prev, 0.0).astype(BF16)
                ab = jnp.broadcast_to(acol[:, hh:hh + 1], (L, L))
                dec = jnp.exp2(jnp.where(tri, ab - arow, -jnp.inf))
                mh = cb * dec * drow
                ch = cm * jnp.exp2(ab)
                lhs = jnp.concatenate([mh, ch], axis=1).astype(BF16)
                rhs = jnp.concatenate([xm, sm], axis=0)
                t = jnp.dot(lhs, rhs, preferred_element_type=F32)
                y = t if y is None else y + t
                bth = (bt * wrow).astype(BF16)
                s_new = s_new + jnp.dot(bth, xm, preferred_element_type=F32)
            st_ref[:, cs] = s_new
            yt = y + dsk_ref[:, cs] * xp
            zt = z_ref[r0:r0 + L, cs].astype(F32)
            yt = yt * _silu(zt)
            ssq = ssq + jnp.sum(yt * yt, axis=-1, keepdims=True)
            yt_ref[:, cs] = yt
        rs = lax.rsqrt(ssq / gw + EPS)
        for j in range(npair):
            cs = slice(j * LANES, (j + 1) * LANES)
            o_ref[r0:r0 + L, cs] = (yt_ref[:, cs] * rs * nw_ref[:, cs]).astype(o_ref.dtype)


def _ssd(proj, dtrow, wrow, acsrow, acscol, conv_w, conv_b, d_skip, norm_w, *, batch, seq, inner, groups):
    m = proj.shape[0]
    gw = inner // groups
    r_heads = gw // HEAD_DIM
    assert gw % LANES == 0 and r_heads % 2 == 0 and SSD_STATE == LANES
    kconv = conv_w.shape[0]
    assert kconv - 1 <= CHUNK
    nc = _pick(seq // CHUNK, (8, 4, 2, 1))
    ts = nc * CHUNK
    nt = seq // ts
    gn = groups * SSD_STATE
    xs_blk, b_blk, c_blk = inner // gw, 2 * inner // SSD_STATE, (2 * inner + gn) // SSD_STATE
    wb_blk, wc_blk = inner // SSD_STATE, (inner + gn) // SSD_STATE
    hp = acsrow.shape[1]
    t_idx = jnp.arange(CHUNK)[None, :, None]
    d_idx = (kconv - 1 - jnp.arange(kconv - 1))[:, None, None]
    s_idx = jnp.arange(2 * CHUNK)[None, None, :]
    shift = (s_idx == CHUNK + t_idx - d_idx).astype(BF16).reshape((kconv - 1) * CHUNK, 2 * CHUNK)
    st_shape = (SSD_STATE, gw)
    blocks = [((ts, gw), BF16), ((ts, gw), BF16), ((ts, SSD_STATE), BF16), ((ts, SSD_STATE), BF16),
              ((CHUNK, gw), BF16), ((CHUNK, SSD_STATE), BF16), ((CHUNK, SSD_STATE), BF16),
              ((nc, hp, LANES), F32), ((nc, hp, LANES), F32), ((nc, hp, LANES), F32),
              ((1, ts, LANES), F32), (shift.shape, BF16),
              ((kconv, gw), F32), ((kconv, SSD_STATE), F32), ((kconv, SSD_STATE), F32),
              ((1, gw), F32), ((1, SSD_STATE), F32), ((1, SSD_STATE), F32), ((1, gw), F32), ((1, gw), F32),
              ((ts, gw), BF16)]
    scratch = [(st_shape, F32), ((CHUNK, gw), F32)]
    row = lambda b, g, c: b * nt + c
    prev = lambda b, g, c: jnp.maximum(row(b, g, c) * nc - 1, 0)
    return pl.pallas_call(
        functools.partial(_ssd_kernel, nc=nc, r_heads=r_heads, kconv=kconv),
        grid=(batch, groups, nt),
        in_specs=[
            pl.BlockSpec((ts, gw), lambda b, g, c: (row(b, g, c), g)),
            pl.BlockSpec((ts, gw), lambda b, g, c: (row(b, g, c), xs_blk + g)),
            pl.BlockSpec((ts, SSD_STATE), lambda b, g, c: (row(b, g, c), b_blk + g)),
            pl.BlockSpec((ts, SSD_STATE), lambda b, g, c: (row(b, g, c), c_blk + g)),
            pl.BlockSpec((CHUNK, gw), lambda b, g, c: (prev(b, g, c), xs_blk + g)),
            pl.BlockSpec((CHUNK, SSD_STATE), lambda b, g, c: (prev(b, g, c), b_blk + g)),
            pl.BlockSpec((CHUNK, SSD_STATE), lambda b, g, c: (prev(b, g, c), c_blk + g)),
            pl.BlockSpec((nc, hp, LANES), lambda b, g, c: (row(b, g, c), 0, 0)),
            pl.BlockSpec((nc, hp, LANES), lambda b, g, c: (row(b, g, c), 0, 0)),
            pl.BlockSpec((nc, hp, LANES), lambda b, g, c: (row(b, g, c), 0, 0)),
            pl.BlockSpec((1, ts, LANES), lambda b, g, c: (g, row(b, g, c), 0)),
            pl.BlockSpec(shift.shape, lambda b, g, c: (0, 0)),
            pl.BlockSpec((kconv, gw), lambda b, g, c: (0, g)),
            pl.BlockSpec((kconv, SSD_STATE), lambda b, g, c: (0, wb_blk + g)),
            pl.BlockSpec((kconv, SSD_STATE), lambda b, g, c: (0, wc_blk + g)),
            pl.BlockSpec((1, gw), lambda b, g, c: (0, g)),
            pl.BlockSpec((1, SSD_STATE), lambda b, g, c: (0, wb_blk + g)),
            pl.BlockSpec((1, SSD_STATE), lambda b, g, c: (0, wc_blk + g)),
            pl.BlockSpec((1, gw), lambda b, g, c: (0, g)),
            pl.BlockSpec((1, gw), lambda b, g, c: (0, g)),
        ],
        out_specs=pl.BlockSpec((ts, gw), lambda b, g, c: (row(b, g, c), g)),
        out_shape=jax.ShapeDtypeStruct((m, inner), BF16),
        scratch_shapes=[pltpu.VMEM(s, d) for s, d in scratch],
        compiler_params=_cparams(3, blocks, scratch, temps=8 << 20),
        name="ssd",
    )(proj, proj, proj, proj, proj, proj, proj, dtrow, wrow, acsrow, acscol, shift,
      conv_w, conv_w, conv_w, conv_b, conv_b, conv_b, d_skip, norm_w)


def _sconv_kernel(gb_ref, gc_ref, xt_ref, w_ref, o_ref, pe_ref, *, ts, kconv, cw):
    halo = SUBLANES

    @pl.when(pl.program_id(1) == 0)
    def _():
        pe_ref[0:halo, :] = jnp.zeros((halo, pe_ref.shape[1]), F32)

    pe_ref[halo:halo + ts, :] = gb_ref[...].astype(F32) * xt_ref[...].astype(F32)
    width = pe_ref.shape[1]
    for c0 in range(0, width, cw):
        acc = None
        for k in range(kconv):
            r = halo - (kconv - 1) + k
            term = pe_ref[r:r + ts, c0:c0 + cw] * w_ref[k:k + 1, c0:c0 + cw]
            acc = term if acc is None else acc + term
        o_ref[:, c0:c0 + cw] = (gc_ref[:, c0:c0 + cw].astype(F32) * acc).astype(o_ref.dtype)
    pe_ref[0:halo, :] = pe_ref[ts:ts + halo, :]


def _sconv(proj, conv_w, *, batch, seq, width, off_b, off_c, off_x):
    m = proj.shape[0]
    kconv = conv_w.shape[0]
    ts = _pick(seq, (256, 128))
    nt = seq // ts
    assert off_b % width == 0 and off_c % width == 0 and off_x % width == 0 and kconv - 1 <= SUBLANES
    jb, jc, jx = off_b // width, off_c // width, off_x // width
    cw = _pick(width, (512, 256, 128))
    pe_shape = (SUBLANES + ts, width)
    blocks = [((ts, width), BF16)] * 4 + [((kconv, width), F32)]
    scratch = [(pe_shape, F32)]
    return pl.pallas_call(
        functools.partial(_sconv_kernel, ts=ts, kconv=kconv, cw=cw),
        grid=(batch, nt),
        in_specs=[pl.BlockSpec((ts, width), lambda b, c: (b * nt + c, jb)),
                  pl.BlockSpec((ts, width), lambda b, c: (b * nt + c, jc)),
                  pl.BlockSpec((ts, width), lambda b, c: (b * nt + c, jx)),
                  pl.BlockSpec((kconv, width), lambda b, c: (0, 0))],
        out_specs=pl.BlockSpec((ts, width), lambda b, c: (b * nt + c, 0)),
        out_shape=jax.ShapeDtypeStruct((m, width), BF16),
        scratch_shapes=[pltpu.VMEM(s, d) for s, d in scratch],
        compiler_params=_cparams(2, blocks, scratch, temps=4 << 20),
        name="short_conv",
    )(proj, proj, proj, conv_w)


def kernel(x, p, norm_mix, w_in, ssd_conv_w, ssd_conv_b, ssd_dt_bias, ssd_a_log, ssd_d, ssd_norm,
           ssd_out, sc_conv_w, sc_out, w_o, norm_ffn, w_gate_up, w_down, norm_ple, ple_gate,
           ple_proj, norm_final):
    batch, seq, d = x.shape
    depth = w_in.shape[0]
    m = batch * seq
    heads = ssd_a_log.shape[1]
    inner = heads * HEAD_DIM
    groups = SSD_GROUPS
    r_heads = heads // groups
    xbc = ssd_conv_w.shape[2]
    scw = sc_conv_w.shape[2]
    assert xbc == inner + 2 * groups * SSD_STATE and heads <= LANES
    dt0 = inner + xbc
    off_scb = dt0
    off_scc, off_scx = off_scb + scw, off_scb + 2 * scw
    off_ga = off_scb + 3 * scw
    off_gb = off_ga + d
    hpad = LANES - heads
    p2 = p.reshape(depth, m, p.shape[-1])
    w_in_t = jnp.swapaxes(w_in, 1, 2)

    h = x.reshape(m, d)
    u, u_ssq = _prenorm(h, norm_mix[0])
    for i in range(depth):
        dt_bias = jnp.pad(ssd_dt_bias[i], (0, hpad)).reshape(1, LANES)
        a_log = jnp.pad(ssd_a_log[i], (0, hpad)).reshape(1, LANES)
        d_skip = jnp.repeat(ssd_d[i], HEAD_DIM).reshape(1, inner)

        proj = _inproj(u, u_ssq, w_in_t, i, dt0, heads)
        dtrow, wrow, acsrow, acscol = _dtprep(u, u_ssq, w_in_t, i, dt0, dt_bias, a_log, groups, r_heads)
        y = _ssd(proj, dtrow, wrow, acsrow, acscol, ssd_conv_w[i], ssd_conv_b[i].reshape(1, xbc), d_skip,
                 ssd_norm[i].reshape(1, inner), batch=batch, seq=seq, inner=inner, groups=groups)
        ysc = _sconv(proj, sc_conv_w[i], batch=batch, seq=seq, width=scw,
                     off_b=off_scb, off_c=off_scc, off_x=off_scx)
        merged = _merge(y, ysc, ssd_out, sc_out, i, proj, off_ga, off_gb)
        h, v, v_ssq = _matmul_residual(merged, w_o, i, h, norm_ffn[i], "w_o_residual", in_place=i > 0)
        act = _swiglu(v, v_ssq, w_gate_up, i)
        h, hn, hn_ssq = _matmul_residual(act, w_down, i, h, norm_ple[i], "w_down_residual")
        g_next = norm_mix[i + 1] if i + 1 < depth else norm_final
        h, u, u_ssq = _ple(hn, hn_ssq, ple_gate, p2, ple_proj, i, h, g_next)
    out = _rmsnorm(h, norm_final, F32)
    return out.reshape(batch, seq, d)
```

```python
import functools

import jax
import jax.numpy as jnp
from jax import lax
from jax.experimental import pallas as pl
from jax.experimental.pallas import tpu as pltpu

F32 = jnp.float32
BF16 = jnp.bfloat16

EPS = 1e-6
HEAD_DIM = 64
SSD_GROUPS = 8
SSD_STATE = 128
CHUNK = 128
LANES = 128
SUBLANES = 8
CAST_ROWS = 256
LOG2E = 1.4426950408889634
VMEM_BYTES_V7X = 64 * 1024 * 1024
VMEM_CAP = VMEM_BYTES_V7X - 4 * 1024 * 1024


def _nbytes(shape, dtype):
    n = 1
    for s in shape:
        n *= s
    return n * jnp.dtype(dtype).itemsize


def _cparams(ngrid, blocks, scratch=(), temps=0):
    need = (2 * sum(_nbytes(s, d) for s, d in blocks) + sum(_nbytes(s, d) for s, d in scratch)
            + temps + (4 << 20))
    return pltpu.CompilerParams(dimension_semantics=("arbitrary",) * ngrid,
                                vmem_limit_bytes=int(min(max(need, 16 << 20), VMEM_CAP)))


def _pick(n, candidates):
    for c in candidates:
        if n % c == 0:
            return c
    raise ValueError(f"no block size in {candidates} divides {n}")


def _cast_tile(w_ref, wsc_ref):
    k = w_ref.shape[0]
    rc = _pick(k, (CAST_ROWS, LANES, SUBLANES))
    for r in range(0, k, rc):
        wsc_ref[r:r + rc, :] = w_ref[r:r + rc, :].astype(BF16)


def _first_token_tile():
    return pl.program_id(1) == 0


def _sigmoid(x):
    return 0.5 * jnp.tanh(0.5 * x) + 0.5


def _silu(x):
    h = 0.5 * x
    return h * jnp.tanh(h) + h


def _emit_scaled(h_new, g_ref, hg_ref, ssq_ref):
    hg_ref[...] = (h_new * g_ref[...]).astype(BF16)
    col = jnp.sum(h_new * h_new, axis=-1, keepdims=True)
    ssq_ref[...] = jnp.transpose(jnp.broadcast_to(col, (col.shape[0], LANES)))[0:1, :]


def _row_rsqrt(ssq_ref, d):
    s = ssq_ref[0]
    for q in range(1, ssq_ref.shape[0]):
        s = s + ssq_ref[q]
    row = lax.rsqrt(s / d + EPS)
    return jnp.transpose(jnp.broadcast_to(row, (LANES, row.shape[1])))[:, 0:1]


def _prenorm_kernel(x_ref, g_ref, hg_ref, ssq_ref):
    _emit_scaled(x_ref[...], g_ref, hg_ref, ssq_ref.at[0])


def _prenorm(x, g):
    m, d = x.shape
    bm = _pick(m, (512, 256, 128))
    blocks = [((bm, d), F32), ((1, d), F32), ((bm, d), BF16), ((1, 1, bm), F32)]
    return pl.pallas_call(
        _prenorm_kernel,
        grid=(m // bm,),
        in_specs=[pl.BlockSpec((bm, d), lambda i: (i, 0)),
                  pl.BlockSpec((1, d), lambda i: (0, 0))],
        out_specs=[pl.BlockSpec((bm, d), lambda i: (i, 0)),
                   pl.BlockSpec((1, 1, bm), lambda i: (0, 0, i))],
        out_shape=[jax.ShapeDtypeStruct((m, d), BF16), jax.ShapeDtypeStruct((1, 1, m), F32)],
        compiler_params=_cparams(1, blocks, temps=2 * bm * d * 4),
        name="prenorm",
    )(x, g.reshape(1, d))


def _rmsnorm_kernel(x_ref, g_ref, o_ref):
    x = x_ref[...]
    ms = jnp.mean(x * x, axis=-1, keepdims=True)
    o_ref[...] = (x * lax.rsqrt(ms + EPS) * g_ref[...]).astype(o_ref.dtype)


def _rmsnorm(x, g, out_dtype):
    m, d = x.shape
    bm = _pick(m, (512, 256, 128))
    blocks = [((bm, d), F32), ((1, d), F32), ((bm, d), out_dtype)]
    return pl.pallas_call(
        _rmsnorm_kernel,
        grid=(m // bm,),
        in_specs=[pl.BlockSpec((bm, d), lambda i: (i, 0)),
                  pl.BlockSpec((1, d), lambda i: (0, 0))],
        out_specs=pl.BlockSpec((bm, d), lambda i: (i, 0)),
        out_shape=jax.ShapeDtypeStruct((m, d), out_dtype),
        compiler_params=_cparams(1, blocks, temps=2 * bm * d * 4),
        name="rmsnorm",
    )(x, g.reshape(1, d))


def _inproj_kernel(a_ref, ssq_ref, w_ref, wn_ref, o_ref, wsc_ref, *, n_aligned, skip):
    j = pl.program_id(0)
    bn = w_ref.shape[0]
    rc = _pick(bn, (CAST_ROWS, LANES))

    @pl.when(jnp.logical_and(_first_token_tile(), j < n_aligned))
    def _():
        _cast_tile(w_ref, wsc_ref)

    @pl.when(jnp.logical_and(_first_token_tile(), j >= n_aligned))
    def _():
        for r in range(0, bn - rc, rc):
            wsc_ref[r:r + rc, :] = w_ref[r + skip:r + skip + rc, :].astype(BF16)
        wsc_ref[bn - rc:bn - skip, :] = w_ref[bn - rc + skip:bn, :].astype(BF16)
        wsc_ref[bn - skip:bn, :] = wn_ref[...].astype(BF16)

    acc = lax.dot_general(a_ref[...], wsc_ref[...], (((1,), (1,)), ((), ())), preferred_element_type=F32)
    o_ref[...] = (acc * _row_rsqrt(ssq_ref, a_ref.shape[1])).astype(o_ref.dtype)


def _inproj(u, ssq, w_in_t, layer, dt0, heads):
    m, k = u.shape
    parts = ssq.shape[0]
    d_in = w_in_t.shape[1]
    n = d_in - heads
    bm = _pick(m, (1024, 512, 256, 128))
    bn = next(c for c in (1024, 512, 256, 128) if dt0 % c == 0 and (n - dt0) % c == 0)
    assert heads % (2 * SUBLANES) == 0 and bn % heads == 0 and d_in % heads == 0 and heads < LANES
    n_aligned = dt0 // bn
    blocks = [((bm, k), BF16), ((parts, 1, bm), F32), ((bn, k), F32), ((heads, k), F32), ((bm, bn), BF16)]
    scratch = [((bn, k), BF16)]
    return pl.pallas_call(
        functools.partial(_inproj_kernel, n_aligned=n_aligned, skip=heads),
        grid=(n // bn, m // bm),
        in_specs=[pl.BlockSpec((bm, k), lambda j, i: (i, 0)),
                  pl.BlockSpec((parts, 1, bm), lambda j, i: (0, 0, i)),
                  pl.BlockSpec((None, bn, k), lambda j, i: (layer, j, 0)),
                  pl.BlockSpec((None, heads, k), lambda j, i: (layer, (j + 1) * (bn // heads), 0))],
        out_specs=pl.BlockSpec((bm, bn), lambda j, i: (i, j)),
        out_shape=jax.ShapeDtypeStruct((m, n), BF16),
        scratch_shapes=[pltpu.VMEM(s, d) for s, d in scratch],
        compiler_params=_cparams(2, blocks, scratch, temps=bm * bn * 4 + 2 * CAST_ROWS * k * 4),
        name="in_proj",
    )(u, ssq, w_in_t, w_in_t)


def _mm_res_kernel(a_ref, w_ref, h_ref, g_ref, o_ref, hg_ref, ssq_ref, wsc_ref):
    @pl.when(_first_token_tile())
    def _():
        _cast_tile(w_ref, wsc_ref)

    h_new = h_ref[...] + jnp.dot(a_ref[...], wsc_ref[...], preferred_element_type=F32)
    o_ref[...] = h_new
    _emit_scaled(h_new, g_ref, hg_ref, ssq_ref)


def _matmul_residual(a, w, layer, h, g_next, name, in_place=True):
    m, k = a.shape
    n = w.shape[2]
    big_k = k > 2048
    bm = _pick(m, (512, 256, 128)) if big_k else _pick(m, (1024, 512, 256, 128))
    bn = _pick(n, (512, 256, 128)) if big_k else _pick(n, (1024, 512, 256, 128))
    blocks = [((bm, k), BF16), ((k, bn), F32), ((bm, bn), F32), ((1, bn), F32),
              ((bm, bn), F32), ((bm, bn), BF16), ((1, bm), F32)]
    scratch = [((k, bn), BF16)]
    return pl.pallas_call(
        _mm_res_kernel,
        grid=(n // bn, m // bm),
        in_specs=[pl.BlockSpec((bm, k), lambda j, i: (i, 0)),
                  pl.BlockSpec((None, k, bn), lambda j, i: (layer, 0, j)),
                  pl.BlockSpec((bm, bn), lambda j, i: (i, j)),
                  pl.BlockSpec((1, bn), lambda j, i: (0, j))],
        out_specs=[pl.BlockSpec((bm, bn), lambda j, i: (i, j)),
                   pl.BlockSpec((bm, bn), lambda j, i: (i, j)),
                   pl.BlockSpec((None, 1, bm), lambda j, i: (j, 0, i))],
        out_shape=[jax.ShapeDtypeStruct((m, n), F32), jax.ShapeDtypeStruct((m, n), BF16),
                   jax.ShapeDtypeStruct((n // bn, 1, m), F32)],
        scratch_shapes=[pltpu.VMEM(s, d) for s, d in scratch],
        input_output_aliases={2: 0} if in_place else {},
        compiler_params=_cparams(2, blocks, scratch, temps=2 * bm * bn * 4 + 2 * CAST_ROWS * bn * 4),
        name=name,
    )(a, w, h, g_next.reshape(1, n))


def _swiglu_kernel(v_ref, ssq_ref, wg_ref, wu_ref, o_ref, wgsc_ref, wusc_ref):
    @pl.when(_first_token_tile())
    def _():
        _cast_tile(wg_ref, wgsc_ref)
        _cast_tile(wu_ref, wusc_ref)

    v = v_ref[...]
    rs = _row_rsqrt(ssq_ref, v.shape[1])
    gate = jnp.dot(v, wgsc_ref[...], preferred_element_type=F32) * rs
    up = jnp.dot(v, wusc_ref[...], preferred_element_type=F32) * rs
    o_ref[...] = (_silu(gate) * up).astype(o_ref.dtype)


def _swiglu(v, ssq, w_gate_up, layer):
    m, k = v.shape
    parts = ssq.shape[0]
    d_ff = w_gate_up.shape[2] // 2
    bm = _pick(m, (1024, 512, 256, 128))
    bn = _pick(d_ff, (512, 256, 128))
    nb = d_ff // bn
    blocks = [((bm, k), BF16), ((parts, 1, bm), F32), ((k, bn), F32), ((k, bn), F32), ((bm, bn), BF16)]
    scratch = [((k, bn), BF16), ((k, bn), BF16)]
    return pl.pallas_call(
        _swiglu_kernel,
        grid=(nb, m // bm),
        in_specs=[pl.BlockSpec((bm, k), lambda j, i: (i, 0)),
                  pl.BlockSpec((parts, 1, bm), lambda j, i: (0, 0, i)),
                  pl.BlockSpec((None, k, bn), lambda j, i: (layer, 0, j)),
                  pl.BlockSpec((None, k, bn), lambda j, i: (layer, 0, j + nb))],
        out_specs=pl.BlockSpec((bm, bn), lambda j, i: (i, j)),
        out_shape=jax.ShapeDtypeStruct((m, d_ff), BF16),
        scratch_shapes=[pltpu.VMEM(s, d) for s, d in scratch],
        compiler_params=_cparams(2, blocks, scratch, temps=3 * bm * bn * 4 + 2 * CAST_ROWS * bn * 4),
        name="swiglu_up",
    )(v, ssq, w_gate_up, w_gate_up)


def _merge_kernel(y_ref, ysc_ref, wa_ref, wb_ref, ga_ref, gb_ref, o_ref, wasc_ref, wbsc_ref):
    @pl.when(_first_token_tile())
    def _():
        _cast_tile(wa_ref, wasc_ref)
        _cast_tile(wb_ref, wbsc_ref)

    ya = jnp.dot(y_ref[...], wasc_ref[...], preferred_element_type=F32)
    yb = jnp.dot(ysc_ref[...], wbsc_ref[...], preferred_element_type=F32)
    ga = _sigmoid(ga_ref[...].astype(F32))
    gb = _sigmoid(gb_ref[...].astype(F32))
    o_ref[...] = (ga * ya + gb * yb).astype(o_ref.dtype)


def _merge(y, ysc, w_a, w_b, layer, proj, off_ga, off_gb):
    m, ka = y.shape
    kb = ysc.shape[1]
    n = w_a.shape[2]
    bm = _pick(m, (512, 256, 128))
    bn = _pick(n, (512, 256, 128))
    ja, jb = off_ga // bn, off_gb // bn
    assert off_ga % bn == 0 and off_gb % bn == 0
    blocks = [((bm, ka), BF16), ((bm, kb), BF16), ((ka, bn), F32), ((kb, bn), F32),
              ((bm, bn), BF16), ((bm, bn), BF16), ((bm, bn), BF16)]
    scratch = [((ka, bn), BF16), ((kb, bn), BF16)]
    return pl.pallas_call(
        _merge_kernel,
        grid=(n // bn, m // bm),
        in_specs=[pl.BlockSpec((bm, ka), lambda j, i: (i, 0)),
                  pl.BlockSpec((bm, kb), lambda j, i: (i, 0)),
                  pl.BlockSpec((None, ka, bn), lambda j, i: (layer, 0, j)),
                  pl.BlockSpec((None, kb, bn), lambda j, i: (layer, 0, j)),
                  pl.BlockSpec((bm, bn), lambda j, i: (i, j + ja)),
                  pl.BlockSpec((bm, bn), lambda j, i: (i, j + jb))],
        out_specs=pl.BlockSpec((bm, bn), lambda j, i: (i, j)),
        out_shape=jax.ShapeDtypeStruct((m, n), BF16),
        scratch_shapes=[pltpu.VMEM(s, d) for s, d in scratch],
        compiler_params=_cparams(2, blocks, scratch, temps=4 * bm * bn * 4 + 2 * CAST_ROWS * bn * 4),
        name="branch_merge",
    )(y, ysc, w_a, w_b, proj, proj)


def _ple_kernel(hn_ref, ssq_ref, wg_ref, p_ref, wp_ref, h_ref, g_ref, o_ref, hg_ref, ssqo_ref,
                wgsc_ref, wpsc_ref):
    @pl.when(_first_token_tile())
    def _():
        _cast_tile(wg_ref, wgsc_ref)
        _cast_tile(wp_ref, wpsc_ref)

    rs = _row_rsqrt(ssq_ref, hn_ref.shape[1])
    pg = _sigmoid(jnp.dot(hn_ref[...], wgsc_ref[...], preferred_element_type=F32) * rs)
    e = jnp.dot(p_ref[...].astype(BF16), wpsc_ref[...], preferred_element_type=F32)
    h_new = h_ref[...] + pg * e
    o_ref[...] = h_new
    _emit_scaled(h_new, g_ref, hg_ref, ssqo_ref)


def _ple(hn, ssq, w_gate, p, w_proj, layer, h, g_next):
    m, k = hn.shape
    parts = ssq.shape[0]
    kp = p.shape[2]
    n = w_gate.shape[2]
    bm = _pick(m, (1024, 512, 256, 128))
    bn = _pick(n, (1024, 512, 256, 128))
    blocks = [((bm, k), BF16), ((parts, 1, bm), F32), ((k, bn), F32), ((bm, kp), F32), ((kp, bn), F32),
              ((bm, bn), F32), ((1, bn), F32), ((bm, bn), F32), ((bm, bn), BF16), ((1, bm), F32)]
    scratch = [((k, bn), BF16), ((kp, bn), BF16)]
    return pl.pallas_call(
        _ple_kernel,
        grid=(n // bn, m // bm),
        in_specs=[pl.BlockSpec((bm, k), lambda j, i: (i, 0)),
                  pl.BlockSpec((parts, 1, bm), lambda j, i: (0, 0, i)),
                  pl.BlockSpec((None, k, bn), lambda j, i: (layer, 0, j)),
                  pl.BlockSpec((None, bm, kp), lambda j, i: (layer, i, 0)),
                  pl.BlockSpec((None, kp, bn), lambda j, i: (layer, 0, j)),
                  pl.BlockSpec((bm, bn), lambda j, i: (i, j)),
                  pl.BlockSpec((1, bn), lambda j, i: (0, j))],
        out_specs=[pl.BlockSpec((bm, bn), lambda j, i: (i, j)),
                   pl.BlockSpec((bm, bn), lambda j, i: (i, j)),
                   pl.BlockSpec((None, 1, bm), lambda j, i: (j, 0, i))],
        out_shape=[jax.ShapeDtypeStruct((m, n), F32), jax.ShapeDtypeStruct((m, n), BF16),
                   jax.ShapeDtypeStruct((n // bn, 1, m), F32)],
        scratch_shapes=[pltpu.VMEM(s, d) for s, d in scratch],
        input_output_aliases={5: 0},
        compiler_params=_cparams(2, blocks, scratch, temps=4 * bm * bn * 4 + 2 * CAST_ROWS * bn * 4),
        name="ple",
    )(hn, ssq, w_gate, p, w_proj, h, g_next.reshape(1, n))


def _dtprep_kernel(u_ref, ssq_ref, w_ref, bias_ref, alog_ref, dtrow_ref, wrow_ref, acsrow_ref, acscol_ref,
                   *, nc, groups, r_heads):
    x = lax.dot_general(u_ref[...], w_ref[...].astype(BF16), (((1,), (1,)), ((), ())),
                        preferred_element_type=F32)
    x = x * _row_rsqrt(ssq_ref, u_ref.shape[1]) + bias_ref[...]
    dt = jnp.maximum(x, 0.0) + jnp.log1p(jnp.exp(-jnp.abs(x)))
    adt = dt * (-jnp.exp(alog_ref[...]))
    row = lax.broadcasted_iota(jnp.int32, (CHUNK, LANES), 0)
    for k in range(nc):
        sl = slice(k * CHUNK, (k + 1) * CHUNK)
        acs = adt[sl]
        sh = 1
        while sh < CHUNK:
            acs = acs + jnp.where(row >= sh, pltpu.roll(acs, sh, 0), 0.0)
            sh *= 2
        acs2 = acs * LOG2E
        acsrow_ref[k] = acs2.T
        dtrow_ref[k] = dt[sl].T
        wrow_ref[k] = (dt[sl] * jnp.exp(acs[CHUNK - 1:CHUNK, :] - acs)).T
        for g in range(groups):
            shift = (LANES - g * r_heads) % LANES
            acscol_ref[g, sl, :] = pltpu.roll(acs2, shift, 1) if shift else acs2


def _dtprep(u, ssq, w_in_t, layer, dt0, bias, a_log, groups, r_heads):
    m, k = u.shape
    parts = ssq.shape[0]
    nc = 4 if m % (4 * CHUNK) == 0 else 1
    ts = nc * CHUNK
    nchunks = m // CHUNK
    rows = ((nc, LANES, LANES), F32)
    blocks = [((ts, k), BF16), ((parts, 1, ts), F32), ((LANES, k), F32), rows, rows, rows,
              ((groups, ts, LANES), F32)]
    row_spec = pl.BlockSpec((nc, LANES, LANES), lambda i: (i, 0, 0))
    row_shape = jax.ShapeDtypeStruct((nchunks, LANES, LANES), F32)
    return pl.pallas_call(
        functools.partial(_dtprep_kernel, nc=nc, groups=groups, r_heads=r_heads),
        grid=(m // ts,),
        in_specs=[pl.BlockSpec((ts, k), lambda i: (i, 0)),
                  pl.BlockSpec((parts, 1, ts), lambda i: (0, 0, i)),
                  pl.BlockSpec((None, LANES, k), lambda i: (layer, dt0 // LANES, 0)),
                  pl.BlockSpec((1, LANES), lambda i: (0, 0)),
                  pl.BlockSpec((1, LANES), lambda i: (0, 0))],
        out_specs=[row_spec, row_spec, row_spec,
                   pl.BlockSpec((groups, ts, LANES), lambda i: (0, i, 0))],
        out_shape=[row_shape, row_shape, row_shape,
                   jax.ShapeDtypeStruct((groups, m, LANES), F32)],
        compiler_params=_cparams(1, blocks, temps=8 * ts * LANES * 4 + k * LANES * 2),
        name="dt_prep",
    )(u, ssq, w_in_t, bias, a_log)


def _ssd_kernel(z_ref, xs_ref, b_ref, c_ref, xsp_ref, bp_ref, cp_ref,
                dtr_ref, wr_ref, acr_ref, acc_ref, shift_ref,
                cwx_ref, cwb_ref, cwc_ref, cbx_ref, cbb_ref, cbc_ref, dsk_ref, nw_ref,
                o_ref, st_ref, yt_ref, *, nc, r_heads, kconv):
    g = pl.program_id(1)
    c = pl.program_id(2)
    L = CHUNK
    N = SSD_STATE
    gw = r_heads * HEAD_DIM
    npair = gw // LANES

    @pl.when(c == 0)
    def _():
        st_ref[...] = jnp.zeros(st_ref.shape, F32)

    tri = (lax.broadcasted_iota(jnp.int32, (L, L), 0) >= lax.broadcasted_iota(jnp.int32, (L, L), 1))
    lo = lax.broadcasted_iota(jnp.int32, (L, LANES), 1) < HEAD_DIM
    hi = jnp.logical_not(lo)
    lo_row = lo[0:1]

    def two_chunks(ref, prev_ref, k):
        if k == 0:
            prev = prev_ref[...]
            prev = jnp.where(c > 0, prev, jnp.zeros_like(prev))
            return jnp.concatenate([prev, ref[0:L, :]], axis=0)
        return ref[(k - 1) * L:(k + 1) * L, :]

    def conv_silu(shifted, cur, w_ref, bias_ref):
        acc = None
        for tap in range(kconv - 1):
            term = shifted[tap * L:(tap + 1) * L, :] * w_ref[tap:tap + 1, :]
            acc = term if acc is None else acc + term
        acc = acc + cur.astype(F32) * w_ref[kconv - 1:kconv, :]
        acc = acc + bias_ref[...]
        return _silu(acc)

    for k in range(nc):
        r0 = k * L
        x2 = jnp.concatenate([two_chunks(xs_ref, xsp_ref, k), two_chunks(b_ref, bp_ref, k),
                              two_chunks(c_ref, cp_ref, k)], axis=1)
        shifted = jnp.dot(shift_ref[...], x2, preferred_element_type=F32)
        cur = x2[L:2 * L]
        x = conv_silu(shifted[:, 0:gw], cur[:, 0:gw], cwx_ref, cbx_ref)
        bm = conv_silu(shifted[:, gw:gw + N], cur[:, gw:gw + N], cwb_ref, cbb_ref)
        cm = conv_silu(shifted[:, gw + N:gw + 2 * N], cur[:, gw + N:gw + 2 * N], cwc_ref, cbc_ref)
        cb = lax.dot_general(cm.astype(BF16), bm.astype(BF16), (((1,), (1,)), ((), ())),
                             preferred_element_type=F32)
        bt = bm.T
        acol = acc_ref[0, r0:r0 + L, :]
        cdec = jnp.exp2(acol[L - 1:L, :])
        ssq = jnp.zeros((L, 1), F32)
        for j in range(npair):
            cs = slice(j * LANES, (j + 1) * LANES)
            xp = x[:, cs]
            s_prev = st_ref[:, cs]
            s_new = s_prev * jnp.where(lo_row, cdec[:, 2 * j:2 * j + 1], cdec[:, 2 * j + 1:2 * j + 2])
            y = None
            for hh, keep in ((2 * j, lo), (2 * j + 1, hi)):
                head = pl.ds(g * r_heads + hh, 1)
                arow = acr_ref[k, head, :]
                drow = dtr_ref[k, head, :]
                wrow = wr_ref[k, head, :]
                xm = jnp.where(keep, xp, 0.0).astype(BF16)
                sm = jnp.where(keep, s_prev, 0.0).astype(BF16)
                ab = jnp.broadcast_to(acol[:, hh:hh + 1], (L, L))
                dec = jnp.exp2(jnp.where(tri, ab - arow, -jnp.inf))
                mh = cb * dec * drow
                ch = cm * jnp.exp2(ab)
                lhs = jnp.concatenate([mh, ch], axis=1).astype(BF16)
                rhs = jnp.concatenate([xm, sm], axis=0)
                t = jnp.dot(lhs, rhs, preferred_element_type=F32)
                y = t if y is None else y + t
                bth = (bt * wrow).astype(BF16)
                s_new = s_new + jnp.dot(bth, xm, preferred_element_type=F32)
            st_ref[:, cs] = s_new
            yt = y + dsk_ref[:, cs] * xp
            zt = z_ref[r0:r0 + L, cs].astype(F32)
            yt = yt * _silu(zt)
            ssq = ssq + jnp.sum(yt * yt, axis=-1, keepdims=True)
            yt_ref[:, cs] = yt
        rs = lax.rsqrt(ssq / gw + EPS)
        for j in range(npair):
            cs = slice(j * LANES, (j + 1) * LANES)
            o_ref[r0:r0 + L, cs] = (yt_ref[:, cs] * rs * nw_ref[:, cs]).astype(o_ref.dtype)


def _ssd(proj, dtrow, wrow, acsrow, acscol, conv_w, conv_b, d_skip, norm_w, *, batch, seq, inner, groups):
    m = proj.shape[0]
    gw = inner // groups
    r_heads = gw // HEAD_DIM
    assert gw % LANES == 0 and r_heads % 2 == 0 and SSD_STATE == LANES
    kconv = conv_w.shape[0]
    assert kconv - 1 <= CHUNK
    nc = _pick(seq // CHUNK, (16, 8, 4, 2, 1))
    ts = nc * CHUNK
    nt = seq // ts
    gn = groups * SSD_STATE
    xs_blk, b_blk, c_blk = inner // gw, 2 * inner // SSD_STATE, (2 * inner + gn) // SSD_STATE
    wb_blk, wc_blk = inner // SSD_STATE, (inner + gn) // SSD_STATE
    hp = acsrow.shape[1]
    t_idx = jnp.arange(CHUNK)[None, :, None]
    d_idx = (kconv - 1 - jnp.arange(kconv - 1))[:, None, None]
    s_idx = jnp.arange(2 * CHUNK)[None, None, :]
    shift = (s_idx == CHUNK + t_idx - d_idx).astype(BF16).reshape((kconv - 1) * CHUNK, 2 * CHUNK)
    st_shape = (SSD_STATE, gw)
    blocks = [((ts, gw), BF16), ((ts, gw), BF16), ((ts, SSD_STATE), BF16), ((ts, SSD_STATE), BF16),
              ((CHUNK, gw), BF16), ((CHUNK, SSD_STATE), BF16), ((CHUNK, SSD_STATE), BF16),
              ((nc, hp, LANES), F32), ((nc, hp, LANES), F32), ((nc, hp, LANES), F32),
              ((1, ts, LANES), F32), (shift.shape, BF16),
              ((kconv, gw), F32), ((kconv, SSD_STATE), F32), ((kconv, SSD_STATE), F32),
              ((1, gw), F32), ((1, SSD_STATE), F32), ((1, SSD_STATE), F32), ((1, gw), F32), ((1, gw), F32),
              ((ts, gw), BF16)]
    scratch = [(st_shape, F32), ((CHUNK, gw), F32)]
    row = lambda b, g, c: b * nt + c
    prev = lambda b, g, c: jnp.maximum(row(b, g, c) * nc - 1, 0)
    return pl.pallas_call(
        functools.partial(_ssd_kernel, nc=nc, r_heads=r_heads, kconv=kconv),
        grid=(batch, groups, nt),
        in_specs=[
            pl.BlockSpec((ts, gw), lambda b, g, c: (row(b, g, c), g)),
            pl.BlockSpec((ts, gw), lambda b, g, c: (row(b, g, c), xs_blk + g)),
            pl.BlockSpec((ts, SSD_STATE), lambda b, g, c: (row(b, g, c), b_blk + g)),
            pl.BlockSpec((ts, SSD_STATE), lambda b, g, c: (row(b, g, c), c_blk + g)),
            pl.BlockSpec((CHUNK, gw), lambda b, g, c: (prev(b, g, c), xs_blk + g)),
            pl.BlockSpec((CHUNK, SSD_STATE), lambda b, g, c: (prev(b, g, c), b_blk + g)),
            pl.BlockSpec((CHUNK, SSD_STATE), lambda b, g, c: (prev(b, g, c), c_blk + g)),
            pl.BlockSpec((nc, hp, LANES), lambda b, g, c: (row(b, g, c), 0, 0)),
            pl.BlockSpec((nc, hp, LANES), lambda b, g, c: (row(b, g, c), 0, 0)),
            pl.BlockSpec((nc, hp, LANES), lambda b, g, c: (row(b, g, c), 0, 0)),
            pl.BlockSpec((1, ts, LANES), lambda b, g, c: (g, row(b, g, c), 0)),
            pl.BlockSpec(shift.shape, lambda b, g, c: (0, 0)),
            pl.BlockSpec((kconv, gw), lambda b, g, c: (0, g)),
            pl.BlockSpec((kconv, SSD_STATE), lambda b, g, c: (0, wb_blk + g)),
            pl.BlockSpec((kconv, SSD_STATE), lambda b, g, c: (0, wc_blk + g)),
            pl.BlockSpec((1, gw), lambda b, g, c: (0, g)),
            pl.BlockSpec((1, SSD_STATE), lambda b, g, c: (0, wb_blk + g)),
            pl.BlockSpec((1, SSD_STATE), lambda b, g, c: (0, wc_blk + g)),
            pl.BlockSpec((1, gw), lambda b, g, c: (0, g)),
            pl.BlockSpec((1, gw), lambda b, g, c: (0, g)),
        ],
        out_specs=pl.BlockSpec((ts, gw), lambda b, g, c: (row(b, g, c), g)),
        out_shape=jax.ShapeDtypeStruct((m, inner), BF16),
        scratch_shapes=[pltpu.VMEM(s, d) for s, d in scratch],
        compiler_params=_cparams(3, blocks, scratch, temps=8 << 20),
        name="ssd",
    )(proj, proj, proj, proj, proj, proj, proj, dtrow, wrow, acsrow, acscol, shift,
      conv_w, conv_w, conv_w, conv_b, conv_b, conv_b, d_skip, norm_w)


def _sconv_kernel(gb_ref, gc_ref, xt_ref, w_ref, o_ref, pe_ref, *, ts, kconv, cw):
    halo = SUBLANES

    @pl.when(pl.program_id(1) == 0)
    def _():
        pe_ref[0:halo, :] = jnp.zeros((halo, pe_ref.shape[1]), F32)

    pe_ref[halo:halo + ts, :] = gb_ref[...].astype(F32) * xt_ref[...].astype(F32)
    width = pe_ref.shape[1]
    for c0 in range(0, width, cw):
        acc = None
        for k in range(kconv):
            r = halo - (kconv - 1) + k
            term = pe_ref[r:r + ts, c0:c0 + cw] * w_ref[k:k + 1, c0:c0 + cw]
            acc = term if acc is None else acc + term
        o_ref[:, c0:c0 + cw] = (gc_ref[:, c0:c0 + cw].astype(F32) * acc).astype(o_ref.dtype)
    pe_ref[0:halo, :] = pe_ref[ts:ts + halo, :]


def _sconv(proj, conv_w, *, batch, seq, width, off_b, off_c, off_x):
    m = proj.shape[0]
    kconv = conv_w.shape[0]
    ts = _pick(seq, (256, 128))
    nt = seq // ts
    assert off_b % width == 0 and off_c % width == 0 and off_x % width == 0 and kconv - 1 <= SUBLANES
    jb, jc, jx = off_b // width, off_c // width, off_x // width
    cw = _pick(width, (512, 256, 128))
    pe_shape = (SUBLANES + ts, width)
    blocks = [((ts, width), BF16)] * 4 + [((kconv, width), F32)]
    scratch = [(pe_shape, F32)]
    return pl.pallas_call(
        functools.partial(_sconv_kernel, ts=ts, kconv=kconv, cw=cw),
        grid=(batch, nt),
        in_specs=[pl.BlockSpec((ts, width), lambda b, c: (b * nt + c, jb)),
                  pl.BlockSpec((ts, width), lambda b, c: (b * nt + c, jc)),
                  pl.BlockSpec((ts, width), lambda b, c: (b * nt + c, jx)),
                  pl.BlockSpec((kconv, width), lambda b, c: (0, 0))],
        out_specs=pl.BlockSpec((ts, width), lambda b, c: (b * nt + c, 0)),
        out_shape=jax.ShapeDtypeStruct((m, width), BF16),
        scratch_shapes=[pltpu.VMEM(s, d) for s, d in scratch],
        compiler_params=_cparams(2, blocks, scratch, temps=4 << 20),
        name="short_conv",
    )(proj, proj, proj, conv_w)


def kernel(x, p, norm_mix, w_in, ssd_conv_w, ssd_conv_b, ssd_dt_bias, ssd_a_log, ssd_d, ssd_norm,
           ssd_out, sc_conv_w, sc_out, w_o, norm_ffn, w_gate_up, w_down, norm_ple, ple_gate,
           ple_proj, norm_final):
    batch, seq, d = x.shape
    depth = w_in.shape[0]
    m = batch * seq
    heads = ssd_a_log.shape[1]
    inner = heads * HEAD_DIM
    groups = SSD_GROUPS
    r_heads = heads // groups
    xbc = ssd_conv_w.shape[2]
    scw = sc_conv_w.shape[2]
    assert xbc == inner + 2 * groups * SSD_STATE and heads <= LANES
    dt0 = inner + xbc
    off_scb = dt0
    off_scc, off_scx = off_scb + scw, off_scb + 2 * scw
    off_ga = off_scb + 3 * scw
    off_gb = off_ga + d
    hpad = LANES - heads
    p2 = p.reshape(depth, m, p.shape[-1])
    w_in_t = jnp.swapaxes(w_in, 1, 2)

    h = x.reshape(m, d)
    u, u_ssq = _prenorm(h, norm_mix[0])
    for i in range(depth):
        dt_bias = jnp.pad(ssd_dt_bias[i], (0, hpad)).reshape(1, LANES)
        a_log = jnp.pad(ssd_a_log[i], (0, hpad)).reshape(1, LANES)
        d_skip = jnp.repeat(ssd_d[i], HEAD_DIM).reshape(1, inner)

        proj = _inproj(u, u_ssq, w_in_t, i, dt0, heads)
        dtrow, wrow, acsrow, acscol = _dtprep(u, u_ssq, w_in_t, i, dt0, dt_bias, a_log, groups, r_heads)
        y = _ssd(proj, dtrow, wrow, acsrow, acscol, ssd_conv_w[i], ssd_conv_b[i].reshape(1, xbc), d_skip,
                 ssd_norm[i].reshape(1, inner), batch=batch, seq=seq, inner=inner, groups=groups)
        ysc = _sconv(proj, sc_conv_w[i], batch=batch, seq=seq, width=scw,
                     off_b=off_scb, off_c=off_scc, off_x=off_scx)
        merged = _merge(y, ysc, ssd_out, sc_out, i, proj, off_ga, off_gb)
        h, v, v_ssq = _matmul_residual(merged, w_o, i, h, norm_ffn[i], "w_o_residual", in_place=i > 0)
        act = _swiglu(v, v_ssq, w_gate_up, i)
        h, hn, hn_ssq = _matmul_residual(act, w_down, i, h, norm_ple[i], "w_down_residual")
        g_next = norm_mix[i + 1] if i + 1 < depth else norm_final
        h, u, u_ssq = _ple(hn, hn_ssq, ple_gate, p2, ple_proj, i, h, g_next)
    out = _rmsnorm(h, norm_final, F32)
    return out.reshape(batch, seq, d)
```

```python
import functools

import jax
import jax.numpy as jnp
from jax import lax
from jax.experimental import pallas as pl
from jax.experimental.pallas import tpu as pltpu

F32 = jnp.float32
BF16 = jnp.bfloat16

EPS = 1e-6
HEAD_DIM = 64
SSD_GROUPS = 8
SSD_STATE = 128
CHUNK = 128
LANES = 128
SUBLANES = 8
CAST_ROWS = 256
LOG2E = 1.4426950408889634
VMEM_BYTES_V7X = 64 * 1024 * 1024
VMEM_CAP = VMEM_BYTES_V7X - 4 * 1024 * 1024


def _nbytes(shape, dtype):
    n = 1
    for s in shape:
        n *= s
    return n * jnp.dtype(dtype).itemsize


def _cparams(ngrid, blocks, scratch=(), temps=0):
    need = (2 * sum(_nbytes(s, d) for s, d in blocks) + sum(_nbytes(s, d) for s, d in scratch)
            + temps + (4 << 20))
    return pltpu.CompilerParams(dimension_semantics=("arbitrary",) * ngrid,
                                vmem_limit_bytes=int(min(max(need, 16 << 20), VMEM_CAP)))


def _pick(n, candidates):
    for c in candidates:
        if n % c == 0:
            return c
    raise ValueError(f"no block size in {candidates} divides {n}")


def _cast_tile(w_ref, wsc_ref):
    k = w_ref.shape[0]
    rc = _pick(k, (CAST_ROWS, LANES, SUBLANES))
    for r in range(0, k, rc):
        wsc_ref[r:r + rc, :] = w_ref[r:r + rc, :].astype(BF16)


def _first_token_tile():
    return pl.program_id(1) == 0


def _sigmoid(x):
    return 0.5 * jnp.tanh(0.5 * x) + 0.5


def _silu(x):
    h = 0.5 * x
    return h * jnp.tanh(h) + h


def _emit_scaled(h_new, g_ref, hg_ref, ssq_ref):
    hg_ref[...] = (h_new * g_ref[...]).astype(BF16)
    col = jnp.sum(h_new * h_new, axis=-1, keepdims=True)
    ssq_ref[...] = jnp.transpose(jnp.broadcast_to(col, (col.shape[0], LANES)))[0:1, :]


def _row_rsqrt(ssq_ref, d):
    s = ssq_ref[0]
    for q in range(1, ssq_ref.shape[0]):
        s = s + ssq_ref[q]
    row = lax.rsqrt(s / d + EPS)
    return jnp.transpose(jnp.broadcast_to(row, (LANES, row.shape[1])))[:, 0:1]


def _prenorm_kernel(x_ref, g_ref, hg_ref, ssq_ref):
    _emit_scaled(x_ref[...], g_ref, hg_ref, ssq_ref.at[0])


def _prenorm(x, g):
    m, d = x.shape
    bm = _pick(m, (512, 256, 128))
    blocks = [((bm, d), F32), ((1, d), F32), ((bm, d), BF16), ((1, 1, bm), F32)]
    return pl.pallas_call(
        _prenorm_kernel,
        grid=(m // bm,),
        in_specs=[pl.BlockSpec((bm, d), lambda i: (i, 0)),
                  pl.BlockSpec((1, d), lambda i: (0, 0))],
        out_specs=[pl.BlockSpec((bm, d), lambda i: (i, 0)),
                   pl.BlockSpec((1, 1, bm), lambda i: (0, 0, i))],
        out_shape=[jax.ShapeDtypeStruct((m, d), BF16), jax.ShapeDtypeStruct((1, 1, m), F32)],
        compiler_params=_cparams(1, blocks, temps=2 * bm * d * 4),
        name="prenorm",
    )(x, g.reshape(1, d))


def _rmsnorm_kernel(x_ref, g_ref, o_ref):
    x = x_ref[...]
    ms = jnp.mean(x * x, axis=-1, keepdims=True)
    o_ref[...] = (x * lax.rsqrt(ms + EPS) * g_ref[...]).astype(o_ref.dtype)


def _rmsnorm(x, g, out_dtype):
    m, d = x.shape
    bm = _pick(m, (512, 256, 128))
    blocks = [((bm, d), F32), ((1, d), F32), ((bm, d), out_dtype)]
    return pl.pallas_call(
        _rmsnorm_kernel,
        grid=(m // bm,),
        in_specs=[pl.BlockSpec((bm, d), lambda i: (i, 0)),
                  pl.BlockSpec((1, d), lambda i: (0, 0))],
        out_specs=pl.BlockSpec((bm, d), lambda i: (i, 0)),
        out_shape=jax.ShapeDtypeStruct((m, d), out_dtype),
        compiler_params=_cparams(1, blocks, temps=2 * bm * d * 4),
        name="rmsnorm",
    )(x, g.reshape(1, d))


def _inproj_kernel(a_ref, ssq_ref, w_ref, wn_ref, o_ref, wsc_ref, *, n_aligned, skip):
    j = pl.program_id(0)
    bn = w_ref.shape[0]
    rc = _pick(bn, (CAST_ROWS, LANES))

    @pl.when(jnp.logical_and(_first_token_tile(), j < n_aligned))
    def _():
        _cast_tile(w_ref, wsc_ref)

    @pl.when(jnp.logical_and(_first_token_tile(), j >= n_aligned))
    def _():
        for r in range(0, bn - rc, rc):
            wsc_ref[r:r + rc, :] = w_ref[r + skip:r + skip + rc, :].astype(BF16)
        wsc_ref[bn - rc:bn - skip, :] = w_ref[bn - rc + skip:bn, :].astype(BF16)
        wsc_ref[bn - skip:bn, :] = wn_ref[...].astype(BF16)

    acc = lax.dot_general(a_ref[...], wsc_ref[...], (((1,), (1,)), ((), ())), preferred_element_type=F32)
    o_ref[...] = (acc * _row_rsqrt(ssq_ref, a_ref.shape[1])).astype(o_ref.dtype)


def _inproj(u, ssq, w_in_t, layer, dt0, heads):
    m, k = u.shape
    parts = ssq.shape[0]
    d_in = w_in_t.shape[1]
    n = d_in - heads
    bm = _pick(m, (512, 256, 128))
    bn = next(c for c in (2048, 1024, 512, 256, 128) if dt0 % c == 0 and (n - dt0) % c == 0)
    assert heads % (2 * SUBLANES) == 0 and bn % heads == 0 and d_in % heads == 0 and heads < LANES
    n_aligned = dt0 // bn
    blocks = [((bm, k), BF16), ((parts, 1, bm), F32), ((bn, k), F32), ((heads, k), F32), ((bm, bn), BF16)]
    scratch = [((bn, k), BF16)]
    return pl.pallas_call(
        functools.partial(_inproj_kernel, n_aligned=n_aligned, skip=heads),
        grid=(n // bn, m // bm),
        in_specs=[pl.BlockSpec((bm, k), lambda j, i: (i, 0)),
                  pl.BlockSpec((parts, 1, bm), lambda j, i: (0, 0, i)),
                  pl.BlockSpec((None, bn, k), lambda j, i: (layer, j, 0)),
                  pl.BlockSpec((None, heads, k), lambda j, i: (layer, (j + 1) * (bn // heads), 0))],
        out_specs=pl.BlockSpec((bm, bn), lambda j, i: (i, j)),
        out_shape=jax.ShapeDtypeStruct((m, n), BF16),
        scratch_shapes=[pltpu.VMEM(s, d) for s, d in scratch],
        compiler_params=_cparams(2, blocks, scratch, temps=bm * bn * 4 + 2 * CAST_ROWS * k * 4),
        name="in_proj",
    )(u, ssq, w_in_t, w_in_t)


def _mm_res_kernel(a_ref, w_ref, h_ref, g_ref, o_ref, hg_ref, ssq_ref, wsc_ref):
    @pl.when(_first_token_tile())
    def _():
        _cast_tile(w_ref, wsc_ref)

    h_new = h_ref[...] + jnp.dot(a_ref[...], wsc_ref[...], preferred_element_type=F32)
    o_ref[...] = h_new
    _emit_scaled(h_new, g_ref, hg_ref, ssq_ref)


def _matmul_residual(a, w, layer, h, g_next, name, in_place=True):
    m, k = a.shape
    n = w.shape[2]
    big_k = k > 2048
    bm = _pick(m, (512, 256, 128)) if big_k else _pick(m, (1024, 512, 256, 128))
    bn = _pick(n, (512, 256, 128)) if big_k else _pick(n, (1024, 512, 256, 128))
    blocks = [((bm, k), BF16), ((k, bn), F32), ((bm, bn), F32), ((1, bn), F32),
              ((bm, bn), F32), ((bm, bn), BF16), ((1, bm), F32)]
    scratch = [((k, bn), BF16)]
    return pl.pallas_call(
        _mm_res_kernel,
        grid=(n // bn, m // bm),
        in_specs=[pl.BlockSpec((bm, k), lambda j, i: (i, 0)),
                  pl.BlockSpec((None, k, bn), lambda j, i: (layer, 0, j)),
                  pl.BlockSpec((bm, bn), lambda j, i: (i, j)),
                  pl.BlockSpec((1, bn), lambda j, i: (0, j))],
        out_specs=[pl.BlockSpec((bm, bn), lambda j, i: (i, j)),
                   pl.BlockSpec((bm, bn), lambda j, i: (i, j)),
                   pl.BlockSpec((None, 1, bm), lambda j, i: (j, 0, i))],
        out_shape=[jax.ShapeDtypeStruct((m, n), F32), jax.ShapeDtypeStruct((m, n), BF16),
                   jax.ShapeDtypeStruct((n // bn, 1, m), F32)],
        scratch_shapes=[pltpu.VMEM(s, d) for s, d in scratch],
        input_output_aliases={2: 0} if in_place else {},
        compiler_params=_cparams(2, blocks, scratch, temps=2 * bm * bn * 4 + 2 * CAST_ROWS * bn * 4),
        name=name,
    )(a, w, h, g_next.reshape(1, n))


def _swiglu_kernel(v_ref, ssq_ref, wg_ref, wu_ref, o_ref, wgsc_ref, wusc_ref):
    @pl.when(_first_token_tile())
    def _():
        _cast_tile(wg_ref, wgsc_ref)
        _cast_tile(wu_ref, wusc_ref)

    v = v_ref[...]
    rs = _row_rsqrt(ssq_ref, v.shape[1])
    gate = jnp.dot(v, wgsc_ref[...], preferred_element_type=F32) * rs
    up = jnp.dot(v, wusc_ref[...], preferred_element_type=F32) * rs
    o_ref[...] = (_silu(gate) * up).astype(o_ref.dtype)


def _swiglu(v, ssq, w_gate_up, layer):
    m, k = v.shape
    parts = ssq.shape[0]
    d_ff = w_gate_up.shape[2] // 2
    bm = _pick(m, (1024, 512, 256, 128))
    bn = _pick(d_ff, (512, 256, 128))
    nb = d_ff // bn
    blocks = [((bm, k), BF16), ((parts, 1, bm), F32), ((k, bn), F32), ((k, bn), F32), ((bm, bn), BF16)]
    scratch = [((k, bn), BF16), ((k, bn), BF16)]
    return pl.pallas_call(
        _swiglu_kernel,
        grid=(nb, m // bm),
        in_specs=[pl.BlockSpec((bm, k), lambda j, i: (i, 0)),
                  pl.BlockSpec((parts, 1, bm), lambda j, i: (0, 0, i)),
                  pl.BlockSpec((None, k, bn), lambda j, i: (layer, 0, j)),
                  pl.BlockSpec((None, k, bn), lambda j, i: (layer, 0, j + nb))],
        out_specs=pl.BlockSpec((bm, bn), lambda j, i: (i, j)),
        out_shape=jax.ShapeDtypeStruct((m, d_ff), BF16),
        scratch_shapes=[pltpu.VMEM(s, d) for s, d in scratch],
        compiler_params=_cparams(2, blocks, scratch, temps=3 * bm * bn * 4 + 2 * CAST_ROWS * bn * 4),
        name="swiglu_up",
    )(v, ssq, w_gate_up, w_gate_up)


def _merge_kernel(y_ref, ysc_ref, wa_ref, wb_ref, ga_ref, gb_ref, o_ref, wasc_ref, wbsc_ref):
    @pl.when(_first_token_tile())
    def _():
        _cast_tile(wa_ref, wasc_ref)
        _cast_tile(wb_ref, wbsc_ref)

    ya = jnp.dot(y_ref[...], wasc_ref[...], preferred_element_type=F32)
    yb = jnp.dot(ysc_ref[...], wbsc_ref[...], preferred_element_type=F32)
    ga = _sigmoid(ga_ref[...].astype(F32))
    gb = _sigmoid(gb_ref[...].astype(F32))
    o_ref[...] = (ga * ya + gb * yb).astype(o_ref.dtype)


def _merge(y, ysc, w_a, w_b, layer, proj, off_ga, off_gb):
    m, ka = y.shape
    kb = ysc.shape[1]
    n = w_a.shape[2]
    bm = _pick(m, (512, 256, 128))
    bn = _pick(n, (512, 256, 128))
    ja, jb = off_ga // bn, off_gb // bn
    assert off_ga % bn == 0 and off_gb % bn == 0
    blocks = [((bm, ka), BF16), ((bm, kb), BF16), ((ka, bn), F32), ((kb, bn), F32),
              ((bm, bn), BF16), ((bm, bn), BF16), ((bm, bn), BF16)]
    scratch = [((ka, bn), BF16), ((kb, bn), BF16)]
    return pl.pallas_call(
        _merge_kernel,
        grid=(n // bn, m // bm),
        in_specs=[pl.BlockSpec((bm, ka), lambda j, i: (i, 0)),
                  pl.BlockSpec((bm, kb), lambda j, i: (i, 0)),
                  pl.BlockSpec((None, ka, bn), lambda j, i: (layer, 0, j)),
                  pl.BlockSpec((None, kb, bn), lambda j, i: (layer, 0, j)),
                  pl.BlockSpec((bm, bn), lambda j, i: (i, j + ja)),
                  pl.BlockSpec((bm, bn), lambda j, i: (i, j + jb))],
        out_specs=pl.BlockSpec((bm, bn), lambda j, i: (i, j)),
        out_shape=jax.ShapeDtypeStruct((m, n), BF16),
        scratch_shapes=[pltpu.VMEM(s, d) for s, d in scratch],
        compiler_params=_cparams(2, blocks, scratch, temps=4 * bm * bn * 4 + 2 * CAST_ROWS * bn * 4),
        name="branch_merge",
    )(y, ysc, w_a, w_b, proj, proj)


def _ple_kernel(hn_ref, ssq_ref, wg_ref, p_ref, wp_ref, h_ref, g_ref, o_ref, hg_ref, ssqo_ref,
                wgsc_ref, wpsc_ref):
    @pl.when(_first_token_tile())
    def _():
        _cast_tile(wg_ref, wgsc_ref)
        _cast_tile(wp_ref, wpsc_ref)

    rs = _row_rsqrt(ssq_ref, hn_ref.shape[1])
    pg = _sigmoid(jnp.dot(hn_ref[...], wgsc_ref[...], preferred_element_type=F32) * rs)
    e = jnp.dot(p_ref[...].astype(BF16), wpsc_ref[...], preferred_element_type=F32)
    h_new = h_ref[...] + pg * e
    o_ref[...] = h_new
    _emit_scaled(h_new, g_ref, hg_ref, ssqo_ref)


def _ple(hn, ssq, w_gate, p, w_proj, layer, h, g_next):
    m, k = hn.shape
    parts = ssq.shape[0]
    kp = p.shape[2]
    n = w_gate.shape[2]
    bm = _pick(m, (1024, 512, 256, 128))
    bn = _pick(n, (1024, 512, 256, 128))
    blocks = [((bm, k), BF16), ((parts, 1, bm), F32), ((k, bn), F32), ((bm, kp), F32), ((kp, bn), F32),
              ((bm, bn), F32), ((1, bn), F32), ((bm, bn), F32), ((bm, bn), BF16), ((1, bm), F32)]
    scratch = [((k, bn), BF16), ((kp, bn), BF16)]
    return pl.pallas_call(
        _ple_kernel,
        grid=(n // bn, m // bm),
        in_specs=[pl.BlockSpec((bm, k), lambda j, i: (i, 0)),
                  pl.BlockSpec((parts, 1, bm), lambda j, i: (0, 0, i)),
                  pl.BlockSpec((None, k, bn), lambda j, i: (layer, 0, j)),
                  pl.BlockSpec((None, bm, kp), lambda j, i: (layer, i, 0)),
                  pl.BlockSpec((None, kp, bn), lambda j, i: (layer, 0, j)),
                  pl.BlockSpec((bm, bn), lambda j, i: (i, j)),
                  pl.BlockSpec((1, bn), lambda j, i: (0, j))],
        out_specs=[pl.BlockSpec((bm, bn), lambda j, i: (i, j)),
                   pl.BlockSpec((bm, bn), lambda j, i: (i, j)),
                   pl.BlockSpec((None, 1, bm), lambda j, i: (j, 0, i))],
        out_shape=[jax.ShapeDtypeStruct((m, n), F32), jax.ShapeDtypeStruct((m, n), BF16),
                   jax.ShapeDtypeStruct((n // bn, 1, m), F32)],
        scratch_shapes=[pltpu.VMEM(s, d) for s, d in scratch],
        input_output_aliases={5: 0},
        compiler_params=_cparams(2, blocks, scratch, temps=4 * bm * bn * 4 + 2 * CAST_ROWS * bn * 4),
        name="ple",
    )(hn, ssq, w_gate, p, w_proj, h, g_next.reshape(1, n))


def _dtprep_kernel(u_ref, ssq_ref, w_ref, bias_ref, alog_ref, dtrow_ref, wrow_ref, acsrow_ref, acscol_ref,
                   *, nc, groups, r_heads):
    x = lax.dot_general(u_ref[...], w_ref[...].astype(BF16), (((1,), (1,)), ((), ())),
                        preferred_element_type=F32)
    x = x * _row_rsqrt(ssq_ref, u_ref.shape[1]) + bias_ref[...]
    dt = jnp.maximum(x, 0.0) + jnp.log1p(jnp.exp(-jnp.abs(x)))
    adt = dt * (-jnp.exp(alog_ref[...]))
    row = lax.broadcasted_iota(jnp.int32, (CHUNK, LANES), 0)
    for k in range(nc):
        sl = slice(k * CHUNK, (k + 1) * CHUNK)
        acs = adt[sl]
        sh = 1
        while sh < CHUNK:
            acs = acs + jnp.where(row >= sh, pltpu.roll(acs, sh, 0), 0.0)
            sh *= 2
        acs2 = acs * LOG2E
        acsrow_ref[k] = acs2.T
        dtrow_ref[k] = dt[sl].T
        wrow_ref[k] = (dt[sl] * jnp.exp(acs[CHUNK - 1:CHUNK, :] - acs)).T
        for g in range(groups):
            shift = (LANES - g * r_heads) % LANES
            acscol_ref[g, sl, :] = pltpu.roll(acs2, shift, 1) if shift else acs2


def _dtprep(u, ssq, w_in_t, layer, dt0, bias, a_log, groups, r_heads):
    m, k = u.shape
    parts = ssq.shape[0]
    nc = 4 if m % (4 * CHUNK) == 0 else 1
    ts = nc * CHUNK
    nchunks = m // CHUNK
    rows = ((nc, LANES, LANES), F32)
    blocks = [((ts, k), BF16), ((parts, 1, ts), F32), ((LANES, k), F32), rows, rows, rows,
              ((groups, ts, LANES), F32)]
    row_spec = pl.BlockSpec((nc, LANES, LANES), lambda i: (i, 0, 0))
    row_shape = jax.ShapeDtypeStruct((nchunks, LANES, LANES), F32)
    return pl.pallas_call(
        functools.partial(_dtprep_kernel, nc=nc, groups=groups, r_heads=r_heads),
        grid=(m // ts,),
        in_specs=[pl.BlockSpec((ts, k), lambda i: (i, 0)),
                  pl.BlockSpec((parts, 1, ts), lambda i: (0, 0, i)),
                  pl.BlockSpec((None, LANES, k), lambda i: (layer, dt0 // LANES, 0)),
                  pl.BlockSpec((1, LANES), lambda i: (0, 0)),
                  pl.BlockSpec((1, LANES), lambda i: (0, 0))],
        out_specs=[row_spec, row_spec, row_spec,
                   pl.BlockSpec((groups, ts, LANES), lambda i: (0, i, 0))],
        out_shape=[row_shape, row_shape, row_shape,
                   jax.ShapeDtypeStruct((groups, m, LANES), F32)],
        compiler_params=_cparams(1, blocks, temps=8 * ts * LANES * 4 + k * LANES * 2),
        name="dt_prep",
    )(u, ssq, w_in_t, bias, a_log)


def _ssd_kernel(z_ref, xs_ref, b_ref, c_ref, xsp_ref, bp_ref, cp_ref,
                dtr_ref, wr_ref, acr_ref, acc_ref, shift_ref,
                cwx_ref, cwb_ref, cwc_ref, cbx_ref, cbb_ref, cbc_ref, dsk_ref, nw_ref,
                o_ref, st_ref, yt_ref, *, nc, r_heads, kconv):
    g = pl.program_id(1)
    c = pl.program_id(2)
    L = CHUNK
    N = SSD_STATE
    gw = r_heads * HEAD_DIM
    npair = gw // LANES

    @pl.when(c == 0)
    def _():
        st_ref[...] = jnp.zeros(st_ref.shape, F32)

    tri = (lax.broadcasted_iota(jnp.int32, (L, L), 0) >= lax.broadcasted_iota(jnp.int32, (L, L), 1))
    lo = lax.broadcasted_iota(jnp.int32, (L, LANES), 1) < HEAD_DIM
    hi = jnp.logical_not(lo)
    lo_row = lo[0:1]

    def two_chunks(ref, prev_ref, k):
        if k == 0:
            prev = prev_ref[...]
            prev = jnp.where(c > 0, prev, jnp.zeros_like(prev))
            return jnp.concatenate([prev, ref[0:L, :]], axis=0)
        return ref[(k - 1) * L:(k + 1) * L, :]

    def conv_silu(shifted, cur, w_ref, bias_ref):
        acc = None
        for tap in range(kconv - 1):
            term = shifted[tap * L:(tap + 1) * L, :] * w_ref[tap:tap + 1, :]
            acc = term if acc is None else acc + term
        acc = acc + cur.astype(F32) * w_ref[kconv - 1:kconv, :]
        acc = acc + bias_ref[...]
        return _silu(acc)

    for k in range(nc):
        r0 = k * L
        x2 = jnp.concatenate([two_chunks(xs_ref, xsp_ref, k), two_chunks(b_ref, bp_ref, k),
                              two_chunks(c_ref, cp_ref, k)], axis=1)
        shifted = jnp.dot(shift_ref[...], x2, preferred_element_type=F32)
        cur = x2[L:2 * L]
        x = conv_silu(shifted[:, 0:gw], cur[:, 0:gw], cwx_ref, cbx_ref)
        bm = conv_silu(shifted[:, gw:gw + N], cur[:, gw:gw + N], cwb_ref, cbb_ref)
        cm = conv_silu(shifted[:, gw + N:gw + 2 * N], cur[:, gw + N:gw + 2 * N], cwc_ref, cbc_ref)
        cb = lax.dot_general(cm.astype(BF16), bm.astype(BF16), (((1,), (1,)), ((), ())),
                             preferred_element_type=F32)
        bt = bm.T
        acol = acc_ref[0, r0:r0 + L, :]
        cdec = jnp.exp2(acol[L - 1:L, :])
        ssq = jnp.zeros((L, 1), F32)
        for j in range(npair):
            cs = slice(j * LANES, (j + 1) * LANES)
            xp = x[:, cs]
            s_prev = st_ref[:, cs]
            s_new = s_prev * jnp.where(lo_row, cdec[:, 2 * j:2 * j + 1], cdec[:, 2 * j + 1:2 * j + 2])
            y = None
            for hh, keep in ((2 * j, lo), (2 * j + 1, hi)):
                head = pl.ds(g * r_heads + hh, 1)
                arow = acr_ref[k, head, :]
                drow = dtr_ref[k, head, :]
                wrow = wr_ref[k, head, :]
                xm = jnp.where(keep, xp, 0.0).astype(BF16)
                sm = jnp.where(keep, s_prev, 0.0).astype(BF16)
                ab = jnp.broadcast_to(acol[:, hh:hh + 1], (L, L))
                dec = jnp.exp2(jnp.where(tri, ab - arow, -jnp.inf))
                mh = cb * dec * drow
                ch = cm * jnp.exp2(ab)
                lhs = jnp.concatenate([mh, ch], axis=1).astype(BF16)
                rhs = jnp.concatenate([xm, sm], axis=0)
                t = jnp.dot(lhs, rhs, preferred_element_type=F32)
                y = t if y is None else y + t
                bth = (bt * wrow).astype(BF16)
                s_new = s_new + jnp.dot(bth, xm, preferred_element_type=F32)
            st_ref[:, cs] = s_new
            yt = y + dsk_ref[:, cs] * xp
            zt = z_ref[r0:r0 + L, cs].astype(F32)
            yt = yt * _silu(zt)
            ssq = ssq + jnp.sum(yt * yt, axis=-1, keepdims=True)
            yt_ref[:, cs] = yt
        rs = lax.rsqrt(ssq / gw + EPS)
        for j in range(npair):
            cs = slice(j * LANES, (j + 1) * LANES)
            o_ref[r0:r0 + L, cs] = (yt_ref[:, cs] * rs * nw_ref[:, cs]).astype(o_ref.dtype)


def _ssd(proj, dtrow, wrow, acsrow, acscol, conv_w, conv_b, d_skip, norm_w, *, batch, seq, inner, groups):
    m = proj.shape[0]
    gw = inner // groups
    r_heads = gw // HEAD_DIM
    assert gw % LANES == 0 and r_heads % 2 == 0 and SSD_STATE == LANES
    kconv = conv_w.shape[0]
    assert kconv - 1 <= CHUNK
    nc = _pick(seq // CHUNK, (16, 8, 4, 2, 1))
    ts = nc * CHUNK
    nt = seq // ts
    gn = groups * SSD_STATE
    xs_blk, b_blk, c_blk = inner // gw, 2 * inner // SSD_STATE, (2 * inner + gn) // SSD_STATE
    wb_blk, wc_blk = inner // SSD_STATE, (inner + gn) // SSD_STATE
    hp = acsrow.shape[1]
    t_idx = jnp.arange(CHUNK)[None, :, None]
    d_idx = (kconv - 1 - jnp.arange(kconv - 1))[:, None, None]
    s_idx = jnp.arange(2 * CHUNK)[None, None, :]
    shift = (s_idx == CHUNK + t_idx - d_idx).astype(BF16).reshape((kconv - 1) * CHUNK, 2 * CHUNK)
    st_shape = (SSD_STATE, gw)
    blocks = [((ts, gw), BF16), ((ts, gw), BF16), ((ts, SSD_STATE), BF16), ((ts, SSD_STATE), BF16),
              ((CHUNK, gw), BF16), ((CHUNK, SSD_STATE), BF16), ((CHUNK, SSD_STATE), BF16),
              ((nc, hp, LANES), F32), ((nc, hp, LANES), F32), ((nc, hp, LANES), F32),
              ((1, ts, LANES), F32), (shift.shape, BF16),
              ((kconv, gw), F32), ((kconv, SSD_STATE), F32), ((kconv, SSD_STATE), F32),
              ((1, gw), F32), ((1, SSD_STATE), F32), ((1, SSD_STATE), F32), ((1, gw), F32), ((1, gw), F32),
              ((ts, gw), BF16)]
    scratch = [(st_shape, F32), ((CHUNK, gw), F32)]
    row = lambda b, g, c: b * nt + c
    prev = lambda b, g, c: jnp.maximum(row(b, g, c) * nc - 1, 0)
    return pl.pallas_call(
        functools.partial(_ssd_kernel, nc=nc, r_heads=r_heads, kconv=kconv),
        grid=(batch, groups, nt),
        in_specs=[
            pl.BlockSpec((ts, gw), lambda b, g, c: (row(b, g, c), g)),
            pl.BlockSpec((ts, gw), lambda b, g, c: (row(b, g, c), xs_blk + g)),
            pl.BlockSpec((ts, SSD_STATE), lambda b, g, c: (row(b, g, c), b_blk + g)),
            pl.BlockSpec((ts, SSD_STATE), lambda b, g, c: (row(b, g, c), c_blk + g)),
            pl.BlockSpec((CHUNK, gw), lambda b, g, c: (prev(b, g, c), xs_blk + g)),
            pl.BlockSpec((CHUNK, SSD_STATE), lambda b, g, c: (prev(b, g, c), b_blk + g)),
            pl.BlockSpec((CHUNK, SSD_STATE), lambda b, g, c: (prev(b, g, c), c_blk + g)),
            pl.BlockSpec((nc, hp, LANES), lambda b, g, c: (row(b, g, c), 0, 0)),
            pl.BlockSpec((nc, hp, LANES), lambda b, g, c: (row(b, g, c), 0, 0)),
            pl.BlockSpec((nc, hp, LANES), lambda b, g, c: (row(b, g, c), 0, 0)),
            pl.BlockSpec((1, ts, LANES), lambda b, g, c: (g, row(b, g, c), 0)),
            pl.BlockSpec(shift.shape, lambda b, g, c: (0, 0)),
            pl.BlockSpec((kconv, gw), lambda b, g, c: (0, g)),
            pl.BlockSpec((kconv, SSD_STATE), lambda b, g, c: (0, wb_blk + g)),
            pl.BlockSpec((kconv, SSD_STATE), lambda b, g, c: (0, wc_blk + g)),
            pl.BlockSpec((1, gw), lambda b, g, c: (0, g)),
            pl.BlockSpec((1, SSD_STATE), lambda b, g, c: (0, wb_blk + g)),
            pl.BlockSpec((1, SSD_STATE), lambda b, g, c: (0, wc_blk + g)),
            pl.BlockSpec((1, gw), lambda b, g, c: (0, g)),
            pl.BlockSpec((1, gw), lambda b, g, c: (0, g)),
        ],
        out_specs=pl.BlockSpec((ts, gw), lambda b, g, c: (row(b, g, c), g)),
        out_shape=jax.ShapeDtypeStruct((m, inner), BF16),
        scratch_shapes=[pltpu.VMEM(s, d) for s, d in scratch],
        compiler_params=_cparams(3, blocks, scratch, temps=8 << 20),
        name="ssd",
    )(proj, proj, proj, proj, proj, proj, proj, dtrow, wrow, acsrow, acscol, shift,
      conv_w, conv_w, conv_w, conv_b, conv_b, conv_b, d_skip, norm_w)


def _sconv_kernel(gb_ref, gc_ref, xt_ref, w_ref, o_ref, pe_ref, *, ts, kconv, cw):
    halo = SUBLANES

    @pl.when(pl.program_id(1) == 0)
    def _():
        pe_ref[0:halo, :] = jnp.zeros((halo, pe_ref.shape[1]), F32)

    pe_ref[halo:halo + ts, :] = gb_ref[...].astype(F32) * xt_ref[...].astype(F32)
    width = pe_ref.shape[1]
    for c0 in range(0, width, cw):
        acc = None
        for k in range(kconv):
            r = halo - (kconv - 1) + k
            term = pe_ref[r:r + ts, c0:c0 + cw] * w_ref[k:k + 1, c0:c0 + cw]
            acc = term if acc is None else acc + term
        o_ref[:, c0:c0 + cw] = (gc_ref[:, c0:c0 + cw].astype(F32) * acc).astype(o_ref.dtype)
    pe_ref[0:halo, :] = pe_ref[ts:ts + halo, :]


def _sconv(proj, conv_w, *, batch, seq, width, off_b, off_c, off_x):
    m = proj.shape[0]
    kconv = conv_w.shape[0]
    ts = _pick(seq, (256, 128))
    nt = seq // ts
    assert off_b % width == 0 and off_c % width == 0 and off_x % width == 0 and kconv - 1 <= SUBLANES
    jb, jc, jx = off_b // width, off_c // width, off_x // width
    cw = _pick(width, (512, 256, 128))
    pe_shape = (SUBLANES + ts, width)
    blocks = [((ts, width), BF16)] * 4 + [((kconv, width), F32)]
    scratch = [(pe_shape, F32)]
    return pl.pallas_call(
        functools.partial(_sconv_kernel, ts=ts, kconv=kconv, cw=cw),
        grid=(batch, nt),
        in_specs=[pl.BlockSpec((ts, width), lambda b, c: (b * nt + c, jb)),
                  pl.BlockSpec((ts, width), lambda b, c: (b * nt + c, jc)),
                  pl.BlockSpec((ts, width), lambda b, c: (b * nt + c, jx)),
                  pl.BlockSpec((kconv, width), lambda b, c: (0, 0))],
        out_specs=pl.BlockSpec((ts, width), lambda b, c: (b * nt + c, 0)),
        out_shape=jax.ShapeDtypeStruct((m, width), BF16),
        scratch_shapes=[pltpu.VMEM(s, d) for s, d in scratch],
        compiler_params=_cparams(2, blocks, scratch, temps=4 << 20),
        name="short_conv",
    )(proj, proj, proj, conv_w)


def kernel(x, p, norm_mix, w_in, ssd_conv_w, ssd_conv_b, ssd_dt_bias, ssd_a_log, ssd_d, ssd_norm,
           ssd_out, sc_conv_w, sc_out, w_o, norm_ffn, w_gate_up, w_down, norm_ple, ple_gate,
           ple_proj, norm_final):
    batch, seq, d = x.shape
    depth = w_in.shape[0]
    m = batch * seq
    heads = ssd_a_log.shape[1]
    inner = heads * HEAD_DIM
    groups = SSD_GROUPS
    r_heads = heads // groups
    xbc = ssd_conv_w.shape[2]
    scw = sc_conv_w.shape[2]
    assert xbc == inner + 2 * groups * SSD_STATE and heads <= LANES
    dt0 = inner + xbc
    off_scb = dt0
    off_scc, off_scx = off_scb + scw, off_scb + 2 * scw
    off_ga = off_scb + 3 * scw
    off_gb = off_ga + d
    hpad = LANES - heads
    p2 = p.reshape(depth, m, p.shape[-1])
    w_in_t = jnp.swapaxes(w_in, 1, 2)

    h = x.reshape(m, d)
    u, u_ssq = _prenorm(h, norm_mix[0])
    for i in range(depth):
        dt_bias = jnp.pad(ssd_dt_bias[i], (0, hpad)).reshape(1, LANES)
        a_log = jnp.pad(ssd_a_log[i], (0, hpad)).reshape(1, LANES)
        d_skip = jnp.repeat(ssd_d[i], HEAD_DIM).reshape(1, inner)

        proj = _inproj(u, u_ssq, w_in_t, i, dt0, heads)
        dtrow, wrow, acsrow, acscol = _dtprep(u, u_ssq, w_in_t, i, dt0, dt_bias, a_log, groups, r_heads)
        y = _ssd(proj, dtrow, wrow, acsrow, acscol, ssd_conv_w[i], ssd_conv_b[i].reshape(1, xbc), d_skip,
                 ssd_norm[i].reshape(1, inner), batch=batch, seq=seq, inner=inner, groups=groups)
        ysc = _sconv(proj, sc_conv_w[i], batch=batch, seq=seq, width=scw,
                     off_b=off_scb, off_c=off_scc, off_x=off_scx)
        merged = _merge(y, ysc, ssd_out, sc_out, i, proj, off_ga, off_gb)
        h, v, v_ssq = _matmul_residual(merged, w_o, i, h, norm_ffn[i], "w_o_residual", in_place=i > 0)
        act = _swiglu(v, v_ssq, w_gate_up, i)
        h, hn, hn_ssq = _matmul_residual(act, w_down, i, h, norm_ple[i], "w_down_residual")
        g_next = norm_mix[i + 1] if i + 1 < depth else norm_final
        h, u, u_ssq = _ple(hn, hn_ssq, ple_gate, p2, ple_proj, i, h, g_next)
    out = _rmsnorm(h, norm_final, F32)
    return out.reshape(batch, seq, d)
```

```python
import functools

import jax
import jax.numpy as jnp
from jax import lax
from jax.experimental import pallas as pl
from jax.experimental.pallas import tpu as pltpu

F32 = jnp.float32
BF16 = jnp.bfloat16

EPS = 1e-6
HEAD_DIM = 64
SSD_GROUPS = 8
SSD_STATE = 128
CHUNK = 128
LANES = 128
SUBLANES = 8
CAST_ROWS = 256
LOG2E = 1.4426950408889634
VMEM_BYTES_V7X = 64 * 1024 * 1024
VMEM_CAP = VMEM_BYTES_V7X - 4 * 1024 * 1024


def _nbytes(shape, dtype):
    n = 1
    for s in shape:
        n *= s
    return n * jnp.dtype(dtype).itemsize


def _cparams(ngrid, blocks, scratch=(), temps=0, single=()):
    need = (2 * sum(_nbytes(s, d) for s, d in blocks) + sum(_nbytes(s, d) for s, d in single)
            + sum(_nbytes(s, d) for s, d in scratch) + temps + (4 << 20))
    return pltpu.CompilerParams(dimension_semantics=("arbitrary",) * ngrid,
                                vmem_limit_bytes=int(min(max(need, 16 << 20), VMEM_CAP)))


def _pick(n, candidates):
    for c in candidates:
        if n % c == 0:
            return c
    raise ValueError(f"no block size in {candidates} divides {n}")


def _cast_tile(w_ref, wsc_ref):
    k = w_ref.shape[0]
    rc = _pick(k, (CAST_ROWS, LANES, SUBLANES))
    for r in range(0, k, rc):
        wsc_ref[r:r + rc, :] = w_ref[r:r + rc, :].astype(BF16)


def _first_token_tile():
    return pl.program_id(1) == 0


def _sigmoid(x):
    return 0.5 * jnp.tanh(0.5 * x) + 0.5


def _silu(x):
    h = 0.5 * x
    return h * jnp.tanh(h) + h


def _emit_scaled(h_new, g_ref, hg_ref, ssq_ref):
    hg_ref[...] = (h_new * g_ref[...]).astype(BF16)
    col = jnp.sum(h_new * h_new, axis=-1, keepdims=True)
    ssq_ref[...] = jnp.transpose(jnp.broadcast_to(col, (col.shape[0], LANES)))[0:1, :]


def _row_rsqrt(ssq_ref, d):
    s = ssq_ref[0]
    for q in range(1, ssq_ref.shape[0]):
        s = s + ssq_ref[q]
    row = lax.rsqrt(s / d + EPS)
    return jnp.transpose(jnp.broadcast_to(row, (LANES, row.shape[1])))[:, 0:1]


def _prenorm_kernel(x_ref, g_ref, hg_ref, ssq_ref):
    _emit_scaled(x_ref[...], g_ref, hg_ref, ssq_ref.at[0])


def _prenorm(x, g):
    m, d = x.shape
    bm = _pick(m, (512, 256, 128))
    blocks = [((bm, d), F32), ((1, d), F32), ((bm, d), BF16), ((1, 1, bm), F32)]
    return pl.pallas_call(
        _prenorm_kernel,
        grid=(m // bm,),
        in_specs=[pl.BlockSpec((bm, d), lambda i: (i, 0)),
                  pl.BlockSpec((1, d), lambda i: (0, 0))],
        out_specs=[pl.BlockSpec((bm, d), lambda i: (i, 0)),
                   pl.BlockSpec((1, 1, bm), lambda i: (0, 0, i))],
        out_shape=[jax.ShapeDtypeStruct((m, d), BF16), jax.ShapeDtypeStruct((1, 1, m), F32)],
        compiler_params=_cparams(1, blocks, temps=2 * bm * d * 4),
        name="prenorm",
    )(x, g.reshape(1, d))


def _rmsnorm_kernel(x_ref, g_ref, o_ref):
    x = x_ref[...]
    ms = jnp.mean(x * x, axis=-1, keepdims=True)
    o_ref[...] = (x * lax.rsqrt(ms + EPS) * g_ref[...]).astype(o_ref.dtype)


def _rmsnorm(x, g, out_dtype):
    m, d = x.shape
    bm = _pick(m, (512, 256, 128))
    blocks = [((bm, d), F32), ((1, d), F32), ((bm, d), out_dtype)]
    return pl.pallas_call(
        _rmsnorm_kernel,
        grid=(m // bm,),
        in_specs=[pl.BlockSpec((bm, d), lambda i: (i, 0)),
                  pl.BlockSpec((1, d), lambda i: (0, 0))],
        out_specs=pl.BlockSpec((bm, d), lambda i: (i, 0)),
        out_shape=jax.ShapeDtypeStruct((m, d), out_dtype),
        compiler_params=_cparams(1, blocks, temps=2 * bm * d * 4),
        name="rmsnorm",
    )(x, g.reshape(1, d))


def _inproj_kernel(a_ref, ssq_ref, w_ref, wn_ref, o_ref, wsc_ref, *, n_aligned, skip):
    j = pl.program_id(0)
    bn = w_ref.shape[0]
    rc = _pick(bn, (CAST_ROWS, LANES))

    @pl.when(jnp.logical_and(_first_token_tile(), j < n_aligned))
    def _():
        _cast_tile(w_ref, wsc_ref)

    @pl.when(jnp.logical_and(_first_token_tile(), j >= n_aligned))
    def _():
        for r in range(0, bn - rc, rc):
            wsc_ref[r:r + rc, :] = w_ref[r + skip:r + skip + rc, :].astype(BF16)
        wsc_ref[bn - rc:bn - skip, :] = w_ref[bn - rc + skip:bn, :].astype(BF16)
        wsc_ref[bn - skip:bn, :] = wn_ref[...].astype(BF16)

    acc = lax.dot_general(a_ref[...], wsc_ref[...], (((1,), (1,)), ((), ())), preferred_element_type=F32)
    o_ref[...] = (acc * _row_rsqrt(ssq_ref, a_ref.shape[1])).astype(o_ref.dtype)


def _inproj(u, ssq, w_in_t, layer, dt0, heads):
    m, k = u.shape
    parts = ssq.shape[0]
    d_in = w_in_t.shape[1]
    n = d_in - heads
    bm = _pick(m, (1024, 512, 256, 128))
    bn = next(c for c in (1024, 512, 256, 128) if dt0 % c == 0 and (n - dt0) % c == 0)
    assert heads % (2 * SUBLANES) == 0 and bn % heads == 0 and d_in % heads == 0 and heads < LANES
    n_aligned = dt0 // bn
    blocks = [((bm, k), BF16), ((parts, 1, bm), F32), ((bn, k), F32), ((heads, k), F32), ((bm, bn), BF16)]
    scratch = [((bn, k), BF16)]
    return pl.pallas_call(
        functools.partial(_inproj_kernel, n_aligned=n_aligned, skip=heads),
        grid=(n // bn, m // bm),
        in_specs=[pl.BlockSpec((bm, k), lambda j, i: (i, 0)),
                  pl.BlockSpec((parts, 1, bm), lambda j, i: (0, 0, i)),
                  pl.BlockSpec((None, bn, k), lambda j, i: (layer, j, 0)),
                  pl.BlockSpec((None, heads, k), lambda j, i: (layer, (j + 1) * (bn // heads), 0))],
        out_specs=pl.BlockSpec((bm, bn), lambda j, i: (i, j)),
        out_shape=jax.ShapeDtypeStruct((m, n), BF16),
        scratch_shapes=[pltpu.VMEM(s, d) for s, d in scratch],
        compiler_params=_cparams(2, blocks, scratch, temps=bm * bn * 4 + 2 * CAST_ROWS * k * 4),
        name="in_proj",
    )(u, ssq, w_in_t, w_in_t)


def _mm_res_kernel(a_ref, w_ref, h_ref, g_ref, o_ref, hg_ref, ssq_ref, wsc_ref):
    @pl.when(_first_token_tile())
    def _():
        _cast_tile(w_ref, wsc_ref)

    h_new = h_ref[...] + jnp.dot(a_ref[...], wsc_ref[...], preferred_element_type=F32)
    o_ref[...] = h_new
    _emit_scaled(h_new, g_ref, hg_ref, ssq_ref)


def _matmul_residual(a, w, layer, h, g_next, name, in_place=True):
    m, k = a.shape
    n = w.shape[2]
    big_k = k > 2048
    bm = _pick(m, (512, 256, 128)) if big_k else _pick(m, (1024, 512, 256, 128))
    bn = _pick(n, (1024, 512, 256, 128))
    w_block = ((k, bn), F32)
    blocks = [((bm, k), BF16), ((bm, bn), F32), ((1, bn), F32),
              ((bm, bn), F32), ((bm, bn), BF16), ((1, bm), F32)] + ([] if big_k else [w_block])
    single = [w_block] if big_k else []
    scratch = [((k, bn), BF16)]
    return pl.pallas_call(
        _mm_res_kernel,
        grid=(n // bn, m // bm),
        in_specs=[pl.BlockSpec((bm, k), lambda j, i: (i, 0)),
                  pl.BlockSpec((None, k, bn), lambda j, i: (layer, 0, j),
                               pipeline_mode=pl.Buffered(1) if big_k else None),
                  pl.BlockSpec((bm, bn), lambda j, i: (i, j)),
                  pl.BlockSpec((1, bn), lambda j, i: (0, j))],
        out_specs=[pl.BlockSpec((bm, bn), lambda j, i: (i, j)),
                   pl.BlockSpec((bm, bn), lambda j, i: (i, j)),
                   pl.BlockSpec((None, 1, bm), lambda j, i: (j, 0, i))],
        out_shape=[jax.ShapeDtypeStruct((m, n), F32), jax.ShapeDtypeStruct((m, n), BF16),
                   jax.ShapeDtypeStruct((n // bn, 1, m), F32)],
        scratch_shapes=[pltpu.VMEM(s, d) for s, d in scratch],
        input_output_aliases={2: 0} if in_place else {},
        compiler_params=_cparams(2, blocks, scratch, temps=2 * bm * bn * 4 + 2 * CAST_ROWS * bn * 4,
                                 single=single),
        name=name,
    )(a, w, h, g_next.reshape(1, n))


def _swiglu_kernel(v_ref, ssq_ref, wg_ref, wu_ref, o_ref, wgsc_ref, wusc_ref):
    @pl.when(_first_token_tile())
    def _():
        _cast_tile(wg_ref, wgsc_ref)
        _cast_tile(wu_ref, wusc_ref)

    v = v_ref[...]
    rs = _row_rsqrt(ssq_ref, v.shape[1])
    gate = jnp.dot(v, wgsc_ref[...], preferred_element_type=F32) * rs
    up = jnp.dot(v, wusc_ref[...], preferred_element_type=F32) * rs
    o_ref[...] = (_silu(gate) * up).astype(o_ref.dtype)


def _swiglu(v, ssq, w_gate_up, layer):
    m, k = v.shape
    parts = ssq.shape[0]
    d_ff = w_gate_up.shape[2] // 2
    bm = _pick(m, (1024, 512, 256, 128))
    bn = _pick(d_ff, (512, 256, 128))
    nb = d_ff // bn
    blocks = [((bm, k), BF16), ((parts, 1, bm), F32), ((k, bn), F32), ((k, bn), F32), ((bm, bn), BF16)]
    scratch = [((k, bn), BF16), ((k, bn), BF16)]
    return pl.pallas_call(
        _swiglu_kernel,
        grid=(nb, m // bm),
        in_specs=[pl.BlockSpec((bm, k), lambda j, i: (i, 0)),
                  pl.BlockSpec((parts, 1, bm), lambda j, i: (0, 0, i)),
                  pl.BlockSpec((None, k, bn), lambda j, i: (layer, 0, j)),
                  pl.BlockSpec((None, k, bn), lambda j, i: (layer, 0, j + nb))],
        out_specs=pl.BlockSpec((bm, bn), lambda j, i: (i, j)),
        out_shape=jax.ShapeDtypeStruct((m, d_ff), BF16),
        scratch_shapes=[pltpu.VMEM(s, d) for s, d in scratch],
        compiler_params=_cparams(2, blocks, scratch, temps=3 * bm * bn * 4 + 2 * CAST_ROWS * bn * 4),
        name="swiglu_up",
    )(v, ssq, w_gate_up, w_gate_up)


def _merge_kernel(y_ref, ysc_ref, wa_ref, wb_ref, ga_ref, gb_ref, o_ref, wasc_ref, wbsc_ref):
    @pl.when(_first_token_tile())
    def _():
        _cast_tile(wa_ref, wasc_ref)
        _cast_tile(wb_ref, wbsc_ref)

    ya = jnp.dot(y_ref[...], wasc_ref[...], preferred_element_type=F32)
    yb = jnp.dot(ysc_ref[...], wbsc_ref[...], preferred_element_type=F32)
    ga = _sigmoid(ga_ref[...].astype(F32))
    gb = _sigmoid(gb_ref[...].astype(F32))
    o_ref[...] = (ga * ya + gb * yb).astype(o_ref.dtype)


def _merge(y, ysc, w_a, w_b, layer, proj, off_ga, off_gb):
    m, ka = y.shape
    kb = ysc.shape[1]
    n = w_a.shape[2]
    bm = _pick(m, (512, 256, 128))
    bn = _pick(n, (1024, 512, 256, 128))
    ja, jb = off_ga // bn, off_gb // bn
    assert off_ga % bn == 0 and off_gb % bn == 0
    blocks = [((bm, ka), BF16), ((bm, kb), BF16), ((bm, bn), BF16), ((bm, bn), BF16), ((bm, bn), BF16)]
    single = [((ka, bn), F32), ((kb, bn), F32)]
    scratch = [((ka, bn), BF16), ((kb, bn), BF16)]
    return pl.pallas_call(
        _merge_kernel,
        grid=(n // bn, m // bm),
        in_specs=[pl.BlockSpec((bm, ka), lambda j, i: (i, 0)),
                  pl.BlockSpec((bm, kb), lambda j, i: (i, 0)),
                  pl.BlockSpec((None, ka, bn), lambda j, i: (layer, 0, j), pipeline_mode=pl.Buffered(1)),
                  pl.BlockSpec((None, kb, bn), lambda j, i: (layer, 0, j), pipeline_mode=pl.Buffered(1)),
                  pl.BlockSpec((bm, bn), lambda j, i: (i, j + ja)),
                  pl.BlockSpec((bm, bn), lambda j, i: (i, j + jb))],
        out_specs=pl.BlockSpec((bm, bn), lambda j, i: (i, j)),
        out_shape=jax.ShapeDtypeStruct((m, n), BF16),
        scratch_shapes=[pltpu.VMEM(s, d) for s, d in scratch],
        compiler_params=_cparams(2, blocks, scratch, temps=4 * bm * bn * 4 + 2 * CAST_ROWS * bn * 4,
                                 single=single),
        name="branch_merge",
    )(y, ysc, w_a, w_b, proj, proj)


def _ple_kernel(hn_ref, ssq_ref, wg_ref, p_ref, wp_ref, h_ref, g_ref, o_ref, hg_ref, ssqo_ref,
                wgsc_ref, wpsc_ref):
    @pl.when(_first_token_tile())
    def _():
        _cast_tile(wg_ref, wgsc_ref)
        _cast_tile(wp_ref, wpsc_ref)

    rs = _row_rsqrt(ssq_ref, hn_ref.shape[1])
    pg = _sigmoid(jnp.dot(hn_ref[...], wgsc_ref[...], preferred_element_type=F32) * rs)
    e = jnp.dot(p_ref[...].astype(BF16), wpsc_ref[...], preferred_element_type=F32)
    h_new = h_ref[...] + pg * e
    o_ref[...] = h_new
    _emit_scaled(h_new, g_ref, hg_ref, ssqo_ref)


def _ple(hn, ssq, w_gate, p, w_proj, layer, h, g_next):
    m, k = hn.shape
    parts = ssq.shape[0]
    kp = p.shape[2]
    n = w_gate.shape[2]
    bm = _pick(m, (1024, 512, 256, 128))
    bn = _pick(n, (1024, 512, 256, 128))
    blocks = [((bm, k), BF16), ((parts, 1, bm), F32), ((k, bn), F32), ((bm, kp), F32), ((kp, bn), F32),
              ((bm, bn), F32), ((1, bn), F32), ((bm, bn), F32), ((bm, bn), BF16), ((1, bm), F32)]
    scratch = [((k, bn), BF16), ((kp, bn), BF16)]
    return pl.pallas_call(
        _ple_kernel,
        grid=(n // bn, m // bm),
        in_specs=[pl.BlockSpec((bm, k), lambda j, i: (i, 0)),
                  pl.BlockSpec((parts, 1, bm), lambda j, i: (0, 0, i)),
                  pl.BlockSpec((None, k, bn), lambda j, i: (layer, 0, j)),
                  pl.BlockSpec((None, bm, kp), lambda j, i: (layer, i, 0)),
                  pl.BlockSpec((None, kp, bn), lambda j, i: (layer, 0, j)),
                  pl.BlockSpec((bm, bn), lambda j, i: (i, j)),
                  pl.BlockSpec((1, bn), lambda j, i: (0, j))],
        out_specs=[pl.BlockSpec((bm, bn), lambda j, i: (i, j)),
                   pl.BlockSpec((bm, bn), lambda j, i: (i, j)),
                   pl.BlockSpec((None, 1, bm), lambda j, i: (j, 0, i))],
        out_shape=[jax.ShapeDtypeStruct((m, n), F32), jax.ShapeDtypeStruct((m, n), BF16),
                   jax.ShapeDtypeStruct((n // bn, 1, m), F32)],
        scratch_shapes=[pltpu.VMEM(s, d) for s, d in scratch],
        input_output_aliases={5: 0},
        compiler_params=_cparams(2, blocks, scratch, temps=4 * bm * bn * 4 + 2 * CAST_ROWS * bn * 4),
        name="ple",
    )(hn, ssq, w_gate, p, w_proj, h, g_next.reshape(1, n))


def _dtprep_kernel(u_ref, ssq_ref, w_ref, bias_ref, alog_ref, dtrow_ref, wrow_ref, acsrow_ref, acscol_ref,
                   *, nc, groups, r_heads):
    x = lax.dot_general(u_ref[...], w_ref[...].astype(BF16), (((1,), (1,)), ((), ())),
                        preferred_element_type=F32)
    x = x * _row_rsqrt(ssq_ref, u_ref.shape[1]) + bias_ref[...]
    dt = jnp.maximum(x, 0.0) + jnp.log1p(jnp.exp(-jnp.abs(x)))
    adt = dt * (-jnp.exp(alog_ref[...]))
    row = lax.broadcasted_iota(jnp.int32, (CHUNK, LANES), 0)
    for k in range(nc):
        sl = slice(k * CHUNK, (k + 1) * CHUNK)
        acs = adt[sl]
        sh = 1
        while sh < CHUNK:
            acs = acs + jnp.where(row >= sh, pltpu.roll(acs, sh, 0), 0.0)
            sh *= 2
        acs2 = acs * LOG2E
        acsrow_ref[k] = acs2.T
        dtrow_ref[k] = dt[sl].T
        wrow_ref[k] = (dt[sl] * jnp.exp(acs[CHUNK - 1:CHUNK, :] - acs)).T
        for g in range(groups):
            shift = (LANES - g * r_heads) % LANES
            acscol_ref[g, sl, :] = pltpu.roll(acs2, shift, 1) if shift else acs2


def _dtprep(u, ssq, w_in_t, layer, dt0, bias, a_log, groups, r_heads):
    m, k = u.shape
    parts = ssq.shape[0]
    nc = 4 if m % (4 * CHUNK) == 0 else 1
    ts = nc * CHUNK
    nchunks = m // CHUNK
    rows = ((nc, LANES, LANES), F32)
    blocks = [((ts, k), BF16), ((parts, 1, ts), F32), ((LANES, k), F32), rows, rows, rows,
              ((groups, ts, LANES), F32)]
    row_spec = pl.BlockSpec((nc, LANES, LANES), lambda i: (i, 0, 0))
    row_shape = jax.ShapeDtypeStruct((nchunks, LANES, LANES), F32)
    return pl.pallas_call(
        functools.partial(_dtprep_kernel, nc=nc, groups=groups, r_heads=r_heads),
        grid=(m // ts,),
        in_specs=[pl.BlockSpec((ts, k), lambda i: (i, 0)),
                  pl.BlockSpec((parts, 1, ts), lambda i: (0, 0, i)),
                  pl.BlockSpec((None, LANES, k), lambda i: (layer, dt0 // LANES, 0)),
                  pl.BlockSpec((1, LANES), lambda i: (0, 0)),
                  pl.BlockSpec((1, LANES), lambda i: (0, 0))],
        out_specs=[row_spec, row_spec, row_spec,
                   pl.BlockSpec((groups, ts, LANES), lambda i: (0, i, 0))],
        out_shape=[row_shape, row_shape, row_shape,
                   jax.ShapeDtypeStruct((groups, m, LANES), F32)],
        compiler_params=_cparams(1, blocks, temps=8 * ts * LANES * 4 + k * LANES * 2),
        name="dt_prep",
    )(u, ssq, w_in_t, bias, a_log)


def _ssd_kernel(z_ref, xs_ref, b_ref, c_ref, xsp_ref, bp_ref, cp_ref,
                dtr_ref, wr_ref, acr_ref, acc_ref, shift_ref,
                cwx_ref, cwb_ref, cwc_ref, cbx_ref, cbb_ref, cbc_ref, dsk_ref, nw_ref,
                o_ref, st_ref, yt_ref, *, nc, r_heads, kconv):
    g = pl.program_id(1)
    c = pl.program_id(2)
    L = CHUNK
    N = SSD_STATE
    gw = r_heads * HEAD_DIM
    npair = gw // LANES

    @pl.when(c == 0)
    def _():
        st_ref[...] = jnp.zeros(st_ref.shape, F32)

    tri = (lax.broadcasted_iota(jnp.int32, (L, L), 0) >= lax.broadcasted_iota(jnp.int32, (L, L), 1))
    lo = lax.broadcasted_iota(jnp.int32, (L, LANES), 1) < HEAD_DIM
    hi = jnp.logical_not(lo)
    lo_row = lo[0:1]

    def two_chunks(ref, prev_ref, k):
        if k == 0:
            prev = prev_ref[...]
            prev = jnp.where(c > 0, prev, jnp.zeros_like(prev))
            return jnp.concatenate([prev, ref[0:L, :]], axis=0)
        return ref[(k - 1) * L:(k + 1) * L, :]

    def conv_silu(shifted, cur, w_ref, bias_ref):
        acc = None
        for tap in range(kconv - 1):
            term = shifted[tap * L:(tap + 1) * L, :] * w_ref[tap:tap + 1, :]
            acc = term if acc is None else acc + term
        acc = acc + cur.astype(F32) * w_ref[kconv - 1:kconv, :]
        acc = acc + bias_ref[...]
        return _silu(acc)

    for k in range(nc):
        r0 = k * L
        x2 = jnp.concatenate([two_chunks(xs_ref, xsp_ref, k), two_chunks(b_ref, bp_ref, k),
                              two_chunks(c_ref, cp_ref, k)], axis=1)
        shifted = jnp.dot(shift_ref[...], x2, preferred_element_type=F32)
        cur = x2[L:2 * L]
        x = conv_silu(shifted[:, 0:gw], cur[:, 0:gw], cwx_ref, cbx_ref)
        bm = conv_silu(shifted[:, gw:gw + N], cur[:, gw:gw + N], cwb_ref, cbb_ref)
        cm = conv_silu(shifted[:, gw + N:gw + 2 * N], cur[:, gw + N:gw + 2 * N], cwc_ref, cbc_ref)
        cb = lax.dot_general(cm.astype(BF16), bm.astype(BF16), (((1,), (1,)), ((), ())),
                             preferred_element_type=F32)
        bt = bm.T
        acol = acc_ref[0, r0:r0 + L, :]
        cdec = jnp.exp2(acol[L - 1:L, :])
        ssq = jnp.zeros((L, 1), F32)
        for j in range(npair):
            cs = slice(j * LANES, (j + 1) * LANES)
            xp = x[:, cs]
            s_prev = st_ref[:, cs]
            s_new = s_prev * jnp.where(lo_row, cdec[:, 2 * j:2 * j + 1], cdec[:, 2 * j + 1:2 * j + 2])
            y = None
            for hh, keep in ((2 * j, lo), (2 * j + 1, hi)):
                head = pl.ds(g * r_heads + hh, 1)
                arow = acr_ref[k, head, :]
                drow = dtr_ref[k, head, :]
                wrow = wr_ref[k, head, :]
                xm = jnp.where(keep, xp, 0.0).astype(BF16)
                sm = jnp.where(keep, s_prev, 0.0).astype(BF16)
                ab = jnp.broadcast_to(acol[:, hh:hh + 1], (L, L))
                dec = jnp.exp2(jnp.where(tri, ab - arow, -jnp.inf))
                mh = cb * dec * drow
                ch = cm * jnp.exp2(ab)
                lhs = jnp.concatenate([mh, ch], axis=1).astype(BF16)
                rhs = jnp.concatenate([xm, sm], axis=0)
                t = jnp.dot(lhs, rhs, preferred_element_type=F32)
                y = t if y is None else y + t
                bth = (bt * wrow).astype(BF16)
                s_new = s_new + jnp.dot(bth, xm, preferred_element_type=F32)
            st_ref[:, cs] = s_new
            yt = y + dsk_ref[:, cs] * xp
            zt = z_ref[r0:r0 + L, cs].astype(F32)
            yt = yt * _silu(zt)
            ssq = ssq + jnp.sum(yt * yt, axis=-1, keepdims=True)
            yt_ref[:, cs] = yt
        rs = lax.rsqrt(ssq / gw + EPS)
        for j in range(npair):
            cs = slice(j * LANES, (j + 1) * LANES)
            o_ref[r0:r0 + L, cs] = (yt_ref[:, cs] * rs * nw_ref[:, cs]).astype(o_ref.dtype)


def _ssd(proj, dtrow, wrow, acsrow, acscol, conv_w, conv_b, d_skip, norm_w, *, batch, seq, inner, groups):
    m = proj.shape[0]
    gw = inner // groups
    r_heads = gw // HEAD_DIM
    assert gw % LANES == 0 and r_heads % 2 == 0 and SSD_STATE == LANES
    kconv = conv_w.shape[0]
    assert kconv - 1 <= CHUNK
    nc = _pick(seq // CHUNK, (16, 8, 4, 2, 1))
    ts = nc * CHUNK
    nt = seq // ts
    gn = groups * SSD_STATE
    xs_blk, b_blk, c_blk = inner // gw, 2 * inner // SSD_STATE, (2 * inner + gn) // SSD_STATE
    wb_blk, wc_blk = inner // SSD_STATE, (inner + gn) // SSD_STATE
    hp = acsrow.shape[1]
    t_idx = jnp.arange(CHUNK)[None, :, None]
    d_idx = (kconv - 1 - jnp.arange(kconv - 1))[:, None, None]
    s_idx = jnp.arange(2 * CHUNK)[None, None, :]
    shift = (s_idx == CHUNK + t_idx - d_idx).astype(BF16).reshape((kconv - 1) * CHUNK, 2 * CHUNK)
    st_shape = (SSD_STATE, gw)
    blocks = [((ts, gw), BF16), ((ts, gw), BF16), ((ts, SSD_STATE), BF16), ((ts, SSD_STATE), BF16),
              ((CHUNK, gw), BF16), ((CHUNK, SSD_STATE), BF16), ((CHUNK, SSD_STATE), BF16),
              ((nc, hp, LANES), F32), ((nc, hp, LANES), F32), ((nc, hp, LANES), F32),
              ((1, ts, LANES), F32), (shift.shape, BF16),
              ((kconv, gw), F32), ((kconv, SSD_STATE), F32), ((kconv, SSD_STATE), F32),
              ((1, gw), F32), ((1, SSD_STATE), F32), ((1, SSD_STATE), F32), ((1, gw), F32), ((1, gw), F32),
              ((ts, gw), BF16)]
    scratch = [(st_shape, F32), ((CHUNK, gw), F32)]
    row = lambda b, g, c: b * nt + c
    prev = lambda b, g, c: jnp.maximum(row(b, g, c) * nc - 1, 0)
    return pl.pallas_call(
        functools.partial(_ssd_kernel, nc=nc, r_heads=r_heads, kconv=kconv),
        grid=(batch, groups, nt),
        in_specs=[
            pl.BlockSpec((ts, gw), lambda b, g, c: (row(b, g, c), g)),
            pl.BlockSpec((ts, gw), lambda b, g, c: (row(b, g, c), xs_blk + g)),
            pl.BlockSpec((ts, SSD_STATE), lambda b, g, c: (row(b, g, c), b_blk + g)),
            pl.BlockSpec((ts, SSD_STATE), lambda b, g, c: (row(b, g, c), c_blk + g)),
            pl.BlockSpec((CHUNK, gw), lambda b, g, c: (prev(b, g, c), xs_blk + g)),
            pl.BlockSpec((CHUNK, SSD_STATE), lambda b, g, c: (prev(b, g, c), b_blk + g)),
            pl.BlockSpec((CHUNK, SSD_STATE), lambda b, g, c: (prev(b, g, c), c_blk + g)),
            pl.BlockSpec((nc, hp, LANES), lambda b, g, c: (row(b, g, c), 0, 0)),
            pl.BlockSpec((nc, hp, LANES), lambda b, g, c: (row(b, g, c), 0, 0)),
            pl.BlockSpec((nc, hp, LANES), lambda b, g, c: (row(b, g, c), 0, 0)),
            pl.BlockSpec((1, ts, LANES), lambda b, g, c: (g, row(b, g, c), 0)),
            pl.BlockSpec(shift.shape, lambda b, g, c: (0, 0)),
            pl.BlockSpec((kconv, gw), lambda b, g, c: (0, g)),
            pl.BlockSpec((kconv, SSD_STATE), lambda b, g, c: (0, wb_blk + g)),
            pl.BlockSpec((kconv, SSD_STATE), lambda b, g, c: (0, wc_blk + g)),
            pl.BlockSpec((1, gw), lambda b, g, c: (0, g)),
            pl.BlockSpec((1, SSD_STATE), lambda b, g, c: (0, wb_blk + g)),
            pl.BlockSpec((1, SSD_STATE), lambda b, g, c: (0, wc_blk + g)),
            pl.BlockSpec((1, gw), lambda b, g, c: (0, g)),
            pl.BlockSpec((1, gw), lambda b, g, c: (0, g)),
        ],
        out_specs=pl.BlockSpec((ts, gw), lambda b, g, c: (row(b, g, c), g)),
        out_shape=jax.ShapeDtypeStruct((m, inner), BF16),
        scratch_shapes=[pltpu.VMEM(s, d) for s, d in scratch],
        compiler_params=_cparams(3, blocks, scratch, temps=8 << 20),
        name="ssd",
    )(proj, proj, proj, proj, proj, proj, proj, dtrow, wrow, acsrow, acscol, shift,
      conv_w, conv_w, conv_w, conv_b, conv_b, conv_b, d_skip, norm_w)


def _sconv_kernel(gb_ref, gc_ref, xt_ref, w_ref, o_ref, pe_ref, *, ts, kconv, cw):
    halo = SUBLANES

    @pl.when(pl.program_id(1) == 0)
    def _():
        pe_ref[0:halo, :] = jnp.zeros((halo, pe_ref.shape[1]), F32)

    pe_ref[halo:halo + ts, :] = gb_ref[...].astype(F32) * xt_ref[...].astype(F32)
    width = pe_ref.shape[1]
    for c0 in range(0, width, cw):
        acc = None
        for k in range(kconv):
            r = halo - (kconv - 1) + k
            term = pe_ref[r:r + ts, c0:c0 + cw] * w_ref[k:k + 1, c0:c0 + cw]
            acc = term if acc is None else acc + term
        o_ref[:, c0:c0 + cw] = (gc_ref[:, c0:c0 + cw].astype(F32) * acc).astype(o_ref.dtype)
    pe_ref[0:halo, :] = pe_ref[ts:ts + halo, :]


def _sconv(proj, conv_w, *, batch, seq, width, off_b, off_c, off_x):
    m = proj.shape[0]
    kconv = conv_w.shape[0]
    ts = _pick(seq, (256, 128))
    nt = seq // ts
    assert off_b % width == 0 and off_c % width == 0 and off_x % width == 0 and kconv - 1 <= SUBLANES
    jb, jc, jx = off_b // width, off_c // width, off_x // width
    cw = _pick(width, (512, 256, 128))
    pe_shape = (SUBLANES + ts, width)
    blocks = [((ts, width), BF16)] * 4 + [((kconv, width), F32)]
    scratch = [(pe_shape, F32)]
    return pl.pallas_call(
        functools.partial(_sconv_kernel, ts=ts, kconv=kconv, cw=cw),
        grid=(batch, nt),
        in_specs=[pl.BlockSpec((ts, width), lambda b, c: (b * nt + c, jb)),
                  pl.BlockSpec((ts, width), lambda b, c: (b * nt + c, jc)),
                  pl.BlockSpec((ts, width), lambda b, c: (b * nt + c, jx)),
                  pl.BlockSpec((kconv, width), lambda b, c: (0, 0))],
        out_specs=pl.BlockSpec((ts, width), lambda b, c: (b * nt + c, 0)),
        out_shape=jax.ShapeDtypeStruct((m, width), BF16),
        scratch_shapes=[pltpu.VMEM(s, d) for s, d in scratch],
        compiler_params=_cparams(2, blocks, scratch, temps=4 << 20),
        name="short_conv",
    )(proj, proj, proj, conv_w)


def kernel(x, p, norm_mix, w_in, ssd_conv_w, ssd_conv_b, ssd_dt_bias, ssd_a_log, ssd_d, ssd_norm,
           ssd_out, sc_conv_w, sc_out, w_o, norm_ffn, w_gate_up, w_down, norm_ple, ple_gate,
           ple_proj, norm_final):
    batch, seq, d = x.shape
    depth = w_in.shape[0]
    m = batch * seq
    heads = ssd_a_log.shape[1]
    inner = heads * HEAD_DIM
    groups = SSD_GROUPS
    r_heads = heads // groups
    xbc = ssd_conv_w.shape[2]
    scw = sc_conv_w.shape[2]
    assert xbc == inner + 2 * groups * SSD_STATE and heads <= LANES
    dt0 = inner + xbc
    off_scb = dt0
    off_scc, off_scx = off_scb + scw, off_scb + 2 * scw
    off_ga = off_scb + 3 * scw
    off_gb = off_ga + d
    hpad = LANES - heads
    p2 = p.reshape(depth, m, p.shape[-1])
    w_in_t = jnp.swapaxes(w_in, 1, 2)

    h = x.reshape(m, d)
    u, u_ssq = _prenorm(h, norm_mix[0])
    for i in range(depth):
        dt_bias = jnp.pad(ssd_dt_bias[i], (0, hpad)).reshape(1, LANES)
        a_log = jnp.pad(ssd_a_log[i], (0, hpad)).reshape(1, LANES)
        d_skip = jnp.repeat(ssd_d[i], HEAD_DIM).reshape(1, inner)

        proj = _inproj(u, u_ssq, w_in_t, i, dt0, heads)
        dtrow, wrow, acsrow, acscol = _dtprep(u, u_ssq, w_in_t, i, dt0, dt_bias, a_log, groups, r_heads)
        y = _ssd(proj, dtrow, wrow, acsrow, acscol, ssd_conv_w[i], ssd_conv_b[i].reshape(1, xbc), d_skip,
                 ssd_norm[i].reshape(1, inner), batch=batch, seq=seq, inner=inner, groups=groups)
        ysc = _sconv(proj, sc_conv_w[i], batch=batch, seq=seq, width=scw,
                     off_b=off_scb, off_c=off_scc, off_x=off_scx)
        merged = _merge(y, ysc, ssd_out, sc_out, i, proj, off_ga, off_gb)
        h, v, v_ssq = _matmul_residual(merged, w_o, i, h, norm_ffn[i], "w_o_residual", in_place=i > 0)
        act = _swiglu(v, v_ssq, w_gate_up, i)
        h, hn, hn_ssq = _matmul_residual(act, w_down, i, h, norm_ple[i], "w_down_residual")
        g_next = norm_mix[i + 1] if i + 1 < depth else norm_final
        h, u, u_ssq = _ple(hn, hn_ssq, ple_gate, p2, ple_proj, i, h, g_next)
    out = _rmsnorm(h, norm_final, F32)
    return out.reshape(batch, seq, d)
```

```python
import functools

import jax
import jax.numpy as jnp
from jax import lax
from jax.experimental import pallas as pl
from jax.experimental.pallas import tpu as pltpu

F32 = jnp.float32
BF16 = jnp.bfloat16

EPS = 1e-6
HEAD_DIM = 64
SSD_GROUPS = 8
SSD_STATE = 128
CHUNK = 128
LANES = 128
SUBLANES = 8
CAST_ROWS = 256
LOG2E = 1.4426950408889634
VMEM_BYTES_V7X = 64 * 1024 * 1024
VMEM_CAP = VMEM_BYTES_V7X - 4 * 1024 * 1024


def _nbytes(shape, dtype):
    n = 1
    for s in shape:
        n *= s
    return n * jnp.dtype(dtype).itemsize


def _cparams(ngrid, blocks, scratch=(), temps=0, single=()):
    need = (2 * sum(_nbytes(s, d) for s, d in blocks) + sum(_nbytes(s, d) for s, d in single)
            + sum(_nbytes(s, d) for s, d in scratch) + temps + (4 << 20))
    return pltpu.CompilerParams(dimension_semantics=("arbitrary",) * ngrid,
                                vmem_limit_bytes=int(min(max(need, 16 << 20), VMEM_CAP)))


def _pick(n, candidates):
    for c in candidates:
        if n % c == 0:
            return c
    raise ValueError(f"no block size in {candidates} divides {n}")


def _cast_tile(w_ref, wsc_ref):
    k = w_ref.shape[0]
    rc = _pick(k, (CAST_ROWS, LANES, SUBLANES))
    for r in range(0, k, rc):
        wsc_ref[r:r + rc, :] = w_ref[r:r + rc, :].astype(BF16)


def _first_token_tile():
    return pl.program_id(1) == 0


def _sigmoid(x):
    return 0.5 * jnp.tanh(0.5 * x) + 0.5


def _silu(x):
    h = 0.5 * x
    return h * jnp.tanh(h) + h


def _emit_scaled(h_new, g_ref, hg_ref, ssq_ref):
    hg_ref[...] = (h_new * g_ref[...]).astype(BF16)
    col = jnp.sum(h_new * h_new, axis=-1, keepdims=True)
    ssq_ref[...] = jnp.transpose(jnp.broadcast_to(col, (col.shape[0], LANES)))[0:1, :]


def _row_rsqrt(ssq_ref, d):
    s = ssq_ref[0]
    for q in range(1, ssq_ref.shape[0]):
        s = s + ssq_ref[q]
    row = lax.rsqrt(s / d + EPS)
    return jnp.transpose(jnp.broadcast_to(row, (LANES, row.shape[1])))[:, 0:1]


def _prenorm_kernel(x_ref, g_ref, hg_ref, ssq_ref):
    _emit_scaled(x_ref[...], g_ref, hg_ref, ssq_ref.at[0])


def _prenorm(x, g):
    m, d = x.shape
    bm = _pick(m, (512, 256, 128))
    blocks = [((bm, d), F32), ((1, d), F32), ((bm, d), BF16), ((1, 1, bm), F32)]
    return pl.pallas_call(
        _prenorm_kernel,
        grid=(m // bm,),
        in_specs=[pl.BlockSpec((bm, d), lambda i: (i, 0)),
                  pl.BlockSpec((1, d), lambda i: (0, 0))],
        out_specs=[pl.BlockSpec((bm, d), lambda i: (i, 0)),
                   pl.BlockSpec((1, 1, bm), lambda i: (0, 0, i))],
        out_shape=[jax.ShapeDtypeStruct((m, d), BF16), jax.ShapeDtypeStruct((1, 1, m), F32)],
        compiler_params=_cparams(1, blocks, temps=2 * bm * d * 4),
        name="prenorm",
    )(x, g.reshape(1, d))


def _rmsnorm_kernel(x_ref, g_ref, o_ref):
    x = x_ref[...]
    ms = jnp.mean(x * x, axis=-1, keepdims=True)
    o_ref[...] = (x * lax.rsqrt(ms + EPS) * g_ref[...]).astype(o_ref.dtype)


def _rmsnorm(x, g, out_dtype):
    m, d = x.shape
    bm = _pick(m, (512, 256, 128))
    blocks = [((bm, d), F32), ((1, d), F32), ((bm, d), out_dtype)]
    return pl.pallas_call(
        _rmsnorm_kernel,
        grid=(m // bm,),
        in_specs=[pl.BlockSpec((bm, d), lambda i: (i, 0)),
                  pl.BlockSpec((1, d), lambda i: (0, 0))],
        out_specs=pl.BlockSpec((bm, d), lambda i: (i, 0)),
        out_shape=jax.ShapeDtypeStruct((m, d), out_dtype),
        compiler_params=_cparams(1, blocks, temps=2 * bm * d * 4),
        name="rmsnorm",
    )(x, g.reshape(1, d))


def _inproj_kernel(a_ref, ssq_ref, w_ref, wn_ref, o_ref, wsc_ref, *, n_aligned, skip):
    j = pl.program_id(0)
    bn = w_ref.shape[0]
    rc = _pick(bn, (CAST_ROWS, LANES))

    @pl.when(jnp.logical_and(_first_token_tile(), j < n_aligned))
    def _():
        _cast_tile(w_ref, wsc_ref)

    @pl.when(jnp.logical_and(_first_token_tile(), j >= n_aligned))
    def _():
        for r in range(0, bn - rc, rc):
            wsc_ref[r:r + rc, :] = w_ref[r + skip:r + skip + rc, :].astype(BF16)
        wsc_ref[bn - rc:bn - skip, :] = w_ref[bn - rc + skip:bn, :].astype(BF16)
        wsc_ref[bn - skip:bn, :] = wn_ref[...].astype(BF16)

    acc = lax.dot_general(a_ref[...], wsc_ref[...], (((1,), (1,)), ((), ())), preferred_element_type=F32)
    o_ref[...] = (acc * _row_rsqrt(ssq_ref, a_ref.shape[1])).astype(o_ref.dtype)


def _inproj(u, ssq, w_in_t, layer, dt0, heads):
    m, k = u.shape
    parts = ssq.shape[0]
    d_in = w_in_t.shape[1]
    n = d_in - heads
    bm = _pick(m, (1024, 512, 256, 128))
    bn = next(c for c in (1024, 512, 256, 128) if dt0 % c == 0 and (n - dt0) % c == 0)
    assert heads % (2 * SUBLANES) == 0 and bn % heads == 0 and d_in % heads == 0 and heads < LANES
    n_aligned = dt0 // bn
    blocks = [((bm, k), BF16), ((parts, 1, bm), F32), ((bn, k), F32), ((heads, k), F32), ((bm, bn), BF16)]
    scratch = [((bn, k), BF16)]
    return pl.pallas_call(
        functools.partial(_inproj_kernel, n_aligned=n_aligned, skip=heads),
        grid=(n // bn, m // bm),
        in_specs=[pl.BlockSpec((bm, k), lambda j, i: (i, 0)),
                  pl.BlockSpec((parts, 1, bm), lambda j, i: (0, 0, i)),
                  pl.BlockSpec((None, bn, k), lambda j, i: (layer, j, 0)),
                  pl.BlockSpec((None, heads, k), lambda j, i: (layer, (j + 1) * (bn // heads), 0))],
        out_specs=pl.BlockSpec((bm, bn), lambda j, i: (i, j)),
        out_shape=jax.ShapeDtypeStruct((m, n), BF16),
        scratch_shapes=[pltpu.VMEM(s, d) for s, d in scratch],
        compiler_params=_cparams(2, blocks, scratch, temps=bm * bn * 4 + 2 * CAST_ROWS * k * 4),
        name="in_proj",
    )(u, ssq, w_in_t, w_in_t)


def _mm_res_kernel(a_ref, w_ref, h_ref, g_ref, o_ref, hg_ref, ssq_ref, wsc_ref):
    @pl.when(_first_token_tile())
    def _():
        _cast_tile(w_ref, wsc_ref)

    h_new = h_ref[...] + jnp.dot(a_ref[...], wsc_ref[...], preferred_element_type=F32)
    o_ref[...] = h_new
    _emit_scaled(h_new, g_ref, hg_ref, ssq_ref)


def _matmul_residual(a, w, layer, h, g_next, name, in_place=True):
    m, k = a.shape
    n = w.shape[2]
    big_k = k > 2048
    bm = _pick(m, (512, 256, 128)) if big_k else _pick(m, (1024, 512, 256, 128))
    bn = _pick(n, (1024, 512, 256, 128))
    w_block = ((k, bn), F32)
    blocks = [((bm, k), BF16), ((bm, bn), F32), ((1, bn), F32),
              ((bm, bn), F32), ((bm, bn), BF16), ((1, bm), F32)] + ([] if big_k else [w_block])
    single = [w_block] if big_k else []
    scratch = [((k, bn), BF16)]
    return pl.pallas_call(
        _mm_res_kernel,
        grid=(n // bn, m // bm),
        in_specs=[pl.BlockSpec((bm, k), lambda j, i: (i, 0)),
                  pl.BlockSpec((None, k, bn), lambda j, i: (layer, 0, j),
                               pipeline_mode=pl.Buffered(1) if big_k else None),
                  pl.BlockSpec((bm, bn), lambda j, i: (i, j)),
                  pl.BlockSpec((1, bn), lambda j, i: (0, j))],
        out_specs=[pl.BlockSpec((bm, bn), lambda j, i: (i, j)),
                   pl.BlockSpec((bm, bn), lambda j, i: (i, j)),
                   pl.BlockSpec((None, 1, bm), lambda j, i: (j, 0, i))],
        out_shape=[jax.ShapeDtypeStruct((m, n), F32), jax.ShapeDtypeStruct((m, n), BF16),
                   jax.ShapeDtypeStruct((n // bn, 1, m), F32)],
        scratch_shapes=[pltpu.VMEM(s, d) for s, d in scratch],
        input_output_aliases={2: 0} if in_place else {},
        compiler_params=_cparams(2, blocks, scratch, temps=2 * bm * bn * 4 + 2 * CAST_ROWS * bn * 4,
                                 single=single),
        name=name,
    )(a, w, h, g_next.reshape(1, n))


def _swiglu_kernel(v_ref, ssq_ref, wg_ref, wu_ref, o_ref, wgsc_ref, wusc_ref):
    @pl.when(_first_token_tile())
    def _():
        _cast_tile(wg_ref, wgsc_ref)
        _cast_tile(wu_ref, wusc_ref)

    v = v_ref[...]
    rs = _row_rsqrt(ssq_ref, v.shape[1])
    gate = jnp.dot(v, wgsc_ref[...], preferred_element_type=F32) * rs
    up = jnp.dot(v, wusc_ref[...], preferred_element_type=F32) * rs
    o_ref[...] = (_silu(gate) * up).astype(o_ref.dtype)


def _swiglu(v, ssq, w_gate_up, layer):
    m, k = v.shape
    parts = ssq.shape[0]
    d_ff = w_gate_up.shape[2] // 2
    bm = _pick(m, (1024, 512, 256, 128))
    bn = _pick(d_ff, (512, 256, 128))
    nb = d_ff // bn
    blocks = [((bm, k), BF16), ((parts, 1, bm), F32), ((k, bn), F32), ((k, bn), F32), ((bm, bn), BF16)]
    scratch = [((k, bn), BF16), ((k, bn), BF16)]
    return pl.pallas_call(
        _swiglu_kernel,
        grid=(nb, m // bm),
        in_specs=[pl.BlockSpec((bm, k), lambda j, i: (i, 0)),
                  pl.BlockSpec((parts, 1, bm), lambda j, i: (0, 0, i)),
                  pl.BlockSpec((None, k, bn), lambda j, i: (layer, 0, j)),
                  pl.BlockSpec((None, k, bn), lambda j, i: (layer, 0, j + nb))],
        out_specs=pl.BlockSpec((bm, bn), lambda j, i: (i, j)),
        out_shape=jax.ShapeDtypeStruct((m, d_ff), BF16),
        scratch_shapes=[pltpu.VMEM(s, d) for s, d in scratch],
        compiler_params=_cparams(2, blocks, scratch, temps=3 * bm * bn * 4 + 2 * CAST_ROWS * bn * 4),
        name="swiglu_up",
    )(v, ssq, w_gate_up, w_gate_up)


def _merge_kernel(y_ref, ysc_ref, wa_ref, wb_ref, ga_ref, gb_ref, o_ref, wasc_ref, wbsc_ref):
    @pl.when(_first_token_tile())
    def _():
        _cast_tile(wa_ref, wasc_ref)
        _cast_tile(wb_ref, wbsc_ref)

    ya = jnp.dot(y_ref[...], wasc_ref[...], preferred_element_type=F32)
    yb = jnp.dot(ysc_ref[...], wbsc_ref[...], preferred_element_type=F32)
    ga = _sigmoid(ga_ref[...].astype(F32))
    gb = _sigmoid(gb_ref[...].astype(F32))
    o_ref[...] = (ga * ya + gb * yb).astype(o_ref.dtype)


def _merge(y, ysc, w_a, w_b, layer, proj, off_ga, off_gb):
    m, ka = y.shape
    kb = ysc.shape[1]
    n = w_a.shape[2]
    bm = _pick(m, (512, 256, 128))
    bn = _pick(n, (1024, 512, 256, 128))
    ja, jb = off_ga // bn, off_gb // bn
    assert off_ga % bn == 0 and off_gb % bn == 0
    blocks = [((bm, ka), BF16), ((bm, kb), BF16), ((bm, bn), BF16), ((bm, bn), BF16), ((bm, bn), BF16)]
    single = [((ka, bn), F32), ((kb, bn), F32)]
    scratch = [((ka, bn), BF16), ((kb, bn), BF16)]
    return pl.pallas_call(
        _merge_kernel,
        grid=(n // bn, m // bm),
        in_specs=[pl.BlockSpec((bm, ka), lambda j, i: (i, 0)),
                  pl.BlockSpec((bm, kb), lambda j, i: (i, 0)),
                  pl.BlockSpec((None, ka, bn), lambda j, i: (layer, 0, j), pipeline_mode=pl.Buffered(1)),
                  pl.BlockSpec((None, kb, bn), lambda j, i: (layer, 0, j), pipeline_mode=pl.Buffered(1)),
                  pl.BlockSpec((bm, bn), lambda j, i: (i, j + ja)),
                  pl.BlockSpec((bm, bn), lambda j, i: (i, j + jb))],
        out_specs=pl.BlockSpec((bm, bn), lambda j, i: (i, j)),
        out_shape=jax.ShapeDtypeStruct((m, n), BF16),
        scratch_shapes=[pltpu.VMEM(s, d) for s, d in scratch],
        compiler_params=_cparams(2, blocks, scratch, temps=4 * bm * bn * 4 + 2 * CAST_ROWS * bn * 4,
                                 single=single),
        name="branch_merge",
    )(y, ysc, w_a, w_b, proj, proj)


def _ple_kernel(hn_ref, ssq_ref, wg_ref, p_ref, wp_ref, h_ref, g_ref, o_ref, hg_ref, ssqo_ref,
                wgsc_ref, wpsc_ref):
    @pl.when(_first_token_tile())
    def _():
        _cast_tile(wg_ref, wgsc_ref)
        _cast_tile(wp_ref, wpsc_ref)

    rs = _row_rsqrt(ssq_ref, hn_ref.shape[1])
    pg = _sigmoid(jnp.dot(hn_ref[...], wgsc_ref[...], preferred_element_type=F32) * rs)
    e = jnp.dot(p_ref[...].astype(BF16), wpsc_ref[...], preferred_element_type=F32)
    h_new = h_ref[...] + pg * e
    o_ref[...] = h_new
    _emit_scaled(h_new, g_ref, hg_ref, ssqo_ref)


def _ple(hn, ssq, w_gate, p, w_proj, layer, h, g_next):
    m, k = hn.shape
    parts = ssq.shape[0]
    kp = p.shape[2]
    n = w_gate.shape[2]
    bm = _pick(m, (1024, 512, 256, 128))
    bn = _pick(n, (1024, 512, 256, 128))
    blocks = [((bm, k), BF16), ((parts, 1, bm), F32), ((k, bn), F32), ((bm, kp), F32), ((kp, bn), F32),
              ((bm, bn), F32), ((1, bn), F32), ((bm, bn), F32), ((bm, bn), BF16), ((1, bm), F32)]
    scratch = [((k, bn), BF16), ((kp, bn), BF16)]
    return pl.pallas_call(
        _ple_kernel,
        grid=(n // bn, m // bm),
        in_specs=[pl.BlockSpec((bm, k), lambda j, i: (i, 0)),
                  pl.BlockSpec((parts, 1, bm), lambda j, i: (0, 0, i)),
                  pl.BlockSpec((None, k, bn), lambda j, i: (layer, 0, j)),
                  pl.BlockSpec((None, bm, kp), lambda j, i: (layer, i, 0)),
                  pl.BlockSpec((None, kp, bn), lambda j, i: (layer, 0, j)),
                  pl.BlockSpec((bm, bn), lambda j, i: (i, j)),
                  pl.BlockSpec((1, bn), lambda j, i: (0, j))],
        out_specs=[pl.BlockSpec((bm, bn), lambda j, i: (i, j)),
                   pl.BlockSpec((bm, bn), lambda j, i: (i, j)),
                   pl.BlockSpec((None, 1, bm), lambda j, i: (j, 0, i))],
        out_shape=[jax.ShapeDtypeStruct((m, n), F32), jax.ShapeDtypeStruct((m, n), BF16),
                   jax.ShapeDtypeStruct((n // bn, 1, m), F32)],
        scratch_shapes=[pltpu.VMEM(s, d) for s, d in scratch],
        input_output_aliases={5: 0},
        compiler_params=_cparams(2, blocks, scratch, temps=4 * bm * bn * 4 + 2 * CAST_ROWS * bn * 4),
        name="ple",
    )(hn, ssq, w_gate, p, w_proj, h, g_next.reshape(1, n))


def _dtprep_kernel(u_ref, ssq_ref, w_ref, bias_ref, alog_ref, dtrow_ref, wrow_ref, acsrow_ref, *, nc):
    x = lax.dot_general(u_ref[...], w_ref[...].astype(BF16), (((1,), (1,)), ((), ())),
                        preferred_element_type=F32)
    x = x * _row_rsqrt(ssq_ref, u_ref.shape[1]) + bias_ref[...]
    dt = jnp.maximum(x, 0.0) + jnp.log1p(jnp.exp(-jnp.abs(x)))
    adt = dt * (-jnp.exp(alog_ref[...]))
    row = lax.broadcasted_iota(jnp.int32, (CHUNK, LANES), 0)
    for k in range(nc):
        sl = slice(k * CHUNK, (k + 1) * CHUNK)
        acs = adt[sl]
        sh = 1
        while sh < CHUNK:
            acs = acs + jnp.where(row >= sh, pltpu.roll(acs, sh, 0), 0.0)
            sh *= 2
        acs2 = acs * LOG2E
        acsrow_ref[k] = acs2.T
        dtrow_ref[k] = dt[sl].T
        wrow_ref[k] = (dt[sl] * jnp.exp(acs[CHUNK - 1:CHUNK, :] - acs)).T


def _dtprep(u, ssq, w_in_t, layer, dt0, bias, a_log):
    m, k = u.shape
    parts = ssq.shape[0]
    nc = 4 if m % (4 * CHUNK) == 0 else 1
    ts = nc * CHUNK
    nchunks = m // CHUNK
    rows = ((nc, LANES, LANES), F32)
    blocks = [((ts, k), BF16), ((parts, 1, ts), F32), ((LANES, k), F32), rows, rows, rows]
    row_spec = pl.BlockSpec((nc, LANES, LANES), lambda i: (i, 0, 0))
    row_shape = jax.ShapeDtypeStruct((nchunks, LANES, LANES), F32)
    return pl.pallas_call(
        functools.partial(_dtprep_kernel, nc=nc),
        grid=(m // ts,),
        in_specs=[pl.BlockSpec((ts, k), lambda i: (i, 0)),
                  pl.BlockSpec((parts, 1, ts), lambda i: (0, 0, i)),
                  pl.BlockSpec((None, LANES, k), lambda i: (layer, dt0 // LANES, 0)),
                  pl.BlockSpec((1, LANES), lambda i: (0, 0)),
                  pl.BlockSpec((1, LANES), lambda i: (0, 0))],
        out_specs=[row_spec, row_spec, row_spec],
        out_shape=[row_shape, row_shape, row_shape],
        compiler_params=_cparams(1, blocks, temps=8 * ts * LANES * 4 + k * LANES * 2),
        name="dt_prep",
    )(u, ssq, w_in_t, bias, a_log)


def _ssd_kernel(z_ref, xs_ref, b_ref, c_ref, xsp_ref, bp_ref, cp_ref,
                dtr_ref, wr_ref, acr_ref, shift_ref,
                cwx_ref, cwb_ref, cwc_ref, cbx_ref, cbb_ref, cbc_ref, dsk_ref, nw_ref,
                o_ref, st_ref, yt_ref, *, nc, r_heads, kconv):
    g = pl.program_id(1)
    c = pl.program_id(2)
    L = CHUNK
    N = SSD_STATE
    gw = r_heads * HEAD_DIM
    npair = gw // LANES

    @pl.when(c == 0)
    def _():
        st_ref[...] = jnp.zeros(st_ref.shape, F32)

    tri = (lax.broadcasted_iota(jnp.int32, (L, L), 0) >= lax.broadcasted_iota(jnp.int32, (L, L), 1))
    lo = lax.broadcasted_iota(jnp.int32, (L, LANES), 1) < HEAD_DIM
    hi = jnp.logical_not(lo)
    lo_row = lo[0:1]

    def two_chunks(ref, prev_ref, k):
        if k == 0:
            prev = prev_ref[...]
            prev = jnp.where(c > 0, prev, jnp.zeros_like(prev))
            return jnp.concatenate([prev, ref[0:L, :]], axis=0)
        return ref[(k - 1) * L:(k + 1) * L, :]

    def conv_silu(shifted, cur, w_ref, bias_ref):
        acc = None
        for tap in range(kconv - 1):
            term = shifted[tap * L:(tap + 1) * L, :] * w_ref[tap:tap + 1, :]
            acc = term if acc is None else acc + term
        acc = acc + cur.astype(F32) * w_ref[kconv - 1:kconv, :]
        acc = acc + bias_ref[...]
        return _silu(acc)

    for k in range(nc):
        r0 = k * L
        x2 = jnp.concatenate([two_chunks(xs_ref, xsp_ref, k), two_chunks(b_ref, bp_ref, k),
                              two_chunks(c_ref, cp_ref, k)], axis=1)
        shifted = jnp.dot(shift_ref[...], x2, preferred_element_type=F32)
        cur = x2[L:2 * L]
        x = conv_silu(shifted[:, 0:gw], cur[:, 0:gw], cwx_ref, cbx_ref)
        bm = conv_silu(shifted[:, gw:gw + N], cur[:, gw:gw + N], cwb_ref, cbb_ref)
        cm = conv_silu(shifted[:, gw + N:gw + 2 * N], cur[:, gw + N:gw + 2 * N], cwc_ref, cbc_ref)
        cb = lax.dot_general(cm.astype(BF16), bm.astype(BF16), (((1,), (1,)), ((), ())),
                             preferred_element_type=F32)
        bt = bm.T
        h0 = g * r_heads
        if r_heads % SUBLANES == 0:
            h0 = pl.multiple_of(h0, SUBLANES)
        arows = acr_ref[k, pl.ds(h0, r_heads), :]
        acol = jnp.transpose(jnp.concatenate(
            [arows, jnp.zeros((LANES - r_heads, L), F32)], axis=0))
        cdec = jnp.exp2(acol[L - 1:L, :])
        ssq = jnp.zeros((L, 1), F32)
        for j in range(npair):
            cs = slice(j * LANES, (j + 1) * LANES)
            xp = x[:, cs]
            s_prev = st_ref[:, cs]
            s_new = s_prev * jnp.where(lo_row, cdec[:, 2 * j:2 * j + 1], cdec[:, 2 * j + 1:2 * j + 2])
            y = None
            for hh, keep in ((2 * j, lo), (2 * j + 1, hi)):
                head = pl.ds(g * r_heads + hh, 1)
                arow = acr_ref[k, head, :]
                drow = dtr_ref[k, head, :]
                wrow = wr_ref[k, head, :]
                xm = jnp.where(keep, xp, 0.0).astype(BF16)
                sm = jnp.where(keep, s_prev, 0.0).astype(BF16)
                ab = jnp.broadcast_to(acol[:, hh:hh + 1], (L, L))
                dec = jnp.exp2(jnp.where(tri, ab - arow, -jnp.inf))
                mh = cb * dec * drow
                ch = cm * jnp.exp2(ab)
                lhs = jnp.concatenate([mh, ch], axis=1).astype(BF16)
                rhs = jnp.concatenate([xm, sm], axis=0)
                t = jnp.dot(lhs, rhs, preferred_element_type=F32)
                y = t if y is None else y + t
                bth = (bt * wrow).astype(BF16)
                s_new = s_new + jnp.dot(bth, xm, preferred_element_type=F32)
            st_ref[:, cs] = s_new
            yt = y + dsk_ref[:, cs] * xp
            zt = z_ref[r0:r0 + L, cs].astype(F32)
            yt = yt * _silu(zt)
            ssq = ssq + jnp.sum(yt * yt, axis=-1, keepdims=True)
            yt_ref[:, cs] = yt
        rs = lax.rsqrt(ssq / gw + EPS)
        for j in range(npair):
            cs = slice(j * LANES, (j + 1) * LANES)
            o_ref[r0:r0 + L, cs] = (yt_ref[:, cs] * rs * nw_ref[:, cs]).astype(o_ref.dtype)


def _ssd(proj, dtrow, wrow, acsrow, conv_w, conv_b, d_skip, norm_w, *, batch, seq, inner, groups):
    m = proj.shape[0]
    gw = inner // groups
    r_heads = gw // HEAD_DIM
    assert gw % LANES == 0 and r_heads % 2 == 0 and SSD_STATE == LANES
    kconv = conv_w.shape[0]
    assert kconv - 1 <= CHUNK
    nc = _pick(seq // CHUNK, (16, 8, 4, 2, 1))
    ts = nc * CHUNK
    nt = seq // ts
    gn = groups * SSD_STATE
    xs_blk, b_blk, c_blk = inner // gw, 2 * inner // SSD_STATE, (2 * inner + gn) // SSD_STATE
    wb_blk, wc_blk = inner // SSD_STATE, (inner + gn) // SSD_STATE
    hp = acsrow.shape[1]
    t_idx = jnp.arange(CHUNK)[None, :, None]
    d_idx = (kconv - 1 - jnp.arange(kconv - 1))[:, None, None]
    s_idx = jnp.arange(2 * CHUNK)[None, None, :]
    shift = (s_idx == CHUNK + t_idx - d_idx).astype(BF16).reshape((kconv - 1) * CHUNK, 2 * CHUNK)
    st_shape = (SSD_STATE, gw)
    blocks = [((ts, gw), BF16), ((ts, gw), BF16), ((ts, SSD_STATE), BF16), ((ts, SSD_STATE), BF16),
              ((CHUNK, gw), BF16), ((CHUNK, SSD_STATE), BF16), ((CHUNK, SSD_STATE), BF16),
              ((nc, hp, LANES), F32), ((nc, hp, LANES), F32), ((nc, hp, LANES), F32),
              (shift.shape, BF16),
              ((kconv, gw), F32), ((kconv, SSD_STATE), F32), ((kconv, SSD_STATE), F32),
              ((1, gw), F32), ((1, SSD_STATE), F32), ((1, SSD_STATE), F32), ((1, gw), F32), ((1, gw), F32),
              ((ts, gw), BF16)]
    scratch = [(st_shape, F32), ((CHUNK, gw), F32)]
    row = lambda b, g, c: b * nt + c
    prev = lambda b, g, c: jnp.maximum(row(b, g, c) * nc - 1, 0)
    return pl.pallas_call(
        functools.partial(_ssd_kernel, nc=nc, r_heads=r_heads, kconv=kconv),
        grid=(batch, groups, nt),
        in_specs=[
            pl.BlockSpec((ts, gw), lambda b, g, c: (row(b, g, c), g)),
            pl.BlockSpec((ts, gw), lambda b, g, c: (row(b, g, c), xs_blk + g)),
            pl.BlockSpec((ts, SSD_STATE), lambda b, g, c: (row(b, g, c), b_blk + g)),
            pl.BlockSpec((ts, SSD_STATE), lambda b, g, c: (row(b, g, c), c_blk + g)),
            pl.BlockSpec((CHUNK, gw), lambda b, g, c: (prev(b, g, c), xs_blk + g)),
            pl.BlockSpec((CHUNK, SSD_STATE), lambda b, g, c: (prev(b, g, c), b_blk + g)),
            pl.BlockSpec((CHUNK, SSD_STATE), lambda b, g, c: (prev(b, g, c), c_blk + g)),
            pl.BlockSpec((nc, hp, LANES), lambda b, g, c: (row(b, g, c), 0, 0)),
            pl.BlockSpec((nc, hp, LANES), lambda b, g, c: (row(b, g, c), 0, 0)),
            pl.BlockSpec((nc, hp, LANES), lambda b, g, c: (row(b, g, c), 0, 0)),
            pl.BlockSpec(shift.shape, lambda b, g, c: (0, 0)),
            pl.BlockSpec((kconv, gw), lambda b, g, c: (0, g)),
            pl.BlockSpec((kconv, SSD_STATE), lambda b, g, c: (0, wb_blk + g)),
            pl.BlockSpec((kconv, SSD_STATE), lambda b, g, c: (0, wc_blk + g)),
            pl.BlockSpec((1, gw), lambda b, g, c: (0, g)),
            pl.BlockSpec((1, SSD_STATE), lambda b, g, c: (0, wb_blk + g)),
            pl.BlockSpec((1, SSD_STATE), lambda b, g, c: (0, wc_blk + g)),
            pl.BlockSpec((1, gw), lambda b, g, c: (0, g)),
            pl.BlockSpec((1, gw), lambda b, g, c: (0, g)),
        ],
        out_specs=pl.BlockSpec((ts, gw), lambda b, g, c: (row(b, g, c), g)),
        out_shape=jax.ShapeDtypeStruct((m, inner), BF16),
        scratch_shapes=[pltpu.VMEM(s, d) for s, d in scratch],
        compiler_params=_cparams(3, blocks, scratch, temps=8 << 20),
        name="ssd",
    )(proj, proj, proj, proj, proj, proj, proj, dtrow, wrow, acsrow, shift,
      conv_w, conv_w, conv_w, conv_b, conv_b, conv_b, d_skip, norm_w)


def _sconv_kernel(gb_ref, gc_ref, xt_ref, w_ref, o_ref, pe_ref, *, ts, kconv, cw):
    halo = SUBLANES

    @pl.when(pl.program_id(1) == 0)
    def _():
        pe_ref[0:halo, :] = jnp.zeros((halo, pe_ref.shape[1]), F32)

    pe_ref[halo:halo + ts, :] = gb_ref[...].astype(F32) * xt_ref[...].astype(F32)
    width = pe_ref.shape[1]
    for c0 in range(0, width, cw):
        acc = None
        for k in range(kconv):
            r = halo - (kconv - 1) + k
            term = pe_ref[r:r + ts, c0:c0 + cw] * w_ref[k:k + 1, c0:c0 + cw]
            acc = term if acc is None else acc + term
        o_ref[:, c0:c0 + cw] = (gc_ref[:, c0:c0 + cw].astype(F32) * acc).astype(o_ref.dtype)
    pe_ref[0:halo, :] = pe_ref[ts:ts + halo, :]


def _sconv(proj, conv_w, *, batch, seq, width, off_b, off_c, off_x):
    m = proj.shape[0]
    kconv = conv_w.shape[0]
    ts = _pick(seq, (256, 128))
    nt = seq // ts
    assert off_b % width == 0 and off_c % width == 0 and off_x % width == 0 and kconv - 1 <= SUBLANES
    jb, jc, jx = off_b // width, off_c // width, off_x // width
    cw = _pick(width, (512, 256, 128))
    pe_shape = (SUBLANES + ts, width)
    blocks = [((ts, width), BF16)] * 4 + [((kconv, width), F32)]
    scratch = [(pe_shape, F32)]
    return pl.pallas_call(
        functools.partial(_sconv_kernel, ts=ts, kconv=kconv, cw=cw),
        grid=(batch, nt),
        in_specs=[pl.BlockSpec((ts, width), lambda b, c: (b * nt + c, jb)),
                  pl.BlockSpec((ts, width), lambda b, c: (b * nt + c, jc)),
                  pl.BlockSpec((ts, width), lambda b, c: (b * nt + c, jx)),
                  pl.BlockSpec((kconv, width), lambda b, c: (0, 0))],
        out_specs=pl.BlockSpec((ts, width), lambda b, c: (b * nt + c, 0)),
        out_shape=jax.ShapeDtypeStruct((m, width), BF16),
        scratch_shapes=[pltpu.VMEM(s, d) for s, d in scratch],
        compiler_params=_cparams(2, blocks, scratch, temps=4 << 20),
        name="short_conv",
    )(proj, proj, proj, conv_w)


def kernel(x, p, norm_mix, w_in, ssd_conv_w, ssd_conv_b, ssd_dt_bias, ssd_a_log, ssd_d, ssd_norm,
           ssd_out, sc_conv_w, sc_out, w_o, norm_ffn, w_gate_up, w_down, norm_ple, ple_gate,
           ple_proj, norm_final):
    batch, seq, d = x.shape
    depth = w_in.shape[0]
    m = batch * seq
    heads = ssd_a_log.shape[1]
    inner = heads * HEAD_DIM
    groups = SSD_GROUPS
    r_heads = heads // groups
    xbc = ssd_conv_w.shape[2]
    scw = sc_conv_w.shape[2]
    assert xbc == inner + 2 * groups * SSD_STATE and heads <= LANES
    dt0 = inner + xbc
    off_scb = dt0
    off_scc, off_scx = off_scb + scw, off_scb + 2 * scw
    off_ga = off_scb + 3 * scw
    off_gb = off_ga + d
    hpad = LANES - heads
    p2 = p.reshape(depth, m, p.shape[-1])
    w_in_t = jnp.swapaxes(w_in, 1, 2)

    h = x.reshape(m, d)
    u, u_ssq = _prenorm(h, norm_mix[0])
    for i in range(depth):
        dt_bias = jnp.pad(ssd_dt_bias[i], (0, hpad)).reshape(1, LANES)
        a_log = jnp.pad(ssd_a_log[i], (0, hpad)).reshape(1, LANES)
        d_skip = jnp.repeat(ssd_d[i], HEAD_DIM).reshape(1, inner)

        proj = _inproj(u, u_ssq, w_in_t, i, dt0, heads)
        dtrow, wrow, acsrow = _dtprep(u, u_ssq, w_in_t, i, dt0, dt_bias, a_log)
        y = _ssd(proj, dtrow, wrow, acsrow, ssd_conv_w[i], ssd_conv_b[i].reshape(1, xbc), d_skip,
                 ssd_norm[i].reshape(1, inner), batch=batch, seq=seq, inner=inner, groups=groups)
        ysc = _sconv(proj, sc_conv_w[i], batch=batch, seq=seq, width=scw,
                     off_b=off_scb, off_c=off_scc, off_x=off_scx)
        merged = _merge(y, ysc, ssd_out, sc_out, i, proj, off_ga, off_gb)
        h, v, v_ssq = _matmul_residual(merged, w_o, i, h, norm_ffn[i], "w_o_residual", in_place=i > 0)
        act = _swiglu(v, v_ssq, w_gate_up, i)
        h, hn, hn_ssq = _matmul_residual(act, w_down, i, h, norm_ple[i], "w_down_residual")
        g_next = norm_mix[i + 1] if i + 1 < depth else norm_final
        h, u, u_ssq = _ple(hn, hn_ssq, ple_gate, p2, ple_proj, i, h, g_next)
    out = _rmsnorm(h, norm_final, F32)
    return out.reshape(batch, seq, d)
```

```python
import functools

import jax
import jax.numpy as jnp
from jax import lax
from jax.experimental import pallas as pl
from jax.experimental.pallas import tpu as pltpu

F32 = jnp.float32
BF16 = jnp.bfloat16

EPS = 1e-6
HEAD_DIM = 64
SSD_GROUPS = 8
SSD_STATE = 128
CHUNK = 128
LANES = 128
SUBLANES = 8
CAST_ROWS = 256
LOG2E = 1.4426950408889634
VMEM_BYTES_V7X = 64 * 1024 * 1024
VMEM_CAP = VMEM_BYTES_V7X - 4 * 1024 * 1024


def _nbytes(shape, dtype):
    n = 1
    for s in shape:
        n *= s
    return n * jnp.dtype(dtype).itemsize


def _cparams(ngrid, blocks, scratch=(), temps=0, single=()):
    need = (2 * sum(_nbytes(s, d) for s, d in blocks) + sum(_nbytes(s, d) for s, d in single)
            + sum(_nbytes(s, d) for s, d in scratch) + temps + (4 << 20))
    return pltpu.CompilerParams(dimension_semantics=("arbitrary",) * ngrid,
                                vmem_limit_bytes=int(min(max(need, 16 << 20), VMEM_CAP)))


def _pick(n, candidates):
    for c in candidates:
        if n % c == 0:
            return c
    raise ValueError(f"no block size in {candidates} divides {n}")


def _cast_tile(w_ref, wsc_ref):
    k = w_ref.shape[0]
    rc = _pick(k, (CAST_ROWS, LANES, SUBLANES))
    for r in range(0, k, rc):
        wsc_ref[r:r + rc, :] = w_ref[r:r + rc, :].astype(BF16)


def _first_token_tile():
    return pl.program_id(1) == 0


def _sigmoid(x):
    return 0.5 * jnp.tanh(0.5 * x) + 0.5


def _silu(x):
    h = 0.5 * x
    return h * jnp.tanh(h) + h


def _emit_scaled(h_new, g_ref, hg_ref, ssq_ref):
    hg_ref[...] = (h_new * g_ref[...]).astype(BF16)
    col = jnp.sum(h_new * h_new, axis=-1, keepdims=True)
    ssq_ref[...] = jnp.transpose(jnp.broadcast_to(col, (col.shape[0], LANES)))[0:1, :]


def _row_rsqrt(ssq_ref, d):
    s = ssq_ref[0]
    for q in range(1, ssq_ref.shape[0]):
        s = s + ssq_ref[q]
    row = lax.rsqrt(s / d + EPS)
    return jnp.transpose(jnp.broadcast_to(row, (LANES, row.shape[1])))[:, 0:1]


def _prenorm_kernel(x_ref, g_ref, hg_ref, ssq_ref):
    _emit_scaled(x_ref[...], g_ref, hg_ref, ssq_ref.at[0])


def _prenorm(x, g):
    m, d = x.shape
    bm = _pick(m, (512, 256, 128))
    blocks = [((bm, d), F32), ((1, d), F32), ((bm, d), BF16), ((1, 1, bm), F32)]
    return pl.pallas_call(
        _prenorm_kernel,
        grid=(m // bm,),
        in_specs=[pl.BlockSpec((bm, d), lambda i: (i, 0)),
                  pl.BlockSpec((1, d), lambda i: (0, 0))],
        out_specs=[pl.BlockSpec((bm, d), lambda i: (i, 0)),
                   pl.BlockSpec((1, 1, bm), lambda i: (0, 0, i))],
        out_shape=[jax.ShapeDtypeStruct((m, d), BF16), jax.ShapeDtypeStruct((1, 1, m), F32)],
        compiler_params=_cparams(1, blocks, temps=2 * bm * d * 4),
        name="prenorm",
    )(x, g.reshape(1, d))


def _rmsnorm_kernel(x_ref, g_ref, o_ref):
    x = x_ref[...]
    ms = jnp.mean(x * x, axis=-1, keepdims=True)
    o_ref[...] = (x * lax.rsqrt(ms + EPS) * g_ref[...]).astype(o_ref.dtype)


def _rmsnorm(x, g, out_dtype):
    m, d = x.shape
    bm = _pick(m, (512, 256, 128))
    blocks = [((bm, d), F32), ((1, d), F32), ((bm, d), out_dtype)]
    return pl.pallas_call(
        _rmsnorm_kernel,
        grid=(m // bm,),
        in_specs=[pl.BlockSpec((bm, d), lambda i: (i, 0)),
                  pl.BlockSpec((1, d), lambda i: (0, 0))],
        out_specs=pl.BlockSpec((bm, d), lambda i: (i, 0)),
        out_shape=jax.ShapeDtypeStruct((m, d), out_dtype),
        compiler_params=_cparams(1, blocks, temps=2 * bm * d * 4),
        name="rmsnorm",
    )(x, g.reshape(1, d))


def _inproj_kernel(a_ref, ssq_ref, w_ref, wn_ref, o_ref, wsc_ref, *, n_aligned, skip):
    j = pl.program_id(0)
    bn = w_ref.shape[0]
    rc = _pick(bn, (CAST_ROWS, LANES))

    @pl.when(jnp.logical_and(_first_token_tile(), j < n_aligned))
    def _():
        _cast_tile(w_ref, wsc_ref)

    @pl.when(jnp.logical_and(_first_token_tile(), j >= n_aligned))
    def _():
        for r in range(0, bn - rc, rc):
            wsc_ref[r:r + rc, :] = w_ref[r + skip:r + skip + rc, :].astype(BF16)
        wsc_ref[bn - rc:bn - skip, :] = w_ref[bn - rc + skip:bn, :].astype(BF16)
        wsc_ref[bn - skip:bn, :] = wn_ref[...].astype(BF16)

    acc = lax.dot_general(a_ref[...], wsc_ref[...], (((1,), (1,)), ((), ())), preferred_element_type=F32)
    o_ref[...] = (acc * _row_rsqrt(ssq_ref, a_ref.shape[1])).astype(o_ref.dtype)


def _inproj(u, ssq, w_in_t, layer, dt0, heads):
    m, k = u.shape
    parts = ssq.shape[0]
    d_in = w_in_t.shape[1]
    n = d_in - heads
    bm = _pick(m, (1024, 512, 256, 128))
    bn = next(c for c in (1024, 512, 256, 128) if dt0 % c == 0 and (n - dt0) % c == 0)
    assert heads % (2 * SUBLANES) == 0 and bn % heads == 0 and d_in % heads == 0 and heads < LANES
    n_aligned = dt0 // bn
    blocks = [((bm, k), BF16), ((parts, 1, bm), F32), ((bn, k), F32), ((heads, k), F32), ((bm, bn), BF16)]
    scratch = [((bn, k), BF16)]
    return pl.pallas_call(
        functools.partial(_inproj_kernel, n_aligned=n_aligned, skip=heads),
        grid=(n // bn, m // bm),
        in_specs=[pl.BlockSpec((bm, k), lambda j, i: (i, 0)),
                  pl.BlockSpec((parts, 1, bm), lambda j, i: (0, 0, i)),
                  pl.BlockSpec((None, bn, k), lambda j, i: (layer, j, 0)),
                  pl.BlockSpec((None, heads, k), lambda j, i: (layer, (j + 1) * (bn // heads), 0))],
        out_specs=pl.BlockSpec((bm, bn), lambda j, i: (i, j)),
        out_shape=jax.ShapeDtypeStruct((m, n), BF16),
        scratch_shapes=[pltpu.VMEM(s, d) for s, d in scratch],
        compiler_params=_cparams(2, blocks, scratch, temps=bm * bn * 4 + 2 * CAST_ROWS * k * 4),
        name="in_proj",
    )(u, ssq, w_in_t, w_in_t)


def _mm_res_kernel(a_ref, w_ref, h_ref, g_ref, o_ref, hg_ref, ssq_ref, wsc_ref):
    @pl.when(_first_token_tile())
    def _():
        _cast_tile(w_ref, wsc_ref)

    h_new = h_ref[...] + jnp.dot(a_ref[...], wsc_ref[...], preferred_element_type=F32)
    o_ref[...] = h_new
    _emit_scaled(h_new, g_ref, hg_ref, ssq_ref)


def _matmul_residual(a, w, layer, h, g_next, name, in_place=True):
    m, k = a.shape
    n = w.shape[2]
    big_k = k > 2048
    bm = _pick(m, (512, 256, 128)) if big_k else _pick(m, (1024, 512, 256, 128))
    bn = _pick(n, (1024, 512, 256, 128))
    w_block = ((k, bn), F32)
    blocks = [((bm, k), BF16), ((bm, bn), F32), ((1, bn), F32),
              ((bm, bn), F32), ((bm, bn), BF16), ((1, bm), F32)] + ([] if big_k else [w_block])
    single = [w_block] if big_k else []
    scratch = [((k, bn), BF16)]
    return pl.pallas_call(
        _mm_res_kernel,
        grid=(n // bn, m // bm),
        in_specs=[pl.BlockSpec((bm, k), lambda j, i: (i, 0)),
                  pl.BlockSpec((None, k, bn), lambda j, i: (layer, 0, j),
                               pipeline_mode=pl.Buffered(1) if big_k else None),
                  pl.BlockSpec((bm, bn), lambda j, i: (i, j)),
                  pl.BlockSpec((1, bn), lambda j, i: (0, j))],
        out_specs=[pl.BlockSpec((bm, bn), lambda j, i: (i, j)),
                   pl.BlockSpec((bm, bn), lambda j, i: (i, j)),
                   pl.BlockSpec((None, 1, bm), lambda j, i: (j, 0, i))],
        out_shape=[jax.ShapeDtypeStruct((m, n), F32), jax.ShapeDtypeStruct((m, n), BF16),
                   jax.ShapeDtypeStruct((n // bn, 1, m), F32)],
        scratch_shapes=[pltpu.VMEM(s, d) for s, d in scratch],
        input_output_aliases={2: 0} if in_place else {},
        compiler_params=_cparams(2, blocks, scratch, temps=2 * bm * bn * 4 + 2 * CAST_ROWS * bn * 4,
                                 single=single),
        name=name,
    )(a, w, h, g_next.reshape(1, n))


def _swiglu_kernel(v_ref, ssq_ref, wg_ref, wu_ref, o_ref, wgsc_ref, wusc_ref):
    @pl.when(_first_token_tile())
    def _():
        _cast_tile(wg_ref, wgsc_ref)
        _cast_tile(wu_ref, wusc_ref)

    v = v_ref[...]
    rs = _row_rsqrt(ssq_ref, v.shape[1])
    gate = jnp.dot(v, wgsc_ref[...], preferred_element_type=F32) * rs
    up = jnp.dot(v, wusc_ref[...], preferred_element_type=F32) * rs
    o_ref[...] = (_silu(gate) * up).astype(o_ref.dtype)


def _swiglu(v, ssq, w_gate_up, layer):
    m, k = v.shape
    parts = ssq.shape[0]
    d_ff = w_gate_up.shape[2] // 2
    bm = _pick(m, (1024, 512, 256, 128))
    bn = _pick(d_ff, (512, 256, 128))
    nb = d_ff // bn
    blocks = [((bm, k), BF16), ((parts, 1, bm), F32), ((k, bn), F32), ((k, bn), F32), ((bm, bn), BF16)]
    scratch = [((k, bn), BF16), ((k, bn), BF16)]
    return pl.pallas_call(
        _swiglu_kernel,
        grid=(nb, m // bm),
        in_specs=[pl.BlockSpec((bm, k), lambda j, i: (i, 0)),
                  pl.BlockSpec((parts, 1, bm), lambda j, i: (0, 0, i)),
                  pl.BlockSpec((None, k, bn), lambda j, i: (layer, 0, j)),
                  pl.BlockSpec((None, k, bn), lambda j, i: (layer, 0, j + nb))],
        out_specs=pl.BlockSpec((bm, bn), lambda j, i: (i, j)),
        out_shape=jax.ShapeDtypeStruct((m, d_ff), BF16),
        scratch_shapes=[pltpu.VMEM(s, d) for s, d in scratch],
        compiler_params=_cparams(2, blocks, scratch, temps=3 * bm * bn * 4 + 2 * CAST_ROWS * bn * 4),
        name="swiglu_up",
    )(v, ssq, w_gate_up, w_gate_up)


def _merge_kernel(y_ref, wa_ref, yb_ref, ga_ref, gb_ref, o_ref, wasc_ref):
    @pl.when(_first_token_tile())
    def _():
        _cast_tile(wa_ref, wasc_ref)

    ya = jnp.dot(y_ref[...], wasc_ref[...], preferred_element_type=F32)
    ga = _sigmoid(ga_ref[...].astype(F32))
    gb = _sigmoid(gb_ref[...].astype(F32))
    o_ref[...] = (ga * ya + gb * yb_ref[...]).astype(o_ref.dtype)


def _merge(y, w_a, layer, yb, proj, off_ga, off_gb):
    m, ka = y.shape
    n = w_a.shape[2]
    bm = _pick(m, (512, 256, 128))
    bn = _pick(n, (1024, 512, 256, 128))
    ja, jb = off_ga // bn, off_gb // bn
    assert off_ga % bn == 0 and off_gb % bn == 0
    blocks = [((bm, ka), BF16), ((bm, bn), F32), ((bm, bn), BF16), ((bm, bn), BF16), ((bm, bn), BF16)]
    single = [((ka, bn), F32)]
    scratch = [((ka, bn), BF16)]
    return pl.pallas_call(
        _merge_kernel,
        grid=(n // bn, m // bm),
        in_specs=[pl.BlockSpec((bm, ka), lambda j, i: (i, 0)),
                  pl.BlockSpec((None, ka, bn), lambda j, i: (layer, 0, j), pipeline_mode=pl.Buffered(1)),
                  pl.BlockSpec((bm, bn), lambda j, i: (i, j)),
                  pl.BlockSpec((bm, bn), lambda j, i: (i, j + ja)),
                  pl.BlockSpec((bm, bn), lambda j, i: (i, j + jb))],
        out_specs=pl.BlockSpec((bm, bn), lambda j, i: (i, j)),
        out_shape=jax.ShapeDtypeStruct((m, n), BF16),
        scratch_shapes=[pltpu.VMEM(s, d) for s, d in scratch],
        compiler_params=_cparams(2, blocks, scratch, temps=4 * bm * bn * 4 + 2 * CAST_ROWS * bn * 4,
                                 single=single),
        name="branch_merge",
    )(y, w_a, yb, proj, proj)


def _ple_kernel(hn_ref, ssq_ref, wg_ref, p_ref, wp_ref, h_ref, g_ref, o_ref, hg_ref, ssqo_ref,
                wgsc_ref, wpsc_ref):
    @pl.when(_first_token_tile())
    def _():
        _cast_tile(wg_ref, wgsc_ref)
        _cast_tile(wp_ref, wpsc_ref)

    rs = _row_rsqrt(ssq_ref, hn_ref.shape[1])
    pg = _sigmoid(jnp.dot(hn_ref[...], wgsc_ref[...], preferred_element_type=F32) * rs)
    e = jnp.dot(p_ref[...].astype(BF16), wpsc_ref[...], preferred_element_type=F32)
    h_new = h_ref[...] + pg * e
    o_ref[...] = h_new
    _emit_scaled(h_new, g_ref, hg_ref, ssqo_ref)


def _ple(hn, ssq, w_gate, p, w_proj, layer, h, g_next):
    m, k = hn.shape
    parts = ssq.shape[0]
    kp = p.shape[2]
    n = w_gate.shape[2]
    bm = _pick(m, (1024, 512, 256, 128))
    bn = _pick(n, (1024, 512, 256, 128))
    blocks = [((bm, k), BF16), ((parts, 1, bm), F32), ((k, bn), F32), ((bm, kp), F32), ((kp, bn), F32),
              ((bm, bn), F32), ((1, bn), F32), ((bm, bn), F32), ((bm, bn), BF16), ((1, bm), F32)]
    scratch = [((k, bn), BF16), ((kp, bn), BF16)]
    return pl.pallas_call(
        _ple_kernel,
        grid=(n // bn, m // bm),
        in_specs=[pl.BlockSpec((bm, k), lambda j, i: (i, 0)),
                  pl.BlockSpec((parts, 1, bm), lambda j, i: (0, 0, i)),
                  pl.BlockSpec((None, k, bn), lambda j, i: (layer, 0, j)),
                  pl.BlockSpec((None, bm, kp), lambda j, i: (layer, i, 0)),
                  pl.BlockSpec((None, kp, bn), lambda j, i: (layer, 0, j)),
                  pl.BlockSpec((bm, bn), lambda j, i: (i, j)),
                  pl.BlockSpec((1, bn), lambda j, i: (0, j))],
        out_specs=[pl.BlockSpec((bm, bn), lambda j, i: (i, j)),
                   pl.BlockSpec((bm, bn), lambda j, i: (i, j)),
                   pl.BlockSpec((None, 1, bm), lambda j, i: (j, 0, i))],
        out_shape=[jax.ShapeDtypeStruct((m, n), F32), jax.ShapeDtypeStruct((m, n), BF16),
                   jax.ShapeDtypeStruct((n // bn, 1, m), F32)],
        scratch_shapes=[pltpu.VMEM(s, d) for s, d in scratch],
        input_output_aliases={5: 0},
        compiler_params=_cparams(2, blocks, scratch, temps=4 * bm * bn * 4 + 2 * CAST_ROWS * bn * 4),
        name="ple",
    )(hn, ssq, w_gate, p, w_proj, h, g_next.reshape(1, n))


def _dtprep_kernel(u_ref, ssq_ref, w_ref, bias_ref, alog_ref, dtrow_ref, wrow_ref, acsrow_ref, acscol_ref,
                   *, nc, groups, r_heads):
    x = lax.dot_general(u_ref[...], w_ref[...].astype(BF16), (((1,), (1,)), ((), ())),
                        preferred_element_type=F32)
    x = x * _row_rsqrt(ssq_ref, u_ref.shape[1]) + bias_ref[...]
    dt = jnp.maximum(x, 0.0) + jnp.log1p(jnp.exp(-jnp.abs(x)))
    adt = dt * (-jnp.exp(alog_ref[...]))
    row = lax.broadcasted_iota(jnp.int32, (CHUNK, LANES), 0)
    for k in range(nc):
        sl = slice(k * CHUNK, (k + 1) * CHUNK)
        acs = adt[sl]
        sh = 1
        while sh < CHUNK:
            acs = acs + jnp.where(row >= sh, pltpu.roll(acs, sh, 0), 0.0)
            sh *= 2
        acs2 = acs * LOG2E
        acsrow_ref[k] = acs2.T
        dtrow_ref[k] = dt[sl].T
        wrow_ref[k] = (dt[sl] * jnp.exp(acs[CHUNK - 1:CHUNK, :] - acs)).T
        for g in range(groups):
            shift = (LANES - g * r_heads) % LANES
            acscol_ref[g, sl, :] = pltpu.roll(acs2, shift, 1) if shift else acs2


def _dtprep(u, ssq, w_in_t, layer, dt0, bias, a_log, groups, r_heads):
    m, k = u.shape
    parts = ssq.shape[0]
    nc = 4 if m % (4 * CHUNK) == 0 else 1
    ts = nc * CHUNK
    nchunks = m // CHUNK
    rows = ((nc, LANES, LANES), F32)
    blocks = [((ts, k), BF16), ((parts, 1, ts), F32), ((LANES, k), F32), rows, rows, rows,
              ((groups, ts, LANES), F32)]
    row_spec = pl.BlockSpec((nc, LANES, LANES), lambda i: (i, 0, 0))
    row_shape = jax.ShapeDtypeStruct((nchunks, LANES, LANES), F32)
    return pl.pallas_call(
        functools.partial(_dtprep_kernel, nc=nc, groups=groups, r_heads=r_heads),
        grid=(m // ts,),
        in_specs=[pl.BlockSpec((ts, k), lambda i: (i, 0)),
                  pl.BlockSpec((parts, 1, ts), lambda i: (0, 0, i)),
                  pl.BlockSpec((None, LANES, k), lambda i: (layer, dt0 // LANES, 0)),
                  pl.BlockSpec((1, LANES), lambda i: (0, 0)),
                  pl.BlockSpec((1, LANES), lambda i: (0, 0))],
        out_specs=[row_spec, row_spec, row_spec,
                   pl.BlockSpec((groups, ts, LANES), lambda i: (0, i, 0))],
        out_shape=[row_shape, row_shape, row_shape,
                   jax.ShapeDtypeStruct((groups, m, LANES), F32)],
        compiler_params=_cparams(1, blocks, temps=8 * ts * LANES * 4 + k * LANES * 2),
        name="dt_prep",
    )(u, ssq, w_in_t, bias, a_log)


def _ssd_kernel(z_ref, xs_ref, b_ref, c_ref, xsp_ref, bp_ref, cp_ref,
                dtr_ref, wr_ref, acr_ref, acc_ref, shift_ref,
                cwx_ref, cwb_ref, cwc_ref, cbx_ref, cbb_ref, cbc_ref, dsk_ref, nw_ref,
                o_ref, st_ref, yt_ref, *, nc, r_heads, kconv):
    g = pl.program_id(1)
    c = pl.program_id(2)
    L = CHUNK
    N = SSD_STATE
    gw = r_heads * HEAD_DIM
    npair = gw // LANES

    @pl.when(c == 0)
    def _():
        st_ref[...] = jnp.zeros(st_ref.shape, F32)

    tri = (lax.broadcasted_iota(jnp.int32, (L, L), 0) >= lax.broadcasted_iota(jnp.int32, (L, L), 1))
    lo = lax.broadcasted_iota(jnp.int32, (L, LANES), 1) < HEAD_DIM
    hi = jnp.logical_not(lo)
    lo_row = lo[0:1]

    def two_chunks(ref, prev_ref, k):
        if k == 0:
            prev = prev_ref[...]
            prev = jnp.where(c > 0, prev, jnp.zeros_like(prev))
            return jnp.concatenate([prev, ref[0:L, :]], axis=0)
        return ref[(k - 1) * L:(k + 1) * L, :]

    def conv_silu(shifted, cur, w_ref, bias_ref):
        acc = None
        for tap in range(kconv - 1):
            term = shifted[tap * L:(tap + 1) * L, :] * w_ref[tap:tap + 1, :]
            acc = term if acc is None else acc + term
        acc = acc + cur.astype(F32) * w_ref[kconv - 1:kconv, :]
        acc = acc + bias_ref[...]
        return _silu(acc)

    for k in range(nc):
        r0 = k * L
        x2 = jnp.concatenate([two_chunks(xs_ref, xsp_ref, k), two_chunks(b_ref, bp_ref, k),
                              two_chunks(c_ref, cp_ref, k)], axis=1)
        shifted = jnp.dot(shift_ref[...], x2, preferred_element_type=F32)
        cur = x2[L:2 * L]
        x = conv_silu(shifted[:, 0:gw], cur[:, 0:gw], cwx_ref, cbx_ref)
        bm = conv_silu(shifted[:, gw:gw + N], cur[:, gw:gw + N], cwb_ref, cbb_ref)
        cm = conv_silu(shifted[:, gw + N:gw + 2 * N], cur[:, gw + N:gw + 2 * N], cwc_ref, cbc_ref)
        cb = lax.dot_general(cm.astype(BF16), bm.astype(BF16), (((1,), (1,)), ((), ())),
                             preferred_element_type=F32)
        bt = bm.T
        acol = acc_ref[0, r0:r0 + L, :]
        cdec = jnp.exp2(acol[L - 1:L, :])
        ssq = jnp.zeros((L, 1), F32)
        for j in range(npair):
            cs = slice(j * LANES, (j + 1) * LANES)
            xp = x[:, cs]
            s_prev = st_ref[:, cs]
            s_new = s_prev * jnp.where(lo_row, cdec[:, 2 * j:2 * j + 1], cdec[:, 2 * j + 1:2 * j + 2])
            y = None
            for hh, keep in ((2 * j, lo), (2 * j + 1, hi)):
                head = pl.ds(g * r_heads + hh, 1)
                arow = acr_ref[k, head, :]
                drow = dtr_ref[k, head, :]
                wrow = wr_ref[k, head, :]
                xm = jnp.where(keep, xp, 0.0).astype(BF16)
                sm = jnp.where(keep, s_prev, 0.0).astype(BF16)
                ab = jnp.broadcast_to(acol[:, hh:hh + 1], (L, L))
                dec = jnp.exp2(jnp.where(tri, ab - arow, -jnp.inf))
                mh = cb * dec * drow
                ch = cm * jnp.exp2(ab)
                lhs = jnp.concatenate([mh, ch], axis=1).astype(BF16)
                rhs = jnp.concatenate([xm, sm], axis=0)
                t = jnp.dot(lhs, rhs, preferred_element_type=F32)
                y = t if y is None else y + t
                bth = (bt * wrow).astype(BF16)
                s_new = s_new + jnp.dot(bth, xm, preferred_element_type=F32)
            st_ref[:, cs] = s_new
            yt = y + dsk_ref[:, cs] * xp
            zt = z_ref[r0:r0 + L, cs].astype(F32)
            yt = yt * _silu(zt)
            ssq = ssq + jnp.sum(yt * yt, axis=-1, keepdims=True)
            yt_ref[:, cs] = yt
        rs = lax.rsqrt(ssq / gw + EPS)
        for j in range(npair):
            cs = slice(j * LANES, (j + 1) * LANES)
            o_ref[r0:r0 + L, cs] = (yt_ref[:, cs] * rs * nw_ref[:, cs]).astype(o_ref.dtype)


def _ssd(proj, dtrow, wrow, acsrow, acscol, conv_w, conv_b, d_skip, norm_w, *, batch, seq, inner, groups):
    m = proj.shape[0]
    gw = inner // groups
    r_heads = gw // HEAD_DIM
    assert gw % LANES == 0 and r_heads % 2 == 0 and SSD_STATE == LANES
    kconv = conv_w.shape[0]
    assert kconv - 1 <= CHUNK
    nc = _pick(seq // CHUNK, (16, 8, 4, 2, 1))
    ts = nc * CHUNK
    nt = seq // ts
    gn = groups * SSD_STATE
    xs_blk, b_blk, c_blk = inner // gw, 2 * inner // SSD_STATE, (2 * inner + gn) // SSD_STATE
    wb_blk, wc_blk = inner // SSD_STATE, (inner + gn) // SSD_STATE
    hp = acsrow.shape[1]
    t_idx = jnp.arange(CHUNK)[None, :, None]
    d_idx = (kconv - 1 - jnp.arange(kconv - 1))[:, None, None]
    s_idx = jnp.arange(2 * CHUNK)[None, None, :]
    shift = (s_idx == CHUNK + t_idx - d_idx).astype(BF16).reshape((kconv - 1) * CHUNK, 2 * CHUNK)
    st_shape = (SSD_STATE, gw)
    blocks = [((ts, gw), BF16), ((ts, gw), BF16), ((ts, SSD_STATE), BF16), ((ts, SSD_STATE), BF16),
              ((CHUNK, gw), BF16), ((CHUNK, SSD_STATE), BF16), ((CHUNK, SSD_STATE), BF16),
              ((nc, hp, LANES), F32), ((nc, hp, LANES), F32), ((nc, hp, LANES), F32),
              ((1, ts, LANES), F32), (shift.shape, BF16),
              ((kconv, gw), F32), ((kconv, SSD_STATE), F32), ((kconv, SSD_STATE), F32),
              ((1, gw), F32), ((1, SSD_STATE), F32), ((1, SSD_STATE), F32), ((1, gw), F32), ((1, gw), F32),
              ((ts, gw), BF16)]
    scratch = [(st_shape, F32), ((CHUNK, gw), F32)]
    row = lambda b, g, c: b * nt + c
    prev = lambda b, g, c: jnp.maximum(row(b, g, c) * nc - 1, 0)
    return pl.pallas_call(
        functools.partial(_ssd_kernel, nc=nc, r_heads=r_heads, kconv=kconv),
        grid=(batch, groups, nt),
        in_specs=[
            pl.BlockSpec((ts, gw), lambda b, g, c: (row(b, g, c), g)),
            pl.BlockSpec((ts, gw), lambda b, g, c: (row(b, g, c), xs_blk + g)),
            pl.BlockSpec((ts, SSD_STATE), lambda b, g, c: (row(b, g, c), b_blk + g)),
            pl.BlockSpec((ts, SSD_STATE), lambda b, g, c: (row(b, g, c), c_blk + g)),
            pl.BlockSpec((CHUNK, gw), lambda b, g, c: (prev(b, g, c), xs_blk + g)),
            pl.BlockSpec((CHUNK, SSD_STATE), lambda b, g, c: (prev(b, g, c), b_blk + g)),
            pl.BlockSpec((CHUNK, SSD_STATE), lambda b, g, c: (prev(b, g, c), c_blk + g)),
            pl.BlockSpec((nc, hp, LANES), lambda b, g, c: (row(b, g, c), 0, 0)),
            pl.BlockSpec((nc, hp, LANES), lambda b, g, c: (row(b, g, c), 0, 0)),
            pl.BlockSpec((nc, hp, LANES), lambda b, g, c: (row(b, g, c), 0, 0)),
            pl.BlockSpec((1, ts, LANES), lambda b, g, c: (g, row(b, g, c), 0)),
            pl.BlockSpec(shift.shape, lambda b, g, c: (0, 0)),
            pl.BlockSpec((kconv, gw), lambda b, g, c: (0, g)),
            pl.BlockSpec((kconv, SSD_STATE), lambda b, g, c: (0, wb_blk + g)),
            pl.BlockSpec((kconv, SSD_STATE), lambda b, g, c: (0, wc_blk + g)),
            pl.BlockSpec((1, gw), lambda b, g, c: (0, g)),
            pl.BlockSpec((1, SSD_STATE), lambda b, g, c: (0, wb_blk + g)),
            pl.BlockSpec((1, SSD_STATE), lambda b, g, c: (0, wc_blk + g)),
            pl.BlockSpec((1, gw), lambda b, g, c: (0, g)),
            pl.BlockSpec((1, gw), lambda b, g, c: (0, g)),
        ],
        out_specs=pl.BlockSpec((ts, gw), lambda b, g, c: (row(b, g, c), g)),
        out_shape=jax.ShapeDtypeStruct((m, inner), BF16),
        scratch_shapes=[pltpu.VMEM(s, d) for s, d in scratch],
        compiler_params=_cparams(3, blocks, scratch, temps=8 << 20),
        name="ssd",
    )(proj, proj, proj, proj, proj, proj, proj, dtrow, wrow, acsrow, acscol, shift,
      conv_w, conv_w, conv_w, conv_b, conv_b, conv_b, d_skip, norm_w)


def _sconv_kernel(gb_ref, gc_ref, xt_ref, w_ref, wo_ref, o_ref, pe_ref, ysc_ref, wsc_ref, *, ts, kconv, cw):
    halo = SUBLANES

    @pl.when(jnp.logical_and(pl.program_id(0) == 0, pl.program_id(1) == 0))
    def _():
        _cast_tile(wo_ref, wsc_ref)

    @pl.when(pl.program_id(1) == 0)
    def _():
        pe_ref[0:halo, :] = jnp.zeros((halo, pe_ref.shape[1]), F32)

    pe_ref[halo:halo + ts, :] = gb_ref[...].astype(F32) * xt_ref[...].astype(F32)
    width = pe_ref.shape[1]
    hr = ts // 2
    for r0 in range(0, ts, hr):
        for c0 in range(0, width, cw):
            acc = None
            for k in range(kconv):
                r = halo - (kconv - 1) + k + r0
                term = pe_ref[r:r + hr, c0:c0 + cw] * w_ref[k:k + 1, c0:c0 + cw]
                acc = term if acc is None else acc + term
            ysc_ref[r0:r0 + hr, c0:c0 + cw] = (gc_ref[r0:r0 + hr, c0:c0 + cw].astype(F32) * acc).astype(BF16)
        o_ref[r0:r0 + hr, :] = jnp.dot(ysc_ref[r0:r0 + hr, :], wsc_ref[...], preferred_element_type=F32)
    pe_ref[0:halo, :] = pe_ref[ts:ts + halo, :]


def _sconv(proj, conv_w, w_out, layer, *, batch, seq, width, off_b, off_c, off_x):
    m = proj.shape[0]
    kconv = conv_w.shape[0]
    n = w_out.shape[2]
    ts = _pick(seq, (512, 256, 128))
    nt = seq // ts
    assert off_b % width == 0 and off_c % width == 0 and off_x % width == 0 and kconv - 1 <= SUBLANES
    jb, jc, jx = off_b // width, off_c // width, off_x // width
    cw = _pick(width, (512, 256, 128))
    pe_shape = (SUBLANES + ts, width)
    blocks = [((ts, width), BF16)] * 3 + [((kconv, width), F32), ((ts, n), F32)]
    single = [((width, n), F32)]
    scratch = [(pe_shape, F32), ((ts, width), BF16), ((width, n), BF16)]
    return pl.pallas_call(
        functools.partial(_sconv_kernel, ts=ts, kconv=kconv, cw=cw),
        grid=(batch, nt),
        in_specs=[pl.BlockSpec((ts, width), lambda b, c: (b * nt + c, jb)),
                  pl.BlockSpec((ts, width), lambda b, c: (b * nt + c, jc)),
                  pl.BlockSpec((ts, width), lambda b, c: (b * nt + c, jx)),
                  pl.BlockSpec((kconv, width), lambda b, c: (0, 0)),
                  pl.BlockSpec((None, width, n), lambda b, c: (layer, 0, 0), pipeline_mode=pl.Buffered(1))],
        out_specs=pl.BlockSpec((ts, n), lambda b, c: (b * nt + c, 0)),
        out_shape=jax.ShapeDtypeStruct((m, n), F32),
        scratch_shapes=[pltpu.VMEM(s, d) for s, d in scratch],
        compiler_params=_cparams(2, blocks, scratch, temps=(4 << 20) + ts * n * 4, single=single),
        name="short_conv",
    )(proj, proj, proj, conv_w, w_out)


def kernel(x, p, norm_mix, w_in, ssd_conv_w, ssd_conv_b, ssd_dt_bias, ssd_a_log, ssd_d, ssd_norm,
           ssd_out, sc_conv_w, sc_out, w_o, norm_ffn, w_gate_up, w_down, norm_ple, ple_gate,
           ple_proj, norm_final):
    batch, seq, d = x.shape
    depth = w_in.shape[0]
    m = batch * seq
    heads = ssd_a_log.shape[1]
    inner = heads * HEAD_DIM
    groups = SSD_GROUPS
    r_heads = heads // groups
    xbc = ssd_conv_w.shape[2]
    scw = sc_conv_w.shape[2]
    assert xbc == inner + 2 * groups * SSD_STATE and heads <= LANES
    dt0 = inner + xbc
    off_scb = dt0
    off_scc, off_scx = off_scb + scw, off_scb + 2 * scw
    off_ga = off_scb + 3 * scw
    off_gb = off_ga + d
    hpad = LANES - heads
    p2 = p.reshape(depth, m, p.shape[-1])
    w_in_t = jnp.swapaxes(w_in, 1, 2)

    h = x.reshape(m, d)
    u, u_ssq = _prenorm(h, norm_mix[0])
    for i in range(depth):
        dt_bias = jnp.pad(ssd_dt_bias[i], (0, hpad)).reshape(1, LANES)
        a_log = jnp.pad(ssd_a_log[i], (0, hpad)).reshape(1, LANES)
        d_skip = jnp.repeat(ssd_d[i], HEAD_DIM).reshape(1, inner)

        proj = _inproj(u, u_ssq, w_in_t, i, dt0, heads)
        dtrow, wrow, acsrow, acscol = _dtprep(u, u_ssq, w_in_t, i, dt0, dt_bias, a_log, groups, r_heads)
        y = _ssd(proj, dtrow, wrow, acsrow, acscol, ssd_conv_w[i], ssd_conv_b[i].reshape(1, xbc), d_skip,
                 ssd_norm[i].reshape(1, inner), batch=batch, seq=seq, inner=inner, groups=groups)
        yb = _sconv(proj, sc_conv_w[i], sc_out, i, batch=batch, seq=seq, width=scw,
                    off_b=off_scb, off_c=off_scc, off_x=off_scx)
        merged = _merge(y, ssd_out, i, yb, proj, off_ga, off_gb)
        h, v, v_ssq = _matmul_residual(merged, w_o, i, h, norm_ffn[i], "w_o_residual", in_place=i > 0)
        act = _swiglu(v, v_ssq, w_gate_up, i)
        h, hn, hn_ssq = _matmul_residual(act, w_down, i, h, norm_ple[i], "w_down_residual")
        g_next = norm_mix[i + 1] if i + 1 < depth else norm_final
        h, u, u_ssq = _ple(hn, hn_ssq, ple_gate, p2, ple_proj, i, h, g_next)
    out = _rmsnorm(h, norm_final, F32)
    return out.reshape(batch, seq, d)
```

```python
import functools

import jax
import jax.numpy as jnp
from jax import lax
from jax.experimental import pallas as pl
from jax.experimental.pallas import tpu as pltpu

F32 = jnp.float32
BF16 = jnp.bfloat16

EPS = 1e-6
HEAD_DIM = 64
SSD_GROUPS = 8
SSD_STATE = 128
CHUNK = 128
LANES = 128
SUBLANES = 8
CAST_ROWS = 256
LOG2E = 1.4426950408889634
VMEM_BYTES_V7X = 64 * 1024 * 1024
VMEM_CAP = VMEM_BYTES_V7X - 4 * 1024 * 1024


def _nbytes(shape, dtype):
    n = 1
    for s in shape:
        n *= s
    return n * jnp.dtype(dtype).itemsize


def _cparams(ngrid, blocks, scratch=(), temps=0, single=()):
    need = (2 * sum(_nbytes(s, d) for s, d in blocks) + sum(_nbytes(s, d) for s, d in single)
            + sum(_nbytes(s, d) for s, d in scratch) + temps + (4 << 20))
    return pltpu.CompilerParams(dimension_semantics=("arbitrary",) * ngrid,
                                vmem_limit_bytes=int(min(max(need, 16 << 20), VMEM_CAP)))


def _pick(n, candidates):
    for c in candidates:
        if n % c == 0:
            return c
    raise ValueError(f"no block size in {candidates} divides {n}")


def _cast_tile(w_ref, wsc_ref):
    k = w_ref.shape[0]
    rc = _pick(k, (CAST_ROWS, LANES, SUBLANES))
    for r in range(0, k, rc):
        wsc_ref[r:r + rc, :] = w_ref[r:r + rc, :].astype(BF16)


def _first_token_tile():
    return pl.program_id(1) == 0


def _sigmoid(x):
    return 0.5 * jnp.tanh(0.5 * x) + 0.5


def _silu(x):
    h = 0.5 * x
    return h * jnp.tanh(h) + h


def _emit_scaled(h_new, g_ref, hg_ref, ssq_ref):
    hg_ref[...] = (h_new * g_ref[...]).astype(BF16)
    col = jnp.sum(h_new * h_new, axis=-1, keepdims=True)
    ssq_ref[...] = jnp.transpose(jnp.broadcast_to(col, (col.shape[0], LANES)))[0:1, :]


def _row_rsqrt(ssq_ref, d):
    s = ssq_ref[0]
    for q in range(1, ssq_ref.shape[0]):
        s = s + ssq_ref[q]
    row = lax.rsqrt(s / d + EPS)
    return jnp.transpose(jnp.broadcast_to(row, (LANES, row.shape[1])))[:, 0:1]


def _prenorm_kernel(x_ref, g_ref, hg_ref, ssq_ref):
    _emit_scaled(x_ref[...], g_ref, hg_ref, ssq_ref.at[0])


def _prenorm(x, g):
    m, d = x.shape
    bm = _pick(m, (512, 256, 128))
    blocks = [((bm, d), F32), ((1, d), F32), ((bm, d), BF16), ((1, 1, bm), F32)]
    return pl.pallas_call(
        _prenorm_kernel,
        grid=(m // bm,),
        in_specs=[pl.BlockSpec((bm, d), lambda i: (i, 0)),
                  pl.BlockSpec((1, d), lambda i: (0, 0))],
        out_specs=[pl.BlockSpec((bm, d), lambda i: (i, 0)),
                   pl.BlockSpec((1, 1, bm), lambda i: (0, 0, i))],
        out_shape=[jax.ShapeDtypeStruct((m, d), BF16), jax.ShapeDtypeStruct((1, 1, m), F32)],
        compiler_params=_cparams(1, blocks, temps=2 * bm * d * 4),
        name="prenorm",
    )(x, g.reshape(1, d))


def _rmsnorm_kernel(x_ref, g_ref, o_ref):
    x = x_ref[...]
    ms = jnp.mean(x * x, axis=-1, keepdims=True)
    o_ref[...] = (x * lax.rsqrt(ms + EPS) * g_ref[...]).astype(o_ref.dtype)


def _rmsnorm(x, g, out_dtype):
    m, d = x.shape
    bm = _pick(m, (512, 256, 128))
    blocks = [((bm, d), F32), ((1, d), F32), ((bm, d), out_dtype)]
    return pl.pallas_call(
        _rmsnorm_kernel,
        grid=(m // bm,),
        in_specs=[pl.BlockSpec((bm, d), lambda i: (i, 0)),
                  pl.BlockSpec((1, d), lambda i: (0, 0))],
        out_specs=pl.BlockSpec((bm, d), lambda i: (i, 0)),
        out_shape=jax.ShapeDtypeStruct((m, d), out_dtype),
        compiler_params=_cparams(1, blocks, temps=2 * bm * d * 4),
        name="rmsnorm",
    )(x, g.reshape(1, d))


def _inproj_kernel(a_ref, ssq_ref, w_ref, wn_ref, o_ref, wsc_ref, *, n_aligned, skip):
    j = pl.program_id(0)
    bn = w_ref.shape[0]
    rc = _pick(bn, (CAST_ROWS, LANES))

    @pl.when(jnp.logical_and(_first_token_tile(), j < n_aligned))
    def _():
        _cast_tile(w_ref, wsc_ref)

    @pl.when(jnp.logical_and(_first_token_tile(), j >= n_aligned))
    def _():
        for r in range(0, bn - rc, rc):
            wsc_ref[r:r + rc, :] = w_ref[r + skip:r + skip + rc, :].astype(BF16)
        wsc_ref[bn - rc:bn - skip, :] = w_ref[bn - rc + skip:bn, :].astype(BF16)
        wsc_ref[bn - skip:bn, :] = wn_ref[...].astype(BF16)

    acc = lax.dot_general(a_ref[...], wsc_ref[...], (((1,), (1,)), ((), ())), preferred_element_type=F32)
    o_ref[...] = (acc * _row_rsqrt(ssq_ref, a_ref.shape[1])).astype(o_ref.dtype)


def _inproj(u, ssq, w_in_t, layer, dt0, heads):
    m, k = u.shape
    parts = ssq.shape[0]
    d_in = w_in_t.shape[1]
    n = d_in - heads
    bm = _pick(m, (1024, 512, 256, 128))
    bn = next(c for c in (1024, 512, 256, 128) if dt0 % c == 0 and (n - dt0) % c == 0)
    assert heads % (2 * SUBLANES) == 0 and bn % heads == 0 and d_in % heads == 0 and heads < LANES
    n_aligned = dt0 // bn
    blocks = [((bm, k), BF16), ((parts, 1, bm), F32), ((bn, k), F32), ((heads, k), F32), ((bm, bn), BF16)]
    scratch = [((bn, k), BF16)]
    return pl.pallas_call(
        functools.partial(_inproj_kernel, n_aligned=n_aligned, skip=heads),
        grid=(n // bn, m // bm),
        in_specs=[pl.BlockSpec((bm, k), lambda j, i: (i, 0)),
                  pl.BlockSpec((parts, 1, bm), lambda j, i: (0, 0, i)),
                  pl.BlockSpec((None, bn, k), lambda j, i: (layer, j, 0)),
                  pl.BlockSpec((None, heads, k), lambda j, i: (layer, (j + 1) * (bn // heads), 0))],
        out_specs=pl.BlockSpec((bm, bn), lambda j, i: (i, j)),
        out_shape=jax.ShapeDtypeStruct((m, n), BF16),
        scratch_shapes=[pltpu.VMEM(s, d) for s, d in scratch],
        compiler_params=_cparams(2, blocks, scratch, temps=bm * bn * 4 + 2 * CAST_ROWS * k * 4),
        name="in_proj",
    )(u, ssq, w_in_t, w_in_t)


def _mm_res_kernel(a_ref, w_ref, h_ref, g_ref, o_ref, hg_ref, ssq_ref, wsc_ref):
    @pl.when(_first_token_tile())
    def _():
        _cast_tile(w_ref, wsc_ref)

    h_new = h_ref[...] + jnp.dot(a_ref[...], wsc_ref[...], preferred_element_type=F32)
    o_ref[...] = h_new
    _emit_scaled(h_new, g_ref, hg_ref, ssq_ref)


def _matmul_residual(a, w, layer, h, g_next, name, in_place=True):
    m, k = a.shape
    n = w.shape[2]
    big_k = k > 2048
    bm = _pick(m, (512, 256, 128)) if big_k else _pick(m, (1024, 512, 256, 128))
    bn = _pick(n, (1024, 512, 256, 128))
    w_block = ((k, bn), F32)
    blocks = [((bm, k), BF16), ((bm, bn), F32), ((1, bn), F32),
              ((bm, bn), F32), ((bm, bn), BF16), ((1, bm), F32)] + ([] if big_k else [w_block])
    single = [w_block] if big_k else []
    scratch = [((k, bn), BF16)]
    return pl.pallas_call(
        _mm_res_kernel,
        grid=(n // bn, m // bm),
        in_specs=[pl.BlockSpec((bm, k), lambda j, i: (i, 0)),
                  pl.BlockSpec((None, k, bn), lambda j, i: (layer, 0, j),
                               pipeline_mode=pl.Buffered(1) if big_k else None),
                  pl.BlockSpec((bm, bn), lambda j, i: (i, j)),
                  pl.BlockSpec((1, bn), lambda j, i: (0, j))],
        out_specs=[pl.BlockSpec((bm, bn), lambda j, i: (i, j)),
                   pl.BlockSpec((bm, bn), lambda j, i: (i, j)),
                   pl.BlockSpec((None, 1, bm), lambda j, i: (j, 0, i))],
        out_shape=[jax.ShapeDtypeStruct((m, n), F32), jax.ShapeDtypeStruct((m, n), BF16),
                   jax.ShapeDtypeStruct((n // bn, 1, m), F32)],
        scratch_shapes=[pltpu.VMEM(s, d) for s, d in scratch],
        input_output_aliases={2: 0} if in_place else {},
        compiler_params=_cparams(2, blocks, scratch, temps=2 * bm * bn * 4 + 2 * CAST_ROWS * bn * 4,
                                 single=single),
        name=name,
    )(a, w, h, g_next.reshape(1, n))


def _swiglu_kernel(v_ref, ssq_ref, wg_ref, wu_ref, o_ref, wgsc_ref, wusc_ref):
    @pl.when(_first_token_tile())
    def _():
        _cast_tile(wg_ref, wgsc_ref)
        _cast_tile(wu_ref, wusc_ref)

    v = v_ref[...]
    rs = _row_rsqrt(ssq_ref, v.shape[1])
    gate = jnp.dot(v, wgsc_ref[...], preferred_element_type=F32) * rs
    up = jnp.dot(v, wusc_ref[...], preferred_element_type=F32) * rs
    o_ref[...] = (_silu(gate) * up).astype(o_ref.dtype)


def _swiglu(v, ssq, w_gate_up, layer):
    m, k = v.shape
    parts = ssq.shape[0]
    d_ff = w_gate_up.shape[2] // 2
    bm = _pick(m, (1024, 512, 256, 128))
    bn = _pick(d_ff, (512, 256, 128))
    nb = d_ff // bn
    blocks = [((bm, k), BF16), ((parts, 1, bm), F32), ((k, bn), F32), ((k, bn), F32), ((bm, bn), BF16)]
    scratch = [((k, bn), BF16), ((k, bn), BF16)]
    return pl.pallas_call(
        _swiglu_kernel,
        grid=(nb, m // bm),
        in_specs=[pl.BlockSpec((bm, k), lambda j, i: (i, 0)),
                  pl.BlockSpec((parts, 1, bm), lambda j, i: (0, 0, i)),
                  pl.BlockSpec((None, k, bn), lambda j, i: (layer, 0, j)),
                  pl.BlockSpec((None, k, bn), lambda j, i: (layer, 0, j + nb))],
        out_specs=pl.BlockSpec((bm, bn), lambda j, i: (i, j)),
        out_shape=jax.ShapeDtypeStruct((m, d_ff), BF16),
        scratch_shapes=[pltpu.VMEM(s, d) for s, d in scratch],
        compiler_params=_cparams(2, blocks, scratch, temps=3 * bm * bn * 4 + 2 * CAST_ROWS * bn * 4),
        name="swiglu_up",
    )(v, ssq, w_gate_up, w_gate_up)


def _merge_kernel(y_ref, wa_ref, yb_ref, ga_ref, gb_ref, o_ref, wasc_ref):
    @pl.when(_first_token_tile())
    def _():
        _cast_tile(wa_ref, wasc_ref)

    ya = jnp.dot(y_ref[...], wasc_ref[...], preferred_element_type=F32)
    ga = _sigmoid(ga_ref[...].astype(F32))
    gb = _sigmoid(gb_ref[...].astype(F32))
    o_ref[...] = (ga * ya + gb * yb_ref[...].astype(F32)).astype(o_ref.dtype)


def _merge(y, w_a, layer, yb, proj, off_ga, off_gb):
    m, ka = y.shape
    n = w_a.shape[2]
    bm = _pick(m, (512, 256, 128))
    bn = _pick(n, (1024, 512, 256, 128))
    ja, jb = off_ga // bn, off_gb // bn
    assert off_ga % bn == 0 and off_gb % bn == 0
    blocks = [((bm, ka), BF16), ((bm, bn), BF16), ((bm, bn), BF16), ((bm, bn), BF16), ((bm, bn), BF16)]
    single = [((ka, bn), F32)]
    scratch = [((ka, bn), BF16)]
    return pl.pallas_call(
        _merge_kernel,
        grid=(n // bn, m // bm),
        in_specs=[pl.BlockSpec((bm, ka), lambda j, i: (i, 0)),
                  pl.BlockSpec((None, ka, bn), lambda j, i: (layer, 0, j), pipeline_mode=pl.Buffered(1)),
                  pl.BlockSpec((bm, bn), lambda j, i: (i, j)),
                  pl.BlockSpec((bm, bn), lambda j, i: (i, j + ja)),
                  pl.BlockSpec((bm, bn), lambda j, i: (i, j + jb))],
        out_specs=pl.BlockSpec((bm, bn), lambda j, i: (i, j)),
        out_shape=jax.ShapeDtypeStruct((m, n), BF16),
        scratch_shapes=[pltpu.VMEM(s, d) for s, d in scratch],
        compiler_params=_cparams(2, blocks, scratch, temps=4 * bm * bn * 4 + 2 * CAST_ROWS * bn * 4,
                                 single=single),
        name="branch_merge",
    )(y, w_a, yb, proj, proj)


def _ple_kernel(hn_ref, ssq_ref, wg_ref, p_ref, wp_ref, h_ref, g_ref, o_ref, hg_ref, ssqo_ref,
                wgsc_ref, wpsc_ref):
    @pl.when(_first_token_tile())
    def _():
        _cast_tile(wg_ref, wgsc_ref)
        _cast_tile(wp_ref, wpsc_ref)

    rs = _row_rsqrt(ssq_ref, hn_ref.shape[1])
    pg = _sigmoid(jnp.dot(hn_ref[...], wgsc_ref[...], preferred_element_type=F32) * rs)
    e = jnp.dot(p_ref[...].astype(BF16), wpsc_ref[...], preferred_element_type=F32)
    h_new = h_ref[...] + pg * e
    o_ref[...] = h_new
    _emit_scaled(h_new, g_ref, hg_ref, ssqo_ref)


def _ple(hn, ssq, w_gate, p, w_proj, layer, h, g_next):
    m, k = hn.shape
    parts = ssq.shape[0]
    kp = p.shape[2]
    n = w_gate.shape[2]
    bm = _pick(m, (1024, 512, 256, 128))
    bn = _pick(n, (1024, 512, 256, 128))
    blocks = [((bm, k), BF16), ((parts, 1, bm), F32), ((k, bn), F32), ((bm, kp), F32), ((kp, bn), F32),
              ((bm, bn), F32), ((1, bn), F32), ((bm, bn), F32), ((bm, bn), BF16), ((1, bm), F32)]
    scratch = [((k, bn), BF16), ((kp, bn), BF16)]
    return pl.pallas_call(
        _ple_kernel,
        grid=(n // bn, m // bm),
        in_specs=[pl.BlockSpec((bm, k), lambda j, i: (i, 0)),
                  pl.BlockSpec((parts, 1, bm), lambda j, i: (0, 0, i)),
                  pl.BlockSpec((None, k, bn), lambda j, i: (layer, 0, j)),
                  pl.BlockSpec((None, bm, kp), lambda j, i: (layer, i, 0)),
                  pl.BlockSpec((None, kp, bn), lambda j, i: (layer, 0, j)),
                  pl.BlockSpec((bm, bn), lambda j, i: (i, j)),
                  pl.BlockSpec((1, bn), lambda j, i: (0, j))],
        out_specs=[pl.BlockSpec((bm, bn), lambda j, i: (i, j)),
                   pl.BlockSpec((bm, bn), lambda j, i: (i, j)),
                   pl.BlockSpec((None, 1, bm), lambda j, i: (j, 0, i))],
        out_shape=[jax.ShapeDtypeStruct((m, n), F32), jax.ShapeDtypeStruct((m, n), BF16),
                   jax.ShapeDtypeStruct((n // bn, 1, m), F32)],
        scratch_shapes=[pltpu.VMEM(s, d) for s, d in scratch],
        input_output_aliases={5: 0},
        compiler_params=_cparams(2, blocks, scratch, temps=4 * bm * bn * 4 + 2 * CAST_ROWS * bn * 4),
        name="ple",
    )(hn, ssq, w_gate, p, w_proj, h, g_next.reshape(1, n))


def _dtprep_kernel(u_ref, ssq_ref, w_ref, bias_ref, alog_ref, dtrow_ref, wrow_ref, acsrow_ref, acscol_ref,
                   *, nc, groups, r_heads):
    x = lax.dot_general(u_ref[...], w_ref[...].astype(BF16), (((1,), (1,)), ((), ())),
                        preferred_element_type=F32)
    x = x * _row_rsqrt(ssq_ref, u_ref.shape[1]) + bias_ref[...]
    dt = jnp.maximum(x, 0.0) + jnp.log1p(jnp.exp(-jnp.abs(x)))
    adt = dt * (-jnp.exp(alog_ref[...]))
    row = lax.broadcasted_iota(jnp.int32, (CHUNK, LANES), 0)
    for k in range(nc):
        sl = slice(k * CHUNK, (k + 1) * CHUNK)
        acs = adt[sl]
        sh = 1
        while sh < CHUNK:
            acs = acs + jnp.where(row >= sh, pltpu.roll(acs, sh, 0), 0.0)
            sh *= 2
        acs2 = acs * LOG2E
        acsrow_ref[k] = acs2.T
        dtrow_ref[k] = dt[sl].T
        wrow_ref[k] = (dt[sl] * jnp.exp(acs[CHUNK - 1:CHUNK, :] - acs)).T
        for g in range(groups):
            shift = (LANES - g * r_heads) % LANES
            acscol_ref[g, sl, :] = pltpu.roll(acs2, shift, 1) if shift else acs2


def _dtprep(u, ssq, w_in_t, layer, dt0, bias, a_log, groups, r_heads):
    m, k = u.shape
    parts = ssq.shape[0]
    nc = 4 if m % (4 * CHUNK) == 0 else 1
    ts = nc * CHUNK
    nchunks = m // CHUNK
    rows = ((nc, LANES, LANES), F32)
    blocks = [((ts, k), BF16), ((parts, 1, ts), F32), ((LANES, k), F32), rows, rows, rows,
              ((groups, ts, LANES), F32)]
    row_spec = pl.BlockSpec((nc, LANES, LANES), lambda i: (i, 0, 0))
    row_shape = jax.ShapeDtypeStruct((nchunks, LANES, LANES), F32)
    return pl.pallas_call(
        functools.partial(_dtprep_kernel, nc=nc, groups=groups, r_heads=r_heads),
        grid=(m // ts,),
        in_specs=[pl.BlockSpec((ts, k), lambda i: (i, 0)),
                  pl.BlockSpec((parts, 1, ts), lambda i: (0, 0, i)),
                  pl.BlockSpec((None, LANES, k), lambda i: (layer, dt0 // LANES, 0)),
                  pl.BlockSpec((1, LANES), lambda i: (0, 0)),
                  pl.BlockSpec((1, LANES), lambda i: (0, 0))],
        out_specs=[row_spec, row_spec, row_spec,
                   pl.BlockSpec((groups, ts, LANES), lambda i: (0, i, 0))],
        out_shape=[row_shape, row_shape, row_shape,
                   jax.ShapeDtypeStruct((groups, m, LANES), F32)],
        compiler_params=_cparams(1, blocks, temps=8 * ts * LANES * 4 + k * LANES * 2),
        name="dt_prep",
    )(u, ssq, w_in_t, bias, a_log)


def _ssd_kernel(z_ref, xs_ref, b_ref, c_ref, xsp_ref, bp_ref, cp_ref,
                dtr_ref, wr_ref, acr_ref, acc_ref, shift_ref,
                cwx_ref, cwb_ref, cwc_ref, cbx_ref, cbb_ref, cbc_ref, dsk_ref, nw_ref,
                o_ref, st_ref, yt_ref, *, nc, r_heads, kconv):
    g = pl.program_id(1)
    c = pl.program_id(2)
    L = CHUNK
    N = SSD_STATE
    gw = r_heads * HEAD_DIM
    npair = gw // LANES

    @pl.when(c == 0)
    def _():
        st_ref[...] = jnp.zeros(st_ref.shape, F32)

    tri = (lax.broadcasted_iota(jnp.int32, (L, L), 0) >= lax.broadcasted_iota(jnp.int32, (L, L), 1))
    lo = lax.broadcasted_iota(jnp.int32, (L, LANES), 1) < HEAD_DIM
    hi = jnp.logical_not(lo)
    lo_row = lo[0:1]

    def two_chunks(ref, prev_ref, k):
        if k == 0:
            prev = prev_ref[...]
            prev = jnp.where(c > 0, prev, jnp.zeros_like(prev))
            return jnp.concatenate([prev, ref[0:L, :]], axis=0)
        return ref[(k - 1) * L:(k + 1) * L, :]

    def conv_silu(shifted, cur, w_ref, bias_ref):
        acc = None
        for tap in range(kconv - 1):
            term = shifted[tap * L:(tap + 1) * L, :] * w_ref[tap:tap + 1, :]
            acc = term if acc is None else acc + term
        acc = acc + cur.astype(F32) * w_ref[kconv - 1:kconv, :]
        acc = acc + bias_ref[...]
        return _silu(acc)

    for k in range(nc):
        r0 = k * L
        x2 = jnp.concatenate([two_chunks(xs_ref, xsp_ref, k), two_chunks(b_ref, bp_ref, k),
                              two_chunks(c_ref, cp_ref, k)], axis=1)
        shifted = jnp.dot(shift_ref[...], x2, preferred_element_type=F32)
        cur = x2[L:2 * L]
        x = conv_silu(shifted[:, 0:gw], cur[:, 0:gw], cwx_ref, cbx_ref)
        bm = conv_silu(shifted[:, gw:gw + N], cur[:, gw:gw + N], cwb_ref, cbb_ref)
        cm = conv_silu(shifted[:, gw + N:gw + 2 * N], cur[:, gw + N:gw + 2 * N], cwc_ref, cbc_ref)
        cb = lax.dot_general(cm.astype(BF16), bm.astype(BF16), (((1,), (1,)), ((), ())),
                             preferred_element_type=F32)
        bt = bm.T
        acol = acc_ref[0, r0:r0 + L, :]
        cdec = jnp.exp2(acol[L - 1:L, :])
        ssq = jnp.zeros((L, 1), F32)
        for j in range(npair):
            cs = slice(j * LANES, (j + 1) * LANES)
            xp = x[:, cs]
            s_prev = st_ref[:, cs]
            s_new = s_prev * jnp.where(lo_row, cdec[:, 2 * j:2 * j + 1], cdec[:, 2 * j + 1:2 * j + 2])
            y = None
            for hh, keep in ((2 * j, lo), (2 * j + 1, hi)):
                head = pl.ds(g * r_heads + hh, 1)
                arow = acr_ref[k, head, :]
                drow = dtr_ref[k, head, :]
                wrow = wr_ref[k, head, :]
                xm = jnp.where(keep, xp, 0.0).astype(BF16)
                sm = jnp.where(keep, s_prev, 0.0).astype(BF16)
                ab = jnp.broadcast_to(acol[:, hh:hh + 1], (L, L))
                dec = jnp.exp2(jnp.where(tri, ab - arow, -jnp.inf))
                mh = cb * dec * drow
                ch = cm * jnp.exp2(ab)
                lhs = jnp.concatenate([mh, ch], axis=1).astype(BF16)
                rhs = jnp.concatenate([xm, sm], axis=0)
                t = jnp.dot(lhs, rhs, preferred_element_type=F32)
                y = t if y is None else y + t
                bth = (bt * wrow).astype(BF16)
                s_new = s_new + jnp.dot(bth, xm, preferred_element_type=F32)
            st_ref[:, cs] = s_new
            yt = y + dsk_ref[:, cs] * xp
            zt = z_ref[r0:r0 + L, cs].astype(F32)
            yt = yt * _silu(zt)
            ssq = ssq + jnp.sum(yt * yt, axis=-1, keepdims=True)
            yt_ref[:, cs] = yt
        rs = lax.rsqrt(ssq / gw + EPS)
        for j in range(npair):
            cs = slice(j * LANES, (j + 1) * LANES)
            o_ref[r0:r0 + L, cs] = (yt_ref[:, cs] * rs * nw_ref[:, cs]).astype(o_ref.dtype)


def _ssd(proj, dtrow, wrow, acsrow, acscol, conv_w, conv_b, d_skip, norm_w, *, batch, seq, inner, groups):
    m = proj.shape[0]
    gw = inner // groups
    r_heads = gw // HEAD_DIM
    assert gw % LANES == 0 and r_heads % 2 == 0 and SSD_STATE == LANES
    kconv = conv_w.shape[0]
    assert kconv - 1 <= CHUNK
    nc = _pick(seq // CHUNK, (16, 8, 4, 2, 1))
    ts = nc * CHUNK
    nt = seq // ts
    gn = groups * SSD_STATE
    xs_blk, b_blk, c_blk = inner // gw, 2 * inner // SSD_STATE, (2 * inner + gn) // SSD_STATE
    wb_blk, wc_blk = inner // SSD_STATE, (inner + gn) // SSD_STATE
    hp = acsrow.shape[1]
    t_idx = jnp.arange(CHUNK)[None, :, None]
    d_idx = (kconv - 1 - jnp.arange(kconv - 1))[:, None, None]
    s_idx = jnp.arange(2 * CHUNK)[None, None, :]
    shift = (s_idx == CHUNK + t_idx - d_idx).astype(BF16).reshape((kconv - 1) * CHUNK, 2 * CHUNK)
    st_shape = (SSD_STATE, gw)
    blocks = [((ts, gw), BF16), ((ts, gw), BF16), ((ts, SSD_STATE), BF16), ((ts, SSD_STATE), BF16),
              ((CHUNK, gw), BF16), ((CHUNK, SSD_STATE), BF16), ((CHUNK, SSD_STATE), BF16),
              ((nc, hp, LANES), F32), ((nc, hp, LANES), F32), ((nc, hp, LANES), F32),
              ((1, ts, LANES), F32), (shift.shape, BF16),
              ((kconv, gw), F32), ((kconv, SSD_STATE), F32), ((kconv, SSD_STATE), F32),
              ((1, gw), F32), ((1, SSD_STATE), F32), ((1, SSD_STATE), F32), ((1, gw), F32), ((1, gw), F32),
              ((ts, gw), BF16)]
    scratch = [(st_shape, F32), ((CHUNK, gw), F32)]
    row = lambda b, g, c: b * nt + c
    prev = lambda b, g, c: jnp.maximum(row(b, g, c) * nc - 1, 0)
    return pl.pallas_call(
        functools.partial(_ssd_kernel, nc=nc, r_heads=r_heads, kconv=kconv),
        grid=(batch, groups, nt),
        in_specs=[
            pl.BlockSpec((ts, gw), lambda b, g, c: (row(b, g, c), g)),
            pl.BlockSpec((ts, gw), lambda b, g, c: (row(b, g, c), xs_blk + g)),
            pl.BlockSpec((ts, SSD_STATE), lambda b, g, c: (row(b, g, c), b_blk + g)),
            pl.BlockSpec((ts, SSD_STATE), lambda b, g, c: (row(b, g, c), c_blk + g)),
            pl.BlockSpec((CHUNK, gw), lambda b, g, c: (prev(b, g, c), xs_blk + g)),
            pl.BlockSpec((CHUNK, SSD_STATE), lambda b, g, c: (prev(b, g, c), b_blk + g)),
            pl.BlockSpec((CHUNK, SSD_STATE), lambda b, g, c: (prev(b, g, c), c_blk + g)),
            pl.BlockSpec((nc, hp, LANES), lambda b, g, c: (row(b, g, c), 0, 0)),
            pl.BlockSpec((nc, hp, LANES), lambda b, g, c: (row(b, g, c), 0, 0)),
            pl.BlockSpec((nc, hp, LANES), lambda b, g, c: (row(b, g, c), 0, 0)),
            pl.BlockSpec((1, ts, LANES), lambda b, g, c: (g, row(b, g, c), 0)),
            pl.BlockSpec(shift.shape, lambda b, g, c: (0, 0)),
            pl.BlockSpec((kconv, gw), lambda b, g, c: (0, g)),
            pl.BlockSpec((kconv, SSD_STATE), lambda b, g, c: (0, wb_blk + g)),
            pl.BlockSpec((kconv, SSD_STATE), lambda b, g, c: (0, wc_blk + g)),
            pl.BlockSpec((1, gw), lambda b, g, c: (0, g)),
            pl.BlockSpec((1, SSD_STATE), lambda b, g, c: (0, wb_blk + g)),
            pl.BlockSpec((1, SSD_STATE), lambda b, g, c: (0, wc_blk + g)),
            pl.BlockSpec((1, gw), lambda b, g, c: (0, g)),
            pl.BlockSpec((1, gw), lambda b, g, c: (0, g)),
        ],
        out_specs=pl.BlockSpec((ts, gw), lambda b, g, c: (row(b, g, c), g)),
        out_shape=jax.ShapeDtypeStruct((m, inner), BF16),
        scratch_shapes=[pltpu.VMEM(s, d) for s, d in scratch],
        compiler_params=_cparams(3, blocks, scratch, temps=8 << 20),
        name="ssd",
    )(proj, proj, proj, proj, proj, proj, proj, dtrow, wrow, acsrow, acscol, shift,
      conv_w, conv_w, conv_w, conv_b, conv_b, conv_b, d_skip, norm_w)


def _sconv_kernel(gb_ref, gc_ref, xt_ref, w_ref, wo_ref, o_ref, pe_ref, ysc_ref, wsc_ref, *, ts, kconv, cw):
    halo = SUBLANES

    @pl.when(jnp.logical_and(pl.program_id(0) == 0, pl.program_id(1) == 0))
    def _():
        _cast_tile(wo_ref, wsc_ref)

    @pl.when(pl.program_id(1) == 0)
    def _():
        pe_ref[0:halo, :] = jnp.zeros((halo, pe_ref.shape[1]), F32)

    pe_ref[halo:halo + ts, :] = gb_ref[...].astype(F32) * xt_ref[...].astype(F32)
    width = pe_ref.shape[1]
    hr = ts // 2
    for r0 in range(0, ts, hr):
        for c0 in range(0, width, cw):
            acc = None
            for k in range(kconv):
                r = halo - (kconv - 1) + k + r0
                term = pe_ref[r:r + hr, c0:c0 + cw] * w_ref[k:k + 1, c0:c0 + cw]
                acc = term if acc is None else acc + term
            ysc_ref[r0:r0 + hr, c0:c0 + cw] = (gc_ref[r0:r0 + hr, c0:c0 + cw].astype(F32) * acc).astype(BF16)
        o_ref[r0:r0 + hr, :] = jnp.dot(ysc_ref[r0:r0 + hr, :], wsc_ref[...],
                                       preferred_element_type=F32).astype(o_ref.dtype)
    pe_ref[0:halo, :] = pe_ref[ts:ts + halo, :]


def _sconv(proj, conv_w, w_out, layer, *, batch, seq, width, off_b, off_c, off_x):
    m = proj.shape[0]
    kconv = conv_w.shape[0]
    n = w_out.shape[2]
    ts = _pick(seq, (512, 256, 128))
    nt = seq // ts
    assert off_b % width == 0 and off_c % width == 0 and off_x % width == 0 and kconv - 1 <= SUBLANES
    jb, jc, jx = off_b // width, off_c // width, off_x // width
    cw = _pick(width, (512, 256, 128))
    pe_shape = (SUBLANES + ts, width)
    blocks = [((ts, width), BF16)] * 3 + [((kconv, width), F32), ((ts, n), BF16)]
    single = [((width, n), F32)]
    scratch = [(pe_shape, F32), ((ts, width), BF16), ((width, n), BF16)]
    return pl.pallas_call(
        functools.partial(_sconv_kernel, ts=ts, kconv=kconv, cw=cw),
        grid=(batch, nt),
        in_specs=[pl.BlockSpec((ts, width), lambda b, c: (b * nt + c, jb)),
                  pl.BlockSpec((ts, width), lambda b, c: (b * nt + c, jc)),
                  pl.BlockSpec((ts, width), lambda b, c: (b * nt + c, jx)),
                  pl.BlockSpec((kconv, width), lambda b, c: (0, 0)),
                  pl.BlockSpec((None, width, n), lambda b, c: (layer, 0, 0), pipeline_mode=pl.Buffered(1))],
        out_specs=pl.BlockSpec((ts, n), lambda b, c: (b * nt + c, 0)),
        out_shape=jax.ShapeDtypeStruct((m, n), BF16),
        scratch_shapes=[pltpu.VMEM(s, d) for s, d in scratch],
        compiler_params=_cparams(2, blocks, scratch, temps=(4 << 20) + ts * n * 4, single=single),
        name="short_conv",
    )(proj, proj, proj, conv_w, w_out)


def kernel(x, p, norm_mix, w_in, ssd_conv_w, ssd_conv_b, ssd_dt_bias, ssd_a_log, ssd_d, ssd_norm,
           ssd_out, sc_conv_w, sc_out, w_o, norm_ffn, w_gate_up, w_down, norm_ple, ple_gate,
           ple_proj, norm_final):
    batch, seq, d = x.shape
    depth = w_in.shape[0]
    m = batch * seq
    heads = ssd_a_log.shape[1]
    inner = heads * HEAD_DIM
    groups = SSD_GROUPS
    r_heads = heads // groups
    xbc = ssd_conv_w.shape[2]
    scw = sc_conv_w.shape[2]
    assert xbc == inner + 2 * groups * SSD_STATE and heads <= LANES
    dt0 = inner + xbc
    off_scb = dt0
    off_scc, off_scx = off_scb + scw, off_scb + 2 * scw
    off_ga = off_scb + 3 * scw
    off_gb = off_ga + d
    hpad = LANES - heads
    p2 = p.reshape(depth, m, p.shape[-1])
    w_in_t = jnp.swapaxes(w_in, 1, 2)

    h = x.reshape(m, d)
    u, u_ssq = _prenorm(h, norm_mix[0])
    for i in range(depth):
        dt_bias = jnp.pad(ssd_dt_bias[i], (0, hpad)).reshape(1, LANES)
        a_log = jnp.pad(ssd_a_log[i], (0, hpad)).reshape(1, LANES)
        d_skip = jnp.repeat(ssd_d[i], HEAD_DIM).reshape(1, inner)

        proj = _inproj(u, u_ssq, w_in_t, i, dt0, heads)
        dtrow, wrow, acsrow, acscol = _dtprep(u, u_ssq, w_in_t, i, dt0, dt_bias, a_log, groups, r_heads)
        y = _ssd(proj, dtrow, wrow, acsrow, acscol, ssd_conv_w[i], ssd_conv_b[i].reshape(1, xbc), d_skip,
                 ssd_norm[i].reshape(1, inner), batch=batch, seq=seq, inner=inner, groups=groups)
        yb = _sconv(proj, sc_conv_w[i], sc_out, i, batch=batch, seq=seq, width=scw,
                    off_b=off_scb, off_c=off_scc, off_x=off_scx)
        merged = _merge(y, ssd_out, i, yb, proj, off_ga, off_gb)
        h, v, v_ssq = _matmul_residual(merged, w_o, i, h, norm_ffn[i], "w_o_residual", in_place=i > 0)
        act = _swiglu(v, v_ssq, w_gate_up, i)
        h, hn, hn_ssq = _matmul_residual(act, w_down, i, h, norm_ple[i], "w_down_residual")
        g_next = norm_mix[i + 1] if i + 1 < depth else norm_final
        h, u, u_ssq = _ple(hn, hn_ssq, ple_gate, p2, ple_proj, i, h, g_next)
    out = _rmsnorm(h, norm_final, F32)
    return out.reshape(batch, seq, d)
```

```python
import functools

import jax
import jax.numpy as jnp
from jax import lax
from jax.experimental import pallas as pl
from jax.experimental.pallas import tpu as pltpu

F32 = jnp.float32
BF16 = jnp.bfloat16

EPS = 1e-6
HEAD_DIM = 64
SSD_GROUPS = 8
SSD_STATE = 128
CHUNK = 128
LANES = 128
SUBLANES = 8
CAST_ROWS = 256
LOG2E = 1.4426950408889634
VMEM_BYTES_V7X = 64 * 1024 * 1024
VMEM_CAP = VMEM_BYTES_V7X - 4 * 1024 * 1024


def _nbytes(shape, dtype):
    n = 1
    for s in shape:
        n *= s
    return n * jnp.dtype(dtype).itemsize


def _cparams(ngrid, blocks, scratch=(), temps=0, single=()):
    need = (2 * sum(_nbytes(s, d) for s, d in blocks) + sum(_nbytes(s, d) for s, d in single)
            + sum(_nbytes(s, d) for s, d in scratch) + temps + (4 << 20))
    return pltpu.CompilerParams(dimension_semantics=("arbitrary",) * ngrid,
                                vmem_limit_bytes=int(min(max(need, 16 << 20), VMEM_CAP)))


def _pick(n, candidates):
    for c in candidates:
        if n % c == 0:
            return c
    raise ValueError(f"no block size in {candidates} divides {n}")


def _cast_tile(w_ref, wsc_ref):
    k = w_ref.shape[0]
    rc = _pick(k, (CAST_ROWS, LANES, SUBLANES))
    for r in range(0, k, rc):
        wsc_ref[r:r + rc, :] = w_ref[r:r + rc, :].astype(BF16)


def _first_token_tile():
    return pl.program_id(1) == 0


def _sigmoid(x):
    return 0.5 * jnp.tanh(0.5 * x) + 0.5


def _silu(x):
    h = 0.5 * x
    return h * jnp.tanh(h) + h


def _emit_scaled(h_new, g_ref, hg_ref, ssq_ref):
    hg_ref[...] = (h_new * g_ref[...]).astype(BF16)
    col = jnp.sum(h_new * h_new, axis=-1, keepdims=True)
    ssq_ref[...] = jnp.transpose(jnp.broadcast_to(col, (col.shape[0], LANES)))[0:1, :]


def _row_rsqrt(ssq_ref, d):
    s = ssq_ref[0]
    for q in range(1, ssq_ref.shape[0]):
        s = s + ssq_ref[q]
    row = lax.rsqrt(s / d + EPS)
    return jnp.transpose(jnp.broadcast_to(row, (LANES, row.shape[1])))[:, 0:1]


def _prenorm_kernel(x_ref, g_ref, hg_ref, ssq_ref):
    _emit_scaled(x_ref[...], g_ref, hg_ref, ssq_ref.at[0])


def _prenorm(x, g):
    m, d = x.shape
    bm = _pick(m, (512, 256, 128))
    blocks = [((bm, d), F32), ((1, d), F32), ((bm, d), BF16), ((1, 1, bm), F32)]
    return pl.pallas_call(
        _prenorm_kernel,
        grid=(m // bm,),
        in_specs=[pl.BlockSpec((bm, d), lambda i: (i, 0)),
                  pl.BlockSpec((1, d), lambda i: (0, 0))],
        out_specs=[pl.BlockSpec((bm, d), lambda i: (i, 0)),
                   pl.BlockSpec((1, 1, bm), lambda i: (0, 0, i))],
        out_shape=[jax.ShapeDtypeStruct((m, d), BF16), jax.ShapeDtypeStruct((1, 1, m), F32)],
        compiler_params=_cparams(1, blocks, temps=2 * bm * d * 4),
        name="prenorm",
    )(x, g.reshape(1, d))


def _rmsnorm_kernel(x_ref, g_ref, o_ref):
    x = x_ref[...]
    ms = jnp.mean(x * x, axis=-1, keepdims=True)
    o_ref[...] = (x * lax.rsqrt(ms + EPS) * g_ref[...]).astype(o_ref.dtype)


def _rmsnorm(x, g, out_dtype):
    m, d = x.shape
    bm = _pick(m, (512, 256, 128))
    blocks = [((bm, d), F32), ((1, d), F32), ((bm, d), out_dtype)]
    return pl.pallas_call(
        _rmsnorm_kernel,
        grid=(m // bm,),
        in_specs=[pl.BlockSpec((bm, d), lambda i: (i, 0)),
                  pl.BlockSpec((1, d), lambda i: (0, 0))],
        out_specs=pl.BlockSpec((bm, d), lambda i: (i, 0)),
        out_shape=jax.ShapeDtypeStruct((m, d), out_dtype),
        compiler_params=_cparams(1, blocks, temps=2 * bm * d * 4),
        name="rmsnorm",
    )(x, g.reshape(1, d))


def _inproj_kernel(a_ref, ssq_ref, w_ref, wn_ref, o_ref, wsc_ref, *, n_aligned, skip):
    j = pl.program_id(0)
    bn = w_ref.shape[0]
    rc = _pick(bn, (CAST_ROWS, LANES))

    @pl.when(jnp.logical_and(_first_token_tile(), j < n_aligned))
    def _():
        _cast_tile(w_ref, wsc_ref)

    @pl.when(jnp.logical_and(_first_token_tile(), j >= n_aligned))
    def _():
        for r in range(0, bn - rc, rc):
            wsc_ref[r:r + rc, :] = w_ref[r + skip:r + skip + rc, :].astype(BF16)
        wsc_ref[bn - rc:bn - skip, :] = w_ref[bn - rc + skip:bn, :].astype(BF16)
        wsc_ref[bn - skip:bn, :] = wn_ref[...].astype(BF16)

    rs = _row_rsqrt(ssq_ref, a_ref.shape[1])
    hm = a_ref.shape[0] // 2
    for r in (0, hm):
        acc = lax.dot_general(a_ref[r:r + hm, :], wsc_ref[...], (((1,), (1,)), ((), ())),
                              preferred_element_type=F32)
        o_ref[r:r + hm, :] = (acc * rs[r:r + hm]).astype(o_ref.dtype)


def _inproj(u, ssq, w_in_t, layer, dt0, heads):
    m, k = u.shape
    parts = ssq.shape[0]
    d_in = w_in_t.shape[1]
    n = d_in - heads
    bm = _pick(m, (2048, 1024, 512, 256, 128))
    bn = next(c for c in (1024, 512, 256, 128) if dt0 % c == 0 and (n - dt0) % c == 0)
    assert heads % (2 * SUBLANES) == 0 and bn % heads == 0 and d_in % heads == 0 and heads < LANES
    n_aligned = dt0 // bn
    blocks = [((bm, k), BF16), ((parts, 1, bm), F32), ((bn, k), F32), ((heads, k), F32), ((bm, bn), BF16)]
    scratch = [((bn, k), BF16)]
    return pl.pallas_call(
        functools.partial(_inproj_kernel, n_aligned=n_aligned, skip=heads),
        grid=(n // bn, m // bm),
        in_specs=[pl.BlockSpec((bm, k), lambda j, i: (i, 0)),
                  pl.BlockSpec((parts, 1, bm), lambda j, i: (0, 0, i)),
                  pl.BlockSpec((None, bn, k), lambda j, i: (layer, j, 0)),
                  pl.BlockSpec((None, heads, k), lambda j, i: (layer, (j + 1) * (bn // heads), 0))],
        out_specs=pl.BlockSpec((bm, bn), lambda j, i: (i, j)),
        out_shape=jax.ShapeDtypeStruct((m, n), BF16),
        scratch_shapes=[pltpu.VMEM(s, d) for s, d in scratch],
        compiler_params=_cparams(2, blocks, scratch, temps=bm * bn * 4 + 2 * CAST_ROWS * k * 4),
        name="in_proj",
    )(u, ssq, w_in_t, w_in_t)


def _mm_res_kernel(a_ref, w_ref, h_ref, g_ref, o_ref, hg_ref, ssq_ref, wsc_ref):
    @pl.when(_first_token_tile())
    def _():
        _cast_tile(w_ref, wsc_ref)

    h_new = h_ref[...] + jnp.dot(a_ref[...], wsc_ref[...], preferred_element_type=F32)
    o_ref[...] = h_new
    _emit_scaled(h_new, g_ref, hg_ref, ssq_ref)


def _matmul_residual(a, w, layer, h, g_next, name, in_place=True):
    m, k = a.shape
    n = w.shape[2]
    big_k = k > 2048
    bm = _pick(m, (512, 256, 128)) if big_k else _pick(m, (1024, 512, 256, 128))
    bn = _pick(n, (1024, 512, 256, 128))
    w_block = ((k, bn), F32)
    blocks = [((bm, k), BF16), ((bm, bn), F32), ((1, bn), F32),
              ((bm, bn), F32), ((bm, bn), BF16), ((1, bm), F32)] + ([] if big_k else [w_block])
    single = [w_block] if big_k else []
    scratch = [((k, bn), BF16)]
    return pl.pallas_call(
        _mm_res_kernel,
        grid=(n // bn, m // bm),
        in_specs=[pl.BlockSpec((bm, k), lambda j, i: (i, 0)),
                  pl.BlockSpec((None, k, bn), lambda j, i: (layer, 0, j),
                               pipeline_mode=pl.Buffered(1) if big_k else None),
                  pl.BlockSpec((bm, bn), lambda j, i: (i, j)),
                  pl.BlockSpec((1, bn), lambda j, i: (0, j))],
        out_specs=[pl.BlockSpec((bm, bn), lambda j, i: (i, j)),
                   pl.BlockSpec((bm, bn), lambda j, i: (i, j)),
                   pl.BlockSpec((None, 1, bm), lambda j, i: (j, 0, i))],
        out_shape=[jax.ShapeDtypeStruct((m, n), F32), jax.ShapeDtypeStruct((m, n), BF16),
                   jax.ShapeDtypeStruct((n // bn, 1, m), F32)],
        scratch_shapes=[pltpu.VMEM(s, d) for s, d in scratch],
        input_output_aliases={2: 0} if in_place else {},
        compiler_params=_cparams(2, blocks, scratch, temps=2 * bm * bn * 4 + 2 * CAST_ROWS * bn * 4,
                                 single=single),
        name=name,
    )(a, w, h, g_next.reshape(1, n))


def _swiglu_kernel(v_ref, ssq_ref, wg_ref, wu_ref, o_ref, wgsc_ref, wusc_ref):
    @pl.when(_first_token_tile())
    def _():
        _cast_tile(wg_ref, wgsc_ref)
        _cast_tile(wu_ref, wusc_ref)

    v = v_ref[...]
    rs = _row_rsqrt(ssq_ref, v.shape[1])
    gate = jnp.dot(v, wgsc_ref[...], preferred_element_type=F32) * rs
    up = jnp.dot(v, wusc_ref[...], preferred_element_type=F32) * rs
    o_ref[...] = (_silu(gate) * up).astype(o_ref.dtype)


def _swiglu(v, ssq, w_gate_up, layer):
    m, k = v.shape
    parts = ssq.shape[0]
    d_ff = w_gate_up.shape[2] // 2
    bm = _pick(m, (1024, 512, 256, 128))
    bn = _pick(d_ff, (512, 256, 128))
    nb = d_ff // bn
    blocks = [((bm, k), BF16), ((parts, 1, bm), F32), ((k, bn), F32), ((k, bn), F32), ((bm, bn), BF16)]
    scratch = [((k, bn), BF16), ((k, bn), BF16)]
    return pl.pallas_call(
        _swiglu_kernel,
        grid=(nb, m // bm),
        in_specs=[pl.BlockSpec((bm, k), lambda j, i: (i, 0)),
                  pl.BlockSpec((parts, 1, bm), lambda j, i: (0, 0, i)),
                  pl.BlockSpec((None, k, bn), lambda j, i: (layer, 0, j)),
                  pl.BlockSpec((None, k, bn), lambda j, i: (layer, 0, j + nb))],
        out_specs=pl.BlockSpec((bm, bn), lambda j, i: (i, j)),
        out_shape=jax.ShapeDtypeStruct((m, d_ff), BF16),
        scratch_shapes=[pltpu.VMEM(s, d) for s, d in scratch],
        compiler_params=_cparams(2, blocks, scratch, temps=3 * bm * bn * 4 + 2 * CAST_ROWS * bn * 4),
        name="swiglu_up",
    )(v, ssq, w_gate_up, w_gate_up)


def _merge_kernel(y_ref, wa_ref, yb_ref, ga_ref, gb_ref, o_ref, wasc_ref):
    @pl.when(_first_token_tile())
    def _():
        _cast_tile(wa_ref, wasc_ref)

    ya = jnp.dot(y_ref[...], wasc_ref[...], preferred_element_type=F32)
    ga = _sigmoid(ga_ref[...].astype(F32))
    gb = _sigmoid(gb_ref[...].astype(F32))
    o_ref[...] = (ga * ya + gb * yb_ref[...]).astype(o_ref.dtype)


def _merge(y, w_a, layer, yb, proj, off_ga, off_gb):
    m, ka = y.shape
    n = w_a.shape[2]
    bm = _pick(m, (512, 256, 128))
    bn = _pick(n, (1024, 512, 256, 128))
    ja, jb = off_ga // bn, off_gb // bn
    assert off_ga % bn == 0 and off_gb % bn == 0
    blocks = [((bm, ka), BF16), ((bm, bn), F32), ((bm, bn), BF16), ((bm, bn), BF16), ((bm, bn), BF16)]
    single = [((ka, bn), F32)]
    scratch = [((ka, bn), BF16)]
    return pl.pallas_call(
        _merge_kernel,
        grid=(n // bn, m // bm),
        in_specs=[pl.BlockSpec((bm, ka), lambda j, i: (i, 0)),
                  pl.BlockSpec((None, ka, bn), lambda j, i: (layer, 0, j), pipeline_mode=pl.Buffered(1)),
                  pl.BlockSpec((bm, bn), lambda j, i: (i, j)),
                  pl.BlockSpec((bm, bn), lambda j, i: (i, j + ja)),
                  pl.BlockSpec((bm, bn), lambda j, i: (i, j + jb))],
        out_specs=pl.BlockSpec((bm, bn), lambda j, i: (i, j)),
        out_shape=jax.ShapeDtypeStruct((m, n), BF16),
        scratch_shapes=[pltpu.VMEM(s, d) for s, d in scratch],
        compiler_params=_cparams(2, blocks, scratch, temps=4 * bm * bn * 4 + 2 * CAST_ROWS * bn * 4,
                                 single=single),
        name="branch_merge",
    )(y, w_a, yb, proj, proj)


def _ple_kernel(hn_ref, ssq_ref, wg_ref, p_ref, wp_ref, h_ref, g_ref, o_ref, hg_ref, ssqo_ref,
                wgsc_ref, wpsc_ref):
    @pl.when(_first_token_tile())
    def _():
        _cast_tile(wg_ref, wgsc_ref)
        _cast_tile(wp_ref, wpsc_ref)

    rs = _row_rsqrt(ssq_ref, hn_ref.shape[1])
    pg = _sigmoid(jnp.dot(hn_ref[...], wgsc_ref[...], preferred_element_type=F32) * rs)
    e = jnp.dot(p_ref[...].astype(BF16), wpsc_ref[...], preferred_element_type=F32)
    h_new = h_ref[...] + pg * e
    o_ref[...] = h_new
    _emit_scaled(h_new, g_ref, hg_ref, ssqo_ref)


def _ple(hn, ssq, w_gate, p, w_proj, layer, h, g_next):
    m, k = hn.shape
    parts = ssq.shape[0]
    kp = p.shape[2]
    n = w_gate.shape[2]
    bm = _pick(m, (1024, 512, 256, 128))
    bn = _pick(n, (1024, 512, 256, 128))
    blocks = [((bm, k), BF16), ((parts, 1, bm), F32), ((k, bn), F32), ((bm, kp), F32), ((kp, bn), F32),
              ((bm, bn), F32), ((1, bn), F32), ((bm, bn), F32), ((bm, bn), BF16), ((1, bm), F32)]
    scratch = [((k, bn), BF16), ((kp, bn), BF16)]
    return pl.pallas_call(
        _ple_kernel,
        grid=(n // bn, m // bm),
        in_specs=[pl.BlockSpec((bm, k), lambda j, i: (i, 0)),
                  pl.BlockSpec((parts, 1, bm), lambda j, i: (0, 0, i)),
                  pl.BlockSpec((None, k, bn), lambda j, i: (layer, 0, j)),
                  pl.BlockSpec((None, bm, kp), lambda j, i: (layer, i, 0)),
                  pl.BlockSpec((None, kp, bn), lambda j, i: (layer, 0, j)),
                  pl.BlockSpec((bm, bn), lambda j, i: (i, j)),
                  pl.BlockSpec((1, bn), lambda j, i: (0, j))],
        out_specs=[pl.BlockSpec((bm, bn), lambda j, i: (i, j)),
                   pl.BlockSpec((bm, bn), lambda j, i: (i, j)),
                   pl.BlockSpec((None, 1, bm), lambda j, i: (j, 0, i))],
        out_shape=[jax.ShapeDtypeStruct((m, n), F32), jax.ShapeDtypeStruct((m, n), BF16),
                   jax.ShapeDtypeStruct((n // bn, 1, m), F32)],
        scratch_shapes=[pltpu.VMEM(s, d) for s, d in scratch],
        input_output_aliases={5: 0},
        compiler_params=_cparams(2, blocks, scratch, temps=4 * bm * bn * 4 + 2 * CAST_ROWS * bn * 4),
        name="ple",
    )(hn, ssq, w_gate, p, w_proj, h, g_next.reshape(1, n))


def _dtprep_kernel(u_ref, ssq_ref, w_ref, bias_ref, alog_ref, dtrow_ref, wrow_ref, acsrow_ref, acscol_ref,
                   *, nc, groups, r_heads):
    x = lax.dot_general(u_ref[...], w_ref[...].astype(BF16), (((1,), (1,)), ((), ())),
                        preferred_element_type=F32)
    x = x * _row_rsqrt(ssq_ref, u_ref.shape[1]) + bias_ref[...]
    dt = jnp.maximum(x, 0.0) + jnp.log1p(jnp.exp(-jnp.abs(x)))
    adt = dt * (-jnp.exp(alog_ref[...]))
    row = lax.broadcasted_iota(jnp.int32, (CHUNK, LANES), 0)
    for k in range(nc):
        sl = slice(k * CHUNK, (k + 1) * CHUNK)
        acs = adt[sl]
        sh = 1
        while sh < CHUNK:
            acs = acs + jnp.where(row >= sh, pltpu.roll(acs, sh, 0), 0.0)
            sh *= 2
        acs2 = acs * LOG2E
        acsrow_ref[k] = acs2.T
        dtrow_ref[k] = dt[sl].T
        wrow_ref[k] = (dt[sl] * jnp.exp(acs[CHUNK - 1:CHUNK, :] - acs)).T
        for g in range(groups):
            shift = (LANES - g * r_heads) % LANES
            acscol_ref[g, sl, :] = pltpu.roll(acs2, shift, 1) if shift else acs2


def _dtprep(u, ssq, w_in_t, layer, dt0, bias, a_log, groups, r_heads):
    m, k = u.shape
    parts = ssq.shape[0]
    nc = 4 if m % (4 * CHUNK) == 0 else 1
    ts = nc * CHUNK
    nchunks = m // CHUNK
    rows = ((nc, LANES, LANES), F32)
    blocks = [((ts, k), BF16), ((parts, 1, ts), F32), ((LANES, k), F32), rows, rows, rows,
              ((groups, ts, LANES), F32)]
    row_spec = pl.BlockSpec((nc, LANES, LANES), lambda i: (i, 0, 0))
    row_shape = jax.ShapeDtypeStruct((nchunks, LANES, LANES), F32)
    return pl.pallas_call(
        functools.partial(_dtprep_kernel, nc=nc, groups=groups, r_heads=r_heads),
        grid=(m // ts,),
        in_specs=[pl.BlockSpec((ts, k), lambda i: (i, 0)),
                  pl.BlockSpec((parts, 1, ts), lambda i: (0, 0, i)),
                  pl.BlockSpec((None, LANES, k), lambda i: (layer, dt0 // LANES, 0)),
                  pl.BlockSpec((1, LANES), lambda i: (0, 0)),
                  pl.BlockSpec((1, LANES), lambda i: (0, 0))],
        out_specs=[row_spec, row_spec, row_spec,
                   pl.BlockSpec((groups, ts, LANES), lambda i: (0, i, 0))],
        out_shape=[row_shape, row_shape, row_shape,
                   jax.ShapeDtypeStruct((groups, m, LANES), F32)],
        compiler_params=_cparams(1, blocks, temps=8 * ts * LANES * 4 + k * LANES * 2),
        name="dt_prep",
    )(u, ssq, w_in_t, bias, a_log)


def _ssd_kernel(z_ref, xs_ref, b_ref, c_ref, xsp_ref, bp_ref, cp_ref,
                dtr_ref, wr_ref, acr_ref, acc_ref, shift_ref,
                cwx_ref, cwb_ref, cwc_ref, cbx_ref, cbb_ref, cbc_ref, dsk_ref, nw_ref,
                o_ref, st_ref, yt_ref, *, nc, r_heads, kconv):
    g = pl.program_id(1)
    c = pl.program_id(2)
    L = CHUNK
    N = SSD_STATE
    gw = r_heads * HEAD_DIM
    npair = gw // LANES

    @pl.when(c == 0)
    def _():
        st_ref[...] = jnp.zeros(st_ref.shape, F32)

    tri = (lax.broadcasted_iota(jnp.int32, (L, L), 0) >= lax.broadcasted_iota(jnp.int32, (L, L), 1))
    lo = lax.broadcasted_iota(jnp.int32, (L, LANES), 1) < HEAD_DIM
    hi = jnp.logical_not(lo)
    lo_row = lo[0:1]

    def two_chunks(ref, prev_ref, k):
        if k == 0:
            prev = prev_ref[...]
            prev = jnp.where(c > 0, prev, jnp.zeros_like(prev))
            return jnp.concatenate([prev, ref[0:L, :]], axis=0)
        return ref[(k - 1) * L:(k + 1) * L, :]

    def conv_silu(shifted, cur, w_ref, bias_ref):
        acc = None
        for tap in range(kconv - 1):
            term = shifted[tap * L:(tap + 1) * L, :] * w_ref[tap:tap + 1, :]
            acc = term if acc is None else acc + term
        acc = acc + cur.astype(F32) * w_ref[kconv - 1:kconv, :]
        acc = acc + bias_ref[...]
        return _silu(acc)

    for k in range(nc):
        r0 = k * L
        x2 = jnp.concatenate([two_chunks(xs_ref, xsp_ref, k), two_chunks(b_ref, bp_ref, k),
                              two_chunks(c_ref, cp_ref, k)], axis=1)
        shifted = jnp.dot(shift_ref[...], x2, preferred_element_type=F32)
        cur = x2[L:2 * L]
        x = conv_silu(shifted[:, 0:gw], cur[:, 0:gw], cwx_ref, cbx_ref)
        bm = conv_silu(shifted[:, gw:gw + N], cur[:, gw:gw + N], cwb_ref, cbb_ref)
        cm = conv_silu(shifted[:, gw + N:gw + 2 * N], cur[:, gw + N:gw + 2 * N], cwc_ref, cbc_ref)
        cb = lax.dot_general(cm.astype(BF16), bm.astype(BF16), (((1,), (1,)), ((), ())),
                             preferred_element_type=F32)
        bt = bm.T
        acol = acc_ref[0, r0:r0 + L, :]
        cdec = jnp.exp2(acol[L - 1:L, :])
        ssq = jnp.zeros((L, 1), F32)
        for j in range(npair):
            cs = slice(j * LANES, (j + 1) * LANES)
            xp = x[:, cs]
            s_prev = st_ref[:, cs]
            s_new = s_prev * jnp.where(lo_row, cdec[:, 2 * j:2 * j + 1], cdec[:, 2 * j + 1:2 * j + 2])
            y = None
            for hh, keep in ((2 * j, lo), (2 * j + 1, hi)):
                head = pl.ds(g * r_heads + hh, 1)
                arow = acr_ref[k, head, :]
                drow = dtr_ref[k, head, :]
                wrow = wr_ref[k, head, :]
                xm = jnp.where(keep, xp, 0.0).astype(BF16)
                sm = jnp.where(keep, s_prev, 0.0).astype(BF16)
                ab = jnp.broadcast_to(acol[:, hh:hh + 1], (L, L))
                dec = jnp.exp2(jnp.where(tri, ab - arow, -jnp.inf))
                mh = cb * dec * drow
                ch = cm * jnp.exp2(ab)
                lhs = jnp.concatenate([mh, ch], axis=1).astype(BF16)
                rhs = jnp.concatenate([xm, sm], axis=0)
                t = jnp.dot(lhs, rhs, preferred_element_type=F32)
                y = t if y is None else y + t
                bth = (bt * wrow).astype(BF16)
                s_new = s_new + jnp.dot(bth, xm, preferred_element_type=F32)
            st_ref[:, cs] = s_new
            yt = y + dsk_ref[:, cs] * xp
            zt = z_ref[r0:r0 + L, cs].astype(F32)
            yt = yt * _silu(zt)
            ssq = ssq + jnp.sum(yt * yt, axis=-1, keepdims=True)
            yt_ref[:, cs] = yt
        rs = lax.rsqrt(ssq / gw + EPS)
        for j in range(npair):
            cs = slice(j * LANES, (j + 1) * LANES)
            o_ref[r0:r0 + L, cs] = (yt_ref[:, cs] * rs * nw_ref[:, cs]).astype(o_ref.dtype)


def _ssd(proj, dtrow, wrow, acsrow, acscol, conv_w, conv_b, d_skip, norm_w, *, batch, seq, inner, groups):
    m = proj.shape[0]
    gw = inner // groups
    r_heads = gw // HEAD_DIM
    assert gw % LANES == 0 and r_heads % 2 == 0 and SSD_STATE == LANES
    kconv = conv_w.shape[0]
    assert kconv - 1 <= CHUNK
    nc = _pick(seq // CHUNK, (16, 8, 4, 2, 1))
    ts = nc * CHUNK
    nt = seq // ts
    gn = groups * SSD_STATE
    xs_blk, b_blk, c_blk = inner // gw, 2 * inner // SSD_STATE, (2 * inner + gn) // SSD_STATE
    wb_blk, wc_blk = inner // SSD_STATE, (inner + gn) // SSD_STATE
    hp = acsrow.shape[1]
    t_idx = jnp.arange(CHUNK)[None, :, None]
    d_idx = (kconv - 1 - jnp.arange(kconv - 1))[:, None, None]
    s_idx = jnp.arange(2 * CHUNK)[None, None, :]
    shift = (s_idx == CHUNK + t_idx - d_idx).astype(BF16).reshape((kconv - 1) * CHUNK, 2 * CHUNK)
    st_shape = (SSD_STATE, gw)
    blocks = [((ts, gw), BF16), ((ts, gw), BF16), ((ts, SSD_STATE), BF16), ((ts, SSD_STATE), BF16),
              ((CHUNK, gw), BF16), ((CHUNK, SSD_STATE), BF16), ((CHUNK, SSD_STATE), BF16),
              ((nc, hp, LANES), F32), ((nc, hp, LANES), F32), ((nc, hp, LANES), F32),
              ((1, ts, LANES), F32), (shift.shape, BF16),
              ((kconv, gw), F32), ((kconv, SSD_STATE), F32), ((kconv, SSD_STATE), F32),
              ((1, gw), F32), ((1, SSD_STATE), F32), ((1, SSD_STATE), F32), ((1, gw), F32), ((1, gw), F32),
              ((ts, gw), BF16)]
    scratch = [(st_shape, F32), ((CHUNK, gw), F32)]
    row = lambda b, g, c: b * nt + c
    prev = lambda b, g, c: jnp.maximum(row(b, g, c) * nc - 1, 0)
    return pl.pallas_call(
        functools.partial(_ssd_kernel, nc=nc, r_heads=r_heads, kconv=kconv),
        grid=(batch, groups, nt),
        in_specs=[
            pl.BlockSpec((ts, gw), lambda b, g, c: (row(b, g, c), g)),
            pl.BlockSpec((ts, gw), lambda b, g, c: (row(b, g, c), xs_blk + g)),
            pl.BlockSpec((ts, SSD_STATE), lambda b, g, c: (row(b, g, c), b_blk + g)),
            pl.BlockSpec((ts, SSD_STATE), lambda b, g, c: (row(b, g, c), c_blk + g)),
            pl.BlockSpec((CHUNK, gw), lambda b, g, c: (prev(b, g, c), xs_blk + g)),
            pl.BlockSpec((CHUNK, SSD_STATE), lambda b, g, c: (prev(b, g, c), b_blk + g)),
            pl.BlockSpec((CHUNK, SSD_STATE), lambda b, g, c: (prev(b, g, c), c_blk + g)),
            pl.BlockSpec((nc, hp, LANES), lambda b, g, c: (row(b, g, c), 0, 0)),
            pl.BlockSpec((nc, hp, LANES), lambda b, g, c: (row(b, g, c), 0, 0)),
            pl.BlockSpec((nc, hp, LANES), lambda b, g, c: (row(b, g, c), 0, 0)),
            pl.BlockSpec((1, ts, LANES), lambda b, g, c: (g, row(b, g, c), 0)),
            pl.BlockSpec(shift.shape, lambda b, g, c: (0, 0)),
            pl.BlockSpec((kconv, gw), lambda b, g, c: (0, g)),
            pl.BlockSpec((kconv, SSD_STATE), lambda b, g, c: (0, wb_blk + g)),
            pl.BlockSpec((kconv, SSD_STATE), lambda b, g, c: (0, wc_blk + g)),
            pl.BlockSpec((1, gw), lambda b, g, c: (0, g)),
            pl.BlockSpec((1, SSD_STATE), lambda b, g, c: (0, wb_blk + g)),
            pl.BlockSpec((1, SSD_STATE), lambda b, g, c: (0, wc_blk + g)),
            pl.BlockSpec((1, gw), lambda b, g, c: (0, g)),
            pl.BlockSpec((1, gw), lambda b, g, c: (0, g)),
        ],
        out_specs=pl.BlockSpec((ts, gw), lambda b, g, c: (row(b, g, c), g)),
        out_shape=jax.ShapeDtypeStruct((m, inner), BF16),
        scratch_shapes=[pltpu.VMEM(s, d) for s, d in scratch],
        compiler_params=_cparams(3, blocks, scratch, temps=8 << 20),
        name="ssd",
    )(proj, proj, proj, proj, proj, proj, proj, dtrow, wrow, acsrow, acscol, shift,
      conv_w, conv_w, conv_w, conv_b, conv_b, conv_b, d_skip, norm_w)


def _sconv_kernel(gb_ref, gc_ref, xt_ref, w_ref, wo_ref, o_ref, pe_ref, ysc_ref, wsc_ref, *, ts, kconv, cw):
    halo = SUBLANES

    @pl.when(jnp.logical_and(pl.program_id(0) == 0, pl.program_id(1) == 0))
    def _():
        _cast_tile(wo_ref, wsc_ref)

    @pl.when(pl.program_id(1) == 0)
    def _():
        pe_ref[0:halo, :] = jnp.zeros((halo, pe_ref.shape[1]), F32)

    pe_ref[halo:halo + ts, :] = gb_ref[...].astype(F32) * xt_ref[...].astype(F32)
    width = pe_ref.shape[1]
    hr = ts // 2
    for r0 in range(0, ts, hr):
        for c0 in range(0, width, cw):
            acc = None
            for k in range(kconv):
                r = halo - (kconv - 1) + k + r0
                term = pe_ref[r:r + hr, c0:c0 + cw] * w_ref[k:k + 1, c0:c0 + cw]
                acc = term if acc is None else acc + term
            ysc_ref[r0:r0 + hr, c0:c0 + cw] = (gc_ref[r0:r0 + hr, c0:c0 + cw].astype(F32) * acc).astype(BF16)
        o_ref[r0:r0 + hr, :] = jnp.dot(ysc_ref[r0:r0 + hr, :], wsc_ref[...], preferred_element_type=F32)
    pe_ref[0:halo, :] = pe_ref[ts:ts + halo, :]


def _sconv(proj, conv_w, w_out, layer, *, batch, seq, width, off_b, off_c, off_x):
    m = proj.shape[0]
    kconv = conv_w.shape[0]
    n = w_out.shape[2]
    ts = _pick(seq, (512, 256, 128))
    nt = seq // ts
    assert off_b % width == 0 and off_c % width == 0 and off_x % width == 0 and kconv - 1 <= SUBLANES
    jb, jc, jx = off_b // width, off_c // width, off_x // width
    cw = _pick(width, (512, 256, 128))
    pe_shape = (SUBLANES + ts, width)
    blocks = [((ts, width), BF16)] * 3 + [((kconv, width), F32), ((ts, n), F32)]
    single = [((width, n), F32)]
    scratch = [(pe_shape, F32), ((ts, width), BF16), ((width, n), BF16)]
    return pl.pallas_call(
        functools.partial(_sconv_kernel, ts=ts, kconv=kconv, cw=cw),
        grid=(batch, nt),
        in_specs=[pl.BlockSpec((ts, width), lambda b, c: (b * nt + c, jb)),
                  pl.BlockSpec((ts, width), lambda b, c: (b * nt + c, jc)),
                  pl.BlockSpec((ts, width), lambda b, c: (b * nt + c, jx)),
                  pl.BlockSpec((kconv, width), lambda b, c: (0, 0)),
                  pl.BlockSpec((None, width, n), lambda b, c: (layer, 0, 0), pipeline_mode=pl.Buffered(1))],
        out_specs=pl.BlockSpec((ts, n), lambda b, c: (b * nt + c, 0)),
        out_shape=jax.ShapeDtypeStruct((m, n), F32),
        scratch_shapes=[pltpu.VMEM(s, d) for s, d in scratch],
        compiler_params=_cparams(2, blocks, scratch, temps=(4 << 20) + ts * n * 4, single=single),
        name="short_conv",
    )(proj, proj, proj, conv_w, w_out)


def kernel(x, p, norm_mix, w_in, ssd_conv_w, ssd_conv_b, ssd_dt_bias, ssd_a_log, ssd_d, ssd_norm,
           ssd_out, sc_conv_w, sc_out, w_o, norm_ffn, w_gate_up, w_down, norm_ple, ple_gate,
           ple_proj, norm_final):
    batch, seq, d = x.shape
    depth = w_in.shape[0]
    m = batch * seq
    heads = ssd_a_log.shape[1]
    inner = heads * HEAD_DIM
    groups = SSD_GROUPS
    r_heads = heads // groups
    xbc = ssd_conv_w.shape[2]
    scw = sc_conv_w.shape[2]
    assert xbc == inner + 2 * groups * SSD_STATE and heads <= LANES
    dt0 = inner + xbc
    off_scb = dt0
    off_scc, off_scx = off_scb + scw, off_scb + 2 * scw
    off_ga = off_scb + 3 * scw
    off_gb = off_ga + d
    hpad = LANES - heads
    p2 = p.reshape(depth, m, p.shape[-1])
    w_in_t = jnp.swapaxes(w_in, 1, 2)

    h = x.reshape(m, d)
    u, u_ssq = _prenorm(h, norm_mix[0])
    for i in range(depth):
        dt_bias = jnp.pad(ssd_dt_bias[i], (0, hpad)).reshape(1, LANES)
        a_log = jnp.pad(ssd_a_log[i], (0, hpad)).reshape(1, LANES)
        d_skip = jnp.repeat(ssd_d[i], HEAD_DIM).reshape(1, inner)

        proj = _inproj(u, u_ssq, w_in_t, i, dt0, heads)
        dtrow, wrow, acsrow, acscol = _dtprep(u, u_ssq, w_in_t, i, dt0, dt_bias, a_log, groups, r_heads)
        y = _ssd(proj, dtrow, wrow, acsrow, acscol, ssd_conv_w[i], ssd_conv_b[i].reshape(1, xbc), d_skip,
                 ssd_norm[i].reshape(1, inner), batch=batch, seq=seq, inner=inner, groups=groups)
        yb = _sconv(proj, sc_conv_w[i], sc_out, i, batch=batch, seq=seq, width=scw,
                    off_b=off_scb, off_c=off_scc, off_x=off_scx)
        merged = _merge(y, ssd_out, i, yb, proj, off_ga, off_gb)
        h, v, v_ssq = _matmul_residual(merged, w_o, i, h, norm_ffn[i], "w_o_residual", in_place=i > 0)
        act = _swiglu(v, v_ssq, w_gate_up, i)
        h, hn, hn_ssq = _matmul_residual(act, w_down, i, h, norm_ple[i], "w_down_residual")
        g_next = norm_mix[i + 1] if i + 1 < depth else norm_final
        h, u, u_ssq = _ple(hn, hn_ssq, ple_gate, p2, ple_proj, i, h, g_next)
    out = _rmsnorm(h, norm_final, F32)
    return out.reshape(batch, seq, d)
```

```python
import functools

import jax
import jax.numpy as jnp
from jax import lax
from jax.experimental import pallas as pl
from jax.experimental.pallas import tpu as pltpu

F32 = jnp.float32
BF16 = jnp.bfloat16

EPS = 1e-6
HEAD_DIM = 64
SSD_GROUPS = 8
SSD_STATE = 128
CHUNK = 128
LANES = 128
SUBLANES = 8
CAST_ROWS = 256
LOG2E = 1.4426950408889634
VMEM_BYTES_V7X = 64 * 1024 * 1024
VMEM_CAP = VMEM_BYTES_V7X - 4 * 1024 * 1024


def _nbytes(shape, dtype):
    n = 1
    for s in shape:
        n *= s
    return n * jnp.dtype(dtype).itemsize


def _cparams(ngrid, blocks, scratch=(), temps=0, single=()):
    need = (2 * sum(_nbytes(s, d) for s, d in blocks) + sum(_nbytes(s, d) for s, d in single)
            + sum(_nbytes(s, d) for s, d in scratch) + temps + (4 << 20))
    return pltpu.CompilerParams(dimension_semantics=("arbitrary",) * ngrid,
                                vmem_limit_bytes=int(min(max(need, 16 << 20), VMEM_CAP)))


def _pick(n, candidates):
    for c in candidates:
        if n % c == 0:
            return c
    raise ValueError(f"no block size in {candidates} divides {n}")


def _cast_tile(w_ref, wsc_ref):
    k = w_ref.shape[0]
    rc = _pick(k, (CAST_ROWS, LANES, SUBLANES))
    for r in range(0, k, rc):
        wsc_ref[r:r + rc, :] = w_ref[r:r + rc, :].astype(BF16)


def _first_token_tile():
    return pl.program_id(1) == 0


def _sigmoid(x):
    return 0.5 * jnp.tanh(0.5 * x) + 0.5


def _silu(x):
    h = 0.5 * x
    return h * jnp.tanh(h) + h


def _emit_scaled(h_new, g_ref, hg_ref, ssq_ref, r0=0):
    rows = h_new.shape[0]
    hg_ref[r0:r0 + rows, :] = (h_new * g_ref[...]).astype(BF16)
    col = jnp.sum(h_new * h_new, axis=-1, keepdims=True)
    ssq_ref[:, r0:r0 + rows] = jnp.transpose(jnp.broadcast_to(col, (rows, LANES)))[0:1, :]


def _row_rsqrt(ssq_ref, d):
    s = ssq_ref[0]
    for q in range(1, ssq_ref.shape[0]):
        s = s + ssq_ref[q]
    row = lax.rsqrt(s / d + EPS)
    return jnp.transpose(jnp.broadcast_to(row, (LANES, row.shape[1])))[:, 0:1]


def _prenorm_kernel(x_ref, g_ref, hg_ref, ssq_ref):
    _emit_scaled(x_ref[...], g_ref, hg_ref, ssq_ref.at[0])


def _prenorm(x, g):
    m, d = x.shape
    bm = _pick(m, (512, 256, 128))
    blocks = [((bm, d), F32), ((1, d), F32), ((bm, d), BF16), ((1, 1, bm), F32)]
    return pl.pallas_call(
        _prenorm_kernel,
        grid=(m // bm,),
        in_specs=[pl.BlockSpec((bm, d), lambda i: (i, 0)),
                  pl.BlockSpec((1, d), lambda i: (0, 0))],
        out_specs=[pl.BlockSpec((bm, d), lambda i: (i, 0)),
                   pl.BlockSpec((1, 1, bm), lambda i: (0, 0, i))],
        out_shape=[jax.ShapeDtypeStruct((m, d), BF16), jax.ShapeDtypeStruct((1, 1, m), F32)],
        compiler_params=_cparams(1, blocks, temps=2 * bm * d * 4),
        name="prenorm",
    )(x, g.reshape(1, d))


def _rmsnorm_kernel(x_ref, g_ref, o_ref):
    x = x_ref[...]
    ms = jnp.mean(x * x, axis=-1, keepdims=True)
    o_ref[...] = (x * lax.rsqrt(ms + EPS) * g_ref[...]).astype(o_ref.dtype)


def _rmsnorm(x, g, out_dtype):
    m, d = x.shape
    bm = _pick(m, (512, 256, 128))
    blocks = [((bm, d), F32), ((1, d), F32), ((bm, d), out_dtype)]
    return pl.pallas_call(
        _rmsnorm_kernel,
        grid=(m // bm,),
        in_specs=[pl.BlockSpec((bm, d), lambda i: (i, 0)),
                  pl.BlockSpec((1, d), lambda i: (0, 0))],
        out_specs=pl.BlockSpec((bm, d), lambda i: (i, 0)),
        out_shape=jax.ShapeDtypeStruct((m, d), out_dtype),
        compiler_params=_cparams(1, blocks, temps=2 * bm * d * 4),
        name="rmsnorm",
    )(x, g.reshape(1, d))


def _inproj_kernel(a_ref, ssq_ref, w_ref, wn_ref, o_ref, wsc_ref, *, n_aligned, skip):
    j = pl.program_id(0)
    bn = w_ref.shape[0]
    rc = _pick(bn, (CAST_ROWS, LANES))

    @pl.when(jnp.logical_and(_first_token_tile(), j < n_aligned))
    def _():
        _cast_tile(w_ref, wsc_ref)

    @pl.when(jnp.logical_and(_first_token_tile(), j >= n_aligned))
    def _():
        for r in range(0, bn - rc, rc):
            wsc_ref[r:r + rc, :] = w_ref[r + skip:r + skip + rc, :].astype(BF16)
        wsc_ref[bn - rc:bn - skip, :] = w_ref[bn - rc + skip:bn, :].astype(BF16)
        wsc_ref[bn - skip:bn, :] = wn_ref[...].astype(BF16)

    rs = _row_rsqrt(ssq_ref, a_ref.shape[1])
    hm = a_ref.shape[0] // 2
    for r in (0, hm):
        acc = lax.dot_general(a_ref[r:r + hm, :], wsc_ref[...], (((1,), (1,)), ((), ())),
                              preferred_element_type=F32)
        o_ref[r:r + hm, :] = (acc * rs[r:r + hm]).astype(o_ref.dtype)


def _inproj(u, ssq, w_in_t, layer, dt0, heads):
    m, k = u.shape
    parts = ssq.shape[0]
    d_in = w_in_t.shape[1]
    n = d_in - heads
    bm = _pick(m, (2048, 1024, 512, 256, 128))
    bn = next(c for c in (1024, 512, 256, 128) if dt0 % c == 0 and (n - dt0) % c == 0)
    assert heads % (2 * SUBLANES) == 0 and bn % heads == 0 and d_in % heads == 0 and heads < LANES
    n_aligned = dt0 // bn
    blocks = [((bm, k), BF16), ((parts, 1, bm), F32), ((bn, k), F32), ((heads, k), F32), ((bm, bn), BF16)]
    scratch = [((bn, k), BF16)]
    return pl.pallas_call(
        functools.partial(_inproj_kernel, n_aligned=n_aligned, skip=heads),
        grid=(n // bn, m // bm),
        in_specs=[pl.BlockSpec((bm, k), lambda j, i: (i, 0)),
                  pl.BlockSpec((parts, 1, bm), lambda j, i: (0, 0, i)),
                  pl.BlockSpec((None, bn, k), lambda j, i: (layer, j, 0)),
                  pl.BlockSpec((None, heads, k), lambda j, i: (layer, (j + 1) * (bn // heads), 0))],
        out_specs=pl.BlockSpec((bm, bn), lambda j, i: (i, j)),
        out_shape=jax.ShapeDtypeStruct((m, n), BF16),
        scratch_shapes=[pltpu.VMEM(s, d) for s, d in scratch],
        compiler_params=_cparams(2, blocks, scratch, temps=bm * bn * 4 + 2 * CAST_ROWS * k * 4),
        name="in_proj",
    )(u, ssq, w_in_t, w_in_t)


def _mm_res_kernel(a_ref, w_ref, h_ref, g_ref, o_ref, hg_ref, ssq_ref, wsc_ref):
    @pl.when(_first_token_tile())
    def _():
        _cast_tile(w_ref, wsc_ref)

    hm = a_ref.shape[0] // 2
    for r in (0, hm):
        h_new = h_ref[r:r + hm, :] + jnp.dot(a_ref[r:r + hm, :], wsc_ref[...], preferred_element_type=F32)
        o_ref[r:r + hm, :] = h_new
        _emit_scaled(h_new, g_ref, hg_ref, ssq_ref, r)


def _matmul_residual(a, w, layer, h, g_next, name, in_place=True):
    m, k = a.shape
    n = w.shape[2]
    big_k = k > 2048
    bm = _pick(m, (512, 256, 128)) if big_k else _pick(m, (1024, 512, 256, 128))
    bn = _pick(n, (1024, 512, 256, 128))
    w_block = ((k, bn), F32)
    blocks = [((bm, k), BF16), ((bm, bn), F32), ((1, bn), F32),
              ((bm, bn), F32), ((bm, bn), BF16), ((1, bm), F32)] + ([] if big_k else [w_block])
    single = [w_block] if big_k else []
    scratch = [((k, bn), BF16)]
    return pl.pallas_call(
        _mm_res_kernel,
        grid=(n // bn, m // bm),
        in_specs=[pl.BlockSpec((bm, k), lambda j, i: (i, 0)),
                  pl.BlockSpec((None, k, bn), lambda j, i: (layer, 0, j),
                               pipeline_mode=pl.Buffered(1) if big_k else None),
                  pl.BlockSpec((bm, bn), lambda j, i: (i, j)),
                  pl.BlockSpec((1, bn), lambda j, i: (0, j))],
        out_specs=[pl.BlockSpec((bm, bn), lambda j, i: (i, j)),
                   pl.BlockSpec((bm, bn), lambda j, i: (i, j)),
                   pl.BlockSpec((None, 1, bm), lambda j, i: (j, 0, i))],
        out_shape=[jax.ShapeDtypeStruct((m, n), F32), jax.ShapeDtypeStruct((m, n), BF16),
                   jax.ShapeDtypeStruct((n // bn, 1, m), F32)],
        scratch_shapes=[pltpu.VMEM(s, d) for s, d in scratch],
        input_output_aliases={2: 0} if in_place else {},
        compiler_params=_cparams(2, blocks, scratch, temps=2 * bm * bn * 4 + 2 * CAST_ROWS * bn * 4,
                                 single=single),
        name=name,
    )(a, w, h, g_next.reshape(1, n))


def _swiglu_kernel(v_ref, ssq_ref, wg_ref, wu_ref, o_ref, wgsc_ref, wusc_ref):
    @pl.when(_first_token_tile())
    def _():
        _cast_tile(wg_ref, wgsc_ref)
        _cast_tile(wu_ref, wusc_ref)

    rs_all = _row_rsqrt(ssq_ref, v_ref.shape[1])
    hm = v_ref.shape[0] // 2
    for r in (0, hm):
        v = v_ref[r:r + hm, :]
        rs = rs_all[r:r + hm]
        gate = jnp.dot(v, wgsc_ref[...], preferred_element_type=F32) * rs
        up = jnp.dot(v, wusc_ref[...], preferred_element_type=F32) * rs
        o_ref[r:r + hm, :] = (_silu(gate) * up).astype(o_ref.dtype)


def _swiglu(v, ssq, w_gate_up, layer):
    m, k = v.shape
    parts = ssq.shape[0]
    d_ff = w_gate_up.shape[2] // 2
    bm = _pick(m, (2048, 1024, 512, 256, 128))
    bn = _pick(d_ff, (512, 256, 128))
    nb = d_ff // bn
    blocks = [((bm, k), BF16), ((parts, 1, bm), F32), ((k, bn), F32), ((k, bn), F32), ((bm, bn), BF16)]
    scratch = [((k, bn), BF16), ((k, bn), BF16)]
    return pl.pallas_call(
        _swiglu_kernel,
        grid=(nb, m // bm),
        in_specs=[pl.BlockSpec((bm, k), lambda j, i: (i, 0)),
                  pl.BlockSpec((parts, 1, bm), lambda j, i: (0, 0, i)),
                  pl.BlockSpec((None, k, bn), lambda j, i: (layer, 0, j)),
                  pl.BlockSpec((None, k, bn), lambda j, i: (layer, 0, j + nb))],
        out_specs=pl.BlockSpec((bm, bn), lambda j, i: (i, j)),
        out_shape=jax.ShapeDtypeStruct((m, d_ff), BF16),
        scratch_shapes=[pltpu.VMEM(s, d) for s, d in scratch],
        compiler_params=_cparams(2, blocks, scratch, temps=3 * bm * bn * 4 + 2 * CAST_ROWS * bn * 4),
        name="swiglu_up",
    )(v, ssq, w_gate_up, w_gate_up)


def _merge_kernel(y_ref, wa_ref, yb_ref, ga_ref, gb_ref, o_ref, wasc_ref):
    @pl.when(_first_token_tile())
    def _():
        _cast_tile(wa_ref, wasc_ref)

    hm = y_ref.shape[0] // 2
    for r in (0, hm):
        rows = slice(r, r + hm)
        ya = jnp.dot(y_ref[rows, :], wasc_ref[...], preferred_element_type=F32)
        ga = _sigmoid(ga_ref[rows, :].astype(F32))
        gb = _sigmoid(gb_ref[rows, :].astype(F32))
        o_ref[rows, :] = (ga * ya + gb * yb_ref[rows, :]).astype(o_ref.dtype)


def _merge(y, w_a, layer, yb, proj, off_ga, off_gb):
    m, ka = y.shape
    n = w_a.shape[2]
    bm = _pick(m, (512, 256, 128))
    bn = _pick(n, (1024, 512, 256, 128))
    ja, jb = off_ga // bn, off_gb // bn
    assert off_ga % bn == 0 and off_gb % bn == 0
    blocks = [((bm, ka), BF16), ((bm, bn), F32), ((bm, bn), BF16), ((bm, bn), BF16), ((bm, bn), BF16)]
    single = [((ka, bn), F32)]
    scratch = [((ka, bn), BF16)]
    return pl.pallas_call(
        _merge_kernel,
        grid=(n // bn, m // bm),
        in_specs=[pl.BlockSpec((bm, ka), lambda j, i: (i, 0)),
                  pl.BlockSpec((None, ka, bn), lambda j, i: (layer, 0, j), pipeline_mode=pl.Buffered(1)),
                  pl.BlockSpec((bm, bn), lambda j, i: (i, j)),
                  pl.BlockSpec((bm, bn), lambda j, i: (i, j + ja)),
                  pl.BlockSpec((bm, bn), lambda j, i: (i, j + jb))],
        out_specs=pl.BlockSpec((bm, bn), lambda j, i: (i, j)),
        out_shape=jax.ShapeDtypeStruct((m, n), BF16),
        scratch_shapes=[pltpu.VMEM(s, d) for s, d in scratch],
        compiler_params=_cparams(2, blocks, scratch, temps=4 * bm * bn * 4 + 2 * CAST_ROWS * bn * 4,
                                 single=single),
        name="branch_merge",
    )(y, w_a, yb, proj, proj)


def _ple_kernel(hn_ref, ssq_ref, wg_ref, p_ref, wp_ref, h_ref, g_ref, o_ref, hg_ref, ssqo_ref,
                wgsc_ref, wpsc_ref):
    @pl.when(_first_token_tile())
    def _():
        _cast_tile(wg_ref, wgsc_ref)
        _cast_tile(wp_ref, wpsc_ref)

    rs = _row_rsqrt(ssq_ref, hn_ref.shape[1])
    hm = hn_ref.shape[0] // 2
    for r in (0, hm):
        rows = slice(r, r + hm)
        pg = _sigmoid(jnp.dot(hn_ref[rows, :], wgsc_ref[...], preferred_element_type=F32) * rs[rows])
        e = jnp.dot(p_ref[rows, :].astype(BF16), wpsc_ref[...], preferred_element_type=F32)
        h_new = h_ref[rows, :] + pg * e
        o_ref[rows, :] = h_new
        _emit_scaled(h_new, g_ref, hg_ref, ssqo_ref, r)


def _ple(hn, ssq, w_gate, p, w_proj, layer, h, g_next):
    m, k = hn.shape
    parts = ssq.shape[0]
    kp = p.shape[2]
    n = w_gate.shape[2]
    bm = _pick(m, (1024, 512, 256, 128))
    bn = _pick(n, (1024, 512, 256, 128))
    blocks = [((bm, k), BF16), ((parts, 1, bm), F32), ((k, bn), F32), ((bm, kp), F32), ((kp, bn), F32),
              ((bm, bn), F32), ((1, bn), F32), ((bm, bn), F32), ((bm, bn), BF16), ((1, bm), F32)]
    scratch = [((k, bn), BF16), ((kp, bn), BF16)]
    return pl.pallas_call(
        _ple_kernel,
        grid=(n // bn, m // bm),
        in_specs=[pl.BlockSpec((bm, k), lambda j, i: (i, 0)),
                  pl.BlockSpec((parts, 1, bm), lambda j, i: (0, 0, i)),
                  pl.BlockSpec((None, k, bn), lambda j, i: (layer, 0, j)),
                  pl.BlockSpec((None, bm, kp), lambda j, i: (layer, i, 0)),
                  pl.BlockSpec((None, kp, bn), lambda j, i: (layer, 0, j)),
                  pl.BlockSpec((bm, bn), lambda j, i: (i, j)),
                  pl.BlockSpec((1, bn), lambda j, i: (0, j))],
        out_specs=[pl.BlockSpec((bm, bn), lambda j, i: (i, j)),
                   pl.BlockSpec((bm, bn), lambda j, i: (i, j)),
                   pl.BlockSpec((None, 1, bm), lambda j, i: (j, 0, i))],
        out_shape=[jax.ShapeDtypeStruct((m, n), F32), jax.ShapeDtypeStruct((m, n), BF16),
                   jax.ShapeDtypeStruct((n // bn, 1, m), F32)],
        scratch_shapes=[pltpu.VMEM(s, d) for s, d in scratch],
        input_output_aliases={5: 0},
        compiler_params=_cparams(2, blocks, scratch, temps=4 * bm * bn * 4 + 2 * CAST_ROWS * bn * 4),
        name="ple",
    )(hn, ssq, w_gate, p, w_proj, h, g_next.reshape(1, n))


def _dtprep_kernel(u_ref, ssq_ref, w_ref, bias_ref, alog_ref, dtrow_ref, wrow_ref, acsrow_ref, acscol_ref,
                   *, nc, groups, r_heads):
    x = lax.dot_general(u_ref[...], w_ref[...].astype(BF16), (((1,), (1,)), ((), ())),
                        preferred_element_type=F32)
    x = x * _row_rsqrt(ssq_ref, u_ref.shape[1]) + bias_ref[...]
    dt = jnp.maximum(x, 0.0) + jnp.log1p(jnp.exp(-jnp.abs(x)))
    adt = dt * (-jnp.exp(alog_ref[...]))
    row = lax.broadcasted_iota(jnp.int32, (CHUNK, LANES), 0)
    for k in range(nc):
        sl = slice(k * CHUNK, (k + 1) * CHUNK)
        acs = adt[sl]
        sh = 1
        while sh < CHUNK:
            acs = acs + jnp.where(row >= sh, pltpu.roll(acs, sh, 0), 0.0)
            sh *= 2
        acs2 = acs * LOG2E
        acsrow_ref[k] = acs2.T
        dtrow_ref[k] = dt[sl].T
        wrow_ref[k] = (dt[sl] * jnp.exp(acs[CHUNK - 1:CHUNK, :] - acs)).T
        for g in range(groups):
            shift = (LANES - g * r_heads) % LANES
            acscol_ref[g, sl, :] = pltpu.roll(acs2, shift, 1) if shift else acs2


def _dtprep(u, ssq, w_in_t, layer, dt0, bias, a_log, groups, r_heads):
    m, k = u.shape
    parts = ssq.shape[0]
    nc = 4 if m % (4 * CHUNK) == 0 else 1
    ts = nc * CHUNK
    nchunks = m // CHUNK
    rows = ((nc, LANES, LANES), F32)
    blocks = [((ts, k), BF16), ((parts, 1, ts), F32), ((LANES, k), F32), rows, rows, rows,
              ((groups, ts, LANES), F32)]
    row_spec = pl.BlockSpec((nc, LANES, LANES), lambda i: (i, 0, 0))
    row_shape = jax.ShapeDtypeStruct((nchunks, LANES, LANES), F32)
    return pl.pallas_call(
        functools.partial(_dtprep_kernel, nc=nc, groups=groups, r_heads=r_heads),
        grid=(m // ts,),
        in_specs=[pl.BlockSpec((ts, k), lambda i: (i, 0)),
                  pl.BlockSpec((parts, 1, ts), lambda i: (0, 0, i)),
                  pl.BlockSpec((None, LANES, k), lambda i: (layer, dt0 // LANES, 0)),
                  pl.BlockSpec((1, LANES), lambda i: (0, 0)),
                  pl.BlockSpec((1, LANES), lambda i: (0, 0))],
        out_specs=[row_spec, row_spec, row_spec,
                   pl.BlockSpec((groups, ts, LANES), lambda i: (0, i, 0))],
        out_shape=[row_shape, row_shape, row_shape,
                   jax.ShapeDtypeStruct((groups, m, LANES), F32)],
        compiler_params=_cparams(1, blocks, temps=8 * ts * LANES * 4 + k * LANES * 2),
        name="dt_prep",
    )(u, ssq, w_in_t, bias, a_log)


def _ssd_kernel(z_ref, xs_ref, b_ref, c_ref, xsp_ref, bp_ref, cp_ref,
                dtr_ref, wr_ref, acr_ref, acc_ref, shift_ref,
                cwx_ref, cwb_ref, cwc_ref, cbx_ref, cbb_ref, cbc_ref, dsk_ref, nw_ref,
                o_ref, st_ref, yt_ref, *, nc, r_heads, kconv):
    g = pl.program_id(1)
    c = pl.program_id(2)
    L = CHUNK
    N = SSD_STATE
    gw = r_heads * HEAD_DIM
    npair = gw // LANES

    @pl.when(c == 0)
    def _():
        st_ref[...] = jnp.zeros(st_ref.shape, F32)

    tri = (lax.broadcasted_iota(jnp.int32, (L, L), 0) >= lax.broadcasted_iota(jnp.int32, (L, L), 1))
    lo = lax.broadcasted_iota(jnp.int32, (L, LANES), 1) < HEAD_DIM
    hi = jnp.logical_not(lo)
    lo_row = lo[0:1]

    def two_chunks(ref, prev_ref, k):
        if k == 0:
            prev = prev_ref[...]
            prev = jnp.where(c > 0, prev, jnp.zeros_like(prev))
            return jnp.concatenate([prev, ref[0:L, :]], axis=0)
        return ref[(k - 1) * L:(k + 1) * L, :]

    def conv_silu(shifted, cur, w_ref, bias_ref):
        acc = None
        for tap in range(kconv - 1):
            term = shifted[tap * L:(tap + 1) * L, :] * w_ref[tap:tap + 1, :]
            acc = term if acc is None else acc + term
        acc = acc + cur.astype(F32) * w_ref[kconv - 1:kconv, :]
        acc = acc + bias_ref[...]
        return _silu(acc)

    for k in range(nc):
        r0 = k * L
        x2 = jnp.concatenate([two_chunks(xs_ref, xsp_ref, k), two_chunks(b_ref, bp_ref, k),
                              two_chunks(c_ref, cp_ref, k)], axis=1)
        shifted = jnp.dot(shift_ref[...], x2, preferred_element_type=F32)
        cur = x2[L:2 * L]
        x = conv_silu(shifted[:, 0:gw], cur[:, 0:gw], cwx_ref, cbx_ref)
        bm = conv_silu(shifted[:, gw:gw + N], cur[:, gw:gw + N], cwb_ref, cbb_ref)
        cm = conv_silu(shifted[:, gw + N:gw + 2 * N], cur[:, gw + N:gw + 2 * N], cwc_ref, cbc_ref)
        cb = lax.dot_general(cm.astype(BF16), bm.astype(BF16), (((1,), (1,)), ((), ())),
                             preferred_element_type=F32)
        bt = bm.T
        acol = acc_ref[0, r0:r0 + L, :]
        cdec = jnp.exp2(acol[L - 1:L, :])
        ssq = jnp.zeros((L, 1), F32)
        for j in range(npair):
            cs = slice(j * LANES, (j + 1) * LANES)
            xp = x[:, cs]
            s_prev = st_ref[:, cs]
            s_new = s_prev * jnp.where(lo_row, cdec[:, 2 * j:2 * j + 1], cdec[:, 2 * j + 1:2 * j + 2])
            y = None
            for hh, keep in ((2 * j, lo), (2 * j + 1, hi)):
                head = pl.ds(g * r_heads + hh, 1)
                arow = acr_ref[k, head, :]
                drow = dtr_ref[k, head, :]
                wrow = wr_ref[k, head, :]
                xm = jnp.where(keep, xp, 0.0).astype(BF16)
                sm = jnp.where(keep, s_prev, 0.0).astype(BF16)
                ab = jnp.broadcast_to(acol[:, hh:hh + 1], (L, L))
                dec = jnp.exp2(jnp.where(tri, ab - arow, -jnp.inf))
                mh = cb * dec * drow
                ch = cm * jnp.exp2(ab)
                lhs = jnp.concatenate([mh, ch], axis=1).astype(BF16)
                rhs = jnp.concatenate([xm, sm], axis=0)
                t = jnp.dot(lhs, rhs, preferred_element_type=F32)
                y = t if y is None else y + t
                bth = (bt * wrow).astype(BF16)
                s_new = s_new + jnp.dot(bth, xm, preferred_element_type=F32)
            st_ref[:, cs] = s_new
            yt = y + dsk_ref[:, cs] * xp
            zt = z_ref[r0:r0 + L, cs].astype(F32)
            yt = yt * _silu(zt)
            ssq = ssq + jnp.sum(yt * yt, axis=-1, keepdims=True)
            yt_ref[:, cs] = yt
        rs = lax.rsqrt(ssq / gw + EPS)
        for j in range(npair):
            cs = slice(j * LANES, (j + 1) * LANES)
            o_ref[r0:r0 + L, cs] = (yt_ref[:, cs] * rs * nw_ref[:, cs]).astype(o_ref.dtype)


def _ssd(proj, dtrow, wrow, acsrow, acscol, conv_w, conv_b, d_skip, norm_w, *, batch, seq, inner, groups):
    m = proj.shape[0]
    gw = inner // groups
    r_heads = gw // HEAD_DIM
    assert gw % LANES == 0 and r_heads % 2 == 0 and SSD_STATE == LANES
    kconv = conv_w.shape[0]
    assert kconv - 1 <= CHUNK
    nc = _pick(seq // CHUNK, (16, 8, 4, 2, 1))
    ts = nc * CHUNK
    nt = seq // ts
    gn = groups * SSD_STATE
    xs_blk, b_blk, c_blk = inner // gw, 2 * inner // SSD_STATE, (2 * inner + gn) // SSD_STATE
    wb_blk, wc_blk = inner // SSD_STATE, (inner + gn) // SSD_STATE
    hp = acsrow.shape[1]
    t_idx = jnp.arange(CHUNK)[None, :, None]
    d_idx = (kconv - 1 - jnp.arange(kconv - 1))[:, None, None]
    s_idx = jnp.arange(2 * CHUNK)[None, None, :]
    shift = (s_idx == CHUNK + t_idx - d_idx).astype(BF16).reshape((kconv - 1) * CHUNK, 2 * CHUNK)
    st_shape = (SSD_STATE, gw)
    blocks = [((ts, gw), BF16), ((ts, gw), BF16), ((ts, SSD_STATE), BF16), ((ts, SSD_STATE), BF16),
              ((CHUNK, gw), BF16), ((CHUNK, SSD_STATE), BF16), ((CHUNK, SSD_STATE), BF16),
              ((nc, hp, LANES), F32), ((nc, hp, LANES), F32), ((nc, hp, LANES), F32),
              ((1, ts, LANES), F32), (shift.shape, BF16),
              ((kconv, gw), F32), ((kconv, SSD_STATE), F32), ((kconv, SSD_STATE), F32),
              ((1, gw), F32), ((1, SSD_STATE), F32), ((1, SSD_STATE), F32), ((1, gw), F32), ((1, gw), F32),
              ((ts, gw), BF16)]
    scratch = [(st_shape, F32), ((CHUNK, gw), F32)]
    row = lambda b, g, c: b * nt + c
    prev = lambda b, g, c: jnp.maximum(row(b, g, c) * nc - 1, 0)
    return pl.pallas_call(
        functools.partial(_ssd_kernel, nc=nc, r_heads=r_heads, kconv=kconv),
        grid=(batch, groups, nt),
        in_specs=[
            pl.BlockSpec((ts, gw), lambda b, g, c: (row(b, g, c), g)),
            pl.BlockSpec((ts, gw), lambda b, g, c: (row(b, g, c), xs_blk + g)),
            pl.BlockSpec((ts, SSD_STATE), lambda b, g, c: (row(b, g, c), b_blk + g)),
            pl.BlockSpec((ts, SSD_STATE), lambda b, g, c: (row(b, g, c), c_blk + g)),
            pl.BlockSpec((CHUNK, gw), lambda b, g, c: (prev(b, g, c), xs_blk + g)),
            pl.BlockSpec((CHUNK, SSD_STATE), lambda b, g, c: (prev(b, g, c), b_blk + g)),
            pl.BlockSpec((CHUNK, SSD_STATE), lambda b, g, c: (prev(b, g, c), c_blk + g)),
            pl.BlockSpec((nc, hp, LANES), lambda b, g, c: (row(b, g, c), 0, 0)),
            pl.BlockSpec((nc, hp, LANES), lambda b, g, c: (row(b, g, c), 0, 0)),
            pl.BlockSpec((nc, hp, LANES), lambda b, g, c: (row(b, g, c), 0, 0)),
            pl.BlockSpec((1, ts, LANES), lambda b, g, c: (g, row(b, g, c), 0)),
            pl.BlockSpec(shift.shape, lambda b, g, c: (0, 0)),
            pl.BlockSpec((kconv, gw), lambda b, g, c: (0, g)),
            pl.BlockSpec((kconv, SSD_STATE), lambda b, g, c: (0, wb_blk + g)),
            pl.BlockSpec((kconv, SSD_STATE), lambda b, g, c: (0, wc_blk + g)),
            pl.BlockSpec((1, gw), lambda b, g, c: (0, g)),
            pl.BlockSpec((1, SSD_STATE), lambda b, g, c: (0, wb_blk + g)),
            pl.BlockSpec((1, SSD_STATE), lambda b, g, c: (0, wc_blk + g)),
            pl.BlockSpec((1, gw), lambda b, g, c: (0, g)),
            pl.BlockSpec((1, gw), lambda b, g, c: (0, g)),
        ],
        out_specs=pl.BlockSpec((ts, gw), lambda b, g, c: (row(b, g, c), g)),
        out_shape=jax.ShapeDtypeStruct((m, inner), BF16),
        scratch_shapes=[pltpu.VMEM(s, d) for s, d in scratch],
        compiler_params=_cparams(3, blocks, scratch, temps=8 << 20),
        name="ssd",
    )(proj, proj, proj, proj, proj, proj, proj, dtrow, wrow, acsrow, acscol, shift,
      conv_w, conv_w, conv_w, conv_b, conv_b, conv_b, d_skip, norm_w)


def _sconv_kernel(gb_ref, gc_ref, xt_ref, w_ref, wo_ref, o_ref, pe_ref, ysc_ref, wsc_ref, *, ts, kconv, cw):
    halo = SUBLANES

    @pl.when(jnp.logical_and(pl.program_id(0) == 0, pl.program_id(1) == 0))
    def _():
        _cast_tile(wo_ref, wsc_ref)

    @pl.when(pl.program_id(1) == 0)
    def _():
        pe_ref[0:halo, :] = jnp.zeros((halo, pe_ref.shape[1]), F32)

    pe_ref[halo:halo + ts, :] = gb_ref[...].astype(F32) * xt_ref[...].astype(F32)
    width = pe_ref.shape[1]
    hr = ts // 2
    for r0 in range(0, ts, hr):
        for c0 in range(0, width, cw):
            acc = None
            for k in range(kconv):
                r = halo - (kconv - 1) + k + r0
                term = pe_ref[r:r + hr, c0:c0 + cw] * w_ref[k:k + 1, c0:c0 + cw]
                acc = term if acc is None else acc + term
            ysc_ref[r0:r0 + hr, c0:c0 + cw] = (gc_ref[r0:r0 + hr, c0:c0 + cw].astype(F32) * acc).astype(BF16)
        o_ref[r0:r0 + hr, :] = jnp.dot(ysc_ref[r0:r0 + hr, :], wsc_ref[...], preferred_element_type=F32)
    pe_ref[0:halo, :] = pe_ref[ts:ts + halo, :]


def _sconv(proj, conv_w, w_out, layer, *, batch, seq, width, off_b, off_c, off_x):
    m = proj.shape[0]
    kconv = conv_w.shape[0]
    n = w_out.shape[2]
    ts = _pick(seq, (512, 256, 128))
    nt = seq // ts
    assert off_b % width == 0 and off_c % width == 0 and off_x % width == 0 and kconv - 1 <= SUBLANES
    jb, jc, jx = off_b // width, off_c // width, off_x // width
    cw = _pick(width, (512, 256, 128))
    pe_shape = (SUBLANES + ts, width)
    blocks = [((ts, width), BF16)] * 3 + [((kconv, width), F32), ((ts, n), F32)]
    single = [((width, n), F32)]
    scratch = [(pe_shape, F32), ((ts, width), BF16), ((width, n), BF16)]
    return pl.pallas_call(
        functools.partial(_sconv_kernel, ts=ts, kconv=kconv, cw=cw),
        grid=(batch, nt),
        in_specs=[pl.BlockSpec((ts, width), lambda b, c: (b * nt + c, jb)),
                  pl.BlockSpec((ts, width), lambda b, c: (b * nt + c, jc)),
                  pl.BlockSpec((ts, width), lambda b, c: (b * nt + c, jx)),
                  pl.BlockSpec((kconv, width), lambda b, c: (0, 0)),
                  pl.BlockSpec((None, width, n), lambda b, c: (layer, 0, 0), pipeline_mode=pl.Buffered(1))],
        out_specs=pl.BlockSpec((ts, n), lambda b, c: (b * nt + c, 0)),
        out_shape=jax.ShapeDtypeStruct((m, n), F32),
        scratch_shapes=[pltpu.VMEM(s, d) for s, d in scratch],
        compiler_params=_cparams(2, blocks, scratch, temps=(4 << 20) + ts * n * 4, single=single),
        name="short_conv",
    )(proj, proj, proj, conv_w, w_out)


def kernel(x, p, norm_mix, w_in, ssd_conv_w, ssd_conv_b, ssd_dt_bias, ssd_a_log, ssd_d, ssd_norm,
           ssd_out, sc_conv_w, sc_out, w_o, norm_ffn, w_gate_up, w_down, norm_ple, ple_gate,
           ple_proj, norm_final):
    batch, seq, d = x.shape
    depth = w_in.shape[0]
    m = batch * seq
    heads = ssd_a_log.shape[1]
    inner = heads * HEAD_DIM
    groups = SSD_GROUPS
    r_heads = heads // groups
    xbc = ssd_conv_w.shape[2]
    scw = sc_conv_w.shape[2]
    assert xbc == inner + 2 * groups * SSD_STATE and heads <= LANES
    dt0 = inner + xbc
    off_scb = dt0
    off_scc, off_scx = off_scb + scw, off_scb + 2 * scw
    off_ga = off_scb + 3 * scw
    off_gb = off_ga + d
    hpad = LANES - heads
    p2 = p.reshape(depth, m, p.shape[-1])
    w_in_t = jnp.swapaxes(w_in, 1, 2)

    h = x.reshape(m, d)
    u, u_ssq = _prenorm(h, norm_mix[0])
    for i in range(depth):
        dt_bias = jnp.pad(ssd_dt_bias[i], (0, hpad)).reshape(1, LANES)
        a_log = jnp.pad(ssd_a_log[i], (0, hpad)).reshape(1, LANES)
        d_skip = jnp.repeat(ssd_d[i], HEAD_DIM).reshape(1, inner)

        proj = _inproj(u, u_ssq, w_in_t, i, dt0, heads)
        dtrow, wrow, acsrow, acscol = _dtprep(u, u_ssq, w_in_t, i, dt0, dt_bias, a_log, groups, r_heads)
        y = _ssd(proj, dtrow, wrow, acsrow, acscol, ssd_conv_w[i], ssd_conv_b[i].reshape(1, xbc), d_skip,
                 ssd_norm[i].reshape(1, inner), batch=batch, seq=seq, inner=inner, groups=groups)
        yb = _sconv(proj, sc_conv_w[i], sc_out, i, batch=batch, seq=seq, width=scw,
                    off_b=off_scb, off_c=off_scc, off_x=off_scx)
        merged = _merge(y, ssd_out, i, yb, proj, off_ga, off_gb)
        h, v, v_ssq = _matmul_residual(merged, w_o, i, h, norm_ffn[i], "w_o_residual", in_place=i > 0)
        act = _swiglu(v, v_ssq, w_gate_up, i)
        h, hn, hn_ssq = _matmul_residual(act, w_down, i, h, norm_ple[i], "w_down_residual")
        g_next = norm_mix[i + 1] if i + 1 < depth else norm_final
        h, u, u_ssq = _ple(hn, hn_ssq, ple_gate, p2, ple_proj, i, h, g_next)
    out = _rmsnorm(h, norm_final, F32)
    return out.reshape(batch, seq, d)
```

```python
import functools

import jax
import jax.numpy as jnp
from jax import lax
from jax.experimental import pallas as pl
from jax.experimental.pallas import tpu as pltpu

F32 = jnp.float32
BF16 = jnp.bfloat16

EPS = 1e-6
HEAD_DIM = 64
SSD_GROUPS = 8
SSD_STATE = 128
CHUNK = 128
LANES = 128
SUBLANES = 8
CAST_ROWS = 256
LOG2E = 1.4426950408889634
VMEM_BYTES_V7X = 64 * 1024 * 1024
VMEM_CAP = VMEM_BYTES_V7X - 3 * 1024 * 1024


def _nbytes(shape, dtype):
    n = 1
    for s in shape:
        n *= s
    return n * jnp.dtype(dtype).itemsize


def _cparams(ngrid, blocks, scratch=(), temps=0, single=()):
    need = (2 * sum(_nbytes(s, d) for s, d in blocks) + sum(_nbytes(s, d) for s, d in single)
            + sum(_nbytes(s, d) for s, d in scratch) + temps + (4 << 20))
    return pltpu.CompilerParams(dimension_semantics=("arbitrary",) * ngrid,
                                vmem_limit_bytes=int(min(max(need, 16 << 20), VMEM_CAP)))


def _pick(n, candidates):
    for c in candidates:
        if n % c == 0:
            return c
    raise ValueError(f"no block size in {candidates} divides {n}")


def _cast_tile(w_ref, wsc_ref):
    k = w_ref.shape[0]
    rc = _pick(k, (CAST_ROWS, LANES, SUBLANES))
    for r in range(0, k, rc):
        wsc_ref[r:r + rc, :] = w_ref[r:r + rc, :].astype(BF16)


def _first_token_tile():
    return pl.program_id(1) == 0


def _sigmoid(x):
    return 0.5 * jnp.tanh(0.5 * x) + 0.5


def _silu(x):
    h = 0.5 * x
    return h * jnp.tanh(h) + h


def _emit_scaled(h_new, g_ref, hg_ref, ssq_ref):
    hg_ref[...] = (h_new * g_ref[...]).astype(BF16)
    col = jnp.sum(h_new * h_new, axis=-1, keepdims=True)
    ssq_ref[...] = jnp.transpose(jnp.broadcast_to(col, (col.shape[0], LANES)))[0:1, :]


def _row_rsqrt(ssq_ref, d):
    s = ssq_ref[0]
    for q in range(1, ssq_ref.shape[0]):
        s = s + ssq_ref[q]
    row = lax.rsqrt(s / d + EPS)
    return jnp.transpose(jnp.broadcast_to(row, (LANES, row.shape[1])))[:, 0:1]


def _prenorm_kernel(x_ref, g_ref, hg_ref, ssq_ref):
    _emit_scaled(x_ref[...], g_ref, hg_ref, ssq_ref.at[0])


def _prenorm(x, g):
    m, d = x.shape
    bm = _pick(m, (512, 256, 128))
    blocks = [((bm, d), F32), ((1, d), F32), ((bm, d), BF16), ((1, 1, bm), F32)]
    return pl.pallas_call(
        _prenorm_kernel,
        grid=(m // bm,),
        in_specs=[pl.BlockSpec((bm, d), lambda i: (i, 0)),
                  pl.BlockSpec((1, d), lambda i: (0, 0))],
        out_specs=[pl.BlockSpec((bm, d), lambda i: (i, 0)),
                   pl.BlockSpec((1, 1, bm), lambda i: (0, 0, i))],
        out_shape=[jax.ShapeDtypeStruct((m, d), BF16), jax.ShapeDtypeStruct((1, 1, m), F32)],
        compiler_params=_cparams(1, blocks, temps=2 * bm * d * 4),
        name="prenorm",
    )(x, g.reshape(1, d))


def _rmsnorm_kernel(x_ref, g_ref, o_ref):
    x = x_ref[...]
    ms = jnp.mean(x * x, axis=-1, keepdims=True)
    o_ref[...] = (x * lax.rsqrt(ms + EPS) * g_ref[...]).astype(o_ref.dtype)


def _rmsnorm(x, g, out_dtype):
    m, d = x.shape
    bm = _pick(m, (512, 256, 128))
    blocks = [((bm, d), F32), ((1, d), F32), ((bm, d), out_dtype)]
    return pl.pallas_call(
        _rmsnorm_kernel,
        grid=(m // bm,),
        in_specs=[pl.BlockSpec((bm, d), lambda i: (i, 0)),
                  pl.BlockSpec((1, d), lambda i: (0, 0))],
        out_specs=pl.BlockSpec((bm, d), lambda i: (i, 0)),
        out_shape=jax.ShapeDtypeStruct((m, d), out_dtype),
        compiler_params=_cparams(1, blocks, temps=2 * bm * d * 4),
        name="rmsnorm",
    )(x, g.reshape(1, d))


def _inproj_kernel(a_ref, ssq_ref, w_ref, wn_ref, o_ref, wsc_ref, *, n_aligned, skip):
    j = pl.program_id(0)
    bn = w_ref.shape[0]
    rc = _pick(bn, (CAST_ROWS, LANES))

    @pl.when(jnp.logical_and(_first_token_tile(), j < n_aligned))
    def _():
        _cast_tile(w_ref, wsc_ref)

    @pl.when(jnp.logical_and(_first_token_tile(), j >= n_aligned))
    def _():
        for r in range(0, bn - rc, rc):
            wsc_ref[r:r + rc, :] = w_ref[r + skip:r + skip + rc, :].astype(BF16)
        wsc_ref[bn - rc:bn - skip, :] = w_ref[bn - rc + skip:bn, :].astype(BF16)
        wsc_ref[bn - skip:bn, :] = wn_ref[...].astype(BF16)

    rs = _row_rsqrt(ssq_ref, a_ref.shape[1])
    hm = a_ref.shape[0] // 2
    for r in (0, hm):
        acc = lax.dot_general(a_ref[r:r + hm, :], wsc_ref[...], (((1,), (1,)), ((), ())),
                              preferred_element_type=F32)
        o_ref[r:r + hm, :] = (acc * rs[r:r + hm]).astype(o_ref.dtype)


def _inproj(u, ssq, w_in_t, layer, dt0, heads):
    m, k = u.shape
    parts = ssq.shape[0]
    d_in = w_in_t.shape[1]
    n = d_in - heads
    bm = _pick(m, (2048, 1024, 512, 256, 128))
    bn = next(c for c in (1024, 512, 256, 128) if dt0 % c == 0 and (n - dt0) % c == 0)
    assert heads % (2 * SUBLANES) == 0 and bn % heads == 0 and d_in % heads == 0 and heads < LANES
    n_aligned = dt0 // bn
    blocks = [((bm, k), BF16), ((parts, 1, bm), F32), ((bn, k), F32), ((heads, k), F32), ((bm, bn), BF16)]
    scratch = [((bn, k), BF16)]
    return pl.pallas_call(
        functools.partial(_inproj_kernel, n_aligned=n_aligned, skip=heads),
        grid=(n // bn, m // bm),
        in_specs=[pl.BlockSpec((bm, k), lambda j, i: (i, 0)),
                  pl.BlockSpec((parts, 1, bm), lambda j, i: (0, 0, i)),
                  pl.BlockSpec((None, bn, k), lambda j, i: (layer, j, 0)),
                  pl.BlockSpec((None, heads, k), lambda j, i: (layer, (j + 1) * (bn // heads), 0))],
        out_specs=pl.BlockSpec((bm, bn), lambda j, i: (i, j)),
        out_shape=jax.ShapeDtypeStruct((m, n), BF16),
        scratch_shapes=[pltpu.VMEM(s, d) for s, d in scratch],
        compiler_params=_cparams(2, blocks, scratch, temps=bm * bn * 4 + 2 * CAST_ROWS * k * 4),
        name="in_proj",
    )(u, ssq, w_in_t, w_in_t)


def _mm_res_kernel(a_ref, w_ref, h_ref, g_ref, o_ref, hg_ref, ssq_ref, wsc_ref):
    @pl.when(_first_token_tile())
    def _():
        _cast_tile(w_ref, wsc_ref)

    h_new = h_ref[...] + jnp.dot(a_ref[...], wsc_ref[...], preferred_element_type=F32)
    o_ref[...] = h_new
    _emit_scaled(h_new, g_ref, hg_ref, ssq_ref)


def _matmul_residual(a, w, layer, h, g_next, name, in_place=True):
    m, k = a.shape
    n = w.shape[2]
    big_k = k > 2048
    bm = _pick(m, (512, 256, 128)) if big_k else _pick(m, (1024, 512, 256, 128))
    bn = _pick(n, (1024, 512, 256, 128))
    w_block = ((k, bn), F32)
    blocks = [((bm, k), BF16), ((bm, bn), F32), ((1, bn), F32),
              ((bm, bn), F32), ((bm, bn), BF16), ((1, bm), F32)] + ([] if big_k else [w_block])
    single = [w_block] if big_k else []
    scratch = [((k, bn), BF16)]
    return pl.pallas_call(
        _mm_res_kernel,
        grid=(n // bn, m // bm),
        in_specs=[pl.BlockSpec((bm, k), lambda j, i: (i, 0)),
                  pl.BlockSpec((None, k, bn), lambda j, i: (layer, 0, j),
                               pipeline_mode=pl.Buffered(1) if big_k else None),
                  pl.BlockSpec((bm, bn), lambda j, i: (i, j)),
                  pl.BlockSpec((1, bn), lambda j, i: (0, j))],
        out_specs=[pl.BlockSpec((bm, bn), lambda j, i: (i, j)),
                   pl.BlockSpec((bm, bn), lambda j, i: (i, j)),
                   pl.BlockSpec((None, 1, bm), lambda j, i: (j, 0, i))],
        out_shape=[jax.ShapeDtypeStruct((m, n), F32), jax.ShapeDtypeStruct((m, n), BF16),
                   jax.ShapeDtypeStruct((n // bn, 1, m), F32)],
        scratch_shapes=[pltpu.VMEM(s, d) for s, d in scratch],
        input_output_aliases={2: 0} if in_place else {},
        compiler_params=_cparams(2, blocks, scratch, temps=2 * bm * bn * 4 + 2 * CAST_ROWS * bn * 4,
                                 single=single),
        name=name,
    )(a, w, h, g_next.reshape(1, n))


def _swiglu_kernel(v_ref, ssq_ref, wg_ref, wu_ref, o_ref, wgsc_ref, wusc_ref):
    @pl.when(_first_token_tile())
    def _():
        _cast_tile(wg_ref, wgsc_ref)
        _cast_tile(wu_ref, wusc_ref)

    rs_all = _row_rsqrt(ssq_ref, v_ref.shape[1])
    hm = v_ref.shape[0] // 2
    for r in (0, hm):
        v = v_ref[r:r + hm, :]
        rs = rs_all[r:r + hm]
        gate = jnp.dot(v, wgsc_ref[...], preferred_element_type=F32) * rs
        up = jnp.dot(v, wusc_ref[...], preferred_element_type=F32) * rs
        o_ref[r:r + hm, :] = (_silu(gate) * up).astype(o_ref.dtype)


def _swiglu(v, ssq, w_gate_up, layer):
    m, k = v.shape
    parts = ssq.shape[0]
    d_ff = w_gate_up.shape[2] // 2
    bm = _pick(m, (2048, 1024, 512, 256, 128))
    bn = _pick(d_ff, (512, 256, 128))
    nb = d_ff // bn
    blocks = [((bm, k), BF16), ((parts, 1, bm), F32), ((k, bn), F32), ((k, bn), F32), ((bm, bn), BF16)]
    scratch = [((k, bn), BF16), ((k, bn), BF16)]
    return pl.pallas_call(
        _swiglu_kernel,
        grid=(nb, m // bm),
        in_specs=[pl.BlockSpec((bm, k), lambda j, i: (i, 0)),
                  pl.BlockSpec((parts, 1, bm), lambda j, i: (0, 0, i)),
                  pl.BlockSpec((None, k, bn), lambda j, i: (layer, 0, j)),
                  pl.BlockSpec((None, k, bn), lambda j, i: (layer, 0, j + nb))],
        out_specs=pl.BlockSpec((bm, bn), lambda j, i: (i, j)),
        out_shape=jax.ShapeDtypeStruct((m, d_ff), BF16),
        scratch_shapes=[pltpu.VMEM(s, d) for s, d in scratch],
        compiler_params=_cparams(2, blocks, scratch, temps=3 * bm * bn * 4 + 2 * CAST_ROWS * bn * 4),
        name="swiglu_up",
    )(v, ssq, w_gate_up, w_gate_up)


def _merge_kernel(y_ref, wa_ref, ybg_ref, ga_ref, o_ref, wasc_ref):
    @pl.when(_first_token_tile())
    def _():
        _cast_tile(wa_ref, wasc_ref)

    ya = jnp.dot(y_ref[...], wasc_ref[...], preferred_element_type=F32)
    ga = _sigmoid(ga_ref[...].astype(F32))
    o_ref[...] = (ga * ya + ybg_ref[...].astype(F32)).astype(o_ref.dtype)


def _merge(y, w_a, layer, ybg, proj, off_ga):
    m, ka = y.shape
    n = w_a.shape[2]
    bm = _pick(m, (1024, 512, 256, 128))
    bn = _pick(n, (1024, 512, 256, 128))
    ja = off_ga // bn
    assert off_ga % bn == 0
    blocks = [((bm, ka), BF16), ((bm, bn), BF16), ((bm, bn), BF16), ((bm, bn), BF16)]
    single = [((ka, bn), F32)]
    scratch = [((ka, bn), BF16)]
    return pl.pallas_call(
        _merge_kernel,
        grid=(n // bn, m // bm),
        in_specs=[pl.BlockSpec((bm, ka), lambda j, i: (i, 0)),
                  pl.BlockSpec((None, ka, bn), lambda j, i: (layer, 0, j), pipeline_mode=pl.Buffered(1)),
                  pl.BlockSpec((bm, bn), lambda j, i: (i, j)),
                  pl.BlockSpec((bm, bn), lambda j, i: (i, j + ja))],
        out_specs=pl.BlockSpec((bm, bn), lambda j, i: (i, j)),
        out_shape=jax.ShapeDtypeStruct((m, n), BF16),
        scratch_shapes=[pltpu.VMEM(s, d) for s, d in scratch],
        compiler_params=_cparams(2, blocks, scratch, temps=4 * bm * bn * 4 + 2 * CAST_ROWS * bn * 4,
                                 single=single),
        name="branch_merge",
    )(y, w_a, ybg, proj)


def _ple_kernel(hn_ref, ssq_ref, wg_ref, p_ref, wp_ref, h_ref, g_ref, o_ref, hg_ref, ssqo_ref,
                wgsc_ref, wpsc_ref):
    @pl.when(_first_token_tile())
    def _():
        _cast_tile(wg_ref, wgsc_ref)
        _cast_tile(wp_ref, wpsc_ref)

    rs = _row_rsqrt(ssq_ref, hn_ref.shape[1])
    pg = _sigmoid(jnp.dot(hn_ref[...], wgsc_ref[...], preferred_element_type=F32) * rs)
    e = jnp.dot(p_ref[...].astype(BF16), wpsc_ref[...], preferred_element_type=F32)
    h_new = h_ref[...] + pg * e
    o_ref[...] = h_new
    _emit_scaled(h_new, g_ref, hg_ref, ssqo_ref)


def _ple(hn, ssq, w_gate, p, w_proj, layer, h, g_next):
    m, k = hn.shape
    parts = ssq.shape[0]
    kp = p.shape[2]
    n = w_gate.shape[2]
    bm = _pick(m, (1024, 512, 256, 128))
    bn = _pick(n, (1024, 512, 256, 128))
    blocks = [((bm, k), BF16), ((parts, 1, bm), F32), ((k, bn), F32), ((bm, kp), F32), ((kp, bn), F32),
              ((bm, bn), F32), ((1, bn), F32), ((bm, bn), F32), ((bm, bn), BF16), ((1, bm), F32)]
    scratch = [((k, bn), BF16), ((kp, bn), BF16)]
    return pl.pallas_call(
        _ple_kernel,
        grid=(n // bn, m // bm),
        in_specs=[pl.BlockSpec((bm, k), lambda j, i: (i, 0)),
                  pl.BlockSpec((parts, 1, bm), lambda j, i: (0, 0, i)),
                  pl.BlockSpec((None, k, bn), lambda j, i: (layer, 0, j)),
                  pl.BlockSpec((None, bm, kp), lambda j, i: (layer, i, 0)),
                  pl.BlockSpec((None, kp, bn), lambda j, i: (layer, 0, j)),
                  pl.BlockSpec((bm, bn), lambda j, i: (i, j)),
                  pl.BlockSpec((1, bn), lambda j, i: (0, j))],
        out_specs=[pl.BlockSpec((bm, bn), lambda j, i: (i, j)),
                   pl.BlockSpec((bm, bn), lambda j, i: (i, j)),
                   pl.BlockSpec((None, 1, bm), lambda j, i: (j, 0, i))],
        out_shape=[jax.ShapeDtypeStruct((m, n), F32), jax.ShapeDtypeStruct((m, n), BF16),
                   jax.ShapeDtypeStruct((n // bn, 1, m), F32)],
        scratch_shapes=[pltpu.VMEM(s, d) for s, d in scratch],
        input_output_aliases={5: 0},
        compiler_params=_cparams(2, blocks, scratch, temps=4 * bm * bn * 4 + 2 * CAST_ROWS * bn * 4),
        name="ple",
    )(hn, ssq, w_gate, p, w_proj, h, g_next.reshape(1, n))


def _dtprep_kernel(u_ref, ssq_ref, w_ref, bias_ref, alog_ref, dtrow_ref, wrow_ref, acsrow_ref, acscol_ref,
                   *, nc, groups, r_heads):
    x = lax.dot_general(u_ref[...], w_ref[...].astype(BF16), (((1,), (1,)), ((), ())),
                        preferred_element_type=F32)
    x = x * _row_rsqrt(ssq_ref, u_ref.shape[1]) + bias_ref[...]
    dt = jnp.maximum(x, 0.0) + jnp.log1p(jnp.exp(-jnp.abs(x)))
    adt = dt * (-jnp.exp(alog_ref[...]))
    row = lax.broadcasted_iota(jnp.int32, (CHUNK, LANES), 0)
    for k in range(nc):
        sl = slice(k * CHUNK, (k + 1) * CHUNK)
        acs = adt[sl]
        sh = 1
        while sh < CHUNK:
            acs = acs + jnp.where(row >= sh, pltpu.roll(acs, sh, 0), 0.0)
            sh *= 2
        acs2 = acs * LOG2E
        acsrow_ref[k] = acs2.T
        dtrow_ref[k] = dt[sl].T
        wrow_ref[k] = (dt[sl] * jnp.exp(acs[CHUNK - 1:CHUNK, :] - acs)).T
        for g in range(groups):
            shift = (LANES - g * r_heads) % LANES
            acscol_ref[g, sl, :] = pltpu.roll(acs2, shift, 1) if shift else acs2


def _dtprep(u, ssq, w_in_t, layer, dt0, bias, a_log, groups, r_heads):
    m, k = u.shape
    parts = ssq.shape[0]
    nc = 4 if m % (4 * CHUNK) == 0 else 1
    ts = nc * CHUNK
    nchunks = m // CHUNK
    rows = ((nc, LANES, LANES), F32)
    blocks = [((ts, k), BF16), ((parts, 1, ts), F32), ((LANES, k), F32), rows, rows, rows,
              ((groups, ts, LANES), F32)]
    row_spec = pl.BlockSpec((nc, LANES, LANES), lambda i: (i, 0, 0))
    row_shape = jax.ShapeDtypeStruct((nchunks, LANES, LANES), F32)
    return pl.pallas_call(
        functools.partial(_dtprep_kernel, nc=nc, groups=groups, r_heads=r_heads),
        grid=(m // ts,),
        in_specs=[pl.BlockSpec((ts, k), lambda i: (i, 0)),
                  pl.BlockSpec((parts, 1, ts), lambda i: (0, 0, i)),
                  pl.BlockSpec((None, LANES, k), lambda i: (layer, dt0 // LANES, 0)),
                  pl.BlockSpec((1, LANES), lambda i: (0, 0)),
                  pl.BlockSpec((1, LANES), lambda i: (0, 0))],
        out_specs=[row_spec, row_spec, row_spec,
                   pl.BlockSpec((groups, ts, LANES), lambda i: (0, i, 0))],
        out_shape=[row_shape, row_shape, row_shape,
                   jax.ShapeDtypeStruct((groups, m, LANES), F32)],
        compiler_params=_cparams(1, blocks, temps=8 * ts * LANES * 4 + k * LANES * 2),
        name="dt_prep",
    )(u, ssq, w_in_t, bias, a_log)


def _ssd_kernel(z_ref, xs_ref, b_ref, c_ref, xsp_ref, bp_ref, cp_ref,
                dtr_ref, wr_ref, acr_ref, acc_ref, shift_ref,
                cwx_ref, cwb_ref, cwc_ref, cbx_ref, cbb_ref, cbc_ref, dsk_ref, nw_ref,
                o_ref, st_ref, yt_ref, *, nc, r_heads, kconv):
    g = pl.program_id(1)
    c = pl.program_id(2)
    L = CHUNK
    N = SSD_STATE
    gw = r_heads * HEAD_DIM
    npair = gw // LANES

    @pl.when(c == 0)
    def _():
        st_ref[...] = jnp.zeros(st_ref.shape, F32)

    tri = (lax.broadcasted_iota(jnp.int32, (L, L), 0) >= lax.broadcasted_iota(jnp.int32, (L, L), 1))
    lo = lax.broadcasted_iota(jnp.int32, (L, LANES), 1) < HEAD_DIM
    hi = jnp.logical_not(lo)
    lo_row = lo[0:1]

    def two_chunks(ref, prev_ref, k):
        if k == 0:
            prev = prev_ref[...]
            prev = jnp.where(c > 0, prev, jnp.zeros_like(prev))
            return jnp.concatenate([prev, ref[0:L, :]], axis=0)
        return ref[(k - 1) * L:(k + 1) * L, :]

    def conv_silu(shifted, cur, w_ref, bias_ref):
        acc = None
        for tap in range(kconv - 1):
            term = shifted[tap * L:(tap + 1) * L, :] * w_ref[tap:tap + 1, :]
            acc = term if acc is None else acc + term
        acc = acc + cur.astype(F32) * w_ref[kconv - 1:kconv, :]
        acc = acc + bias_ref[...]
        return _silu(acc)

    for k in range(nc):
        r0 = k * L
        x2 = jnp.concatenate([two_chunks(xs_ref, xsp_ref, k), two_chunks(b_ref, bp_ref, k),
                              two_chunks(c_ref, cp_ref, k)], axis=1)
        shifted = jnp.dot(shift_ref[...], x2, preferred_element_type=F32)
        cur = x2[L:2 * L]
        x = conv_silu(shifted[:, 0:gw], cur[:, 0:gw], cwx_ref, cbx_ref)
        bm = conv_silu(shifted[:, gw:gw + N], cur[:, gw:gw + N], cwb_ref, cbb_ref)
        cm = conv_silu(shifted[:, gw + N:gw + 2 * N], cur[:, gw + N:gw + 2 * N], cwc_ref, cbc_ref)
        cb = lax.dot_general(cm.astype(BF16), bm.astype(BF16), (((1,), (1,)), ((), ())),
                             preferred_element_type=F32)
        bt = bm.T
        acol = acc_ref[0, r0:r0 + L, :]
        cdec = jnp.exp2(acol[L - 1:L, :])
        ssq = jnp.zeros((L, 1), F32)
        for j in range(npair):
            cs = slice(j * LANES, (j + 1) * LANES)
            xp = x[:, cs]
            s_prev = st_ref[:, cs]
            s_new = s_prev * jnp.where(lo_row, cdec[:, 2 * j:2 * j + 1], cdec[:, 2 * j + 1:2 * j + 2])
            y = None
            for hh, keep in ((2 * j, lo), (2 * j + 1, hi)):
                head = pl.ds(g * r_heads + hh, 1)
                arow = acr_ref[k, head, :]
                drow = dtr_ref[k, head, :]
                wrow = wr_ref[k, head, :]
                xm = jnp.where(keep, xp, 0.0).astype(BF16)
                sm = jnp.where(keep, s_prev, 0.0).astype(BF16)
                ab = jnp.broadcast_to(acol[:, hh:hh + 1], (L, L))
                dec = jnp.exp2(jnp.where(tri, ab - arow, -jnp.inf))
                mh = cb * dec * drow
                ch = cm * jnp.exp2(ab)
                lhs = jnp.concatenate([mh, ch], axis=1).astype(BF16)
                rhs = jnp.concatenate([xm, sm], axis=0)
                t = jnp.dot(lhs, rhs, preferred_element_type=F32)
                y = t if y is None else y + t
                bth = (bt * wrow).astype(BF16)
                s_new = s_new + jnp.dot(bth, xm, preferred_element_type=F32)
            st_ref[:, cs] = s_new
            yt = y + dsk_ref[:, cs] * xp
            zt = z_ref[r0:r0 + L, cs].astype(F32)
            yt = yt * _silu(zt)
            ssq = ssq + jnp.sum(yt * yt, axis=-1, keepdims=True)
            yt_ref[:, cs] = yt
        rs = lax.rsqrt(ssq / gw + EPS)
        for j in range(npair):
            cs = slice(j * LANES, (j + 1) * LANES)
            o_ref[r0:r0 + L, cs] = (yt_ref[:, cs] * rs * nw_ref[:, cs]).astype(o_ref.dtype)


def _ssd(proj, dtrow, wrow, acsrow, acscol, conv_w, conv_b, d_skip, norm_w, *, batch, seq, inner, groups):
    m = proj.shape[0]
    gw = inner // groups
    r_heads = gw // HEAD_DIM
    assert gw % LANES == 0 and r_heads % 2 == 0 and SSD_STATE == LANES
    kconv = conv_w.shape[0]
    assert kconv - 1 <= CHUNK
    nc = _pick(seq // CHUNK, (16, 8, 4, 2, 1))
    ts = nc * CHUNK
    nt = seq // ts
    gn = groups * SSD_STATE
    xs_blk, b_blk, c_blk = inner // gw, 2 * inner // SSD_STATE, (2 * inner + gn) // SSD_STATE
    wb_blk, wc_blk = inner // SSD_STATE, (inner + gn) // SSD_STATE
    hp = acsrow.shape[1]
    t_idx = jnp.arange(CHUNK)[None, :, None]
    d_idx = (kconv - 1 - jnp.arange(kconv - 1))[:, None, None]
    s_idx = jnp.arange(2 * CHUNK)[None, None, :]
    shift = (s_idx == CHUNK + t_idx - d_idx).astype(BF16).reshape((kconv - 1) * CHUNK, 2 * CHUNK)
    st_shape = (SSD_STATE, gw)
    blocks = [((ts, gw), BF16), ((ts, gw), BF16), ((ts, SSD_STATE), BF16), ((ts, SSD_STATE), BF16),
              ((CHUNK, gw), BF16), ((CHUNK, SSD_STATE), BF16), ((CHUNK, SSD_STATE), BF16),
              ((nc, hp, LANES), F32), ((nc, hp, LANES), F32), ((nc, hp, LANES), F32),
              ((1, ts, LANES), F32), (shift.shape, BF16),
              ((kconv, gw), F32), ((kconv, SSD_STATE), F32), ((kconv, SSD_STATE), F32),
              ((1, gw), F32), ((1, SSD_STATE), F32), ((1, SSD_STATE), F32), ((1, gw), F32), ((1, gw), F32),
              ((ts, gw), BF16)]
    scratch = [(st_shape, F32), ((CHUNK, gw), F32)]
    row = lambda b, g, c: b * nt + c
    prev = lambda b, g, c: jnp.maximum(row(b, g, c) * nc - 1, 0)
    return pl.pallas_call(
        functools.partial(_ssd_kernel, nc=nc, r_heads=r_heads, kconv=kconv),
        grid=(batch, groups, nt),
        in_specs=[
            pl.BlockSpec((ts, gw), lambda b, g, c: (row(b, g, c), g)),
            pl.BlockSpec((ts, gw), lambda b, g, c: (row(b, g, c), xs_blk + g)),
            pl.BlockSpec((ts, SSD_STATE), lambda b, g, c: (row(b, g, c), b_blk + g)),
            pl.BlockSpec((ts, SSD_STATE), lambda b, g, c: (row(b, g, c), c_blk + g)),
            pl.BlockSpec((CHUNK, gw), lambda b, g, c: (prev(b, g, c), xs_blk + g)),
            pl.BlockSpec((CHUNK, SSD_STATE), lambda b, g, c: (prev(b, g, c), b_blk + g)),
            pl.BlockSpec((CHUNK, SSD_STATE), lambda b, g, c: (prev(b, g, c), c_blk + g)),
            pl.BlockSpec((nc, hp, LANES), lambda b, g, c: (row(b, g, c), 0, 0)),
            pl.BlockSpec((nc, hp, LANES), lambda b, g, c: (row(b, g, c), 0, 0)),
            pl.BlockSpec((nc, hp, LANES), lambda b, g, c: (row(b, g, c), 0, 0)),
            pl.BlockSpec((1, ts, LANES), lambda b, g, c: (g, row(b, g, c), 0)),
            pl.BlockSpec(shift.shape, lambda b, g, c: (0, 0)),
            pl.BlockSpec((kconv, gw), lambda b, g, c: (0, g)),
            pl.BlockSpec((kconv, SSD_STATE), lambda b, g, c: (0, wb_blk + g)),
            pl.BlockSpec((kconv, SSD_STATE), lambda b, g, c: (0, wc_blk + g)),
            pl.BlockSpec((1, gw), lambda b, g, c: (0, g)),
            pl.BlockSpec((1, SSD_STATE), lambda b, g, c: (0, wb_blk + g)),
            pl.BlockSpec((1, SSD_STATE), lambda b, g, c: (0, wc_blk + g)),
            pl.BlockSpec((1, gw), lambda b, g, c: (0, g)),
            pl.BlockSpec((1, gw), lambda b, g, c: (0, g)),
        ],
        out_specs=pl.BlockSpec((ts, gw), lambda b, g, c: (row(b, g, c), g)),
        out_shape=jax.ShapeDtypeStruct((m, inner), BF16),
        scratch_shapes=[pltpu.VMEM(s, d) for s, d in scratch],
        compiler_params=_cparams(3, blocks, scratch, temps=8 << 20),
        name="ssd",
    )(proj, proj, proj, proj, proj, proj, proj, dtrow, wrow, acsrow, acscol, shift,
      conv_w, conv_w, conv_w, conv_b, conv_b, conv_b, d_skip, norm_w)


def _sconv_kernel(gb_ref, gc_ref, xt_ref, gate_ref, w_ref, wo_ref, o_ref, pe_ref, ysc_ref, wsc_ref,
                  *, ts, kconv, cw):
    halo = SUBLANES

    @pl.when(jnp.logical_and(pl.program_id(0) == 0, pl.program_id(1) == 0))
    def _():
        _cast_tile(wo_ref, wsc_ref)

    @pl.when(pl.program_id(1) == 0)
    def _():
        pe_ref[0:halo, :] = jnp.zeros((halo, pe_ref.shape[1]), F32)

    pe_ref[halo:halo + ts, :] = gb_ref[...].astype(F32) * xt_ref[...].astype(F32)
    width = pe_ref.shape[1]
    for c0 in range(0, width, cw):
        acc = None
        for k in range(kconv):
            r = halo - (kconv - 1) + k
            term = pe_ref[r:r + ts, c0:c0 + cw] * w_ref[k:k + 1, c0:c0 + cw]
            acc = term if acc is None else acc + term
        ysc_ref[:, c0:c0 + cw] = (gc_ref[:, c0:c0 + cw].astype(F32) * acc).astype(BF16)
    pe_ref[0:halo, :] = pe_ref[ts:ts + halo, :]
    yb = jnp.dot(ysc_ref[...], wsc_ref[...], preferred_element_type=F32)
    o_ref[...] = (_sigmoid(gate_ref[...].astype(F32)) * yb).astype(o_ref.dtype)


def _sconv(proj, conv_w, w_out, layer, *, batch, seq, width, off_b, off_c, off_x, off_gate):
    m = proj.shape[0]
    kconv = conv_w.shape[0]
    n = w_out.shape[2]
    ts = _pick(seq, (512, 256, 128))
    nt = seq // ts
    assert off_b % width == 0 and off_c % width == 0 and off_x % width == 0 and kconv - 1 <= SUBLANES
    assert off_gate % n == 0
    jb, jc, jx, jg = off_b // width, off_c // width, off_x // width, off_gate // n
    cw = _pick(width, (512, 256, 128))
    pe_shape = (SUBLANES + ts, width)
    blocks = [((ts, width), BF16)] * 3 + [((ts, n), BF16), ((kconv, width), F32), ((ts, n), BF16)]
    single = [((width, n), F32)]
    scratch = [(pe_shape, F32), ((ts, width), BF16), ((width, n), BF16)]
    return pl.pallas_call(
        functools.partial(_sconv_kernel, ts=ts, kconv=kconv, cw=cw),
        grid=(batch, nt),
        in_specs=[pl.BlockSpec((ts, width), lambda b, c: (b * nt + c, jb)),
                  pl.BlockSpec((ts, width), lambda b, c: (b * nt + c, jc)),
                  pl.BlockSpec((ts, width), lambda b, c: (b * nt + c, jx)),
                  pl.BlockSpec((ts, n), lambda b, c: (b * nt + c, jg)),
                  pl.BlockSpec((kconv, width), lambda b, c: (0, 0)),
                  pl.BlockSpec((None, width, n), lambda b, c: (layer, 0, 0), pipeline_mode=pl.Buffered(1))],
        out_specs=pl.BlockSpec((ts, n), lambda b, c: (b * nt + c, 0)),
        out_shape=jax.ShapeDtypeStruct((m, n), BF16),
        scratch_shapes=[pltpu.VMEM(s, d) for s, d in scratch],
        compiler_params=_cparams(2, blocks, scratch, temps=(4 << 20) + 2 * ts * n * 4, single=single),
        name="short_conv",
    )(proj, proj, proj, proj, conv_w, w_out)


def kernel(x, p, norm_mix, w_in, ssd_conv_w, ssd_conv_b, ssd_dt_bias, ssd_a_log, ssd_d, ssd_norm,
           ssd_out, sc_conv_w, sc_out, w_o, norm_ffn, w_gate_up, w_down, norm_ple, ple_gate,
           ple_proj, norm_final):
    batch, seq, d = x.shape
    depth = w_in.shape[0]
    m = batch * seq
    heads = ssd_a_log.shape[1]
    inner = heads * HEAD_DIM
    groups = SSD_GROUPS
    r_heads = heads // groups
    xbc = ssd_conv_w.shape[2]
    scw = sc_conv_w.shape[2]
    assert xbc == inner + 2 * groups * SSD_STATE and heads <= LANES
    dt0 = inner + xbc
    off_scb = dt0
    off_scc, off_scx = off_scb + scw, off_scb + 2 * scw
    off_ga = off_scb + 3 * scw
    off_gb = off_ga + d
    hpad = LANES - heads
    p2 = p.reshape(depth, m, p.shape[-1])
    w_in_t = jnp.swapaxes(w_in, 1, 2)

    h = x.reshape(m, d)
    u, u_ssq = _prenorm(h, norm_mix[0])
    for i in range(depth):
        dt_bias = jnp.pad(ssd_dt_bias[i], (0, hpad)).reshape(1, LANES)
        a_log = jnp.pad(ssd_a_log[i], (0, hpad)).reshape(1, LANES)
        d_skip = jnp.repeat(ssd_d[i], HEAD_DIM).reshape(1, inner)

        proj = _inproj(u, u_ssq, w_in_t, i, dt0, heads)
        dtrow, wrow, acsrow, acscol = _dtprep(u, u_ssq, w_in_t, i, dt0, dt_bias, a_log, groups, r_heads)
        y = _ssd(proj, dtrow, wrow, acsrow, acscol, ssd_conv_w[i], ssd_conv_b[i].reshape(1, xbc), d_skip,
                 ssd_norm[i].reshape(1, inner), batch=batch, seq=seq, inner=inner, groups=groups)
        ybg = _sconv(proj, sc_conv_w[i], sc_out, i, batch=batch, seq=seq, width=scw,
                     off_b=off_scb, off_c=off_scc, off_x=off_scx, off_gate=off_gb)
        merged = _merge(y, ssd_out, i, ybg, proj, off_ga)
        h, v, v_ssq = _matmul_residual(merged, w_o, i, h, norm_ffn[i], "w_o_residual", in_place=i > 0)
        act = _swiglu(v, v_ssq, w_gate_up, i)
        h, hn, hn_ssq = _matmul_residual(act, w_down, i, h, norm_ple[i], "w_down_residual")
        g_next = norm_mix[i + 1] if i + 1 < depth else norm_final
        h, u, u_ssq = _ple(hn, hn_ssq, ple_gate, p2, ple_proj, i, h, g_next)
    out = _rmsnorm(h, norm_final, F32)
    return out.reshape(batch, seq, d)
```

```python
import functools

import jax
import jax.numpy as jnp
from jax import lax
from jax.experimental import pallas as pl
from jax.experimental.pallas import tpu as pltpu

F32 = jnp.float32
BF16 = jnp.bfloat16

EPS = 1e-6
HEAD_DIM = 64
SSD_GROUPS = 8
SSD_STATE = 128
CHUNK = 128
LANES = 128
SUBLANES = 8
CAST_ROWS = 256
LOG2E = 1.4426950408889634
VMEM_BYTES_V7X = 64 * 1024 * 1024
VMEM_CAP = VMEM_BYTES_V7X - 3 * 1024 * 1024


def _nbytes(shape, dtype):
    n = 1
    for s in shape:
        n *= s
    return n * jnp.dtype(dtype).itemsize


def _cparams(ngrid, blocks, scratch=(), temps=0, single=()):
    need = (2 * sum(_nbytes(s, d) for s, d in blocks) + sum(_nbytes(s, d) for s, d in single)
            + sum(_nbytes(s, d) for s, d in scratch) + temps + (4 << 20))
    return pltpu.CompilerParams(dimension_semantics=("arbitrary",) * ngrid,
                                vmem_limit_bytes=int(min(max(need, 16 << 20), VMEM_CAP)))


def _pick(n, candidates):
    for c in candidates:
        if n % c == 0:
            return c
    raise ValueError(f"no block size in {candidates} divides {n}")


def _cast_tile(w_ref, wsc_ref):
    k = w_ref.shape[0]
    rc = _pick(k, (CAST_ROWS, LANES, SUBLANES))
    for r in range(0, k, rc):
        wsc_ref[r:r + rc, :] = w_ref[r:r + rc, :].astype(BF16)


def _first_token_tile():
    return pl.program_id(1) == 0


def _sigmoid(x):
    return 0.5 * jnp.tanh(0.5 * x) + 0.5


def _silu(x):
    h = 0.5 * x
    return h * jnp.tanh(h) + h


def _emit_scaled(h_new, g_ref, hg_ref, ssq_ref):
    hg_ref[...] = (h_new * g_ref[...]).astype(BF16)
    col = jnp.sum(h_new * h_new, axis=-1, keepdims=True)
    ssq_ref[...] = jnp.transpose(jnp.broadcast_to(col, (col.shape[0], LANES)))[0:1, :]


def _row_rsqrt(ssq_ref, d):
    s = ssq_ref[0]
    for q in range(1, ssq_ref.shape[0]):
        s = s + ssq_ref[q]
    row = lax.rsqrt(s / d + EPS)
    return jnp.transpose(jnp.broadcast_to(row, (LANES, row.shape[1])))[:, 0:1]


def _prenorm_kernel(x_ref, g_ref, hg_ref, ssq_ref):
    _emit_scaled(x_ref[...], g_ref, hg_ref, ssq_ref.at[0])


def _prenorm(x, g):
    m, d = x.shape
    bm = _pick(m, (512, 256, 128))
    blocks = [((bm, d), F32), ((1, d), F32), ((bm, d), BF16), ((1, 1, bm), F32)]
    return pl.pallas_call(
        _prenorm_kernel,
        grid=(m // bm,),
        in_specs=[pl.BlockSpec((bm, d), lambda i: (i, 0)),
                  pl.BlockSpec((1, d), lambda i: (0, 0))],
        out_specs=[pl.BlockSpec((bm, d), lambda i: (i, 0)),
                   pl.BlockSpec((1, 1, bm), lambda i: (0, 0, i))],
        out_shape=[jax.ShapeDtypeStruct((m, d), BF16), jax.ShapeDtypeStruct((1, 1, m), F32)],
        compiler_params=_cparams(1, blocks, temps=2 * bm * d * 4),
        name="prenorm",
    )(x, g.reshape(1, d))


def _rmsnorm_kernel(x_ref, g_ref, o_ref):
    x = x_ref[...]
    ms = jnp.mean(x * x, axis=-1, keepdims=True)
    o_ref[...] = (x * lax.rsqrt(ms + EPS) * g_ref[...]).astype(o_ref.dtype)


def _rmsnorm(x, g, out_dtype):
    m, d = x.shape
    bm = _pick(m, (512, 256, 128))
    blocks = [((bm, d), F32), ((1, d), F32), ((bm, d), out_dtype)]
    return pl.pallas_call(
        _rmsnorm_kernel,
        grid=(m // bm,),
        in_specs=[pl.BlockSpec((bm, d), lambda i: (i, 0)),
                  pl.BlockSpec((1, d), lambda i: (0, 0))],
        out_specs=pl.BlockSpec((bm, d), lambda i: (i, 0)),
        out_shape=jax.ShapeDtypeStruct((m, d), out_dtype),
        compiler_params=_cparams(1, blocks, temps=2 * bm * d * 4),
        name="rmsnorm",
    )(x, g.reshape(1, d))


def _inproj_kernel(a_ref, ssq_ref, w_ref, wn_ref, o_ref, wsc_ref, *, n_aligned, skip):
    j = pl.program_id(0)
    bn = w_ref.shape[0]
    rc = _pick(bn, (CAST_ROWS, LANES))

    @pl.when(jnp.logical_and(_first_token_tile(), j < n_aligned))
    def _():
        _cast_tile(w_ref, wsc_ref)

    @pl.when(jnp.logical_and(_first_token_tile(), j >= n_aligned))
    def _():
        for r in range(0, bn - rc, rc):
            wsc_ref[r:r + rc, :] = w_ref[r + skip:r + skip + rc, :].astype(BF16)
        wsc_ref[bn - rc:bn - skip, :] = w_ref[bn - rc + skip:bn, :].astype(BF16)
        wsc_ref[bn - skip:bn, :] = wn_ref[...].astype(BF16)

    rs = _row_rsqrt(ssq_ref, a_ref.shape[1])
    hm = a_ref.shape[0] // 2
    for r in (0, hm):
        acc = lax.dot_general(a_ref[r:r + hm, :], wsc_ref[...], (((1,), (1,)), ((), ())),
                              preferred_element_type=F32)
        o_ref[r:r + hm, :] = (acc * rs[r:r + hm]).astype(o_ref.dtype)


def _inproj(u, ssq, w_in_t, layer, dt0, heads):
    m, k = u.shape
    parts = ssq.shape[0]
    d_in = w_in_t.shape[1]
    n = d_in - heads
    bm = _pick(m, (2048, 1024, 512, 256, 128))
    bn = next(c for c in (1024, 512, 256, 128) if dt0 % c == 0 and (n - dt0) % c == 0)
    assert heads % (2 * SUBLANES) == 0 and bn % heads == 0 and d_in % heads == 0 and heads < LANES
    n_aligned = dt0 // bn
    blocks = [((bm, k), BF16), ((parts, 1, bm), F32), ((bn, k), F32), ((heads, k), F32), ((bm, bn), BF16)]
    scratch = [((bn, k), BF16)]
    return pl.pallas_call(
        functools.partial(_inproj_kernel, n_aligned=n_aligned, skip=heads),
        grid=(n // bn, m // bm),
        in_specs=[pl.BlockSpec((bm, k), lambda j, i: (i, 0)),
                  pl.BlockSpec((parts, 1, bm), lambda j, i: (0, 0, i)),
                  pl.BlockSpec((None, bn, k), lambda j, i: (layer, j, 0)),
                  pl.BlockSpec((None, heads, k), lambda j, i: (layer, (j + 1) * (bn // heads), 0))],
        out_specs=pl.BlockSpec((bm, bn), lambda j, i: (i, j)),
        out_shape=jax.ShapeDtypeStruct((m, n), BF16),
        scratch_shapes=[pltpu.VMEM(s, d) for s, d in scratch],
        compiler_params=_cparams(2, blocks, scratch, temps=bm * bn * 4 + 2 * CAST_ROWS * k * 4),
        name="in_proj",
    )(u, ssq, w_in_t, w_in_t)


def _mm_res_kernel(a_ref, w_ref, h_ref, g_ref, o_ref, hg_ref, ssq_ref, wsc_ref):
    @pl.when(_first_token_tile())
    def _():
        _cast_tile(w_ref, wsc_ref)

    h_new = h_ref[...] + jnp.dot(a_ref[...], wsc_ref[...], preferred_element_type=F32)
    o_ref[...] = h_new
    _emit_scaled(h_new, g_ref, hg_ref, ssq_ref)


def _matmul_residual(a, w, layer, h, g_next, name, in_place=True):
    m, k = a.shape
    n = w.shape[2]
    big_k = k > 2048
    bm = _pick(m, (512, 256, 128))
    bn = _pick(n, (1024, 512, 256, 128)) if big_k else _pick(n, (2048, 1024, 512, 256, 128))
    blocks = [((bm, k), BF16), ((bm, bn), F32), ((1, bn), F32),
              ((bm, bn), F32), ((bm, bn), BF16), ((1, bm), F32)]
    single = [((k, bn), F32)]
    scratch = [((k, bn), BF16)]
    return pl.pallas_call(
        _mm_res_kernel,
        grid=(n // bn, m // bm),
        in_specs=[pl.BlockSpec((bm, k), lambda j, i: (i, 0)),
                  pl.BlockSpec((None, k, bn), lambda j, i: (layer, 0, j), pipeline_mode=pl.Buffered(1)),
                  pl.BlockSpec((bm, bn), lambda j, i: (i, j)),
                  pl.BlockSpec((1, bn), lambda j, i: (0, j))],
        out_specs=[pl.BlockSpec((bm, bn), lambda j, i: (i, j)),
                   pl.BlockSpec((bm, bn), lambda j, i: (i, j)),
                   pl.BlockSpec((None, 1, bm), lambda j, i: (j, 0, i))],
        out_shape=[jax.ShapeDtypeStruct((m, n), F32), jax.ShapeDtypeStruct((m, n), BF16),
                   jax.ShapeDtypeStruct((n // bn, 1, m), F32)],
        scratch_shapes=[pltpu.VMEM(s, d) for s, d in scratch],
        input_output_aliases={2: 0} if in_place else {},
        compiler_params=_cparams(2, blocks, scratch, temps=2 * bm * bn * 4 + 2 * CAST_ROWS * bn * 4,
                                 single=single),
        name=name,
    )(a, w, h, g_next.reshape(1, n))


def _swiglu_kernel(v_ref, ssq_ref, wg_ref, wu_ref, o_ref, wgsc_ref, wusc_ref):
    @pl.when(_first_token_tile())
    def _():
        _cast_tile(wg_ref, wgsc_ref)
        _cast_tile(wu_ref, wusc_ref)

    rs_all = _row_rsqrt(ssq_ref, v_ref.shape[1])
    hm = v_ref.shape[0] // 2
    for r in (0, hm):
        v = v_ref[r:r + hm, :]
        rs = rs_all[r:r + hm]
        gate = jnp.dot(v, wgsc_ref[...], preferred_element_type=F32) * rs
        up = jnp.dot(v, wusc_ref[...], preferred_element_type=F32) * rs
        o_ref[r:r + hm, :] = (_silu(gate) * up).astype(o_ref.dtype)


def _swiglu(v, ssq, w_gate_up, layer):
    m, k = v.shape
    parts = ssq.shape[0]
    d_ff = w_gate_up.shape[2] // 2
    bm = _pick(m, (2048, 1024, 512, 256, 128))
    bn = _pick(d_ff, (512, 256, 128))
    nb = d_ff // bn
    blocks = [((bm, k), BF16), ((parts, 1, bm), F32), ((k, bn), F32), ((k, bn), F32), ((bm, bn), BF16)]
    scratch = [((k, bn), BF16), ((k, bn), BF16)]
    return pl.pallas_call(
        _swiglu_kernel,
        grid=(nb, m // bm),
        in_specs=[pl.BlockSpec((bm, k), lambda j, i: (i, 0)),
                  pl.BlockSpec((parts, 1, bm), lambda j, i: (0, 0, i)),
                  pl.BlockSpec((None, k, bn), lambda j, i: (layer, 0, j)),
                  pl.BlockSpec((None, k, bn), lambda j, i: (layer, 0, j + nb))],
        out_specs=pl.BlockSpec((bm, bn), lambda j, i: (i, j)),
        out_shape=jax.ShapeDtypeStruct((m, d_ff), BF16),
        scratch_shapes=[pltpu.VMEM(s, d) for s, d in scratch],
        compiler_params=_cparams(2, blocks, scratch, temps=3 * bm * bn * 4 + 2 * CAST_ROWS * bn * 4),
        name="swiglu_up",
    )(v, ssq, w_gate_up, w_gate_up)


def _merge_kernel(y_ref, wa_ref, ybg_ref, ga_ref, o_ref, wasc_ref):
    @pl.when(_first_token_tile())
    def _():
        _cast_tile(wa_ref, wasc_ref)

    ya = jnp.dot(y_ref[...], wasc_ref[...], preferred_element_type=F32)
    ga = _sigmoid(ga_ref[...].astype(F32))
    o_ref[...] = (ga * ya + ybg_ref[...].astype(F32)).astype(o_ref.dtype)


def _merge(y, w_a, layer, ybg, proj, off_ga):
    m, ka = y.shape
    n = w_a.shape[2]
    bm = _pick(m, (1024, 512, 256, 128))
    bn = _pick(n, (1024, 512, 256, 128))
    ja = off_ga // bn
    assert off_ga % bn == 0
    blocks = [((bm, ka), BF16), ((bm, bn), BF16), ((bm, bn), BF16), ((bm, bn), BF16)]
    single = [((ka, bn), F32)]
    scratch = [((ka, bn), BF16)]
    return pl.pallas_call(
        _merge_kernel,
        grid=(n // bn, m // bm),
        in_specs=[pl.BlockSpec((bm, ka), lambda j, i: (i, 0)),
                  pl.BlockSpec((None, ka, bn), lambda j, i: (layer, 0, j), pipeline_mode=pl.Buffered(1)),
                  pl.BlockSpec((bm, bn), lambda j, i: (i, j)),
                  pl.BlockSpec((bm, bn), lambda j, i: (i, j + ja))],
        out_specs=pl.BlockSpec((bm, bn), lambda j, i: (i, j)),
        out_shape=jax.ShapeDtypeStruct((m, n), BF16),
        scratch_shapes=[pltpu.VMEM(s, d) for s, d in scratch],
        compiler_params=_cparams(2, blocks, scratch, temps=4 * bm * bn * 4 + 2 * CAST_ROWS * bn * 4,
                                 single=single),
        name="branch_merge",
    )(y, w_a, ybg, proj)


def _ple_kernel(hn_ref, ssq_ref, wg_ref, p_ref, wp_ref, h_ref, g_ref, o_ref, hg_ref, ssqo_ref,
                wgsc_ref, wpsc_ref):
    @pl.when(_first_token_tile())
    def _():
        _cast_tile(wg_ref, wgsc_ref)
        _cast_tile(wp_ref, wpsc_ref)

    rs = _row_rsqrt(ssq_ref, hn_ref.shape[1])
    pg = _sigmoid(jnp.dot(hn_ref[...], wgsc_ref[...], preferred_element_type=F32) * rs)
    e = jnp.dot(p_ref[...].astype(BF16), wpsc_ref[...], preferred_element_type=F32)
    h_new = h_ref[...] + pg * e
    o_ref[...] = h_new
    _emit_scaled(h_new, g_ref, hg_ref, ssqo_ref)


def _ple(hn, ssq, w_gate, p, w_proj, layer, h, g_next):
    m, k = hn.shape
    parts = ssq.shape[0]
    kp = p.shape[2]
    n = w_gate.shape[2]
    bm = _pick(m, (512, 256, 128))
    bn = _pick(n, (2048, 1024, 512, 256, 128))
    blocks = [((bm, k), BF16), ((parts, 1, bm), F32), ((bm, kp), F32),
              ((bm, bn), F32), ((1, bn), F32), ((bm, bn), F32), ((bm, bn), BF16), ((1, bm), F32)]
    single = [((k, bn), F32), ((kp, bn), F32)]
    scratch = [((k, bn), BF16), ((kp, bn), BF16)]
    return pl.pallas_call(
        _ple_kernel,
        grid=(n // bn, m // bm),
        in_specs=[pl.BlockSpec((bm, k), lambda j, i: (i, 0)),
                  pl.BlockSpec((parts, 1, bm), lambda j, i: (0, 0, i)),
                  pl.BlockSpec((None, k, bn), lambda j, i: (layer, 0, j), pipeline_mode=pl.Buffered(1)),
                  pl.BlockSpec((None, bm, kp), lambda j, i: (layer, i, 0)),
                  pl.BlockSpec((None, kp, bn), lambda j, i: (layer, 0, j), pipeline_mode=pl.Buffered(1)),
                  pl.BlockSpec((bm, bn), lambda j, i: (i, j)),
                  pl.BlockSpec((1, bn), lambda j, i: (0, j))],
        out_specs=[pl.BlockSpec((bm, bn), lambda j, i: (i, j)),
                   pl.BlockSpec((bm, bn), lambda j, i: (i, j)),
                   pl.BlockSpec((None, 1, bm), lambda j, i: (j, 0, i))],
        out_shape=[jax.ShapeDtypeStruct((m, n), F32), jax.ShapeDtypeStruct((m, n), BF16),
                   jax.ShapeDtypeStruct((n // bn, 1, m), F32)],
        scratch_shapes=[pltpu.VMEM(s, d) for s, d in scratch],
        input_output_aliases={5: 0},
        compiler_params=_cparams(2, blocks, scratch, temps=4 * bm * bn * 4 + 2 * CAST_ROWS * bn * 4,
                                 single=single),
        name="ple",
    )(hn, ssq, w_gate, p, w_proj, h, g_next.reshape(1, n))


def _dtprep_kernel(u_ref, ssq_ref, w_ref, bias_ref, alog_ref, dtrow_ref, wrow_ref, acsrow_ref, acscol_ref,
                   *, nc, groups, r_heads):
    x = lax.dot_general(u_ref[...], w_ref[...].astype(BF16), (((1,), (1,)), ((), ())),
                        preferred_element_type=F32)
    x = x * _row_rsqrt(ssq_ref, u_ref.shape[1]) + bias_ref[...]
    dt = jnp.maximum(x, 0.0) + jnp.log1p(jnp.exp(-jnp.abs(x)))
    adt = dt * (-jnp.exp(alog_ref[...]))
    row = lax.broadcasted_iota(jnp.int32, (CHUNK, LANES), 0)
    for k in range(nc):
        sl = slice(k * CHUNK, (k + 1) * CHUNK)
        acs = adt[sl]
        sh = 1
        while sh < CHUNK:
            acs = acs + jnp.where(row >= sh, pltpu.roll(acs, sh, 0), 0.0)
            sh *= 2
        acs2 = acs * LOG2E
        acsrow_ref[k] = acs2.T
        dtrow_ref[k] = dt[sl].T
        wrow_ref[k] = (dt[sl] * jnp.exp(acs[CHUNK - 1:CHUNK, :] - acs)).T
        for g in range(groups):
            shift = (LANES - g * r_heads) % LANES
            acscol_ref[g, sl, :] = pltpu.roll(acs2, shift, 1) if shift else acs2


def _dtprep(u, ssq, w_in_t, layer, dt0, bias, a_log, groups, r_heads):
    m, k = u.shape
    parts = ssq.shape[0]
    nc = 4 if m % (4 * CHUNK) == 0 else 1
    ts = nc * CHUNK
    nchunks = m // CHUNK
    rows = ((nc, LANES, LANES), F32)
    blocks = [((ts, k), BF16), ((parts, 1, ts), F32), ((LANES, k), F32), rows, rows, rows,
              ((groups, ts, LANES), F32)]
    row_spec = pl.BlockSpec((nc, LANES, LANES), lambda i: (i, 0, 0))
    row_shape = jax.ShapeDtypeStruct((nchunks, LANES, LANES), F32)
    return pl.pallas_call(
        functools.partial(_dtprep_kernel, nc=nc, groups=groups, r_heads=r_heads),
        grid=(m // ts,),
        in_specs=[pl.BlockSpec((ts, k), lambda i: (i, 0)),
                  pl.BlockSpec((parts, 1, ts), lambda i: (0, 0, i)),
                  pl.BlockSpec((None, LANES, k), lambda i: (layer, dt0 // LANES, 0)),
                  pl.BlockSpec((1, LANES), lambda i: (0, 0)),
                  pl.BlockSpec((1, LANES), lambda i: (0, 0))],
        out_specs=[row_spec, row_spec, row_spec,
                   pl.BlockSpec((groups, ts, LANES), lambda i: (0, i, 0))],
        out_shape=[row_shape, row_shape, row_shape,
                   jax.ShapeDtypeStruct((groups, m, LANES), F32)],
        compiler_params=_cparams(1, blocks, temps=8 * ts * LANES * 4 + k * LANES * 2),
        name="dt_prep",
    )(u, ssq, w_in_t, bias, a_log)


def _ssd_kernel(z_ref, xs_ref, b_ref, c_ref, xsp_ref, bp_ref, cp_ref,
                dtr_ref, wr_ref, acr_ref, acc_ref, shift_ref,
                cwx_ref, cwb_ref, cwc_ref, cbx_ref, cbb_ref, cbc_ref, dsk_ref, nw_ref,
                o_ref, st_ref, yt_ref, *, nc, r_heads, kconv):
    g = pl.program_id(1)
    c = pl.program_id(2)
    L = CHUNK
    N = SSD_STATE
    gw = r_heads * HEAD_DIM
    npair = gw // LANES

    @pl.when(c == 0)
    def _():
        st_ref[...] = jnp.zeros(st_ref.shape, F32)

    tri = (lax.broadcasted_iota(jnp.int32, (L, L), 0) >= lax.broadcasted_iota(jnp.int32, (L, L), 1))
    lo = lax.broadcasted_iota(jnp.int32, (L, LANES), 1) < HEAD_DIM
    hi = jnp.logical_not(lo)
    lo_row = lo[0:1]

    def two_chunks(ref, prev_ref, k):
        if k == 0:
            prev = prev_ref[...]
            prev = jnp.where(c > 0, prev, jnp.zeros_like(prev))
            return jnp.concatenate([prev, ref[0:L, :]], axis=0)
        return ref[(k - 1) * L:(k + 1) * L, :]

    def conv_silu(shifted, cur, w_ref, bias_ref):
        acc = None
        for tap in range(kconv - 1):
            term = shifted[tap * L:(tap + 1) * L, :] * w_ref[tap:tap + 1, :]
            acc = term if acc is None else acc + term
        acc = acc + cur.astype(F32) * w_ref[kconv - 1:kconv, :]
        acc = acc + bias_ref[...]
        return _silu(acc)

    for k in range(nc):
        r0 = k * L
        x2 = jnp.concatenate([two_chunks(xs_ref, xsp_ref, k), two_chunks(b_ref, bp_ref, k),
                              two_chunks(c_ref, cp_ref, k)], axis=1)
        shifted = jnp.dot(shift_ref[...], x2, preferred_element_type=F32)
        cur = x2[L:2 * L]
        x = conv_silu(shifted[:, 0:gw], cur[:, 0:gw], cwx_ref, cbx_ref)
        bm = conv_silu(shifted[:, gw:gw + N], cur[:, gw:gw + N], cwb_ref, cbb_ref)
        cm = conv_silu(shifted[:, gw + N:gw + 2 * N], cur[:, gw + N:gw + 2 * N], cwc_ref, cbc_ref)
        cb = lax.dot_general(cm.astype(BF16), bm.astype(BF16), (((1,), (1,)), ((), ())),
                             preferred_element_type=F32)
        bt = bm.T
        acol = acc_ref[0, r0:r0 + L, :]
        cdec = jnp.exp2(acol[L - 1:L, :])
        ssq = jnp.zeros((L, 1), F32)
        for j in range(npair):
            cs = slice(j * LANES, (j + 1) * LANES)
            xp = x[:, cs]
            s_prev = st_ref[:, cs]
            s_new = s_prev * jnp.where(lo_row, cdec[:, 2 * j:2 * j + 1], cdec[:, 2 * j + 1:2 * j + 2])
            y = None
            for hh, keep in ((2 * j, lo), (2 * j + 1, hi)):
                head = pl.ds(g * r_heads + hh, 1)
                arow = acr_ref[k, head, :]
                drow = dtr_ref[k, head, :]
                wrow = wr_ref[k, head, :]
                xm = jnp.where(keep, xp, 0.0).astype(BF16)
                sm = jnp.where(keep, s_prev, 0.0).astype(BF16)
                ab = jnp.broadcast_to(acol[:, hh:hh + 1], (L, L))
                dec = jnp.exp2(jnp.where(tri, ab - arow, -jnp.inf))
                mh = cb * dec * drow
                ch = cm * jnp.exp2(ab)
                lhs = jnp.concatenate([mh, ch], axis=1).astype(BF16)
                rhs = jnp.concatenate([xm, sm], axis=0)
                t = jnp.dot(lhs, rhs, preferred_element_type=F32)
                y = t if y is None else y + t
                bth = (bt * wrow).astype(BF16)
                s_new = s_new + jnp.dot(bth, xm, preferred_element_type=F32)
            st_ref[:, cs] = s_new
            yt = y + dsk_ref[:, cs] * xp
            zt = z_ref[r0:r0 + L, cs].astype(F32)
            yt = yt * _silu(zt)
            ssq = ssq + jnp.sum(yt * yt, axis=-1, keepdims=True)
            yt_ref[:, cs] = yt
        rs = lax.rsqrt(ssq / gw + EPS)
        for j in range(npair):
            cs = slice(j * LANES, (j + 1) * LANES)
            o_ref[r0:r0 + L, cs] = (yt_ref[:, cs] * rs * nw_ref[:, cs]).astype(o_ref.dtype)


def _ssd(proj, dtrow, wrow, acsrow, acscol, conv_w, conv_b, d_skip, norm_w, *, batch, seq, inner, groups):
    m = proj.shape[0]
    gw = inner // groups
    r_heads = gw // HEAD_DIM
    assert gw % LANES == 0 and r_heads % 2 == 0 and SSD_STATE == LANES
    kconv = conv_w.shape[0]
    assert kconv - 1 <= CHUNK
    nc = _pick(seq // CHUNK, (16, 8, 4, 2, 1))
    ts = nc * CHUNK
    nt = seq // ts
    gn = groups * SSD_STATE
    xs_blk, b_blk, c_blk = inner // gw, 2 * inner // SSD_STATE, (2 * inner + gn) // SSD_STATE
    wb_blk, wc_blk = inner // SSD_STATE, (inner + gn) // SSD_STATE
    hp = acsrow.shape[1]
    t_idx = jnp.arange(CHUNK)[None, :, None]
    d_idx = (kconv - 1 - jnp.arange(kconv - 1))[:, None, None]
    s_idx = jnp.arange(2 * CHUNK)[None, None, :]
    shift = (s_idx == CHUNK + t_idx - d_idx).astype(BF16).reshape((kconv - 1) * CHUNK, 2 * CHUNK)
    st_shape = (SSD_STATE, gw)
    blocks = [((ts, gw), BF16), ((ts, gw), BF16), ((ts, SSD_STATE), BF16), ((ts, SSD_STATE), BF16),
              ((CHUNK, gw), BF16), ((CHUNK, SSD_STATE), BF16), ((CHUNK, SSD_STATE), BF16),
              ((nc, hp, LANES), F32), ((nc, hp, LANES), F32), ((nc, hp, LANES), F32),
              ((1, ts, LANES), F32), (shift.shape, BF16),
              ((kconv, gw), F32), ((kconv, SSD_STATE), F32), ((kconv, SSD_STATE), F32),
              ((1, gw), F32), ((1, SSD_STATE), F32), ((1, SSD_STATE), F32), ((1, gw), F32), ((1, gw), F32),
              ((ts, gw), BF16)]
    scratch = [(st_shape, F32), ((CHUNK, gw), F32)]
    row = lambda b, g, c: b * nt + c
    prev = lambda b, g, c: jnp.maximum(row(b, g, c) * nc - 1, 0)
    return pl.pallas_call(
        functools.partial(_ssd_kernel, nc=nc, r_heads=r_heads, kconv=kconv),
        grid=(batch, groups, nt),
        in_specs=[
            pl.BlockSpec((ts, gw), lambda b, g, c: (row(b, g, c), g)),
            pl.BlockSpec((ts, gw), lambda b, g, c: (row(b, g, c), xs_blk + g)),
            pl.BlockSpec((ts, SSD_STATE), lambda b, g, c: (row(b, g, c), b_blk + g)),
            pl.BlockSpec((ts, SSD_STATE), lambda b, g, c: (row(b, g, c), c_blk + g)),
            pl.BlockSpec((CHUNK, gw), lambda b, g, c: (prev(b, g, c), xs_blk + g)),
            pl.BlockSpec((CHUNK, SSD_STATE), lambda b, g, c: (prev(b, g, c), b_blk + g)),
            pl.BlockSpec((CHUNK, SSD_STATE), lambda b, g, c: (prev(b, g, c), c_blk + g)),
            pl.BlockSpec((nc, hp, LANES), lambda b, g, c: (row(b, g, c), 0, 0)),
            pl.BlockSpec((nc, hp, LANES), lambda b, g, c: (row(b, g, c), 0, 0)),
            pl.BlockSpec((nc, hp, LANES), lambda b, g, c: (row(b, g, c), 0, 0)),
            pl.BlockSpec((1, ts, LANES), lambda b, g, c: (g, row(b, g, c), 0)),
            pl.BlockSpec(shift.shape, lambda b, g, c: (0, 0)),
            pl.BlockSpec((kconv, gw), lambda b, g, c: (0, g)),
            pl.BlockSpec((kconv, SSD_STATE), lambda b, g, c: (0, wb_blk + g)),
            pl.BlockSpec((kconv, SSD_STATE), lambda b, g, c: (0, wc_blk + g)),
            pl.BlockSpec((1, gw), lambda b, g, c: (0, g)),
            pl.BlockSpec((1, SSD_STATE), lambda b, g, c: (0, wb_blk + g)),
            pl.BlockSpec((1, SSD_STATE), lambda b, g, c: (0, wc_blk + g)),
            pl.BlockSpec((1, gw), lambda b, g, c: (0, g)),
            pl.BlockSpec((1, gw), lambda b, g, c: (0, g)),
        ],
        out_specs=pl.BlockSpec((ts, gw), lambda b, g, c: (row(b, g, c), g)),
        out_shape=jax.ShapeDtypeStruct((m, inner), BF16),
        scratch_shapes=[pltpu.VMEM(s, d) for s, d in scratch],
        compiler_params=_cparams(3, blocks, scratch, temps=8 << 20),
        name="ssd",
    )(proj, proj, proj, proj, proj, proj, proj, dtrow, wrow, acsrow, acscol, shift,
      conv_w, conv_w, conv_w, conv_b, conv_b, conv_b, d_skip, norm_w)


def _sconv_kernel(gb_ref, gc_ref, xt_ref, gate_ref, w_ref, wo_ref, o_ref, pe_ref, ysc_ref, wsc_ref,
                  *, ts, kconv, cw):
    halo = SUBLANES

    @pl.when(jnp.logical_and(pl.program_id(0) == 0, pl.program_id(1) == 0))
    def _():
        _cast_tile(wo_ref, wsc_ref)

    @pl.when(pl.program_id(1) == 0)
    def _():
        pe_ref[0:halo, :] = jnp.zeros((halo, pe_ref.shape[1]), F32)

    pe_ref[halo:halo + ts, :] = gb_ref[...].astype(F32) * xt_ref[...].astype(F32)
    width = pe_ref.shape[1]
    for c0 in range(0, width, cw):
        acc = None
        for k in range(kconv):
            r = halo - (kconv - 1) + k
            term = pe_ref[r:r + ts, c0:c0 + cw] * w_ref[k:k + 1, c0:c0 + cw]
            acc = term if acc is None else acc + term
        ysc_ref[:, c0:c0 + cw] = (gc_ref[:, c0:c0 + cw].astype(F32) * acc).astype(BF16)
    pe_ref[0:halo, :] = pe_ref[ts:ts + halo, :]
    yb = jnp.dot(ysc_ref[...], wsc_ref[...], preferred_element_type=F32)
    o_ref[...] = (_sigmoid(gate_ref[...].astype(F32)) * yb).astype(o_ref.dtype)


def _sconv(proj, conv_w, w_out, layer, *, batch, seq, width, off_b, off_c, off_x, off_gate):
    m = proj.shape[0]
    kconv = conv_w.shape[0]
    n = w_out.shape[2]
    ts = _pick(seq, (512, 256, 128))
    nt = seq // ts
    assert off_b % width == 0 and off_c % width == 0 and off_x % width == 0 and kconv - 1 <= SUBLANES
    assert off_gate % n == 0
    jb, jc, jx, jg = off_b // width, off_c // width, off_x // width, off_gate // n
    cw = _pick(width, (512, 256, 128))
    pe_shape = (SUBLANES + ts, width)
    blocks = [((ts, width), BF16)] * 3 + [((ts, n), BF16), ((kconv, width), F32), ((ts, n), BF16)]
    single = [((width, n), F32)]
    scratch = [(pe_shape, F32), ((ts, width), BF16), ((width, n), BF16)]
    return pl.pallas_call(
        functools.partial(_sconv_kernel, ts=ts, kconv=kconv, cw=cw),
        grid=(batch, nt),
        in_specs=[pl.BlockSpec((ts, width), lambda b, c: (b * nt + c, jb)),
                  pl.BlockSpec((ts, width), lambda b, c: (b * nt + c, jc)),
                  pl.BlockSpec((ts, width), lambda b, c: (b * nt + c, jx)),
                  pl.BlockSpec((ts, n), lambda b, c: (b * nt + c, jg)),
                  pl.BlockSpec((kconv, width), lambda b, c: (0, 0)),
                  pl.BlockSpec((None, width, n), lambda b, c: (layer, 0, 0), pipeline_mode=pl.Buffered(1))],
        out_specs=pl.BlockSpec((ts, n), lambda b, c: (b * nt + c, 0)),
        out_shape=jax.ShapeDtypeStruct((m, n), BF16),
        scratch_shapes=[pltpu.VMEM(s, d) for s, d in scratch],
        compiler_params=_cparams(2, blocks, scratch, temps=(4 << 20) + 2 * ts * n * 4, single=single),
        name="short_conv",
    )(proj, proj, proj, proj, conv_w, w_out)


def kernel(x, p, norm_mix, w_in, ssd_conv_w, ssd_conv_b, ssd_dt_bias, ssd_a_log, ssd_d, ssd_norm,
           ssd_out, sc_conv_w, sc_out, w_o, norm_ffn, w_gate_up, w_down, norm_ple, ple_gate,
           ple_proj, norm_final):
    batch, seq, d = x.shape
    depth = w_in.shape[0]
    m = batch * seq
    heads = ssd_a_log.shape[1]
    inner = heads * HEAD_DIM
    groups = SSD_GROUPS
    r_heads = heads // groups
    xbc = ssd_conv_w.shape[2]
    scw = sc_conv_w.shape[2]
    assert xbc == inner + 2 * groups * SSD_STATE and heads <= LANES
    dt0 = inner + xbc
    off_scb = dt0
    off_scc, off_scx = off_scb + scw, off_scb + 2 * scw
    off_ga = off_scb + 3 * scw
    off_gb = off_ga + d
    hpad = LANES - heads
    p2 = p.reshape(depth, m, p.shape[-1])
    w_in_t = jnp.swapaxes(w_in, 1, 2)

    h = x.reshape(m, d)
    u, u_ssq = _prenorm(h, norm_mix[0])
    for i in range(depth):
        dt_bias = jnp.pad(ssd_dt_bias[i], (0, hpad)).reshape(1, LANES)
        a_log = jnp.pad(ssd_a_log[i], (0, hpad)).reshape(1, LANES)
        d_skip = jnp.repeat(ssd_d[i], HEAD_DIM).reshape(1, inner)

        proj = _inproj(u, u_ssq, w_in_t, i, dt0, heads)
        dtrow, wrow, acsrow, acscol = _dtprep(u, u_ssq, w_in_t, i, dt0, dt_bias, a_log, groups, r_heads)
        y = _ssd(proj, dtrow, wrow, acsrow, acscol, ssd_conv_w[i], ssd_conv_b[i].reshape(1, xbc), d_skip,
                 ssd_norm[i].reshape(1, inner), batch=batch, seq=seq, inner=inner, groups=groups)
        ybg = _sconv(proj, sc_conv_w[i], sc_out, i, batch=batch, seq=seq, width=scw,
                     off_b=off_scb, off_c=off_scc, off_x=off_scx, off_gate=off_gb)
        merged = _merge(y, ssd_out, i, ybg, proj, off_ga)
        h, v, v_ssq = _matmul_residual(merged, w_o, i, h, norm_ffn[i], "w_o_residual", in_place=i > 0)
        act = _swiglu(v, v_ssq, w_gate_up, i)
        h, hn, hn_ssq = _matmul_residual(act, w_down, i, h, norm_ple[i], "w_down_residual")
        g_next = norm_mix[i + 1] if i + 1 < depth else norm_final
        h, u, u_ssq = _ple(hn, hn_ssq, ple_gate, p2, ple_proj, i, h, g_next)
    out = _rmsnorm(h, norm_final, F32)
    return out.reshape(batch, seq, d)
```

```python
import functools

import jax
import jax.numpy as jnp
from jax import lax
from jax.experimental import pallas as pl
from jax.experimental.pallas import tpu as pltpu

F32 = jnp.float32
BF16 = jnp.bfloat16

EPS = 1e-6
HEAD_DIM = 64
SSD_GROUPS = 8
SSD_STATE = 128
CHUNK = 128
LANES = 128
SUBLANES = 8
CAST_ROWS = 256
LOG2E = 1.4426950408889634
VMEM_BYTES_V7X = 64 * 1024 * 1024
VMEM_CAP = VMEM_BYTES_V7X - 3 * 1024 * 1024


def _nbytes(shape, dtype):
    n = 1
    for s in shape:
        n *= s
    return n * jnp.dtype(dtype).itemsize


def _cparams(ngrid, blocks, scratch=(), temps=0, single=()):
    need = (2 * sum(_nbytes(s, d) for s, d in blocks) + sum(_nbytes(s, d) for s, d in single)
            + sum(_nbytes(s, d) for s, d in scratch) + temps + (4 << 20))
    return pltpu.CompilerParams(dimension_semantics=("arbitrary",) * ngrid,
                                vmem_limit_bytes=int(min(max(need, 16 << 20), VMEM_CAP)))


def _pick(n, candidates):
    for c in candidates:
        if n % c == 0:
            return c
    raise ValueError(f"no block size in {candidates} divides {n}")


def _cast_tile(w_ref, wsc_ref):
    k = w_ref.shape[0]
    rc = _pick(k, (CAST_ROWS, LANES, SUBLANES))
    for r in range(0, k, rc):
        wsc_ref[r:r + rc, :] = w_ref[r:r + rc, :].astype(BF16)


def _first_token_tile():
    return pl.program_id(1) == 0


def _sigmoid(x):
    return 0.5 * jnp.tanh(0.5 * x) + 0.5


def _silu(x):
    h = 0.5 * x
    return h * jnp.tanh(h) + h


def _emit_scaled(h_new, g_ref, hg_ref, ssq_ref):
    hg_ref[...] = (h_new * g_ref[...]).astype(BF16)
    col = jnp.sum(h_new * h_new, axis=-1, keepdims=True)
    ssq_ref[...] = jnp.transpose(jnp.broadcast_to(col, (col.shape[0], LANES)))[0:1, :]


def _row_rsqrt(ssq_ref, d):
    s = ssq_ref[0]
    for q in range(1, ssq_ref.shape[0]):
        s = s + ssq_ref[q]
    row = lax.rsqrt(s / d + EPS)
    return jnp.transpose(jnp.broadcast_to(row, (LANES, row.shape[1])))[:, 0:1]


def _prenorm_kernel(x_ref, g_ref, hg_ref, ssq_ref):
    _emit_scaled(x_ref[...], g_ref, hg_ref, ssq_ref.at[0])


def _prenorm(x, g):
    m, d = x.shape
    bm = _pick(m, (512, 256, 128))
    blocks = [((bm, d), F32), ((1, d), F32), ((bm, d), BF16), ((1, 1, bm), F32)]
    return pl.pallas_call(
        _prenorm_kernel,
        grid=(m // bm,),
        in_specs=[pl.BlockSpec((bm, d), lambda i: (i, 0)),
                  pl.BlockSpec((1, d), lambda i: (0, 0))],
        out_specs=[pl.BlockSpec((bm, d), lambda i: (i, 0)),
                   pl.BlockSpec((1, 1, bm), lambda i: (0, 0, i))],
        out_shape=[jax.ShapeDtypeStruct((m, d), BF16), jax.ShapeDtypeStruct((1, 1, m), F32)],
        compiler_params=_cparams(1, blocks, temps=2 * bm * d * 4),
        name="prenorm",
    )(x, g.reshape(1, d))


def _inproj_kernel(a_ref, ssq_ref, w_ref, wn_ref, o_ref, wsc_ref, *, n_aligned, skip):
    j = pl.program_id(0)
    bn = w_ref.shape[0]
    rc = _pick(bn, (CAST_ROWS, LANES))

    @pl.when(jnp.logical_and(_first_token_tile(), j < n_aligned))
    def _():
        _cast_tile(w_ref, wsc_ref)

    @pl.when(jnp.logical_and(_first_token_tile(), j >= n_aligned))
    def _():
        for r in range(0, bn - rc, rc):
            wsc_ref[r:r + rc, :] = w_ref[r + skip:r + skip + rc, :].astype(BF16)
        wsc_ref[bn - rc:bn - skip, :] = w_ref[bn - rc + skip:bn, :].astype(BF16)
        wsc_ref[bn - skip:bn, :] = wn_ref[...].astype(BF16)

    rs = _row_rsqrt(ssq_ref, a_ref.shape[1])
    hm = a_ref.shape[0] // 2
    for r in (0, hm):
        acc = lax.dot_general(a_ref[r:r + hm, :], wsc_ref[...], (((1,), (1,)), ((), ())),
                              preferred_element_type=F32)
        o_ref[r:r + hm, :] = (acc * rs[r:r + hm]).astype(o_ref.dtype)


def _inproj(u, ssq, w_in_t, layer, dt0, heads):
    m, k = u.shape
    parts = ssq.shape[0]
    d_in = w_in_t.shape[1]
    n = d_in - heads
    bm = _pick(m, (2048, 1024, 512, 256, 128))
    bn = next(c for c in (1024, 512, 256, 128) if dt0 % c == 0 and (n - dt0) % c == 0)
    assert heads % (2 * SUBLANES) == 0 and bn % heads == 0 and d_in % heads == 0 and heads < LANES
    n_aligned = dt0 // bn
    blocks = [((bm, k), BF16), ((parts, 1, bm), F32), ((bn, k), F32), ((heads, k), F32), ((bm, bn), BF16)]
    scratch = [((bn, k), BF16)]
    return pl.pallas_call(
        functools.partial(_inproj_kernel, n_aligned=n_aligned, skip=heads),
        grid=(n // bn, m // bm),
        in_specs=[pl.BlockSpec((bm, k), lambda j, i: (i, 0)),
                  pl.BlockSpec((parts, 1, bm), lambda j, i: (0, 0, i)),
                  pl.BlockSpec((None, bn, k), lambda j, i: (layer, j, 0)),
                  pl.BlockSpec((None, heads, k), lambda j, i: (layer, (j + 1) * (bn // heads), 0))],
        out_specs=pl.BlockSpec((bm, bn), lambda j, i: (i, j)),
        out_shape=jax.ShapeDtypeStruct((m, n), BF16),
        scratch_shapes=[pltpu.VMEM(s, d) for s, d in scratch],
        compiler_params=_cparams(2, blocks, scratch, temps=bm * bn * 4 + 2 * CAST_ROWS * k * 4),
        name="in_proj",
    )(u, ssq, w_in_t, w_in_t)


def _mm_res_kernel(a_ref, w_ref, h_ref, g_ref, o_ref, hg_ref, ssq_ref, wsc_ref):
    @pl.when(_first_token_tile())
    def _():
        _cast_tile(w_ref, wsc_ref)

    h_new = h_ref[...] + jnp.dot(a_ref[...], wsc_ref[...], preferred_element_type=F32)
    o_ref[...] = h_new
    _emit_scaled(h_new, g_ref, hg_ref, ssq_ref)


def _matmul_residual(a, w, layer, h, g_next, name, in_place=True):
    m, k = a.shape
    n = w.shape[2]
    big_k = k > 2048
    bm = _pick(m, (512, 256, 128))
    bn = _pick(n, (1024, 512, 256, 128)) if big_k else _pick(n, (2048, 1024, 512, 256, 128))
    blocks = [((bm, k), BF16), ((bm, bn), F32), ((1, bn), F32),
              ((bm, bn), F32), ((bm, bn), BF16), ((1, bm), F32)]
    single = [((k, bn), F32)]
    scratch = [((k, bn), BF16)]
    return pl.pallas_call(
        _mm_res_kernel,
        grid=(n // bn, m // bm),
        in_specs=[pl.BlockSpec((bm, k), lambda j, i: (i, 0)),
                  pl.BlockSpec((None, k, bn), lambda j, i: (layer, 0, j), pipeline_mode=pl.Buffered(1)),
                  pl.BlockSpec((bm, bn), lambda j, i: (i, j)),
                  pl.BlockSpec((1, bn), lambda j, i: (0, j))],
        out_specs=[pl.BlockSpec((bm, bn), lambda j, i: (i, j)),
                   pl.BlockSpec((bm, bn), lambda j, i: (i, j)),
                   pl.BlockSpec((None, 1, bm), lambda j, i: (j, 0, i))],
        out_shape=[jax.ShapeDtypeStruct((m, n), F32), jax.ShapeDtypeStruct((m, n), BF16),
                   jax.ShapeDtypeStruct((n // bn, 1, m), F32)],
        scratch_shapes=[pltpu.VMEM(s, d) for s, d in scratch],
        input_output_aliases={2: 0} if in_place else {},
        compiler_params=_cparams(2, blocks, scratch, temps=2 * bm * bn * 4 + 2 * CAST_ROWS * bn * 4,
                                 single=single),
        name=name,
    )(a, w, h, g_next.reshape(1, n))


def _swiglu_kernel(v_ref, ssq_ref, wg_ref, wu_ref, o_ref, wgsc_ref, wusc_ref):
    @pl.when(_first_token_tile())
    def _():
        _cast_tile(wg_ref, wgsc_ref)
        _cast_tile(wu_ref, wusc_ref)

    rs_all = _row_rsqrt(ssq_ref, v_ref.shape[1])
    hm = v_ref.shape[0] // 2
    for r in (0, hm):
        v = v_ref[r:r + hm, :]
        rs = rs_all[r:r + hm]
        gate = jnp.dot(v, wgsc_ref[...], preferred_element_type=F32) * rs
        up = jnp.dot(v, wusc_ref[...], preferred_element_type=F32) * rs
        o_ref[r:r + hm, :] = (_silu(gate) * up).astype(o_ref.dtype)


def _swiglu(v, ssq, w_gate_up, layer):
    m, k = v.shape
    parts = ssq.shape[0]
    d_ff = w_gate_up.shape[2] // 2
    bm = _pick(m, (2048, 1024, 512, 256, 128))
    bn = _pick(d_ff, (512, 256, 128))
    nb = d_ff // bn
    blocks = [((bm, k), BF16), ((parts, 1, bm), F32), ((k, bn), F32), ((k, bn), F32), ((bm, bn), BF16)]
    scratch = [((k, bn), BF16), ((k, bn), BF16)]
    return pl.pallas_call(
        _swiglu_kernel,
        grid=(nb, m // bm),
        in_specs=[pl.BlockSpec((bm, k), lambda j, i: (i, 0)),
                  pl.BlockSpec((parts, 1, bm), lambda j, i: (0, 0, i)),
                  pl.BlockSpec((None, k, bn), lambda j, i: (layer, 0, j)),
                  pl.BlockSpec((None, k, bn), lambda j, i: (layer, 0, j + nb))],
        out_specs=pl.BlockSpec((bm, bn), lambda j, i: (i, j)),
        out_shape=jax.ShapeDtypeStruct((m, d_ff), BF16),
        scratch_shapes=[pltpu.VMEM(s, d) for s, d in scratch],
        compiler_params=_cparams(2, blocks, scratch, temps=3 * bm * bn * 4 + 2 * CAST_ROWS * bn * 4),
        name="swiglu_up",
    )(v, ssq, w_gate_up, w_gate_up)


def _merge_kernel(y_ref, wa_ref, ybg_ref, ga_ref, o_ref, wasc_ref):
    @pl.when(_first_token_tile())
    def _():
        _cast_tile(wa_ref, wasc_ref)

    ya = jnp.dot(y_ref[...], wasc_ref[...], preferred_element_type=F32)
    ga = _sigmoid(ga_ref[...].astype(F32))
    o_ref[...] = (ga * ya + ybg_ref[...].astype(F32)).astype(o_ref.dtype)


def _merge(y, w_a, layer, ybg, proj, off_ga):
    m, ka = y.shape
    n = w_a.shape[2]
    bm = _pick(m, (1024, 512, 256, 128))
    bn = _pick(n, (1024, 512, 256, 128))
    ja = off_ga // bn
    assert off_ga % bn == 0
    blocks = [((bm, ka), BF16), ((bm, bn), BF16), ((bm, bn), BF16), ((bm, bn), BF16)]
    single = [((ka, bn), F32)]
    scratch = [((ka, bn), BF16)]
    return pl.pallas_call(
        _merge_kernel,
        grid=(n // bn, m // bm),
        in_specs=[pl.BlockSpec((bm, ka), lambda j, i: (i, 0)),
                  pl.BlockSpec((None, ka, bn), lambda j, i: (layer, 0, j), pipeline_mode=pl.Buffered(1)),
                  pl.BlockSpec((bm, bn), lambda j, i: (i, j)),
                  pl.BlockSpec((bm, bn), lambda j, i: (i, j + ja))],
        out_specs=pl.BlockSpec((bm, bn), lambda j, i: (i, j)),
        out_shape=jax.ShapeDtypeStruct((m, n), BF16),
        scratch_shapes=[pltpu.VMEM(s, d) for s, d in scratch],
        compiler_params=_cparams(2, blocks, scratch, temps=4 * bm * bn * 4 + 2 * CAST_ROWS * bn * 4,
                                 single=single),
        name="branch_merge",
    )(y, w_a, ybg, proj)


def _ple_kernel(hn_ref, ssq_ref, wg_ref, p_ref, wp_ref, h_ref, g_ref, o_ref, *rest, last):
    hg_ref, ssqo_ref = (None, None) if last else rest[:2]
    wgsc_ref, wpsc_ref = rest[-2:]

    @pl.when(_first_token_tile())
    def _():
        _cast_tile(wg_ref, wgsc_ref)
        _cast_tile(wp_ref, wpsc_ref)

    rs = _row_rsqrt(ssq_ref, hn_ref.shape[1])
    pg = _sigmoid(jnp.dot(hn_ref[...], wgsc_ref[...], preferred_element_type=F32) * rs)
    e = jnp.dot(p_ref[...].astype(BF16), wpsc_ref[...], preferred_element_type=F32)
    h_new = h_ref[...] + pg * e
    if last:
        ms = jnp.mean(h_new * h_new, axis=-1, keepdims=True)
        o_ref[...] = h_new * lax.rsqrt(ms + EPS) * g_ref[...]
    else:
        o_ref[...] = h_new
        _emit_scaled(h_new, g_ref, hg_ref, ssqo_ref)


def _ple(hn, ssq, w_gate, p, w_proj, layer, h, g_next, last):
    m, k = hn.shape
    parts = ssq.shape[0]
    kp = p.shape[2]
    n = w_gate.shape[2]
    bm = _pick(m, (512, 256, 128))
    bn = _pick(n, (2048, 1024, 512, 256, 128))
    blocks = [((bm, k), BF16), ((parts, 1, bm), F32), ((bm, kp), F32),
              ((bm, bn), F32), ((1, bn), F32), ((bm, bn), F32), ((bm, bn), BF16), ((1, bm), F32)]
    single = [((k, bn), F32), ((kp, bn), F32)]
    scratch = [((k, bn), BF16), ((kp, bn), BF16)]
    assert not last or bn == n
    out_specs = [pl.BlockSpec((bm, bn), lambda j, i: (i, j)),
                 pl.BlockSpec((bm, bn), lambda j, i: (i, j)),
                 pl.BlockSpec((None, 1, bm), lambda j, i: (j, 0, i))]
    out_shape = [jax.ShapeDtypeStruct((m, n), F32), jax.ShapeDtypeStruct((m, n), BF16),
                 jax.ShapeDtypeStruct((n // bn, 1, m), F32)]
    return pl.pallas_call(
        functools.partial(_ple_kernel, last=last),
        grid=(n // bn, m // bm),
        in_specs=[pl.BlockSpec((bm, k), lambda j, i: (i, 0)),
                  pl.BlockSpec((parts, 1, bm), lambda j, i: (0, 0, i)),
                  pl.BlockSpec((None, k, bn), lambda j, i: (layer, 0, j), pipeline_mode=pl.Buffered(1)),
                  pl.BlockSpec((None, bm, kp), lambda j, i: (layer, i, 0)),
                  pl.BlockSpec((None, kp, bn), lambda j, i: (layer, 0, j), pipeline_mode=pl.Buffered(1)),
                  pl.BlockSpec((bm, bn), lambda j, i: (i, j)),
                  pl.BlockSpec((1, bn), lambda j, i: (0, j))],
        out_specs=out_specs[0] if last else out_specs,
        out_shape=out_shape[0] if last else out_shape,
        scratch_shapes=[pltpu.VMEM(s, d) for s, d in scratch],
        input_output_aliases={5: 0},
        compiler_params=_cparams(2, blocks, scratch, temps=4 * bm * bn * 4 + 2 * CAST_ROWS * bn * 4,
                                 single=single),
        name="ple",
    )(hn, ssq, w_gate, p, w_proj, h, g_next.reshape(1, n))


def _dtprep_kernel(u_ref, ssq_ref, w_ref, bias_ref, alog_ref, dtrow_ref, wrow_ref, acsrow_ref, acscol_ref,
                   *, nc, groups, r_heads):
    x = lax.dot_general(u_ref[...], w_ref[...].astype(BF16), (((1,), (1,)), ((), ())),
                        preferred_element_type=F32)
    x = x * _row_rsqrt(ssq_ref, u_ref.shape[1]) + bias_ref[...]
    dt = jnp.maximum(x, 0.0) + jnp.log1p(jnp.exp(-jnp.abs(x)))
    adt = dt * (-jnp.exp(alog_ref[...]))
    row = lax.broadcasted_iota(jnp.int32, (CHUNK, LANES), 0)
    for k in range(nc):
        sl = slice(k * CHUNK, (k + 1) * CHUNK)
        acs = adt[sl]
        sh = 1
        while sh < CHUNK:
            acs = acs + jnp.where(row >= sh, pltpu.roll(acs, sh, 0), 0.0)
            sh *= 2
        acs2 = acs * LOG2E
        acsrow_ref[k] = acs2.T
        dtrow_ref[k] = dt[sl].T
        wrow_ref[k] = (dt[sl] * jnp.exp(acs[CHUNK - 1:CHUNK, :] - acs)).T
        for g in range(groups):
            shift = (LANES - g * r_heads) % LANES
            acscol_ref[g, sl, :] = pltpu.roll(acs2, shift, 1) if shift else acs2


def _dtprep(u, ssq, w_in_t, layer, dt0, bias, a_log, groups, r_heads):
    m, k = u.shape
    parts = ssq.shape[0]
    nc = 4 if m % (4 * CHUNK) == 0 else 1
    ts = nc * CHUNK
    nchunks = m // CHUNK
    rows = ((nc, LANES, LANES), F32)
    blocks = [((ts, k), BF16), ((parts, 1, ts), F32), ((LANES, k), F32), rows, rows, rows,
              ((groups, ts, LANES), F32)]
    row_spec = pl.BlockSpec((nc, LANES, LANES), lambda i: (i, 0, 0))
    row_shape = jax.ShapeDtypeStruct((nchunks, LANES, LANES), F32)
    return pl.pallas_call(
        functools.partial(_dtprep_kernel, nc=nc, groups=groups, r_heads=r_heads),
        grid=(m // ts,),
        in_specs=[pl.BlockSpec((ts, k), lambda i: (i, 0)),
                  pl.BlockSpec((parts, 1, ts), lambda i: (0, 0, i)),
                  pl.BlockSpec((None, LANES, k), lambda i: (layer, dt0 // LANES, 0)),
                  pl.BlockSpec((1, LANES), lambda i: (0, 0)),
                  pl.BlockSpec((1, LANES), lambda i: (0, 0))],
        out_specs=[row_spec, row_spec, row_spec,
                   pl.BlockSpec((groups, ts, LANES), lambda i: (0, i, 0))],
        out_shape=[row_shape, row_shape, row_shape,
                   jax.ShapeDtypeStruct((groups, m, LANES), F32)],
        compiler_params=_cparams(1, blocks, temps=8 * ts * LANES * 4 + k * LANES * 2),
        name="dt_prep",
    )(u, ssq, w_in_t, bias, a_log)


def _ssd_kernel(z_ref, xs_ref, b_ref, c_ref, xsp_ref, bp_ref, cp_ref,
                dtr_ref, wr_ref, acr_ref, acc_ref, shift_ref,
                cwx_ref, cwb_ref, cwc_ref, cbx_ref, cbb_ref, cbc_ref, dsk_ref, nw_ref,
                o_ref, st_ref, yt_ref, *, nc, r_heads, kconv):
    g = pl.program_id(1)
    c = pl.program_id(2)
    L = CHUNK
    N = SSD_STATE
    gw = r_heads * HEAD_DIM
    npair = gw // LANES

    @pl.when(c == 0)
    def _():
        st_ref[...] = jnp.zeros(st_ref.shape, F32)

    tri = (lax.broadcasted_iota(jnp.int32, (L, L), 0) >= lax.broadcasted_iota(jnp.int32, (L, L), 1))
    lo = lax.broadcasted_iota(jnp.int32, (L, LANES), 1) < HEAD_DIM
    hi = jnp.logical_not(lo)
    lo_row = lo[0:1]

    def two_chunks(ref, prev_ref, k):
        if k == 0:
            prev = prev_ref[...]
            prev = jnp.where(c > 0, prev, jnp.zeros_like(prev))
            return jnp.concatenate([prev, ref[0:L, :]], axis=0)
        return ref[(k - 1) * L:(k + 1) * L, :]

    def conv_silu(shifted, cur, w_ref, bias_ref):
        acc = None
        for tap in range(kconv - 1):
            term = shifted[tap * L:(tap + 1) * L, :] * w_ref[tap:tap + 1, :]
            acc = term if acc is None else acc + term
        acc = acc + cur.astype(F32) * w_ref[kconv - 1:kconv, :]
        acc = acc + bias_ref[...]
        return _silu(acc)

    for k in range(nc):
        r0 = k * L
        x2 = jnp.concatenate([two_chunks(xs_ref, xsp_ref, k), two_chunks(b_ref, bp_ref, k),
                              two_chunks(c_ref, cp_ref, k)], axis=1)
        shifted = jnp.dot(shift_ref[...], x2, preferred_element_type=F32)
        cur = x2[L:2 * L]
        x = conv_silu(shifted[:, 0:gw], cur[:, 0:gw], cwx_ref, cbx_ref)
        bm = conv_silu(shifted[:, gw:gw + N], cur[:, gw:gw + N], cwb_ref, cbb_ref)
        cm = conv_silu(shifted[:, gw + N:gw + 2 * N], cur[:, gw + N:gw + 2 * N], cwc_ref, cbc_ref)
        cb = lax.dot_general(cm.astype(BF16), bm.astype(BF16), (((1,), (1,)), ((), ())),
                             preferred_element_type=F32)
        bt = bm.T
        acol = acc_ref[0, r0:r0 + L, :]
        cdec = jnp.exp2(acol[L - 1:L, :])
        ssq = jnp.zeros((L, 1), F32)
        for j in range(npair):
            cs = slice(j * LANES, (j + 1) * LANES)
            xp = x[:, cs]
            s_prev = st_ref[:, cs]
            s_new = s_prev * jnp.where(lo_row, cdec[:, 2 * j:2 * j + 1], cdec[:, 2 * j + 1:2 * j + 2])
            y = None
            for hh, keep in ((2 * j, lo), (2 * j + 1, hi)):
                head = pl.ds(g * r_heads + hh, 1)
                arow = acr_ref[k, head, :]
                drow = dtr_ref[k, head, :]
                wrow = wr_ref[k, head, :]
                xm = jnp.where(keep, xp, 0.0).astype(BF16)
                sm = jnp.where(keep, s_prev, 0.0).astype(BF16)
                ab = jnp.broadcast_to(acol[:, hh:hh + 1], (L, L))
                dec = jnp.exp2(jnp.where(tri, ab - arow, -jnp.inf))
                mh = cb * dec * drow
                ch = cm * jnp.exp2(ab)
                lhs = jnp.concatenate([mh, ch], axis=1).astype(BF16)
                rhs = jnp.concatenate([xm, sm], axis=0)
                t = jnp.dot(lhs, rhs, preferred_element_type=F32)
                y = t if y is None else y + t
                bth = (bt * wrow).astype(BF16)
                s_new = s_new + jnp.dot(bth, xm, preferred_element_type=F32)
            st_ref[:, cs] = s_new
            yt = y + dsk_ref[:, cs] * xp
            zt = z_ref[r0:r0 + L, cs].astype(F32)
            yt = yt * _silu(zt)
            ssq = ssq + jnp.sum(yt * yt, axis=-1, keepdims=True)
            yt_ref[:, cs] = yt
        rs = lax.rsqrt(ssq / gw + EPS)
        for j in range(npair):
            cs = slice(j * LANES, (j + 1) * LANES)
            o_ref[r0:r0 + L, cs] = (yt_ref[:, cs] * rs * nw_ref[:, cs]).astype(o_ref.dtype)


def _ssd(proj, dtrow, wrow, acsrow, acscol, conv_w, conv_b, d_skip, norm_w, *, batch, seq, inner, groups):
    m = proj.shape[0]
    gw = inner // groups
    r_heads = gw // HEAD_DIM
    assert gw % LANES == 0 and r_heads % 2 == 0 and SSD_STATE == LANES
    kconv = conv_w.shape[0]
    assert kconv - 1 <= CHUNK
    nc = _pick(seq // CHUNK, (16, 8, 4, 2, 1))
    ts = nc * CHUNK
    nt = seq // ts
    gn = groups * SSD_STATE
    xs_blk, b_blk, c_blk = inner // gw, 2 * inner // SSD_STATE, (2 * inner + gn) // SSD_STATE
    wb_blk, wc_blk = inner // SSD_STATE, (inner + gn) // SSD_STATE
    hp = acsrow.shape[1]
    t_idx = jnp.arange(CHUNK)[None, :, None]
    d_idx = (kconv - 1 - jnp.arange(kconv - 1))[:, None, None]
    s_idx = jnp.arange(2 * CHUNK)[None, None, :]
    shift = (s_idx == CHUNK + t_idx - d_idx).astype(BF16).reshape((kconv - 1) * CHUNK, 2 * CHUNK)
    st_shape = (SSD_STATE, gw)
    blocks = [((ts, gw), BF16), ((ts, gw), BF16), ((ts, SSD_STATE), BF16), ((ts, SSD_STATE), BF16),
              ((CHUNK, gw), BF16), ((CHUNK, SSD_STATE), BF16), ((CHUNK, SSD_STATE), BF16),
              ((nc, hp, LANES), F32), ((nc, hp, LANES), F32), ((nc, hp, LANES), F32),
              ((1, ts, LANES), F32), (shift.shape, BF16),
              ((kconv, gw), F32), ((kconv, SSD_STATE), F32), ((kconv, SSD_STATE), F32),
              ((1, gw), F32), ((1, SSD_STATE), F32), ((1, SSD_STATE), F32), ((1, gw), F32), ((1, gw), F32),
              ((ts, gw), BF16)]
    scratch = [(st_shape, F32), ((CHUNK, gw), F32)]
    row = lambda b, g, c: b * nt + c
    prev = lambda b, g, c: jnp.maximum(row(b, g, c) * nc - 1, 0)
    return pl.pallas_call(
        functools.partial(_ssd_kernel, nc=nc, r_heads=r_heads, kconv=kconv),
        grid=(batch, groups, nt),
        in_specs=[
            pl.BlockSpec((ts, gw), lambda b, g, c: (row(b, g, c), g)),
            pl.BlockSpec((ts, gw), lambda b, g, c: (row(b, g, c), xs_blk + g)),
            pl.BlockSpec((ts, SSD_STATE), lambda b, g, c: (row(b, g, c), b_blk + g)),
            pl.BlockSpec((ts, SSD_STATE), lambda b, g, c: (row(b, g, c), c_blk + g)),
            pl.BlockSpec((CHUNK, gw), lambda b, g, c: (prev(b, g, c), xs_blk + g)),
            pl.BlockSpec((CHUNK, SSD_STATE), lambda b, g, c: (prev(b, g, c), b_blk + g)),
            pl.BlockSpec((CHUNK, SSD_STATE), lambda b, g, c: (prev(b, g, c), c_blk + g)),
            pl.BlockSpec((nc, hp, LANES), lambda b, g, c: (row(b, g, c), 0, 0)),
            pl.BlockSpec((nc, hp, LANES), lambda b, g, c: (row(b, g, c), 0, 0)),
            pl.BlockSpec((nc, hp, LANES), lambda b, g, c: (row(b, g, c), 0, 0)),
            pl.BlockSpec((1, ts, LANES), lambda b, g, c: (g, row(b, g, c), 0)),
            pl.BlockSpec(shift.shape, lambda b, g, c: (0, 0)),
            pl.BlockSpec((kconv, gw), lambda b, g, c: (0, g)),
            pl.BlockSpec((kconv, SSD_STATE), lambda b, g, c: (0, wb_blk + g)),
            pl.BlockSpec((kconv, SSD_STATE), lambda b, g, c: (0, wc_blk + g)),
            pl.BlockSpec((1, gw), lambda b, g, c: (0, g)),
            pl.BlockSpec((1, SSD_STATE), lambda b, g, c: (0, wb_blk + g)),
            pl.BlockSpec((1, SSD_STATE), lambda b, g, c: (0, wc_blk + g)),
            pl.BlockSpec((1, gw), lambda b, g, c: (0, g)),
            pl.BlockSpec((1, gw), lambda b, g, c: (0, g)),
        ],
        out_specs=pl.BlockSpec((ts, gw), lambda b, g, c: (row(b, g, c), g)),
        out_shape=jax.ShapeDtypeStruct((m, inner), BF16),
        scratch_shapes=[pltpu.VMEM(s, d) for s, d in scratch],
        compiler_params=_cparams(3, blocks, scratch, temps=8 << 20),
        name="ssd",
    )(proj, proj, proj, proj, proj, proj, proj, dtrow, wrow, acsrow, acscol, shift,
      conv_w, conv_w, conv_w, conv_b, conv_b, conv_b, d_skip, norm_w)


def _sconv_kernel(gb_ref, gc_ref, xt_ref, gate_ref, w_ref, wo_ref, o_ref, pe_ref, ysc_ref, wsc_ref,
                  *, ts, kconv, cw):
    halo = SUBLANES

    @pl.when(jnp.logical_and(pl.program_id(0) == 0, pl.program_id(1) == 0))
    def _():
        _cast_tile(wo_ref, wsc_ref)

    @pl.when(pl.program_id(1) == 0)
    def _():
        pe_ref[0:halo, :] = jnp.zeros((halo, pe_ref.shape[1]), F32)

    pe_ref[halo:halo + ts, :] = gb_ref[...].astype(F32) * xt_ref[...].astype(F32)
    width = pe_ref.shape[1]
    for c0 in range(0, width, cw):
        acc = None
        for k in range(kconv):
            r = halo - (kconv - 1) + k
            term = pe_ref[r:r + ts, c0:c0 + cw] * w_ref[k:k + 1, c0:c0 + cw]
            acc = term if acc is None else acc + term
        ysc_ref[:, c0:c0 + cw] = (gc_ref[:, c0:c0 + cw].astype(F32) * acc).astype(BF16)
    pe_ref[0:halo, :] = pe_ref[ts:ts + halo, :]
    yb = jnp.dot(ysc_ref[...], wsc_ref[...], preferred_element_type=F32)
    o_ref[...] = (_sigmoid(gate_ref[...].astype(F32)) * yb).astype(o_ref.dtype)


def _sconv(proj, conv_w, w_out, layer, *, batch, seq, width, off_b, off_c, off_x, off_gate):
    m = proj.shape[0]
    kconv = conv_w.shape[0]
    n = w_out.shape[2]
    ts = _pick(seq, (512, 256, 128))
    nt = seq // ts
    assert off_b % width == 0 and off_c % width == 0 and off_x % width == 0 and kconv - 1 <= SUBLANES
    assert off_gate % n == 0
    jb, jc, jx, jg = off_b // width, off_c // width, off_x // width, off_gate // n
    cw = _pick(width, (512, 256, 128))
    pe_shape = (SUBLANES + ts, width)
    blocks = [((ts, width), BF16)] * 3 + [((ts, n), BF16), ((kconv, width), F32), ((ts, n), BF16)]
    single = [((width, n), F32)]
    scratch = [(pe_shape, F32), ((ts, width), BF16), ((width, n), BF16)]
    return pl.pallas_call(
        functools.partial(_sconv_kernel, ts=ts, kconv=kconv, cw=cw),
        grid=(batch, nt),
        in_specs=[pl.BlockSpec((ts, width), lambda b, c: (b * nt + c, jb)),
                  pl.BlockSpec((ts, width), lambda b, c: (b * nt + c, jc)),
                  pl.BlockSpec((ts, width), lambda b, c: (b * nt + c, jx)),
                  pl.BlockSpec((ts, n), lambda b, c: (b * nt + c, jg)),
                  pl.BlockSpec((kconv, width), lambda b, c: (0, 0)),
                  pl.BlockSpec((None, width, n), lambda b, c: (layer, 0, 0), pipeline_mode=pl.Buffered(1))],
        out_specs=pl.BlockSpec((ts, n), lambda b, c: (b * nt + c, 0)),
        out_shape=jax.ShapeDtypeStruct((m, n), BF16),
        scratch_shapes=[pltpu.VMEM(s, d) for s, d in scratch],
        compiler_params=_cparams(2, blocks, scratch, temps=(4 << 20) + 2 * ts * n * 4, single=single),
        name="short_conv",
    )(proj, proj, proj, proj, conv_w, w_out)


def kernel(x, p, norm_mix, w_in, ssd_conv_w, ssd_conv_b, ssd_dt_bias, ssd_a_log, ssd_d, ssd_norm,
           ssd_out, sc_conv_w, sc_out, w_o, norm_ffn, w_gate_up, w_down, norm_ple, ple_gate,
           ple_proj, norm_final):
    batch, seq, d = x.shape
    depth = w_in.shape[0]
    m = batch * seq
    heads = ssd_a_log.shape[1]
    inner = heads * HEAD_DIM
    groups = SSD_GROUPS
    r_heads = heads // groups
    xbc = ssd_conv_w.shape[2]
    scw = sc_conv_w.shape[2]
    assert xbc == inner + 2 * groups * SSD_STATE and heads <= LANES
    dt0 = inner + xbc
    off_scb = dt0
    off_scc, off_scx = off_scb + scw, off_scb + 2 * scw
    off_ga = off_scb + 3 * scw
    off_gb = off_ga + d
    hpad = LANES - heads
    p2 = p.reshape(depth, m, p.shape[-1])
    w_in_t = jnp.swapaxes(w_in, 1, 2)

    h = x.reshape(m, d)
    u, u_ssq = _prenorm(h, norm_mix[0])
    for i in range(depth):
        dt_bias = jnp.pad(ssd_dt_bias[i], (0, hpad)).reshape(1, LANES)
        a_log = jnp.pad(ssd_a_log[i], (0, hpad)).reshape(1, LANES)
        d_skip = jnp.repeat(ssd_d[i], HEAD_DIM).reshape(1, inner)

        proj = _inproj(u, u_ssq, w_in_t, i, dt0, heads)
        dtrow, wrow, acsrow, acscol = _dtprep(u, u_ssq, w_in_t, i, dt0, dt_bias, a_log, groups, r_heads)
        y = _ssd(proj, dtrow, wrow, acsrow, acscol, ssd_conv_w[i], ssd_conv_b[i].reshape(1, xbc), d_skip,
                 ssd_norm[i].reshape(1, inner), batch=batch, seq=seq, inner=inner, groups=groups)
        ybg = _sconv(proj, sc_conv_w[i], sc_out, i, batch=batch, seq=seq, width=scw,
                     off_b=off_scb, off_c=off_scc, off_x=off_scx, off_gate=off_gb)
        merged = _merge(y, ssd_out, i, ybg, proj, off_ga)
        h, v, v_ssq = _matmul_residual(merged, w_o, i, h, norm_ffn[i], "w_o_residual", in_place=i > 0)
        act = _swiglu(v, v_ssq, w_gate_up, i)
        h, hn, hn_ssq = _matmul_residual(act, w_down, i, h, norm_ple[i], "w_down_residual")
        if i + 1 < depth:
            h, u, u_ssq = _ple(hn, hn_ssq, ple_gate, p2, ple_proj, i, h, norm_mix[i + 1], last=False)
        else:
            out = _ple(hn, hn_ssq, ple_gate, p2, ple_proj, i, h, norm_final, last=True)
    return out.reshape(batch, seq, d)
```

```python
import functools

import jax
import jax.numpy as jnp
from jax import lax
from jax.experimental import pallas as pl
from jax.experimental.pallas import tpu as pltpu

F32 = jnp.float32
BF16 = jnp.bfloat16

EPS = 1e-6
HEAD_DIM = 64
SSD_GROUPS = 8
SSD_STATE = 128
CHUNK = 128
LANES = 128
SUBLANES = 8
CAST_ROWS = 256
LOG2E = 1.4426950408889634
VMEM_BYTES_V7X = 64 * 1024 * 1024
VMEM_CAP = VMEM_BYTES_V7X - 3 * 1024 * 1024


def _nbytes(shape, dtype):
    n = 1
    for s in shape:
        n *= s
    return n * jnp.dtype(dtype).itemsize


def _cparams(ngrid, blocks, scratch=(), temps=0, single=()):
    need = (2 * sum(_nbytes(s, d) for s, d in blocks) + sum(_nbytes(s, d) for s, d in single)
            + sum(_nbytes(s, d) for s, d in scratch) + temps + (4 << 20))
    return pltpu.CompilerParams(dimension_semantics=("arbitrary",) * ngrid,
                                vmem_limit_bytes=int(min(max(need, 16 << 20), VMEM_CAP)))


def _pick(n, candidates):
    for c in candidates:
        if n % c == 0:
            return c
    raise ValueError(f"no block size in {candidates} divides {n}")


def _cast_tile(w_ref, wsc_ref):
    k = w_ref.shape[0]
    rc = _pick(k, (CAST_ROWS, LANES, SUBLANES))
    for r in range(0, k, rc):
        wsc_ref[r:r + rc, :] = w_ref[r:r + rc, :].astype(BF16)


def _first_token_tile():
    return pl.program_id(1) == 0


def _sigmoid(x):
    return 0.5 * jnp.tanh(0.5 * x) + 0.5


def _silu(x):
    h = 0.5 * x
    return h * jnp.tanh(h) + h


def _emit_scaled(h_new, g_ref, hg_ref, ssq_ref):
    hg_ref[...] = (h_new * g_ref[...]).astype(BF16)
    col = jnp.sum(h_new * h_new, axis=-1, keepdims=True)
    ssq_ref[...] = jnp.transpose(jnp.broadcast_to(col, (col.shape[0], LANES)))[0:1, :]


def _row_rsqrt(ssq_ref, d):
    s = ssq_ref[0]
    for q in range(1, ssq_ref.shape[0]):
        s = s + ssq_ref[q]
    row = lax.rsqrt(s / d + EPS)
    return jnp.transpose(jnp.broadcast_to(row, (LANES, row.shape[1])))[:, 0:1]


def _prenorm_kernel(x_ref, g_ref, hg_ref, ssq_ref):
    _emit_scaled(x_ref[...], g_ref, hg_ref, ssq_ref.at[0])


def _prenorm(x, g):
    m, d = x.shape
    bm = _pick(m, (512, 256, 128))
    blocks = [((bm, d), F32), ((1, d), F32), ((bm, d), BF16), ((1, 1, bm), F32)]
    return pl.pallas_call(
        _prenorm_kernel,
        grid=(m // bm,),
        in_specs=[pl.BlockSpec((bm, d), lambda i: (i, 0)),
                  pl.BlockSpec((1, d), lambda i: (0, 0))],
        out_specs=[pl.BlockSpec((bm, d), lambda i: (i, 0)),
                   pl.BlockSpec((1, 1, bm), lambda i: (0, 0, i))],
        out_shape=[jax.ShapeDtypeStruct((m, d), BF16), jax.ShapeDtypeStruct((1, 1, m), F32)],
        compiler_params=_cparams(1, blocks, temps=2 * bm * d * 4),
        name="prenorm",
    )(x, g.reshape(1, d))


def _inproj_kernel(a_ref, ssq_ref, w_ref, wn_ref, o_ref, wsc_ref, *, n_aligned, skip):
    j = pl.program_id(0)
    bn = w_ref.shape[0]
    rc = _pick(bn, (CAST_ROWS, LANES))

    @pl.when(jnp.logical_and(_first_token_tile(), j < n_aligned))
    def _():
        _cast_tile(w_ref, wsc_ref)

    @pl.when(jnp.logical_and(_first_token_tile(), j >= n_aligned))
    def _():
        for r in range(0, bn - rc, rc):
            wsc_ref[r:r + rc, :] = w_ref[r + skip:r + skip + rc, :].astype(BF16)
        wsc_ref[bn - rc:bn - skip, :] = w_ref[bn - rc + skip:bn, :].astype(BF16)
        wsc_ref[bn - skip:bn, :] = wn_ref[...].astype(BF16)

    rs = _row_rsqrt(ssq_ref, a_ref.shape[1])
    hm = a_ref.shape[0] // 2
    for r in (0, hm):
        acc = lax.dot_general(a_ref[r:r + hm, :], wsc_ref[...], (((1,), (1,)), ((), ())),
                              preferred_element_type=F32)
        o_ref[r:r + hm, :] = (acc * rs[r:r + hm]).astype(o_ref.dtype)


def _inproj(u, ssq, w_in_t, layer, dt0, heads):
    m, k = u.shape
    parts = ssq.shape[0]
    d_in = w_in_t.shape[1]
    n = d_in - heads
    bm = _pick(m, (2048, 1024, 512, 256, 128))
    bn = next(c for c in (1024, 512, 256, 128) if dt0 % c == 0 and (n - dt0) % c == 0)
    assert heads % (2 * SUBLANES) == 0 and bn % heads == 0 and d_in % heads == 0 and heads < LANES
    n_aligned = dt0 // bn
    blocks = [((bm, k), BF16), ((parts, 1, bm), F32), ((bn, k), F32), ((heads, k), F32), ((bm, bn), BF16)]
    scratch = [((bn, k), BF16)]
    return pl.pallas_call(
        functools.partial(_inproj_kernel, n_aligned=n_aligned, skip=heads),
        grid=(n // bn, m // bm),
        in_specs=[pl.BlockSpec((bm, k), lambda j, i: (i, 0)),
                  pl.BlockSpec((parts, 1, bm), lambda j, i: (0, 0, i)),
                  pl.BlockSpec((None, bn, k), lambda j, i: (layer, j, 0)),
                  pl.BlockSpec((None, heads, k), lambda j, i: (layer, (j + 1) * (bn // heads), 0))],
        out_specs=pl.BlockSpec((bm, bn), lambda j, i: (i, j)),
        out_shape=jax.ShapeDtypeStruct((m, n), BF16),
        scratch_shapes=[pltpu.VMEM(s, d) for s, d in scratch],
        compiler_params=_cparams(2, blocks, scratch, temps=bm * bn * 4 + 2 * CAST_ROWS * k * 4),
        name="in_proj",
    )(u, ssq, w_in_t, w_in_t)


def _mm_res_kernel(a_ref, w_ref, h_ref, g_ref, o_ref, hg_ref, ssq_ref, wsc_ref):
    @pl.when(_first_token_tile())
    def _():
        _cast_tile(w_ref, wsc_ref)

    h_new = h_ref[...] + jnp.dot(a_ref[...], wsc_ref[...], preferred_element_type=F32)
    o_ref[...] = h_new
    _emit_scaled(h_new, g_ref, hg_ref, ssq_ref)


def _matmul_residual(a, w, layer, h, g_next, name, in_place=True):
    m, k = a.shape
    n = w.shape[2]
    big_k = k > 2048
    bm = _pick(m, (512, 256, 128))
    bn = _pick(n, (1024, 512, 256, 128)) if big_k else _pick(n, (2048, 1024, 512, 256, 128))
    blocks = [((bm, k), BF16), ((bm, bn), F32), ((1, bn), F32),
              ((bm, bn), F32), ((bm, bn), BF16), ((1, bm), F32)]
    single = [((k, bn), F32)]
    scratch = [((k, bn), BF16)]
    return pl.pallas_call(
        _mm_res_kernel,
        grid=(n // bn, m // bm),
        in_specs=[pl.BlockSpec((bm, k), lambda j, i: (i, 0)),
                  pl.BlockSpec((None, k, bn), lambda j, i: (layer, 0, j), pipeline_mode=pl.Buffered(1)),
                  pl.BlockSpec((bm, bn), lambda j, i: (i, j)),
                  pl.BlockSpec((1, bn), lambda j, i: (0, j))],
        out_specs=[pl.BlockSpec((bm, bn), lambda j, i: (i, j)),
                   pl.BlockSpec((bm, bn), lambda j, i: (i, j)),
                   pl.BlockSpec((None, 1, bm), lambda j, i: (j, 0, i))],
        out_shape=[jax.ShapeDtypeStruct((m, n), F32), jax.ShapeDtypeStruct((m, n), BF16),
                   jax.ShapeDtypeStruct((n // bn, 1, m), F32)],
        scratch_shapes=[pltpu.VMEM(s, d) for s, d in scratch],
        input_output_aliases={2: 0} if in_place else {},
        compiler_params=_cparams(2, blocks, scratch, temps=2 * bm * bn * 4 + 2 * CAST_ROWS * bn * 4,
                                 single=single),
        name=name,
    )(a, w, h, g_next.reshape(1, n))


def _swiglu_kernel(v_ref, ssq_ref, wg_ref, wu_ref, o_ref, wgsc_ref, wusc_ref):
    @pl.when(_first_token_tile())
    def _():
        _cast_tile(wg_ref, wgsc_ref)
        _cast_tile(wu_ref, wusc_ref)

    rs_all = _row_rsqrt(ssq_ref, v_ref.shape[1])
    hm = v_ref.shape[0] // 2
    for r in (0, hm):
        v = v_ref[r:r + hm, :]
        rs = rs_all[r:r + hm]
        gate = jnp.dot(v, wgsc_ref[...], preferred_element_type=F32) * rs
        up = jnp.dot(v, wusc_ref[...], preferred_element_type=F32) * rs
        o_ref[r:r + hm, :] = (_silu(gate) * up).astype(o_ref.dtype)


def _swiglu(v, ssq, w_gate_up, layer):
    m, k = v.shape
    parts = ssq.shape[0]
    d_ff = w_gate_up.shape[2] // 2
    bm = _pick(m, (2048, 1024, 512, 256, 128))
    bn = _pick(d_ff, (512, 256, 128))
    nb = d_ff // bn
    blocks = [((bm, k), BF16), ((parts, 1, bm), F32), ((k, bn), F32), ((k, bn), F32), ((bm, bn), BF16)]
    scratch = [((k, bn), BF16), ((k, bn), BF16)]
    return pl.pallas_call(
        _swiglu_kernel,
        grid=(nb, m // bm),
        in_specs=[pl.BlockSpec((bm, k), lambda j, i: (i, 0)),
                  pl.BlockSpec((parts, 1, bm), lambda j, i: (0, 0, i)),
                  pl.BlockSpec((None, k, bn), lambda j, i: (layer, 0, j)),
                  pl.BlockSpec((None, k, bn), lambda j, i: (layer, 0, j + nb))],
        out_specs=pl.BlockSpec((bm, bn), lambda j, i: (i, j)),
        out_shape=jax.ShapeDtypeStruct((m, d_ff), BF16),
        scratch_shapes=[pltpu.VMEM(s, d) for s, d in scratch],
        compiler_params=_cparams(2, blocks, scratch, temps=3 * bm * bn * 4 + 2 * CAST_ROWS * bn * 4),
        name="swiglu_up",
    )(v, ssq, w_gate_up, w_gate_up)


def _merge_kernel(y_ref, wa_ref, ybg_ref, ga_ref, o_ref, wasc_ref):
    @pl.when(_first_token_tile())
    def _():
        _cast_tile(wa_ref, wasc_ref)

    ya = jnp.dot(y_ref[...], wasc_ref[...], preferred_element_type=F32)
    ga = _sigmoid(ga_ref[...].astype(F32))
    o_ref[...] = (ga * ya + ybg_ref[...].astype(F32)).astype(o_ref.dtype)


def _merge(y, w_a, layer, ybg, proj, off_ga):
    m, ka = y.shape
    n = w_a.shape[2]
    bm = _pick(m, (1024, 512, 256, 128))
    bn = _pick(n, (1024, 512, 256, 128))
    ja = off_ga // bn
    assert off_ga % bn == 0
    blocks = [((bm, ka), BF16), ((bm, bn), BF16), ((bm, bn), BF16), ((bm, bn), BF16)]
    single = [((ka, bn), F32)]
    scratch = [((ka, bn), BF16)]
    return pl.pallas_call(
        _merge_kernel,
        grid=(n // bn, m // bm),
        in_specs=[pl.BlockSpec((bm, ka), lambda j, i: (i, 0)),
                  pl.BlockSpec((None, ka, bn), lambda j, i: (layer, 0, j), pipeline_mode=pl.Buffered(1)),
                  pl.BlockSpec((bm, bn), lambda j, i: (i, j)),
                  pl.BlockSpec((bm, bn), lambda j, i: (i, j + ja))],
        out_specs=pl.BlockSpec((bm, bn), lambda j, i: (i, j)),
        out_shape=jax.ShapeDtypeStruct((m, n), BF16),
        scratch_shapes=[pltpu.VMEM(s, d) for s, d in scratch],
        compiler_params=_cparams(2, blocks, scratch, temps=4 * bm * bn * 4 + 2 * CAST_ROWS * bn * 4,
                                 single=single),
        name="branch_merge",
    )(y, w_a, ybg, proj)


def _ple_kernel(hn_ref, ssq_ref, wg_ref, p_ref, wp_ref, h_ref, g_ref, o_ref, *rest, last):
    hg_ref, ssqo_ref = (None, None) if last else rest[:2]
    wgsc_ref, wpsc_ref = rest[-2:]

    @pl.when(_first_token_tile())
    def _():
        _cast_tile(wg_ref, wgsc_ref)
        _cast_tile(wp_ref, wpsc_ref)

    rs = _row_rsqrt(ssq_ref, hn_ref.shape[1])
    pg = _sigmoid(jnp.dot(hn_ref[...], wgsc_ref[...], preferred_element_type=F32) * rs)
    e = jnp.dot(p_ref[...].astype(BF16), wpsc_ref[...], preferred_element_type=F32)
    h_new = h_ref[...] + pg * e
    if last:
        ms = jnp.mean(h_new * h_new, axis=-1, keepdims=True)
        o_ref[...] = h_new * lax.rsqrt(ms + EPS) * g_ref[...]
    else:
        o_ref[...] = h_new
        _emit_scaled(h_new, g_ref, hg_ref, ssqo_ref)


def _ple(hn, ssq, w_gate, p, w_proj, layer, h, g_next, last):
    m, k = hn.shape
    parts = ssq.shape[0]
    kp = p.shape[2]
    n = w_gate.shape[2]
    bm = _pick(m, (512, 256, 128))
    bn = _pick(n, (2048, 1024, 512, 256, 128))
    blocks = [((bm, k), BF16), ((parts, 1, bm), F32), ((bm, kp), F32),
              ((bm, bn), F32), ((1, bn), F32), ((bm, bn), F32), ((bm, bn), BF16), ((1, bm), F32)]
    single = [((k, bn), F32), ((kp, bn), F32)]
    scratch = [((k, bn), BF16), ((kp, bn), BF16)]
    assert not last or bn == n
    out_specs = [pl.BlockSpec((bm, bn), lambda j, i: (i, j)),
                 pl.BlockSpec((bm, bn), lambda j, i: (i, j)),
                 pl.BlockSpec((None, 1, bm), lambda j, i: (j, 0, i))]
    out_shape = [jax.ShapeDtypeStruct((m, n), F32), jax.ShapeDtypeStruct((m, n), BF16),
                 jax.ShapeDtypeStruct((n // bn, 1, m), F32)]
    return pl.pallas_call(
        functools.partial(_ple_kernel, last=last),
        grid=(n // bn, m // bm),
        in_specs=[pl.BlockSpec((bm, k), lambda j, i: (i, 0)),
                  pl.BlockSpec((parts, 1, bm), lambda j, i: (0, 0, i)),
                  pl.BlockSpec((None, k, bn), lambda j, i: (layer, 0, j), pipeline_mode=pl.Buffered(1)),
                  pl.BlockSpec((None, bm, kp), lambda j, i: (layer, i, 0)),
                  pl.BlockSpec((None, kp, bn), lambda j, i: (layer, 0, j), pipeline_mode=pl.Buffered(1)),
                  pl.BlockSpec((bm, bn), lambda j, i: (i, j)),
                  pl.BlockSpec((1, bn), lambda j, i: (0, j))],
        out_specs=out_specs[0] if last else out_specs,
        out_shape=out_shape[0] if last else out_shape,
        scratch_shapes=[pltpu.VMEM(s, d) for s, d in scratch],
        input_output_aliases={5: 0},
        compiler_params=_cparams(2, blocks, scratch, temps=4 * bm * bn * 4 + 2 * CAST_ROWS * bn * 4,
                                 single=single),
        name="ple",
    )(hn, ssq, w_gate, p, w_proj, h, g_next.reshape(1, n))


def _dtprep_kernel(u_ref, ssq_ref, w_ref, bias_ref, alog_ref, dtrow_ref, wrow_ref, acsrow_ref, acscol_ref,
                   *, nc, groups, r_heads):
    x = lax.dot_general(u_ref[...], w_ref[...].astype(BF16), (((1,), (1,)), ((), ())),
                        preferred_element_type=F32)
    x = x * _row_rsqrt(ssq_ref, u_ref.shape[1]) + bias_ref[...]
    dt = jnp.maximum(x, 0.0) + jnp.log1p(jnp.exp(-jnp.abs(x)))
    adt = dt * (-jnp.exp(alog_ref[...]))
    row = lax.broadcasted_iota(jnp.int32, (CHUNK, LANES), 0)
    for k in range(nc):
        sl = slice(k * CHUNK, (k + 1) * CHUNK)
        acs = adt[sl]
        sh = 1
        while sh < CHUNK:
            acs = acs + jnp.where(row >= sh, pltpu.roll(acs, sh, 0), 0.0)
            sh *= 2
        acs2 = acs * LOG2E
        acsrow_ref[k] = acs2.T
        dtrow_ref[k] = dt[sl].T
        wrow_ref[k] = (dt[sl] * jnp.exp(acs[CHUNK - 1:CHUNK, :] - acs)).T
        for g in range(groups):
            shift = (LANES - g * r_heads) % LANES
            acscol_ref[g, sl, :] = pltpu.roll(acs2, shift, 1) if shift else acs2


def _dtprep(u, ssq, w_in_t, layer, dt0, bias, a_log, groups, r_heads):
    m, k = u.shape
    parts = ssq.shape[0]
    nc = 4 if m % (4 * CHUNK) == 0 else 1
    ts = nc * CHUNK
    nchunks = m // CHUNK
    rows = ((nc, LANES, LANES), F32)
    blocks = [((ts, k), BF16), ((parts, 1, ts), F32), ((LANES, k), F32), rows, rows, rows,
              ((groups, ts, LANES), F32)]
    row_spec = pl.BlockSpec((nc, LANES, LANES), lambda i: (i, 0, 0))
    row_shape = jax.ShapeDtypeStruct((nchunks, LANES, LANES), F32)
    return pl.pallas_call(
        functools.partial(_dtprep_kernel, nc=nc, groups=groups, r_heads=r_heads),
        grid=(m // ts,),
        in_specs=[pl.BlockSpec((ts, k), lambda i: (i, 0)),
                  pl.BlockSpec((parts, 1, ts), lambda i: (0, 0, i)),
                  pl.BlockSpec((None, LANES, k), lambda i: (layer, dt0 // LANES, 0)),
                  pl.BlockSpec((1, LANES), lambda i: (0, 0)),
                  pl.BlockSpec((1, LANES), lambda i: (0, 0))],
        out_specs=[row_spec, row_spec, row_spec,
                   pl.BlockSpec((groups, ts, LANES), lambda i: (0, i, 0))],
        out_shape=[row_shape, row_shape, row_shape,
                   jax.ShapeDtypeStruct((groups, m, LANES), F32)],
        compiler_params=_cparams(1, blocks, temps=8 * ts * LANES * 4 + k * LANES * 2),
        name="dt_prep",
    )(u, ssq, w_in_t, bias, a_log)


def _ssd_kernel(z_ref, xs_ref, b_ref, c_ref, xsp_ref, bp_ref, cp_ref,
                dtr_ref, wr_ref, acr_ref, acc_ref, shift_ref,
                cwx_ref, cwb_ref, cwc_ref, cbx_ref, cbb_ref, cbc_ref, dsk_ref, nw_ref,
                o_ref, st_ref, yt_ref, *, nc, r_heads, kconv):
    g = pl.program_id(1)
    c = pl.program_id(2)
    L = CHUNK
    N = SSD_STATE
    gw = r_heads * HEAD_DIM
    npair = gw // LANES

    @pl.when(c == 0)
    def _():
        st_ref[...] = jnp.zeros(st_ref.shape, F32)

    tri = (lax.broadcasted_iota(jnp.int32, (L, L), 0) >= lax.broadcasted_iota(jnp.int32, (L, L), 1))
    lo = lax.broadcasted_iota(jnp.int32, (L, LANES), 1) < HEAD_DIM
    hi = jnp.logical_not(lo)
    lo_row = lo[0:1]

    def two_chunks(ref, prev_ref, k):
        if k == 0:
            prev = prev_ref[...]
            prev = jnp.where(c > 0, prev, jnp.zeros_like(prev))
            return jnp.concatenate([prev, ref[0:L, :]], axis=0)
        return ref[(k - 1) * L:(k + 1) * L, :]

    def conv_silu(shifted, cur, w_ref, bias_ref):
        acc = None
        for tap in range(kconv - 1):
            term = shifted[tap * L:(tap + 1) * L, :] * w_ref[tap:tap + 1, :]
            acc = term if acc is None else acc + term
        acc = acc + cur.astype(F32) * w_ref[kconv - 1:kconv, :]
        acc = acc + bias_ref[...]
        return _silu(acc)

    def conv_chunk(k):
        x2 = jnp.concatenate([two_chunks(xs_ref, xsp_ref, k), two_chunks(b_ref, bp_ref, k),
                              two_chunks(c_ref, cp_ref, k)], axis=1)
        shifted = jnp.dot(shift_ref[...], x2, preferred_element_type=F32)
        cur = x2[L:2 * L]
        return (conv_silu(shifted[:, 0:gw], cur[:, 0:gw], cwx_ref, cbx_ref),
                conv_silu(shifted[:, gw:gw + N], cur[:, gw:gw + N], cwb_ref, cbb_ref),
                conv_silu(shifted[:, gw + N:gw + 2 * N], cur[:, gw + N:gw + 2 * N], cwc_ref, cbc_ref))

    nxt = conv_chunk(0)
    for k in range(nc):
        r0 = k * L
        x, bm, cm = nxt
        if k + 1 < nc:
            nxt = conv_chunk(k + 1)
        cb = lax.dot_general(cm.astype(BF16), bm.astype(BF16), (((1,), (1,)), ((), ())),
                             preferred_element_type=F32)
        bt = bm.T
        acol = acc_ref[0, r0:r0 + L, :]
        cdec = jnp.exp2(acol[L - 1:L, :])
        ssq = jnp.zeros((L, 1), F32)
        for j in range(npair):
            cs = slice(j * LANES, (j + 1) * LANES)
            xp = x[:, cs]
            s_prev = st_ref[:, cs]
            s_new = s_prev * jnp.where(lo_row, cdec[:, 2 * j:2 * j + 1], cdec[:, 2 * j + 1:2 * j + 2])
            y = None
            for hh, keep in ((2 * j, lo), (2 * j + 1, hi)):
                head = pl.ds(g * r_heads + hh, 1)
                arow = acr_ref[k, head, :]
                drow = dtr_ref[k, head, :]
                wrow = wr_ref[k, head, :]
                xm = jnp.where(keep, xp, 0.0).astype(BF16)
                sm = jnp.where(keep, s_prev, 0.0).astype(BF16)
                ab = jnp.broadcast_to(acol[:, hh:hh + 1], (L, L))
                dec = jnp.exp2(jnp.where(tri, ab - arow, -jnp.inf))
                mh = cb * dec * drow
                ch = cm * jnp.exp2(ab)
                lhs = jnp.concatenate([mh, ch], axis=1).astype(BF16)
                rhs = jnp.concatenate([xm, sm], axis=0)
                t = jnp.dot(lhs, rhs, preferred_element_type=F32)
                y = t if y is None else y + t
                bth = (bt * wrow).astype(BF16)
                s_new = s_new + jnp.dot(bth, xm, preferred_element_type=F32)
            st_ref[:, cs] = s_new
            yt = y + dsk_ref[:, cs] * xp
            zt = z_ref[r0:r0 + L, cs].astype(F32)
            yt = yt * _silu(zt)
            ssq = ssq + jnp.sum(yt * yt, axis=-1, keepdims=True)
            yt_ref[:, cs] = yt
        rs = lax.rsqrt(ssq / gw + EPS)
        for j in range(npair):
            cs = slice(j * LANES, (j + 1) * LANES)
            o_ref[r0:r0 + L, cs] = (yt_ref[:, cs] * rs * nw_ref[:, cs]).astype(o_ref.dtype)


def _ssd(proj, dtrow, wrow, acsrow, acscol, conv_w, conv_b, d_skip, norm_w, *, batch, seq, inner, groups):
    m = proj.shape[0]
    gw = inner // groups
    r_heads = gw // HEAD_DIM
    assert gw % LANES == 0 and r_heads % 2 == 0 and SSD_STATE == LANES
    kconv = conv_w.shape[0]
    assert kconv - 1 <= CHUNK
    nc = _pick(seq // CHUNK, (16, 8, 4, 2, 1))
    ts = nc * CHUNK
    nt = seq // ts
    gn = groups * SSD_STATE
    xs_blk, b_blk, c_blk = inner // gw, 2 * inner // SSD_STATE, (2 * inner + gn) // SSD_STATE
    wb_blk, wc_blk = inner // SSD_STATE, (inner + gn) // SSD_STATE
    hp = acsrow.shape[1]
    t_idx = jnp.arange(CHUNK)[None, :, None]
    d_idx = (kconv - 1 - jnp.arange(kconv - 1))[:, None, None]
    s_idx = jnp.arange(2 * CHUNK)[None, None, :]
    shift = (s_idx == CHUNK + t_idx - d_idx).astype(BF16).reshape((kconv - 1) * CHUNK, 2 * CHUNK)
    st_shape = (SSD_STATE, gw)
    blocks = [((ts, gw), BF16), ((ts, gw), BF16), ((ts, SSD_STATE), BF16), ((ts, SSD_STATE), BF16),
              ((CHUNK, gw), BF16), ((CHUNK, SSD_STATE), BF16), ((CHUNK, SSD_STATE), BF16),
              ((nc, hp, LANES), F32), ((nc, hp, LANES), F32), ((nc, hp, LANES), F32),
              ((1, ts, LANES), F32), (shift.shape, BF16),
              ((kconv, gw), F32), ((kconv, SSD_STATE), F32), ((kconv, SSD_STATE), F32),
              ((1, gw), F32), ((1, SSD_STATE), F32), ((1, SSD_STATE), F32), ((1, gw), F32), ((1, gw), F32),
              ((ts, gw), BF16)]
    scratch = [(st_shape, F32), ((CHUNK, gw), F32)]
    row = lambda b, g, c: b * nt + c
    prev = lambda b, g, c: jnp.maximum(row(b, g, c) * nc - 1, 0)
    return pl.pallas_call(
        functools.partial(_ssd_kernel, nc=nc, r_heads=r_heads, kconv=kconv),
        grid=(batch, groups, nt),
        in_specs=[
            pl.BlockSpec((ts, gw), lambda b, g, c: (row(b, g, c), g)),
            pl.BlockSpec((ts, gw), lambda b, g, c: (row(b, g, c), xs_blk + g)),
            pl.BlockSpec((ts, SSD_STATE), lambda b, g, c: (row(b, g, c), b_blk + g)),
            pl.BlockSpec((ts, SSD_STATE), lambda b, g, c: (row(b, g, c), c_blk + g)),
            pl.BlockSpec((CHUNK, gw), lambda b, g, c: (prev(b, g, c), xs_blk + g)),
            pl.BlockSpec((CHUNK, SSD_STATE), lambda b, g, c: (prev(b, g, c), b_blk + g)),
            pl.BlockSpec((CHUNK, SSD_STATE), lambda b, g, c: (prev(b, g, c), c_blk + g)),
            pl.BlockSpec((nc, hp, LANES), lambda b, g, c: (row(b, g, c), 0, 0)),
            pl.BlockSpec((nc, hp, LANES), lambda b, g, c: (row(b, g, c), 0, 0)),
            pl.BlockSpec((nc, hp, LANES), lambda b, g, c: (row(b, g, c), 0, 0)),
            pl.BlockSpec((1, ts, LANES), lambda b, g, c: (g, row(b, g, c), 0)),
            pl.BlockSpec(shift.shape, lambda b, g, c: (0, 0)),
            pl.BlockSpec((kconv, gw), lambda b, g, c: (0, g)),
            pl.BlockSpec((kconv, SSD_STATE), lambda b, g, c: (0, wb_blk + g)),
            pl.BlockSpec((kconv, SSD_STATE), lambda b, g, c: (0, wc_blk + g)),
            pl.BlockSpec((1, gw), lambda b, g, c: (0, g)),
            pl.BlockSpec((1, SSD_STATE), lambda b, g, c: (0, wb_blk + g)),
            pl.BlockSpec((1, SSD_STATE), lambda b, g, c: (0, wc_blk + g)),
            pl.BlockSpec((1, gw), lambda b, g, c: (0, g)),
            pl.BlockSpec((1, gw), lambda b, g, c: (0, g)),
        ],
        out_specs=pl.BlockSpec((ts, gw), lambda b, g, c: (row(b, g, c), g)),
        out_shape=jax.ShapeDtypeStruct((m, inner), BF16),
        scratch_shapes=[pltpu.VMEM(s, d) for s, d in scratch],
        compiler_params=_cparams(3, blocks, scratch, temps=8 << 20),
        name="ssd",
    )(proj, proj, proj, proj, proj, proj, proj, dtrow, wrow, acsrow, acscol, shift,
      conv_w, conv_w, conv_w, conv_b, conv_b, conv_b, d_skip, norm_w)


def _sconv_kernel(gb_ref, gc_ref, xt_ref, gate_ref, w_ref, wo_ref, o_ref, pe_ref, ysc_ref, wsc_ref,
                  *, ts, kconv, cw):
    halo = SUBLANES

    @pl.when(jnp.logical_and(pl.program_id(0) == 0, pl.program_id(1) == 0))
    def _():
        _cast_tile(wo_ref, wsc_ref)

    @pl.when(pl.program_id(1) == 0)
    def _():
        pe_ref[0:halo, :] = jnp.zeros((halo, pe_ref.shape[1]), F32)

    pe_ref[halo:halo + ts, :] = gb_ref[...].astype(F32) * xt_ref[...].astype(F32)
    width = pe_ref.shape[1]
    for c0 in range(0, width, cw):
        acc = None
        for k in range(kconv):
            r = halo - (kconv - 1) + k
            term = pe_ref[r:r + ts, c0:c0 + cw] * w_ref[k:k + 1, c0:c0 + cw]
            acc = term if acc is None else acc + term
        ysc_ref[:, c0:c0 + cw] = (gc_ref[:, c0:c0 + cw].astype(F32) * acc).astype(BF16)
    pe_ref[0:halo, :] = pe_ref[ts:ts + halo, :]
    yb = jnp.dot(ysc_ref[...], wsc_ref[...], preferred_element_type=F32)
    o_ref[...] = (_sigmoid(gate_ref[...].astype(F32)) * yb).astype(o_ref.dtype)


def _sconv(proj, conv_w, w_out, layer, *, batch, seq, width, off_b, off_c, off_x, off_gate):
    m = proj.shape[0]
    kconv = conv_w.shape[0]
    n = w_out.shape[2]
    ts = _pick(seq, (512, 256, 128))
    nt = seq // ts
    assert off_b % width == 0 and off_c % width == 0 and off_x % width == 0 and kconv - 1 <= SUBLANES
    assert off_gate % n == 0
    jb, jc, jx, jg = off_b // width, off_c // width, off_x // width, off_gate // n
    cw = _pick(width, (512, 256, 128))
    pe_shape = (SUBLANES + ts, width)
    blocks = [((ts, width), BF16)] * 3 + [((ts, n), BF16), ((kconv, width), F32), ((ts, n), BF16)]
    single = [((width, n), F32)]
    scratch = [(pe_shape, F32), ((ts, width), BF16), ((width, n), BF16)]
    return pl.pallas_call(
        functools.partial(_sconv_kernel, ts=ts, kconv=kconv, cw=cw),
        grid=(batch, nt),
        in_specs=[pl.BlockSpec((ts, width), lambda b, c: (b * nt + c, jb)),
                  pl.BlockSpec((ts, width), lambda b, c: (b * nt + c, jc)),
                  pl.BlockSpec((ts, width), lambda b, c: (b * nt + c, jx)),
                  pl.BlockSpec((ts, n), lambda b, c: (b * nt + c, jg)),
                  pl.BlockSpec((kconv, width), lambda b, c: (0, 0)),
                  pl.BlockSpec((None, width, n), lambda b, c: (layer, 0, 0), pipeline_mode=pl.Buffered(1))],
        out_specs=pl.BlockSpec((ts, n), lambda b, c: (b * nt + c, 0)),
        out_shape=jax.ShapeDtypeStruct((m, n), BF16),
        scratch_shapes=[pltpu.VMEM(s, d) for s, d in scratch],
        compiler_params=_cparams(2, blocks, scratch, temps=(4 << 20) + 2 * ts * n * 4, single=single),
        name="short_conv",
    )(proj, proj, proj, proj, conv_w, w_out)


def kernel(x, p, norm_mix, w_in, ssd_conv_w, ssd_conv_b, ssd_dt_bias, ssd_a_log, ssd_d, ssd_norm,
           ssd_out, sc_conv_w, sc_out, w_o, norm_ffn, w_gate_up, w_down, norm_ple, ple_gate,
           ple_proj, norm_final):
    batch, seq, d = x.shape
    depth = w_in.shape[0]
    m = batch * seq
    heads = ssd_a_log.shape[1]
    inner = heads * HEAD_DIM
    groups = SSD_GROUPS
    r_heads = heads // groups
    xbc = ssd_conv_w.shape[2]
    scw = sc_conv_w.shape[2]
    assert xbc == inner + 2 * groups * SSD_STATE and heads <= LANES
    dt0 = inner + xbc
    off_scb = dt0
    off_scc, off_scx = off_scb + scw, off_scb + 2 * scw
    off_ga = off_scb + 3 * scw
    off_gb = off_ga + d
    hpad = LANES - heads
    p2 = p.reshape(depth, m, p.shape[-1])
    w_in_t = jnp.swapaxes(w_in, 1, 2)

    h = x.reshape(m, d)
    u, u_ssq = _prenorm(h, norm_mix[0])
    for i in range(depth):
        dt_bias = jnp.pad(ssd_dt_bias[i], (0, hpad)).reshape(1, LANES)
        a_log = jnp.pad(ssd_a_log[i], (0, hpad)).reshape(1, LANES)
        d_skip = jnp.repeat(ssd_d[i], HEAD_DIM).reshape(1, inner)

        proj = _inproj(u, u_ssq, w_in_t, i, dt0, heads)
        dtrow, wrow, acsrow, acscol = _dtprep(u, u_ssq, w_in_t, i, dt0, dt_bias, a_log, groups, r_heads)
        y = _ssd(proj, dtrow, wrow, acsrow, acscol, ssd_conv_w[i], ssd_conv_b[i].reshape(1, xbc), d_skip,
                 ssd_norm[i].reshape(1, inner), batch=batch, seq=seq, inner=inner, groups=groups)
        ybg = _sconv(proj, sc_conv_w[i], sc_out, i, batch=batch, seq=seq, width=scw,
                     off_b=off_scb, off_c=off_scc, off_x=off_scx, off_gate=off_gb)
        merged = _merge(y, ssd_out, i, ybg, proj, off_ga)
        h, v, v_ssq = _matmul_residual(merged, w_o, i, h, norm_ffn[i], "w_o_residual", in_place=i > 0)
        act = _swiglu(v, v_ssq, w_gate_up, i)
        h, hn, hn_ssq = _matmul_residual(act, w_down, i, h, norm_ple[i], "w_down_residual")
        if i + 1 < depth:
            h, u, u_ssq = _ple(hn, hn_ssq, ple_gate, p2, ple_proj, i, h, norm_mix[i + 1], last=False)
        else:
            out = _ple(hn, hn_ssq, ple_gate, p2, ple_proj, i, h, norm_final, last=True)
    return out.reshape(batch, seq, d)
```

```python
import functools

import jax
import jax.numpy as jnp
from jax import lax
from jax.experimental import pallas as pl
from jax.experimental.pallas import tpu as pltpu

F32 = jnp.float32
BF16 = jnp.bfloat16

EPS = 1e-6
HEAD_DIM = 64
SSD_GROUPS = 8
SSD_STATE = 128
CHUNK = 128
LANES = 128
SUBLANES = 8
CAST_ROWS = 256
LOG2E = 1.4426950408889634
VMEM_BYTES_V7X = 64 * 1024 * 1024
VMEM_CAP = VMEM_BYTES_V7X - 3 * 1024 * 1024


def _nbytes(shape, dtype):
    n = 1
    for s in shape:
        n *= s
    return n * jnp.dtype(dtype).itemsize


def _cparams(ngrid, blocks, scratch=(), temps=0, single=()):
    need = (2 * sum(_nbytes(s, d) for s, d in blocks) + sum(_nbytes(s, d) for s, d in single)
            + sum(_nbytes(s, d) for s, d in scratch) + temps + (4 << 20))
    return pltpu.CompilerParams(dimension_semantics=("arbitrary",) * ngrid,
                                vmem_limit_bytes=int(min(max(need, 16 << 20), VMEM_CAP)))


def _pick(n, candidates):
    for c in candidates:
        if n % c == 0:
            return c
    raise ValueError(f"no block size in {candidates} divides {n}")


def _cast_tile(w_ref, wsc_ref):
    k = w_ref.shape[0]
    rc = _pick(k, (CAST_ROWS, LANES, SUBLANES))
    for r in range(0, k, rc):
        wsc_ref[r:r + rc, :] = w_ref[r:r + rc, :].astype(BF16)


def _first_token_tile():
    return pl.program_id(1) == 0


def _sigmoid(x):
    return 0.5 * jnp.tanh(0.5 * x) + 0.5


def _silu(x):
    h = 0.5 * x
    return h * jnp.tanh(h) + h


def _emit_scaled(h_new, g_ref, hg_ref, ssq_ref):
    hg_ref[...] = (h_new * g_ref[...]).astype(BF16)
    col = jnp.sum(h_new * h_new, axis=-1, keepdims=True)
    ssq_ref[...] = jnp.transpose(jnp.broadcast_to(col, (col.shape[0], LANES)))[0:1, :]


def _row_rsqrt(ssq_ref, d):
    s = ssq_ref[0]
    for q in range(1, ssq_ref.shape[0]):
        s = s + ssq_ref[q]
    row = lax.rsqrt(s / d + EPS)
    return jnp.transpose(jnp.broadcast_to(row, (LANES, row.shape[1])))[:, 0:1]


def _prenorm_kernel(x_ref, g_ref, hg_ref, ssq_ref):
    _emit_scaled(x_ref[...], g_ref, hg_ref, ssq_ref.at[0])


def _prenorm(x, g):
    m, d = x.shape
    bm = _pick(m, (512, 256, 128))
    blocks = [((bm, d), F32), ((1, d), F32), ((bm, d), BF16), ((1, 1, bm), F32)]
    return pl.pallas_call(
        _prenorm_kernel,
        grid=(m // bm,),
        in_specs=[pl.BlockSpec((bm, d), lambda i: (i, 0)),
                  pl.BlockSpec((1, d), lambda i: (0, 0))],
        out_specs=[pl.BlockSpec((bm, d), lambda i: (i, 0)),
                   pl.BlockSpec((1, 1, bm), lambda i: (0, 0, i))],
        out_shape=[jax.ShapeDtypeStruct((m, d), BF16), jax.ShapeDtypeStruct((1, 1, m), F32)],
        compiler_params=_cparams(1, blocks, temps=2 * bm * d * 4),
        name="prenorm",
    )(x, g.reshape(1, d))


def _inproj_kernel(a_ref, ssq_ref, w_ref, wn_ref, o_ref, wsc_ref, *, n_aligned, skip):
    j = pl.program_id(0)
    bn = w_ref.shape[0]
    rc = _pick(bn, (CAST_ROWS, LANES))

    @pl.when(jnp.logical_and(_first_token_tile(), j < n_aligned))
    def _():
        _cast_tile(w_ref, wsc_ref)

    @pl.when(jnp.logical_and(_first_token_tile(), j >= n_aligned))
    def _():
        for r in range(0, bn - rc, rc):
            wsc_ref[r:r + rc, :] = w_ref[r + skip:r + skip + rc, :].astype(BF16)
        wsc_ref[bn - rc:bn - skip, :] = w_ref[bn - rc + skip:bn, :].astype(BF16)
        wsc_ref[bn - skip:bn, :] = wn_ref[...].astype(BF16)

    rs = _row_rsqrt(ssq_ref, a_ref.shape[1])
    hm = a_ref.shape[0] // 2
    for r in (0, hm):
        acc = lax.dot_general(a_ref[r:r + hm, :], wsc_ref[...], (((1,), (1,)), ((), ())),
                              preferred_element_type=F32)
        o_ref[r:r + hm, :] = (acc * rs[r:r + hm]).astype(o_ref.dtype)


def _inproj(u, ssq, w_in_t, layer, dt0, heads):
    m, k = u.shape
    parts = ssq.shape[0]
    d_in = w_in_t.shape[1]
    n = d_in - heads
    bm = _pick(m, (2048, 1024, 512, 256, 128))
    bn = next(c for c in (1024, 512, 256, 128) if dt0 % c == 0 and (n - dt0) % c == 0)
    assert heads % (2 * SUBLANES) == 0 and bn % heads == 0 and d_in % heads == 0 and heads < LANES
    n_aligned = dt0 // bn
    blocks = [((bm, k), BF16), ((parts, 1, bm), F32), ((bn, k), F32), ((heads, k), F32), ((bm, bn), BF16)]
    scratch = [((bn, k), BF16)]
    return pl.pallas_call(
        functools.partial(_inproj_kernel, n_aligned=n_aligned, skip=heads),
        grid=(n // bn, m // bm),
        in_specs=[pl.BlockSpec((bm, k), lambda j, i: (i, 0)),
                  pl.BlockSpec((parts, 1, bm), lambda j, i: (0, 0, i)),
                  pl.BlockSpec((None, bn, k), lambda j, i: (layer, j, 0)),
                  pl.BlockSpec((None, heads, k), lambda j, i: (layer, (j + 1) * (bn // heads), 0))],
        out_specs=pl.BlockSpec((bm, bn), lambda j, i: (i, j)),
        out_shape=jax.ShapeDtypeStruct((m, n), BF16),
        scratch_shapes=[pltpu.VMEM(s, d) for s, d in scratch],
        compiler_params=_cparams(2, blocks, scratch, temps=bm * bn * 4 + 2 * CAST_ROWS * k * 4),
        name="in_proj",
    )(u, ssq, w_in_t, w_in_t)


def _mm_res_kernel(a_ref, w_ref, h_ref, g_ref, o_ref, hg_ref, ssq_ref, wsc_ref):
    @pl.when(_first_token_tile())
    def _():
        _cast_tile(w_ref, wsc_ref)

    h_new = h_ref[...] + jnp.dot(a_ref[...], wsc_ref[...], preferred_element_type=F32)
    o_ref[...] = h_new
    _emit_scaled(h_new, g_ref, hg_ref, ssq_ref)


def _matmul_residual(a, w, layer, h, g_next, name, in_place=True):
    m, k = a.shape
    n = w.shape[2]
    big_k = k > 2048
    bm = _pick(m, (512, 256, 128))
    bn = _pick(n, (1024, 512, 256, 128)) if big_k else _pick(n, (2048, 1024, 512, 256, 128))
    blocks = [((bm, k), BF16), ((bm, bn), F32), ((1, bn), F32),
              ((bm, bn), F32), ((bm, bn), BF16), ((1, bm), F32)]
    single = [((k, bn), F32)]
    scratch = [((k, bn), BF16)]
    return pl.pallas_call(
        _mm_res_kernel,
        grid=(n // bn, m // bm),
        in_specs=[pl.BlockSpec((bm, k), lambda j, i: (i, 0)),
                  pl.BlockSpec((None, k, bn), lambda j, i: (layer, 0, j), pipeline_mode=pl.Buffered(1)),
                  pl.BlockSpec((bm, bn), lambda j, i: (i, j)),
                  pl.BlockSpec((1, bn), lambda j, i: (0, j))],
        out_specs=[pl.BlockSpec((bm, bn), lambda j, i: (i, j)),
                   pl.BlockSpec((bm, bn), lambda j, i: (i, j)),
                   pl.BlockSpec((None, 1, bm), lambda j, i: (j, 0, i))],
        out_shape=[jax.ShapeDtypeStruct((m, n), F32), jax.ShapeDtypeStruct((m, n), BF16),
                   jax.ShapeDtypeStruct((n // bn, 1, m), F32)],
        scratch_shapes=[pltpu.VMEM(s, d) for s, d in scratch],
        input_output_aliases={2: 0} if in_place else {},
        compiler_params=_cparams(2, blocks, scratch, temps=2 * bm * bn * 4 + 2 * CAST_ROWS * bn * 4,
                                 single=single),
        name=name,
    )(a, w, h, g_next.reshape(1, n))


def _swiglu_kernel(v_ref, ssq_ref, wg_ref, wu_ref, o_ref, wgsc_ref, wusc_ref):
    @pl.when(_first_token_tile())
    def _():
        _cast_tile(wg_ref, wgsc_ref)
        _cast_tile(wu_ref, wusc_ref)

    rs_all = _row_rsqrt(ssq_ref, v_ref.shape[1])
    hm = v_ref.shape[0] // 2
    for r in (0, hm):
        v = v_ref[r:r + hm, :]
        rs = rs_all[r:r + hm]
        gate = jnp.dot(v, wgsc_ref[...], preferred_element_type=F32) * rs
        up = jnp.dot(v, wusc_ref[...], preferred_element_type=F32) * rs
        o_ref[r:r + hm, :] = (_silu(gate) * up).astype(o_ref.dtype)


def _swiglu(v, ssq, w_gate_up, layer):
    m, k = v.shape
    parts = ssq.shape[0]
    d_ff = w_gate_up.shape[2] // 2
    bm = _pick(m, (2048, 1024, 512, 256, 128))
    bn = _pick(d_ff, (512, 256, 128))
    nb = d_ff // bn
    blocks = [((bm, k), BF16), ((parts, 1, bm), F32), ((k, bn), F32), ((k, bn), F32), ((bm, bn), BF16)]
    scratch = [((k, bn), BF16), ((k, bn), BF16)]
    return pl.pallas_call(
        _swiglu_kernel,
        grid=(nb, m // bm),
        in_specs=[pl.BlockSpec((bm, k), lambda j, i: (i, 0)),
                  pl.BlockSpec((parts, 1, bm), lambda j, i: (0, 0, i)),
                  pl.BlockSpec((None, k, bn), lambda j, i: (layer, 0, j)),
                  pl.BlockSpec((None, k, bn), lambda j, i: (layer, 0, j + nb))],
        out_specs=pl.BlockSpec((bm, bn), lambda j, i: (i, j)),
        out_shape=jax.ShapeDtypeStruct((m, d_ff), BF16),
        scratch_shapes=[pltpu.VMEM(s, d) for s, d in scratch],
        compiler_params=_cparams(2, blocks, scratch, temps=3 * bm * bn * 4 + 2 * CAST_ROWS * bn * 4),
        name="swiglu_up",
    )(v, ssq, w_gate_up, w_gate_up)


def _merge_kernel(y_ref, wa_ref, ybg_ref, ga_ref, o_ref, wasc_ref):
    @pl.when(_first_token_tile())
    def _():
        _cast_tile(wa_ref, wasc_ref)

    hn = o_ref.shape[1] // 2
    for c0 in (0, hn):
        cols = slice(c0, c0 + hn)
        ya = jnp.dot(y_ref[...], wasc_ref[:, cols], preferred_element_type=F32)
        ga = _sigmoid(ga_ref[:, cols].astype(F32))
        o_ref[:, cols] = (ga * ya + ybg_ref[:, cols].astype(F32)).astype(o_ref.dtype)


def _merge(y, w_a, layer, ybg, proj, off_ga):
    m, ka = y.shape
    n = w_a.shape[2]
    bm = _pick(m, (1024, 512, 256, 128))
    bn = _pick(n, (1024, 512, 256, 128))
    ja = off_ga // bn
    assert off_ga % bn == 0
    blocks = [((bm, ka), BF16), ((bm, bn), BF16), ((bm, bn), BF16), ((bm, bn), BF16)]
    single = [((ka, bn), F32)]
    scratch = [((ka, bn), BF16)]
    return pl.pallas_call(
        _merge_kernel,
        grid=(n // bn, m // bm),
        in_specs=[pl.BlockSpec((bm, ka), lambda j, i: (i, 0)),
                  pl.BlockSpec((None, ka, bn), lambda j, i: (layer, 0, j), pipeline_mode=pl.Buffered(1)),
                  pl.BlockSpec((bm, bn), lambda j, i: (i, j)),
                  pl.BlockSpec((bm, bn), lambda j, i: (i, j + ja))],
        out_specs=pl.BlockSpec((bm, bn), lambda j, i: (i, j)),
        out_shape=jax.ShapeDtypeStruct((m, n), BF16),
        scratch_shapes=[pltpu.VMEM(s, d) for s, d in scratch],
        compiler_params=_cparams(2, blocks, scratch, temps=4 * bm * bn * 4 + 2 * CAST_ROWS * bn * 4,
                                 single=single),
        name="branch_merge",
    )(y, w_a, ybg, proj)


def _ple_kernel(hn_ref, ssq_ref, wg_ref, p_ref, wp_ref, h_ref, g_ref, o_ref, *rest, last):
    hg_ref, ssqo_ref = (None, None) if last else rest[:2]
    wgsc_ref, wpsc_ref = rest[-2:]

    @pl.when(_first_token_tile())
    def _():
        _cast_tile(wg_ref, wgsc_ref)
        _cast_tile(wp_ref, wpsc_ref)

    rs = _row_rsqrt(ssq_ref, hn_ref.shape[1])
    pg = _sigmoid(jnp.dot(hn_ref[...], wgsc_ref[...], preferred_element_type=F32) * rs)
    e = jnp.dot(p_ref[...].astype(BF16), wpsc_ref[...], preferred_element_type=F32)
    h_new = h_ref[...] + pg * e
    if last:
        ms = jnp.mean(h_new * h_new, axis=-1, keepdims=True)
        o_ref[...] = h_new * lax.rsqrt(ms + EPS) * g_ref[...]
    else:
        o_ref[...] = h_new
        _emit_scaled(h_new, g_ref, hg_ref, ssqo_ref)


def _ple(hn, ssq, w_gate, p, w_proj, layer, h, g_next, last):
    m, k = hn.shape
    parts = ssq.shape[0]
    kp = p.shape[2]
    n = w_gate.shape[2]
    bm = _pick(m, (512, 256, 128))
    bn = _pick(n, (2048, 1024, 512, 256, 128))
    blocks = [((bm, k), BF16), ((parts, 1, bm), F32), ((bm, kp), F32),
              ((bm, bn), F32), ((1, bn), F32), ((bm, bn), F32), ((bm, bn), BF16), ((1, bm), F32)]
    single = [((k, bn), F32), ((kp, bn), F32)]
    scratch = [((k, bn), BF16), ((kp, bn), BF16)]
    assert not last or bn == n
    out_specs = [pl.BlockSpec((bm, bn), lambda j, i: (i, j)),
                 pl.BlockSpec((bm, bn), lambda j, i: (i, j)),
                 pl.BlockSpec((None, 1, bm), lambda j, i: (j, 0, i))]
    out_shape = [jax.ShapeDtypeStruct((m, n), F32), jax.ShapeDtypeStruct((m, n), BF16),
                 jax.ShapeDtypeStruct((n // bn, 1, m), F32)]
    return pl.pallas_call(
        functools.partial(_ple_kernel, last=last),
        grid=(n // bn, m // bm),
        in_specs=[pl.BlockSpec((bm, k), lambda j, i: (i, 0)),
                  pl.BlockSpec((parts, 1, bm), lambda j, i: (0, 0, i)),
                  pl.BlockSpec((None, k, bn), lambda j, i: (layer, 0, j), pipeline_mode=pl.Buffered(1)),
                  pl.BlockSpec((None, bm, kp), lambda j, i: (layer, i, 0)),
                  pl.BlockSpec((None, kp, bn), lambda j, i: (layer, 0, j), pipeline_mode=pl.Buffered(1)),
                  pl.BlockSpec((bm, bn), lambda j, i: (i, j)),
                  pl.BlockSpec((1, bn), lambda j, i: (0, j))],
        out_specs=out_specs[0] if last else out_specs,
        out_shape=out_shape[0] if last else out_shape,
        scratch_shapes=[pltpu.VMEM(s, d) for s, d in scratch],
        input_output_aliases={5: 0},
        compiler_params=_cparams(2, blocks, scratch, temps=4 * bm * bn * 4 + 2 * CAST_ROWS * bn * 4,
                                 single=single),
        name="ple",
    )(hn, ssq, w_gate, p, w_proj, h, g_next.reshape(1, n))


def _dtprep_kernel(u_ref, ssq_ref, w_ref, bias_ref, alog_ref, dtrow_ref, wrow_ref, acsrow_ref, acscol_ref,
                   *, nc, groups, r_heads):
    x = lax.dot_general(u_ref[...], w_ref[...].astype(BF16), (((1,), (1,)), ((), ())),
                        preferred_element_type=F32)
    x = x * _row_rsqrt(ssq_ref, u_ref.shape[1]) + bias_ref[...]
    dt = jnp.maximum(x, 0.0) + jnp.log1p(jnp.exp(-jnp.abs(x)))
    adt = dt * (-jnp.exp(alog_ref[...]))
    row = lax.broadcasted_iota(jnp.int32, (CHUNK, LANES), 0)
    for k in range(nc):
        sl = slice(k * CHUNK, (k + 1) * CHUNK)
        acs = adt[sl]
        sh = 1
        while sh < CHUNK:
            acs = acs + jnp.where(row >= sh, pltpu.roll(acs, sh, 0), 0.0)
            sh *= 2
        acs2 = acs * LOG2E
        acsrow_ref[k] = acs2.T
        dtrow_ref[k] = dt[sl].T
        wrow_ref[k] = (dt[sl] * jnp.exp(acs[CHUNK - 1:CHUNK, :] - acs)).T
        for g in range(groups):
            shift = (LANES - g * r_heads) % LANES
            acscol_ref[g, sl, :] = pltpu.roll(acs2, shift, 1) if shift else acs2


def _dtprep(u, ssq, w_in_t, layer, dt0, bias, a_log, groups, r_heads):
    m, k = u.shape
    parts = ssq.shape[0]
    nc = 4 if m % (4 * CHUNK) == 0 else 1
    ts = nc * CHUNK
    nchunks = m // CHUNK
    rows = ((nc, LANES, LANES), F32)
    blocks = [((ts, k), BF16), ((parts, 1, ts), F32), ((LANES, k), F32), rows, rows, rows,
              ((groups, ts, LANES), F32)]
    row_spec = pl.BlockSpec((nc, LANES, LANES), lambda i: (i, 0, 0))
    row_shape = jax.ShapeDtypeStruct((nchunks, LANES, LANES), F32)
    return pl.pallas_call(
        functools.partial(_dtprep_kernel, nc=nc, groups=groups, r_heads=r_heads),
        grid=(m // ts,),
        in_specs=[pl.BlockSpec((ts, k), lambda i: (i, 0)),
                  pl.BlockSpec((parts, 1, ts), lambda i: (0, 0, i)),
                  pl.BlockSpec((None, LANES, k), lambda i: (layer, dt0 // LANES, 0)),
                  pl.BlockSpec((1, LANES), lambda i: (0, 0)),
                  pl.BlockSpec((1, LANES), lambda i: (0, 0))],
        out_specs=[row_spec, row_spec, row_spec,
                   pl.BlockSpec((groups, ts, LANES), lambda i: (0, i, 0))],
        out_shape=[row_shape, row_shape, row_shape,
                   jax.ShapeDtypeStruct((groups, m, LANES), F32)],
        compiler_params=_cparams(1, blocks, temps=8 * ts * LANES * 4 + k * LANES * 2),
        name="dt_prep",
    )(u, ssq, w_in_t, bias, a_log)


def _ssd_kernel(z_ref, xs_ref, b_ref, c_ref, xsp_ref, bp_ref, cp_ref,
                dtr_ref, wr_ref, acr_ref, acc_ref, shift_ref,
                cwx_ref, cwb_ref, cwc_ref, cbx_ref, cbb_ref, cbc_ref, dsk_ref, nw_ref,
                o_ref, st_ref, yt_ref, *, nc, r_heads, kconv):
    g = pl.program_id(1)
    c = pl.program_id(2)
    L = CHUNK
    N = SSD_STATE
    gw = r_heads * HEAD_DIM
    npair = gw // LANES

    @pl.when(c == 0)
    def _():
        st_ref[...] = jnp.zeros(st_ref.shape, F32)

    tri = (lax.broadcasted_iota(jnp.int32, (L, L), 0) >= lax.broadcasted_iota(jnp.int32, (L, L), 1))
    lo = lax.broadcasted_iota(jnp.int32, (L, LANES), 1) < HEAD_DIM
    hi = jnp.logical_not(lo)
    lo_row = lo[0:1]

    def two_chunks(ref, prev_ref, k):
        if k == 0:
            prev = prev_ref[...]
            prev = jnp.where(c > 0, prev, jnp.zeros_like(prev))
            return jnp.concatenate([prev, ref[0:L, :]], axis=0)
        return ref[(k - 1) * L:(k + 1) * L, :]

    def conv_silu(shifted, cur, w_ref, bias_ref):
        acc = None
        for tap in range(kconv - 1):
            term = shifted[tap * L:(tap + 1) * L, :] * w_ref[tap:tap + 1, :]
            acc = term if acc is None else acc + term
        acc = acc + cur.astype(F32) * w_ref[kconv - 1:kconv, :]
        acc = acc + bias_ref[...]
        return _silu(acc)

    def conv_chunk(k):
        x2 = jnp.concatenate([two_chunks(xs_ref, xsp_ref, k), two_chunks(b_ref, bp_ref, k),
                              two_chunks(c_ref, cp_ref, k)], axis=1)
        shifted = jnp.dot(shift_ref[...], x2, preferred_element_type=F32)
        cur = x2[L:2 * L]
        return (conv_silu(shifted[:, 0:gw], cur[:, 0:gw], cwx_ref, cbx_ref),
                conv_silu(shifted[:, gw:gw + N], cur[:, gw:gw + N], cwb_ref, cbb_ref),
                conv_silu(shifted[:, gw + N:gw + 2 * N], cur[:, gw + N:gw + 2 * N], cwc_ref, cbc_ref))

    nxt = conv_chunk(0)
    for k in range(nc):
        r0 = k * L
        x, bm, cm = nxt
        if k + 1 < nc:
            nxt = conv_chunk(k + 1)
        cb = lax.dot_general(cm.astype(BF16), bm.astype(BF16), (((1,), (1,)), ((), ())),
                             preferred_element_type=F32)
        bt = bm.T
        acol = acc_ref[0, r0:r0 + L, :]
        cdec = jnp.exp2(acol[L - 1:L, :])
        ssq = jnp.zeros((L, 1), F32)
        for j in range(npair):
            cs = slice(j * LANES, (j + 1) * LANES)
            xp = x[:, cs]
            s_prev = st_ref[:, cs]
            s_new = s_prev * jnp.where(lo_row, cdec[:, 2 * j:2 * j + 1], cdec[:, 2 * j + 1:2 * j + 2])
            y = None
            for hh, keep in ((2 * j, lo), (2 * j + 1, hi)):
                head = pl.ds(g * r_heads + hh, 1)
                arow = acr_ref[k, head, :]
                drow = dtr_ref[k, head, :]
                wrow = wr_ref[k, head, :]
                xm = jnp.where(keep, xp, 0.0).astype(BF16)
                sm = jnp.where(keep, s_prev, 0.0).astype(BF16)
                ab = jnp.broadcast_to(acol[:, hh:hh + 1], (L, L))
                dec = jnp.exp2(jnp.where(tri, ab - arow, -jnp.inf))
                mh = cb * dec * drow
                ch = cm * jnp.exp2(ab)
                lhs = jnp.concatenate([mh, ch], axis=1).astype(BF16)
                rhs = jnp.concatenate([xm, sm], axis=0)
                t = jnp.dot(lhs, rhs, preferred_element_type=F32)
                y = t if y is None else y + t
                bth = (bt * wrow).astype(BF16)
                s_new = s_new + jnp.dot(bth, xm, preferred_element_type=F32)
            st_ref[:, cs] = s_new
            yt = y + dsk_ref[:, cs] * xp
            zt = z_ref[r0:r0 + L, cs].astype(F32)
            yt = yt * _silu(zt)
            ssq = ssq + jnp.sum(yt * yt, axis=-1, keepdims=True)
            yt_ref[:, cs] = yt
        rs = lax.rsqrt(ssq / gw + EPS)
        for j in range(npair):
            cs = slice(j * LANES, (j + 1) * LANES)
            o_ref[r0:r0 + L, cs] = (yt_ref[:, cs] * rs * nw_ref[:, cs]).astype(o_ref.dtype)


def _ssd(proj, dtrow, wrow, acsrow, acscol, conv_w, conv_b, d_skip, norm_w, *, batch, seq, inner, groups):
    m = proj.shape[0]
    gw = inner // groups
    r_heads = gw // HEAD_DIM
    assert gw % LANES == 0 and r_heads % 2 == 0 and SSD_STATE == LANES
    kconv = conv_w.shape[0]
    assert kconv - 1 <= CHUNK
    nc = _pick(seq // CHUNK, (16, 8, 4, 2, 1))
    ts = nc * CHUNK
    nt = seq // ts
    gn = groups * SSD_STATE
    xs_blk, b_blk, c_blk = inner // gw, 2 * inner // SSD_STATE, (2 * inner + gn) // SSD_STATE
    wb_blk, wc_blk = inner // SSD_STATE, (inner + gn) // SSD_STATE
    hp = acsrow.shape[1]
    t_idx = jnp.arange(CHUNK)[None, :, None]
    d_idx = (kconv - 1 - jnp.arange(kconv - 1))[:, None, None]
    s_idx = jnp.arange(2 * CHUNK)[None, None, :]
    shift = (s_idx == CHUNK + t_idx - d_idx).astype(BF16).reshape((kconv - 1) * CHUNK, 2 * CHUNK)
    st_shape = (SSD_STATE, gw)
    blocks = [((ts, gw), BF16), ((ts, gw), BF16), ((ts, SSD_STATE), BF16), ((ts, SSD_STATE), BF16),
              ((CHUNK, gw), BF16), ((CHUNK, SSD_STATE), BF16), ((CHUNK, SSD_STATE), BF16),
              ((nc, hp, LANES), F32), ((nc, hp, LANES), F32), ((nc, hp, LANES), F32),
              ((1, ts, LANES), F32), (shift.shape, BF16),
              ((kconv, gw), F32), ((kconv, SSD_STATE), F32), ((kconv, SSD_STATE), F32),
              ((1, gw), F32), ((1, SSD_STATE), F32), ((1, SSD_STATE), F32), ((1, gw), F32), ((1, gw), F32),
              ((ts, gw), BF16)]
    scratch = [(st_shape, F32), ((CHUNK, gw), F32)]
    row = lambda b, g, c: b * nt + c
    prev = lambda b, g, c: jnp.maximum(row(b, g, c) * nc - 1, 0)
    return pl.pallas_call(
        functools.partial(_ssd_kernel, nc=nc, r_heads=r_heads, kconv=kconv),
        grid=(batch, groups, nt),
        in_specs=[
            pl.BlockSpec((ts, gw), lambda b, g, c: (row(b, g, c), g)),
            pl.BlockSpec((ts, gw), lambda b, g, c: (row(b, g, c), xs_blk + g)),
            pl.BlockSpec((ts, SSD_STATE), lambda b, g, c: (row(b, g, c), b_blk + g)),
            pl.BlockSpec((ts, SSD_STATE), lambda b, g, c: (row(b, g, c), c_blk + g)),
            pl.BlockSpec((CHUNK, gw), lambda b, g, c: (prev(b, g, c), xs_blk + g)),
            pl.BlockSpec((CHUNK, SSD_STATE), lambda b, g, c: (prev(b, g, c), b_blk + g)),
            pl.BlockSpec((CHUNK, SSD_STATE), lambda b, g, c: (prev(b, g, c), c_blk + g)),
            pl.BlockSpec((nc, hp, LANES), lambda b, g, c: (row(b, g, c), 0, 0)),
            pl.BlockSpec((nc, hp, LANES), lambda b, g, c: (row(b, g, c), 0, 0)),
            pl.BlockSpec((nc, hp, LANES), lambda b, g, c: (row(b, g, c), 0, 0)),
            pl.BlockSpec((1, ts, LANES), lambda b, g, c: (g, row(b, g, c), 0)),
            pl.BlockSpec(shift.shape, lambda b, g, c: (0, 0)),
            pl.BlockSpec((kconv, gw), lambda b, g, c: (0, g)),
            pl.BlockSpec((kconv, SSD_STATE), lambda b, g, c: (0, wb_blk + g)),
            pl.BlockSpec((kconv, SSD_STATE), lambda b, g, c: (0, wc_blk + g)),
            pl.BlockSpec((1, gw), lambda b, g, c: (0, g)),
            pl.BlockSpec((1, SSD_STATE), lambda b, g, c: (0, wb_blk + g)),
            pl.BlockSpec((1, SSD_STATE), lambda b, g, c: (0, wc_blk + g)),
            pl.BlockSpec((1, gw), lambda b, g, c: (0, g)),
            pl.BlockSpec((1, gw), lambda b, g, c: (0, g)),
        ],
        out_specs=pl.BlockSpec((ts, gw), lambda b, g, c: (row(b, g, c), g)),
        out_shape=jax.ShapeDtypeStruct((m, inner), BF16),
        scratch_shapes=[pltpu.VMEM(s, d) for s, d in scratch],
        compiler_params=_cparams(3, blocks, scratch, temps=8 << 20),
        name="ssd",
    )(proj, proj, proj, proj, proj, proj, proj, dtrow, wrow, acsrow, acscol, shift,
      conv_w, conv_w, conv_w, conv_b, conv_b, conv_b, d_skip, norm_w)


def _sconv_kernel(gb_ref, gc_ref, xt_ref, gate_ref, w_ref, wo_ref, o_ref, pe_ref, ysc_ref, wsc_ref,
                  *, ts, kconv, cw):
    halo = SUBLANES

    @pl.when(jnp.logical_and(pl.program_id(0) == 0, pl.program_id(1) == 0))
    def _():
        _cast_tile(wo_ref, wsc_ref)

    @pl.when(pl.program_id(1) == 0)
    def _():
        pe_ref[0:halo, :] = jnp.zeros((halo, pe_ref.shape[1]), F32)

    pe_ref[halo:halo + ts, :] = gb_ref[...].astype(F32) * xt_ref[...].astype(F32)
    width = pe_ref.shape[1]
    for c0 in range(0, width, cw):
        acc = None
        for k in range(kconv):
            r = halo - (kconv - 1) + k
            term = pe_ref[r:r + ts, c0:c0 + cw] * w_ref[k:k + 1, c0:c0 + cw]
            acc = term if acc is None else acc + term
        ysc_ref[:, c0:c0 + cw] = (gc_ref[:, c0:c0 + cw].astype(F32) * acc).astype(BF16)
    pe_ref[0:halo, :] = pe_ref[ts:ts + halo, :]
    yb = jnp.dot(ysc_ref[...], wsc_ref[...], preferred_element_type=F32)
    o_ref[...] = (_sigmoid(gate_ref[...].astype(F32)) * yb).astype(o_ref.dtype)


def _sconv(proj, conv_w, w_out, layer, *, batch, seq, width, off_b, off_c, off_x, off_gate):
    m = proj.shape[0]
    kconv = conv_w.shape[0]
    n = w_out.shape[2]
    ts = _pick(seq, (512, 256, 128))
    nt = seq // ts
    assert off_b % width == 0 and off_c % width == 0 and off_x % width == 0 and kconv - 1 <= SUBLANES
    assert off_gate % n == 0
    jb, jc, jx, jg = off_b // width, off_c // width, off_x // width, off_gate // n
    cw = _pick(width, (512, 256, 128))
    pe_shape = (SUBLANES + ts, width)
    blocks = [((ts, width), BF16)] * 3 + [((ts, n), BF16), ((kconv, width), F32), ((ts, n), BF16)]
    single = [((width, n), F32)]
    scratch = [(pe_shape, F32), ((ts, width), BF16), ((width, n), BF16)]
    return pl.pallas_call(
        functools.partial(_sconv_kernel, ts=ts, kconv=kconv, cw=cw),
        grid=(batch, nt),
        in_specs=[pl.BlockSpec((ts, width), lambda b, c: (b * nt + c, jb)),
                  pl.BlockSpec((ts, width), lambda b, c: (b * nt + c, jc)),
                  pl.BlockSpec((ts, width), lambda b, c: (b * nt + c, jx)),
                  pl.BlockSpec((ts, n), lambda b, c: (b * nt + c, jg)),
                  pl.BlockSpec((kconv, width), lambda b, c: (0, 0)),
                  pl.BlockSpec((None, width, n), lambda b, c: (layer, 0, 0), pipeline_mode=pl.Buffered(1))],
        out_specs=pl.BlockSpec((ts, n), lambda b, c: (b * nt + c, 0)),
        out_shape=jax.ShapeDtypeStruct((m, n), BF16),
        scratch_shapes=[pltpu.VMEM(s, d) for s, d in scratch],
        compiler_params=_cparams(2, blocks, scratch, temps=(4 << 20) + 2 * ts * n * 4, single=single),
        name="short_conv",
    )(proj, proj, proj, proj, conv_w, w_out)


def kernel(x, p, norm_mix, w_in, ssd_conv_w, ssd_conv_b, ssd_dt_bias, ssd_a_log, ssd_d, ssd_norm,
           ssd_out, sc_conv_w, sc_out, w_o, norm_ffn, w_gate_up, w_down, norm_ple, ple_gate,
           ple_proj, norm_final):
    batch, seq, d = x.shape
    depth = w_in.shape[0]
    m = batch * seq
    heads = ssd_a_log.shape[1]
    inner = heads * HEAD_DIM
    groups = SSD_GROUPS
    r_heads = heads // groups
    xbc = ssd_conv_w.shape[2]
    scw = sc_conv_w.shape[2]
    assert xbc == inner + 2 * groups * SSD_STATE and heads <= LANES
    dt0 = inner + xbc
    off_scb = dt0
    off_scc, off_scx = off_scb + scw, off_scb + 2 * scw
    off_ga = off_scb + 3 * scw
    off_gb = off_ga + d
    hpad = LANES - heads
    p2 = p.reshape(depth, m, p.shape[-1])
    w_in_t = jnp.swapaxes(w_in, 1, 2)

    h = x.reshape(m, d)
    u, u_ssq = _prenorm(h, norm_mix[0])
    for i in range(depth):
        dt_bias = jnp.pad(ssd_dt_bias[i], (0, hpad)).reshape(1, LANES)
        a_log = jnp.pad(ssd_a_log[i], (0, hpad)).reshape(1, LANES)
        d_skip = jnp.repeat(ssd_d[i], HEAD_DIM).reshape(1, inner)

        proj = _inproj(u, u_ssq, w_in_t, i, dt0, heads)
        dtrow, wrow, acsrow, acscol = _dtprep(u, u_ssq, w_in_t, i, dt0, dt_bias, a_log, groups, r_heads)
        y = _ssd(proj, dtrow, wrow, acsrow, acscol, ssd_conv_w[i], ssd_conv_b[i].reshape(1, xbc), d_skip,
                 ssd_norm[i].reshape(1, inner), batch=batch, seq=seq, inner=inner, groups=groups)
        ybg = _sconv(proj, sc_conv_w[i], sc_out, i, batch=batch, seq=seq, width=scw,
                     off_b=off_scb, off_c=off_scc, off_x=off_scx, off_gate=off_gb)
        merged = _merge(y, ssd_out, i, ybg, proj, off_ga)
        h, v, v_ssq = _matmul_residual(merged, w_o, i, h, norm_ffn[i], "w_o_residual", in_place=i > 0)
        act = _swiglu(v, v_ssq, w_gate_up, i)
        h, hn, hn_ssq = _matmul_residual(act, w_down, i, h, norm_ple[i], "w_down_residual")
        if i + 1 < depth:
            h, u, u_ssq = _ple(hn, hn_ssq, ple_gate, p2, ple_proj, i, h, norm_mix[i + 1], last=False)
        else:
            out = _ple(hn, hn_ssq, ple_gate, p2, ple_proj, i, h, norm_final, last=True)
    return out.reshape(batch, seq, d)
```
